```python
import math
import jax, jax.numpy as jnp
from jax import lax
import numpy as np

D_MODEL = 1024
BATCH = 8
SEQ = 4096
DEPTH = 1

SB_HEADS = 8
SB_HEAD_DIM = 64
SB_WIDTH = SB_HEADS * SB_HEAD_DIM
DIFF_HEADS = 4
DIFF_HEAD_DIM = 64
DIFF_V_DIM = 2 * DIFF_HEAD_DIM
DIFF_QK_WIDTH = DIFF_HEADS * 2 * DIFF_HEAD_DIM
DIFF_V_WIDTH = DIFF_HEADS * DIFF_V_DIM
N_BRANCHES = 2
IN_WIDTH = 3 * SB_WIDTH + 2 * DIFF_QK_WIDTH + DIFF_V_WIDTH + N_BRANCHES * D_MODEL
Q_BLOCK = 128
N_EXPERTS = 32
TOP_K = 4
D_EXPERT = D_MODEL
SWIGLU_LIMIT = 7.0
SWIGLU_ALPHA = 1.702
RMS_EPS = 1e-6
N_MOD = 6

kernel_name = 'hybrid_sb_diffattn_moe_block'


def rms_norm(x, g):
    xf = x.astype(jnp.float32)
    y = xf * lax.rsqrt(jnp.mean(xf * xf, axis=-1, keepdims=True) + RMS_EPS)
    return (y * g.astype(jnp.float32)).astype(x.dtype)


def modulate(h, shift, scale):
    return h * (1.0 + scale[:, None, :]) + shift[:, None, :]


def alibi_slopes(n_heads):
    return 2.0 ** (-8.0 * jnp.arange(1, n_heads + 1, dtype=jnp.float32) / n_heads)


def _q_blocks(t):
    b, s, h, d = t.shape
    return t.reshape(b, s // Q_BLOCK, Q_BLOCK, h, d).transpose(1, 0, 3, 2, 4)


def _merge_blocks(o):
    nb, b, h, q, d = o.shape
    return o.transpose(1, 0, 3, 2, 4).reshape(b, nb * q, h, d)


def stick_breaking_attention(q, k, v):
    b, s, h, d = q.shape
    kh = k.transpose(0, 2, 1, 3)
    vh = v.transpose(0, 2, 1, 3)
    kpos = jnp.arange(s)
    scale = 1.0 / math.sqrt(SB_HEAD_DIM)

    def block(args):
        qb, t0 = args
        z = jnp.einsum('bhqd,bhkd->bhqk', qb, kh).astype(jnp.float32) * scale
        qpos = t0 + jnp.arange(Q_BLOCK)
        mask = kpos[None, :] < qpos[:, None]
        log_1mb = jnp.where(mask, jax.nn.log_sigmoid(-z), 0.0)
        between = lax.cumsum(log_1mb, axis=3, reverse=True) - log_1mb
        a = jnp.where(mask, jnp.exp(jax.nn.log_sigmoid(z) + between), 0.0)
        return jnp.einsum('bhqk,bhkd->bhqd', a.astype(vh.dtype), vh)

    starts = jnp.arange(s // Q_BLOCK) * Q_BLOCK
    o = _merge_blocks(lax.map(block, (_q_blocks(q), starts)))
    return o.reshape(b, s, h * d)


def differential_attention(q1, q2, k1, k2, v, lam, lam_init, g_subln):
    b, s, h, d = q1.shape
    k1h = k1.transpose(0, 2, 1, 3)
    k2h = k2.transpose(0, 2, 1, 3)
    vh = v.transpose(0, 2, 1, 3)
    kpos = jnp.arange(s)
    slopes = alibi_slopes(DIFF_HEADS)
    scale = 1.0 / math.sqrt(DIFF_HEAD_DIM)

    def block(args):
        q1b, q2b, t0 = args
        qpos = t0 + jnp.arange(Q_BLOCK)
        dist = (qpos[:, None] - kpos[None, :]).astype(jnp.float32)
        mask = dist >= 0.0
        bias = -slopes[:, None, None] * dist

        def probs(qb, kb):
            z = jnp.einsum('bhqd,bhkd->bhqk', qb, kb).astype(jnp.float32) * scale + bias
            return jax.nn.softmax(jnp.where(mask, z, -jnp.inf), axis=-1)

        w = probs(q1b, k1h) - lam * probs(q2b, k2h)
        return jnp.einsum('bhqk,bhkv->bhqv', w.astype(vh.dtype), vh)

    starts = jnp.arange(s // Q_BLOCK) * Q_BLOCK
    o = _merge_blocks(lax.map(block, (_q_blocks(q1), _q_blocks(q2), starts)))
    o = rms_norm(o, g_subln) * (1.0 - lam_init)
    return o.reshape(b, s, h * DIFF_V_DIM)


def moe_ffn(h, w_router, b_router, w_gate_up, b_gate_up, w_down, b_down):
    b, s, d = h.shape
    hf = h.reshape(b * s, d)
    logits = (hf @ w_router + b_router).astype(jnp.float32)
    top_val, top_idx = lax.top_k(logits, TOP_K)
    top_w = jax.nn.softmax(top_val, axis=-1)
    combine = jnp.sum(jax.nn.one_hot(top_idx, N_EXPERTS, dtype=jnp.float32) * top_w[..., None],
                      axis=1).astype(h.dtype)
    y = jnp.zeros_like(hf)
    for e in range(N_EXPERTS):
        gu = hf @ w_gate_up[e] + b_gate_up[e]
        gate = jnp.minimum(gu[:, :D_EXPERT], SWIGLU_LIMIT)
        up = jnp.clip(gu[:, D_EXPERT:], -SWIGLU_LIMIT, SWIGLU_LIMIT)
        act = (up + 1.0) * (gate * jax.nn.sigmoid(SWIGLU_ALPHA * gate))
        y = y + combine[:, e:e + 1] * (act @ w_down[e] + b_down[e])
    return y.reshape(b, s, d)


def setup_inputs(seed: int = 0) -> dict:
    key = jax.random.key(seed)
    ks = jax.random.split(key, 24)
    D = D_MODEL
    L = DEPTH

    def nrm(k, shape, std):
        return jax.random.normal(k, shape, jnp.float32) * std

    return {
        'x': nrm(ks[0], (BATCH, SEQ, D), 1.0),
        'c': nrm(ks[1], (BATCH, D), 1.0),
        'w_mod': nrm(ks[2], (L, D, N_MOD * D), 0.5 * D ** -0.5),
        'b_mod': nrm(ks[3], (L, N_MOD * D), 0.01),
        'g_pre_mix': 1.0 + nrm(ks[4], (L, D), 0.05),
        'g_post_mix': 1.0 + nrm(ks[5], (L, D), 0.05),
        'w_in': nrm(ks[6], (L, D, IN_WIDTH), D ** -0.5),
        'lambda_q1': nrm(ks[7], (L, DIFF_HEAD_DIM), 0.1),
        'lambda_k1': nrm(ks[8], (L, DIFF_HEAD_DIM), 0.1),
        'lambda_q2': nrm(ks[9], (L, DIFF_HEAD_DIM), 0.1),
        'lambda_k2': nrm(ks[10], (L, DIFF_HEAD_DIM), 0.1),
        'g_subln': 1.0 + nrm(ks[11], (L, DIFF_V_DIM), 0.05),
        'w_branch_sb': nrm(ks[12], (L, SB_WIDTH, D), SB_WIDTH ** -0.5),
        'w_branch_diff': nrm(ks[13], (L, DIFF_V_WIDTH, D), DIFF_V_WIDTH ** -0.5),
        'w_out': nrm(ks[14], (L, D, D), D ** -0.5),
        'g_pre_ffn': 1.0 + nrm(ks[15], (L, D), 0.05),
        'g_post_ffn': 1.0 + nrm(ks[16], (L, D), 0.05),
        'w_router': nrm(ks[17], (L, D, N_EXPERTS), D ** -0.5),
        'b_router': nrm(ks[18], (L, N_EXPERTS), 0.01),
        'w_gate_up': nrm(ks[19], (L, N_EXPERTS, D, 2 * D_EXPERT), D ** -0.5),
        'b_gate_up': nrm(ks[20], (L, N_EXPERTS, 2 * D_EXPERT), 0.01),
        'w_down': nrm(ks[21], (L, N_EXPERTS, D_EXPERT, D), D_EXPERT ** -0.5),
        'b_down': nrm(ks[22], (L, N_EXPERTS, D), 0.01),
    }


def reference(x, c, w_mod, b_mod, g_pre_mix, g_post_mix, w_in, lambda_q1, lambda_k1,
              lambda_q2, lambda_k2, g_subln, w_branch_sb, w_branch_diff, w_out,
              g_pre_ffn, g_post_ffn, w_router, b_router, w_gate_up, b_gate_up,
              w_down, b_down):
    b, s, d = x.shape
    c_act = jax.nn.silu(c)
    for l in range(DEPTH):
        mod = c_act @ w_mod[l] + b_mod[l]
        sh_m, sc_m, gt_m, sh_f, sc_f, gt_f = jnp.split(mod, N_MOD, axis=-1)

        h = modulate(rms_norm(x, g_pre_mix[l]), sh_m, sc_m)
        proj = h @ w_in[l]
        o0 = 0
        q_sb = proj[..., o0:o0 + SB_WIDTH].reshape(b, s, SB_HEADS, SB_HEAD_DIM); o0 += SB_WIDTH
        k_sb = proj[..., o0:o0 + SB_WIDTH].reshape(b, s, SB_HEADS, SB_HEAD_DIM); o0 += SB_WIDTH
        v_sb = proj[..., o0:o0 + SB_WIDTH].reshape(b, s, SB_HEADS, SB_HEAD_DIM); o0 += SB_WIDTH
        q_df = proj[..., o0:o0 + DIFF_QK_WIDTH].reshape(b, s, DIFF_HEADS, 2, DIFF_HEAD_DIM); o0 += DIFF_QK_WIDTH
        k_df = proj[..., o0:o0 + DIFF_QK_WIDTH].reshape(b, s, DIFF_HEADS, 2, DIFF_HEAD_DIM); o0 += DIFF_QK_WIDTH
        v_df = proj[..., o0:o0 + DIFF_V_WIDTH].reshape(b, s, DIFF_HEADS, DIFF_V_DIM); o0 += DIFF_V_WIDTH
        gates = jax.nn.sigmoid(proj[..., o0:o0 + N_BRANCHES * d])
        gate_sb, gate_df = gates[..., :d], gates[..., d:]

        y_sb = stick_breaking_attention(q_sb, k_sb, v_sb)
        lam_init = 0.8 - 0.6 * math.exp(-0.3 * l)
        lam = (jnp.exp(jnp.sum(lambda_q1[l].astype(jnp.float32) * lambda_k1[l].astype(jnp.float32)))
               - jnp.exp(jnp.sum(lambda_q2[l].astype(jnp.float32) * lambda_k2[l].astype(jnp.float32)))
               + lam_init)
        y_df = differential_attention(q_df[..., 0, :], q_df[..., 1, :], k_df[..., 0, :],
                                      k_df[..., 1, :], v_df, lam, lam_init, g_subln[l])

        merged = gate_sb * (y_sb @ w_branch_sb[l]) + gate_df * (y_df @ w_branch_diff[l])
        mix_out = merged @ w_out[l]
        x = x + gt_m[:, None, :] * rms_norm(mix_out, g_post_mix[l])

        h2 = modulate(rms_norm(x, g_pre_ffn[l]), sh_f, sc_f)
        ffn_out = moe_ffn(h2, w_router[l], b_router[l], w_gate_up[l], b_gate_up[l],
                          w_down[l], b_down[l])
        x = x + gt_f[:, None, :] * rms_norm(ffn_out, g_post_ffn[l])
    return x
```

```python
import functools
import math

import jax
import jax.numpy as jnp
from jax import lax
from jax.experimental import pallas as pl
from jax.experimental.pallas import tpu as pltpu

D_MODEL = 1024
SB_HEADS = 8
SB_HEAD_DIM = 64
SB_WIDTH = SB_HEADS * SB_HEAD_DIM
DIFF_HEADS = 4
DIFF_HEAD_DIM = 64
DIFF_V_DIM = 2 * DIFF_HEAD_DIM
DIFF_QK_WIDTH = DIFF_HEADS * 2 * DIFF_HEAD_DIM
DIFF_V_WIDTH = DIFF_HEADS * DIFF_V_DIM
N_EXPERTS = 32
TOP_K = 4
D_EXPERT = D_MODEL
SWIGLU_LIMIT = 7.0
SWIGLU_ALPHA = 1.702
RMS_EPS = 1e-6
N_MOD = 6
LAM_INIT = 0.8 - 0.6 * math.exp(-0.3 * 0)

LANES = 128
SUBLANES = 8
NEG_BIG = -1e30

MAIN_WIDTH = 2 * SB_WIDTH + 2 * DIFF_QK_WIDTH + 2 * D_MODEL
VT_ROWS = SB_WIDTH + DIFF_V_WIDTH
COLBLK_K_SB = SB_WIDTH // LANES
COLBLK_Q_DF = 2 * SB_WIDTH // LANES
COLBLK_K_DF = COLBLK_Q_DF + DIFF_QK_WIDTH // LANES
GATE_COL0 = 2 * SB_WIDTH + 2 * DIFF_QK_WIDTH

VMEM_LIMIT = 56 * 1024 * 1024


def _cparams(sem, vmem=VMEM_LIMIT):
    return pltpu.CompilerParams(dimension_semantics=sem, vmem_limit_bytes=vmem)


def _rms(x):
    return x * lax.rsqrt(jnp.mean(x * x, axis=-1, keepdims=True) + RMS_EPS)


def _mod_kernel(c_ref, w_ref, b_ref, lamv_ref, mod_ref, lam_ref):
    c = c_ref[...]
    ca = c * jax.nn.sigmoid(c)
    mod_ref[...] = jnp.dot(ca, w_ref[...], preferred_element_type=jnp.float32,
                           precision=lax.Precision.HIGHEST) + b_ref[...]
    lv = lamv_ref[...]
    s1 = jnp.sum(lv[0:1] * lv[1:2], axis=-1, keepdims=True)
    s2 = jnp.sum(lv[2:3] * lv[3:4], axis=-1, keepdims=True)
    lam = jnp.exp(s1) - jnp.exp(s2) + LAM_INIT
    lam_ref[...] = jnp.broadcast_to(lam, lam_ref.shape)


def _mod_proj(c, w_mod, b_mod, lamv):
    bsz = c.shape[0]
    tn = 1536
    n = w_mod.shape[1]
    return pl.pallas_call(
        _mod_kernel,
        grid=(n // tn,),
        in_specs=[
            pl.BlockSpec((bsz, D_MODEL), lambda j: (0, 0)),
            pl.BlockSpec((D_MODEL, tn), lambda j: (0, j)),
            pl.BlockSpec((1, tn), lambda j: (0, j)),
            pl.BlockSpec((4, DIFF_HEAD_DIM), lambda j: (0, 0)),
        ],
        out_specs=[
            pl.BlockSpec((bsz, tn), lambda j: (0, j)),
            pl.BlockSpec((SUBLANES, LANES), lambda j: (0, 0)),
        ],
        out_shape=[
            jax.ShapeDtypeStruct((bsz, n), jnp.float32),
            jax.ShapeDtypeStruct((SUBLANES, LANES), jnp.float32),
        ],
        compiler_params=_cparams(("arbitrary",)),
        name="mod_proj",
    )(c, w_mod, b_mod.reshape(1, n), lamv)


IN_CHUNK = 1024


def _in_proj_kernel(x_ref, mod_ref, g_ref, wm_ref, wvt_ref, main_ref, vt_ref):
    x = x_ref[0]
    mod = mod_ref[0]
    h = _rms(x) * g_ref[...]
    h = h * (1.0 + mod[1:2]) + mod[0:1]
    hb = h.astype(jnp.bfloat16)
    for ci in range(MAIN_WIDTH // IN_CHUNK):
        c0 = ci * IN_CHUNK
        p = jnp.dot(hb, wm_ref[:, c0:c0 + IN_CHUNK], preferred_element_type=jnp.float32)
        if c0 < GATE_COL0:
            half = IN_CHUNK // 2
            qscale = 0.0625 if c0 == 0 else 0.125
            main_ref[0, :, c0:c0 + half] = (p[:, :half] * qscale).astype(jnp.bfloat16)
            main_ref[0, :, c0 + half:c0 + IN_CHUNK] = p[:, half:].astype(jnp.bfloat16)
        else:
            main_ref[0, :, c0:c0 + IN_CHUNK] = jax.nn.sigmoid(p).astype(jnp.bfloat16)
    vt = lax.dot_general(wvt_ref[...], hb, (((1,), (1,)), ((), ())),
                         preferred_element_type=jnp.float32)
    vt_ref[0] = vt.astype(jnp.bfloat16)


def _in_proj(x, mod3, g_pre, w_main, w_vt, ts):
    bsz, seq, _ = x.shape
    return pl.pallas_call(
        _in_proj_kernel,
        grid=(bsz, seq // ts),
        in_specs=[
            pl.BlockSpec((1, ts, D_MODEL), lambda b, i: (b, i, 0)),
            pl.BlockSpec((1, N_MOD, D_MODEL), lambda b, i: (b, 0, 0)),
            pl.BlockSpec((1, D_MODEL), lambda b, i: (0, 0)),
            pl.BlockSpec((D_MODEL, MAIN_WIDTH), lambda b, i: (0, 0)),
            pl.BlockSpec((VT_ROWS, D_MODEL), lambda b, i: (0, 0)),
        ],
        out_specs=[
            pl.BlockSpec((1, ts, MAIN_WIDTH), lambda b, i: (b, i, 0)),
            pl.BlockSpec((1, VT_ROWS, ts), lambda b, i: (b, 0, i)),
        ],
        out_shape=[
            jax.ShapeDtypeStruct((bsz, seq, MAIN_WIDTH), jnp.bfloat16),
            jax.ShapeDtypeStruct((bsz, VT_ROWS, seq), jnp.bfloat16),
        ],
        compiler_params=_cparams(("arbitrary", "arbitrary")),
        name="in_proj",
    )(x, mod3, g_pre, w_main, w_vt)


def _suffix_excl_prod8(tot):
    sub = lax.broadcasted_iota(jnp.int32, tot.shape, 0)
    x = jnp.where(sub < SUBLANES - 1, pltpu.roll(tot, SUBLANES - 1, 0), 1.0)
    for sh in (1, 2, 4):
        x = x * jnp.where(sub + sh < SUBLANES, pltpu.roll(x, SUBLANES - sh, 0), 1.0)
    return x


def _sb_block(k_blk, q_h, vt_h, c8, ok, groups):
    tq = q_h.shape[0]
    zt = lax.dot_general(k_blk, q_h, (((1,), (1,)), ((), ())),
                         preferred_element_type=jnp.float32)
    r = 0.5 - 0.5 * jnp.tanh(zt)
    if ok is not None:
        r = jnp.where(ok, r, 1.0)
    rg = [r[g * SUBLANES:(g + 1) * SUBLANES, :] for g in range(groups)]
    tot = rg[0]
    for g in range(1, groups):
        tot = tot * rg[g]
    p = c8 * _suffix_excl_prod8(tot)
    pieces = [None] * groups
    for g in range(groups - 1, -1, -1):
        pn = p * rg[g]
        pieces[g] = p - pn
        p = pn
    a = jnp.concatenate(pieces, axis=0).astype(jnp.bfloat16)
    upd = jnp.dot(vt_h, a, preferred_element_type=jnp.float32)
    c_new = jnp.broadcast_to(p[0:1, :], (SUBLANES, tq))
    return upd, c_new


def _sb_attn_kernel(q_ref, k_ref, v_ref, o_ref, acc_ref, *, tq, tk):
    i = pl.program_id(2)
    groups = tk // SUBLANES
    q2 = q_ref[0]
    lane = lax.broadcasted_iota(jnp.int32, q2.shape, 1)
    zero = jnp.zeros_like(q2)
    q_heads = (jnp.where(lane < SB_HEAD_DIM, q2, zero), jnp.where(lane < SB_HEAD_DIM, zero, q2))

    row = lax.broadcasted_iota(jnp.int32, (tk, tq), 0)
    col = lax.broadcasted_iota(jnp.int32, (tk, tq), 1)
    key = (row % SUBLANES) * groups + row // SUBLANES
    ok = key < col

    ones8 = jnp.ones((SUBLANES, tq), jnp.float32)
    k_blk = k_ref[0, pl.ds(pl.multiple_of(i * tk, tk), tk), :]
    carries = []
    for h in range(2):
        vt_h = v_ref[0, h * SB_HEAD_DIM:(h + 1) * SB_HEAD_DIM, pl.ds(pl.multiple_of(i * tk, tk), tk)]
        upd, c8 = _sb_block(k_blk, q_heads[h], vt_h, ones8, ok, groups)
        acc_ref[h] = upd
        carries.append(c8)

    def body(n, carry):
        j = i - 1 - n
        off = pl.multiple_of(j * tk, tk)
        kb = k_ref[0, pl.ds(off, tk), :]
        new = []
        for h in range(2):
            vt_h = v_ref[0, h * SB_HEAD_DIM:(h + 1) * SB_HEAD_DIM, pl.ds(off, tk)]
            upd, c8 = _sb_block(kb, q_heads[h], vt_h, carry[h], None, groups)
            acc_ref[h] += upd
            new.append(c8)
        return tuple(new)

    lax.fori_loop(0, i, body, tuple(carries))
    ot = jnp.concatenate([acc_ref[0], acc_ref[1]], axis=0)
    o_ref[0] = ot.T.astype(jnp.bfloat16)


def _sb_attn(main, k_perm, vt_perm, tq, tk):
    bsz, seq, _ = main.shape
    assert tq == tk
    kern = functools.partial(_sb_attn_kernel, tq=tq, tk=tk)
    return pl.pallas_call(
        kern,
        grid=(bsz, SB_WIDTH // LANES, seq // tq),
        in_specs=[
            pl.BlockSpec((1, tq, LANES), lambda b, p, i: (b, i, p)),
            pl.BlockSpec((1, seq, LANES), lambda b, p, i: (b, 0, p)),
            pl.BlockSpec((1, LANES, seq), lambda b, p, i: (b, p, 0)),
        ],
        out_specs=pl.BlockSpec((1, tq, LANES), lambda b, p, i: (b, i, p)),
        out_shape=jax.ShapeDtypeStruct((bsz, seq, SB_WIDTH), jnp.bfloat16),
        scratch_shapes=[pltpu.VMEM((2, SB_HEAD_DIM, tq), jnp.float32)],
        compiler_params=_cparams(("arbitrary", "arbitrary", "arbitrary")),
        name="sb_attn",
    )(main, k_perm, vt_perm)


def _diff_attn_kernel(slopes_ref, q_ref, k_ref, v_ref, lam_ref, g_ref, o_ref,
                      acc_ref, m_ref, l_ref, *, tq, tk):
    hd = pl.program_id(1)
    i = pl.program_id(2)
    slope = slopes_ref[hd]
    q2 = q_ref[0]
    lane = lax.broadcasted_iota(jnp.int32, q2.shape, 1)
    zero = jnp.zeros_like(q2)
    q_maps = (jnp.where(lane < DIFF_HEAD_DIM, q2, zero), jnp.where(lane < DIFF_HEAD_DIM, zero, q2))

    row = lax.broadcasted_iota(jnp.int32, (tk, tq), 0)
    col = lax.broadcasted_iota(jnp.int32, (tk, tq), 1)
    bias = slope * (row - col).astype(jnp.float32)
    ok = row <= col

    acc_ref[...] = jnp.zeros_like(acc_ref)
    m_ref[...] = jnp.full_like(m_ref, NEG_BIG)
    l_ref[...] = jnp.zeros_like(l_ref)

    def block(j, masked):
        off = pl.multiple_of(j * tk, tk)
        kb = k_ref[0, pl.ds(off, tk), :]
        vtb = v_ref[0, :, pl.ds(off, tk)]
        cb = slope * ((j - i) * tk).astype(jnp.float32)
        for m in range(2):
            s = lax.dot_general(kb, q_maps[m], (((1,), (1,)), ((), ())),
                                preferred_element_type=jnp.float32) + bias
            if masked:
                s = jnp.where(ok, s, NEG_BIG)
            m_old = m_ref[m]
            m_new = jnp.maximum(m_old, jnp.max(s, axis=0, keepdims=True) + cb)
            alpha = jnp.exp(m_old - m_new)
            p = jnp.exp(s - (m_new - cb))
            l_ref[m] = alpha * l_ref[m] + jnp.sum(p, axis=0, keepdims=True)
            acc_ref[m] = alpha * acc_ref[m] + jnp.dot(
                vtb, p.astype(jnp.bfloat16), preferred_element_type=jnp.float32)
            m_ref[m] = m_new

    def body(j, carry):
        block(j, False)
        return carry

    lax.fori_loop(0, i, body, 0)
    block(i, True)

    lam = lam_ref[0:1, 0:1]
    o = acc_ref[0] / l_ref[0] - lam * (acc_ref[1] / l_ref[1])
    ms = jnp.mean(o * o, axis=0, keepdims=True)
    y = o * lax.rsqrt(ms + RMS_EPS) * g_ref[...] * (1.0 - LAM_INIT)
    o_ref[0] = y.T.astype(jnp.bfloat16)


def _diff_attn(main, vt, slopes, lam, g_col, tq, tk):
    bsz, seq, _ = main.shape
    assert tq == tk
    kern = functools.partial(_diff_attn_kernel, tq=tq, tk=tk)
    vrow0 = SB_WIDTH // LANES
    return pl.pallas_call(
        kern,
        grid=(bsz, DIFF_HEADS, seq // tq),
        in_specs=[
            pl.BlockSpec(memory_space=pltpu.SMEM),
            pl.BlockSpec((1, tq, LANES), lambda b, h, i: (b, i, COLBLK_Q_DF + h)),
            pl.BlockSpec((1, seq, LANES), lambda b, h, i: (b, 0, COLBLK_K_DF + h)),
            pl.BlockSpec((1, DIFF_V_DIM, seq), lambda b, h, i: (b, vrow0 + h, 0)),
            pl.BlockSpec((SUBLANES, LANES), lambda b, h, i: (0, 0)),
            pl.BlockSpec((DIFF_V_DIM, 1), lambda b, h, i: (0, 0)),
        ],
        out_specs=pl.BlockSpec((1, tq, DIFF_V_DIM), lambda b, h, i: (b, i, h)),
        out_shape=jax.ShapeDtypeStruct((bsz, seq, DIFF_V_WIDTH), jnp.bfloat16),
        scratch_shapes=[
            pltpu.VMEM((2, DIFF_V_DIM, tq), jnp.float32),
            pltpu.VMEM((2, 1, tq), jnp.float32),
            pltpu.VMEM((2, 1, tq), jnp.float32),
        ],
        compiler_params=_cparams(("arbitrary", "arbitrary", "arbitrary")),
        name="diff_attn",
    )(slopes, main, main, vt, lam, g_col)


def _pack_bf16_pair(a, b):
    ab = pltpu.bitcast(a.astype(jnp.bfloat16).astype(jnp.float32), jnp.uint32)
    bb = pltpu.bitcast(b.astype(jnp.bfloat16).astype(jnp.float32), jnp.uint32)
    return ab | (bb >> 16)


def _unpack_bf16_pair(w):
    hi = pltpu.bitcast(w & jnp.uint32(0xFFFF0000), jnp.float32)
    lo = pltpu.bitcast(w << 16, jnp.float32)
    return jnp.concatenate([hi, lo], axis=1)


def _merge_router_kernel(ysb_ref, ydf_ref, gates_ref, x_ref, mod_ref, wsb_ref, wdf_ref, wout_ref,
                         gpost_ref, gpre_ref, wrh_ref, wrl_ref, br_ref,
                         x1_ref, h2_ref, idx_ref, wgt_ref):
    mod = mod_ref[0]
    a = jnp.dot(ysb_ref[0], wsb_ref[...], preferred_element_type=jnp.float32)
    b = jnp.dot(ydf_ref[0], wdf_ref[...], preferred_element_type=jnp.float32)
    g = gates_ref[0].astype(jnp.float32)
    merged = g[:, :D_MODEL] * a + g[:, D_MODEL:] * b
    mix = jnp.dot(merged.astype(jnp.bfloat16), wout_ref[...], preferred_element_type=jnp.float32)
    x1 = x_ref[0] + mod[2:3] * (_rms(mix) * gpost_ref[...])
    x1_ref[0] = x1
    h2 = _rms(x1) * gpre_ref[...]
    h2 = h2 * (1.0 + mod[4:5]) + mod[3:4]
    half = D_MODEL // 2
    h2_ref[0] = _pack_bf16_pair(h2[:, :half], h2[:, half:])

    hh = h2.astype(jnp.bfloat16)
    hl = (h2 - hh.astype(jnp.float32)).astype(jnp.bfloat16)
    logits = (jnp.dot(hh, wrh_ref[...], preferred_element_type=jnp.float32)
              + jnp.dot(hh, wrl_ref[...], preferred_element_type=jnp.float32)
              + jnp.dot(hl, wrh_ref[...], preferred_element_type=jnp.float32)
              + br_ref[...])
    lane = lax.broadcasted_iota(jnp.int32, logits.shape, 1)
    lanef = lane.astype(jnp.float32)
    vals, idxs = [], []
    cur = logits
    for _ in range(TOP_K):
        mx = jnp.max(cur, axis=-1, keepdims=True)
        ix = jnp.min(jnp.where(cur == mx, lanef, float(LANES)), axis=-1, keepdims=True)
        cur = jnp.where(lanef == ix, -jnp.inf, cur)
        vals.append(mx)
        idxs.append(ix)
    es = [jnp.exp(v - vals[0]) for v in vals]
    den = es[0] + es[1] + es[2] + es[3]
    oi = jnp.zeros(logits.shape, jnp.float32)
    ow = jnp.zeros(logits.shape, jnp.float32)
    for k in range(TOP_K):
        oi = jnp.where(lane == k, idxs[k], oi)
        ow = jnp.where(lane == k, es[k] / den, ow)
    idx_ref[0] = oi.astype(jnp.int32)
    wgt_ref[0] = ow


def _merge_router(ysb, ydf, main, x, mod3, wsb, wdf, wout, gpost, gpre, wrh, wrl, br, ts):
    bsz, seq, _ = x.shape
    const = lambda b, i: (0, 0)
    return pl.pallas_call(
        _merge_router_kernel,
        grid=(bsz, seq // ts),
        in_specs=[
            pl.BlockSpec((1, ts, SB_WIDTH), lambda b, i: (b, i, 0)),
            pl.BlockSpec((1, ts, DIFF_V_WIDTH), lambda b, i: (b, i, 0)),
            pl.BlockSpec((1, ts, 2 * D_MODEL), lambda b, i: (b, i, GATE_COL0 // (2 * D_MODEL))),
            pl.BlockSpec((1, ts, D_MODEL), lambda b, i: (b, i, 0)),
            pl.BlockSpec((1, N_MOD, D_MODEL), lambda b, i: (b, 0, 0)),
            pl.BlockSpec((SB_WIDTH, D_MODEL), const),
            pl.BlockSpec((DIFF_V_WIDTH, D_MODEL), const),
            pl.BlockSpec((D_MODEL, D_MODEL), const),
            pl.BlockSpec((1, D_MODEL), const),
            pl.BlockSpec((1, D_MODEL), const),
            pl.BlockSpec((D_MODEL, LANES), const),
            pl.BlockSpec((D_MODEL, LANES), const),
            pl.BlockSpec((1, LANES), const),
        ],
        out_specs=[
            pl.BlockSpec((1, ts, D_MODEL), lambda b, i: (b, i, 0)),
            pl.BlockSpec((1, ts, D_MODEL // 2), lambda b, i: (b, i, 0)),
            pl.BlockSpec((1, ts, LANES), lambda b, i: (b, i, 0)),
            pl.BlockSpec((1, ts, LANES), lambda b, i: (b, i, 0)),
        ],
        out_shape=[
            jax.ShapeDtypeStruct((bsz, seq, D_MODEL), jnp.float32),
            jax.ShapeDtypeStruct((bsz, seq, D_MODEL // 2), jnp.uint32),
            jax.ShapeDtypeStruct((bsz, seq, LANES), jnp.int32),
            jax.ShapeDtypeStruct((bsz, seq, LANES), jnp.float32),
        ],
        compiler_params=_cparams(("arbitrary", "arbitrary")),
        name="merge_router",
    )(ysb, ydf, main, x, mod3, wsb, wdf, wout, gpost, gpre, wrh, wrl, br)


GATHER_UNROLL = 8


def _row_gather_start(src_hbm, idx_ref, buf_ref, sem, n_rows):
    def body(c, carry):
        for u in range(GATHER_UNROLL):
            r = c * GATHER_UNROLL + u
            pltpu.make_async_copy(src_hbm.at[pl.ds(idx_ref[0, 0, r], 1), :],
                                  buf_ref.at[pl.ds(r, 1), :], sem).start()
        return carry
    lax.fori_loop(0, n_rows // GATHER_UNROLL, body, 0)


def _row_gather_wait(src_hbm, buf_ref, sem, n_rows):
    pltpu.make_async_copy(src_hbm.at[pl.ds(0, n_rows), :], buf_ref, sem).wait()


def _moe_ffn_kernel(te_ref, nt_ref, rows0_ref, rows_next_ref, h2_hbm, wgu_ref, bgu_ref, wd_ref,
                    bd_ref, o_ref, buf_ref, sem_ref, *, tm):
    i = pl.program_id(0)
    n_valid = nt_ref[0]
    slot = i % 2

    @pl.when(i == 0)
    def _():
        _row_gather_start(h2_hbm, rows0_ref, buf_ref.at[0], sem_ref.at[0], tm)

    @pl.when(i + 1 < n_valid)
    def _():
        _row_gather_start(h2_hbm, rows_next_ref, buf_ref.at[1 - slot], sem_ref.at[1 - slot], tm)

    @pl.when(i < n_valid)
    def _():
        _row_gather_wait(h2_hbm, buf_ref.at[slot], sem_ref.at[slot], tm)
        xb = _unpack_bf16_pair(buf_ref[slot]).astype(jnp.bfloat16)
        gu = jnp.dot(xb, wgu_ref[0].astype(jnp.bfloat16),
                     preferred_element_type=jnp.float32) + bgu_ref[0]
        gate = jnp.minimum(gu[:, :D_EXPERT], SWIGLU_LIMIT)
        up = jnp.clip(gu[:, D_EXPERT:], -SWIGLU_LIMIT, SWIGLU_LIMIT)
        act = (up + 1.0) * (gate * jax.nn.sigmoid(SWIGLU_ALPHA * gate))
        out = jnp.dot(act.astype(jnp.bfloat16), wd_ref[0].astype(jnp.bfloat16),
                      preferred_element_type=jnp.float32) + bd_ref[0]
        half = D_MODEL // 2
        o_ref[...] = _pack_bf16_pair(out[:, :half], out[:, half:])

    @pl.when(i >= n_valid)
    def _():
        o_ref[...] = jnp.zeros_like(o_ref)


def _moe_ffn(tile_expert, n_valid, row_token, h2p, wgu, bgu, wd, bd, tm):
    n_tiles = row_token.shape[0]
    kern = functools.partial(_moe_ffn_kernel, tm=tm)
    grid_spec = pltpu.PrefetchScalarGridSpec(
        num_scalar_prefetch=2,
        grid=(n_tiles,),
        in_specs=[
            pl.BlockSpec((1, 1, tm), lambda i, te, nt: (0, 0, 0), memory_space=pltpu.SMEM),
            pl.BlockSpec((1, 1, tm), lambda i, te, nt: (jnp.minimum(i + 1, n_tiles - 1), 0, 0),
                         memory_space=pltpu.SMEM),
            pl.BlockSpec(memory_space=pl.ANY),
            pl.BlockSpec((1, D_MODEL, 2 * D_EXPERT), lambda i, te, nt: (te[i], 0, 0)),
            pl.BlockSpec((1, 1, 2 * D_EXPERT), lambda i, te, nt: (te[i], 0, 0)),
            pl.BlockSpec((1, D_EXPERT, D_MODEL), lambda i, te, nt: (te[i], 0, 0)),
            pl.BlockSpec((1, 1, D_MODEL), lambda i, te, nt: (te[i], 0, 0)),
        ],
        out_specs=pl.BlockSpec((tm, D_MODEL // 2), lambda i, te, nt: (i, 0)),
        scratch_shapes=[
            pltpu.VMEM((2, tm, D_MODEL // 2), jnp.uint32),
            pltpu.SemaphoreType.DMA((2,)),
        ],
    )
    return pl.pallas_call(
        kern,
        grid_spec=grid_spec,
        out_shape=jax.ShapeDtypeStruct((n_tiles * tm, D_MODEL // 2), jnp.uint32),
        compiler_params=_cparams(("arbitrary",)),
        name="moe_ffn",
    )(tile_expert, n_valid, row_token, row_token, h2p, wgu, bgu.reshape(N_EXPERTS, 1, -1), wd,
      bd.reshape(N_EXPERTS, 1, -1))


def _moe_combine_kernel(pos0_ref, pos_next_ref, rows_hbm, wgt_ref, x1_ref, mod_ref, g_ref, o_ref,
                        buf_ref, sem_ref, *, ts, n_steps):
    i = pl.program_id(0)
    slot = i % 2
    nrow = TOP_K * ts

    @pl.when(i == 0)
    def _():
        _row_gather_start(rows_hbm, pos0_ref, buf_ref.at[0], sem_ref.at[0], nrow)

    @pl.when(i + 1 < n_steps)
    def _():
        _row_gather_start(rows_hbm, pos_next_ref, buf_ref.at[1 - slot], sem_ref.at[1 - slot], nrow)

    _row_gather_wait(rows_hbm, buf_ref.at[slot], sem_ref.at[slot], nrow)
    w = wgt_ref[0]
    y = jnp.zeros((ts, D_MODEL), jnp.float32)
    for k in range(TOP_K):
        rk = _unpack_bf16_pair(buf_ref[slot, k * ts:(k + 1) * ts, :])
        y = y + w[:, k:k + 1] * rk
    mod = mod_ref[0]
    o_ref[0] = x1_ref[0] + mod[5:6] * (_rms(y) * g_ref[...])


def _moe_combine(pos_steps, rows, wgt, x1, mod3, g_post, ts):
    bsz, seq, _ = x1.shape
    per_b = seq // ts
    n_steps = bsz * per_b
    kern = functools.partial(_moe_combine_kernel, ts=ts, n_steps=n_steps)
    grid_spec = pltpu.PrefetchScalarGridSpec(
        num_scalar_prefetch=0,
        grid=(n_steps,),
        in_specs=[
            pl.BlockSpec((1, 1, TOP_K * ts), lambda i: (0, 0, 0), memory_space=pltpu.SMEM),
            pl.BlockSpec((1, 1, TOP_K * ts), lambda i: (jnp.minimum(i + 1, n_steps - 1), 0, 0),
                         memory_space=pltpu.SMEM),
            pl.BlockSpec(memory_space=pl.ANY),
            pl.BlockSpec((1, ts, LANES), lambda i: (i // per_b, i % per_b, 0)),
            pl.BlockSpec((1, ts, D_MODEL), lambda i: (i // per_b, i % per_b, 0)),
            pl.BlockSpec((1, N_MOD, D_MODEL), lambda i: (i // per_b, 0, 0)),
            pl.BlockSpec((1, D_MODEL), lambda i: (0, 0)),
        ],
        out_specs=pl.BlockSpec((1, ts, D_MODEL), lambda i: (i // per_b, i % per_b, 0)),
        scratch_shapes=[
            pltpu.VMEM((2, TOP_K * ts, D_MODEL // 2), jnp.uint32),
            pltpu.SemaphoreType.DMA((2,)),
        ],
    )
    return pl.pallas_call(
        kern,
        grid_spec=grid_spec,
        out_shape=jax.ShapeDtypeStruct((bsz, seq, D_MODEL), jnp.float32),
        compiler_params=_cparams(("arbitrary",)),
        name="moe_combine",
    )(pos_steps, pos_steps, rows, wgt, x1, mod3, g_post)


def _perm_keys_rows(a, tk):
    bsz, seq, w = a.shape
    g = tk // SUBLANES
    return a.reshape(bsz, seq // tk, SUBLANES, g, w).swapaxes(2, 3).reshape(bsz, seq, w)


def _perm_keys_cols(a, tk):
    bsz, r, seq = a.shape
    g = tk // SUBLANES
    return a.reshape(bsz, r, seq // tk, SUBLANES, g).swapaxes(3, 4).reshape(bsz, r, seq)


def _routing(top_idx, tm, n_tiles):
    n_pairs = top_idx.shape[0] * TOP_K
    e_flat = top_idx.reshape(-1)
    order = jnp.argsort(e_flat, stable=True).astype(jnp.int32)
    counts = jnp.sum(jax.nn.one_hot(e_flat, N_EXPERTS, dtype=jnp.int32), axis=0)
    padded = ((counts + tm - 1) // tm) * tm
    start = jnp.cumsum(counts) - counts
    pstart = jnp.cumsum(padded) - padded
    pend = pstart + padded
    inv = jnp.zeros((n_pairs,), jnp.int32).at[order].set(jnp.arange(n_pairs, dtype=jnp.int32))
    pos = inv - start[e_flat] + pstart[e_flat]
    rho = jnp.arange(n_tiles * tm, dtype=jnp.int32)
    e_row = jnp.minimum(jnp.searchsorted(pend, rho, side="right"), N_EXPERTS - 1).astype(jnp.int32)
    off = rho - pstart[e_row]
    valid = off < counts[e_row]
    src = jnp.clip(start[e_row] + off, 0, n_pairs - 1)
    row_token = jnp.where(valid, order[src] // TOP_K, 0).astype(jnp.int32)
    n_valid = (pend[-1] // tm).astype(jnp.int32)
    tile_rho = jnp.arange(n_tiles, dtype=jnp.int32) * tm
    last = jnp.maximum(n_valid - 1, 0) * tm
    tile_expert = e_row[jnp.minimum(tile_rho, last)]
    return pos.astype(jnp.int32), row_token.reshape(n_tiles, 1, tm), tile_expert, n_valid.reshape(1)


def _alibi_slopes(n_heads):
    return 2.0 ** (-8.0 * jnp.arange(1, n_heads + 1, dtype=jnp.float32) / n_heads)


def _layer(x, c, w_mod, b_mod, g_pre_mix, g_post_mix, w_in, lamv, g_subln, w_branch_sb,
           w_branch_diff, w_out, g_pre_ffn, g_post_ffn, w_router, b_router, w_gate_up,
           b_gate_up, w_down, b_down, *, ts_in, tq, ts_merge, tm, ts_comb):
    bsz, seq, d = x.shape
    n_tok = bsz * seq
    bf = jnp.bfloat16

    mod, lam = _mod_proj(c, w_mod, b_mod, lamv)
    mod3 = mod.reshape(bsz, N_MOD, d)

    o_vsb = 2 * SB_WIDTH
    o_qdf = 3 * SB_WIDTH
    o_vdf = o_qdf + 2 * DIFF_QK_WIDTH
    o_g = o_vdf + DIFF_V_WIDTH
    w_main = jnp.concatenate([w_in[:, :o_vsb], w_in[:, o_qdf:o_vdf], w_in[:, o_g:]], axis=1).astype(bf)
    w_vt = jnp.concatenate([w_in[:, o_vsb:o_qdf], w_in[:, o_vdf:o_g]], axis=1).T.astype(bf)

    main, vt = _in_proj(x, mod3, g_pre_mix.reshape(1, d), w_main, w_vt, ts_in)

    k_perm = _perm_keys_rows(main[:, :, SB_WIDTH:2 * SB_WIDTH], tq)
    vt_perm = _perm_keys_cols(vt[:, :SB_WIDTH, :], tq)
    y_sb = _sb_attn(main, k_perm, vt_perm, tq, tq)
    y_df = _diff_attn(main, vt, _alibi_slopes(DIFF_HEADS), lam,
                      g_subln.reshape(DIFF_V_DIM, 1), tq, tq)

    wr = jnp.zeros((d, LANES), jnp.float32).at[:, :N_EXPERTS].set(w_router)
    wrh = wr.astype(bf)
    wrl = (wr - wrh.astype(jnp.float32)).astype(bf)
    br = jnp.full((1, LANES), NEG_BIG, jnp.float32).at[0, :N_EXPERTS].set(b_router)
    x1, h2p, top_idx, top_w = _merge_router(
        y_sb, y_df, main, x, mod3, w_branch_sb.astype(bf), w_branch_diff.astype(bf),
        w_out.astype(bf), g_post_mix.reshape(1, d), g_pre_ffn.reshape(1, d), wrh, wrl, br, ts_merge)

    n_tiles = (n_tok * TOP_K) // tm + N_EXPERTS
    pos, row_token, tile_expert, n_valid = _routing(
        top_idx.reshape(n_tok, LANES)[:, :TOP_K], tm, n_tiles)
    rows = _moe_ffn(tile_expert, n_valid, row_token, h2p.reshape(n_tok, d // 2),
                    w_gate_up, b_gate_up, w_down, b_down, tm)

    n_steps = n_tok // ts_comb
    pos_steps = pos.reshape(n_steps, ts_comb, TOP_K).swapaxes(1, 2).reshape(n_steps, 1, TOP_K * ts_comb)
    return _moe_combine(pos_steps, rows, top_w, x1, mod3, g_post_ffn.reshape(1, d), ts_comb)


def kernel(x, c, w_mod, b_mod, g_pre_mix, g_post_mix, w_in, lambda_q1, lambda_k1, lambda_q2,
           lambda_k2, g_subln, w_branch_sb, w_branch_diff, w_out, g_pre_ffn, g_post_ffn,
           w_router, b_router, w_gate_up, b_gate_up, w_down, b_down):
    depth = w_mod.shape[0]
    for l in range(depth):
        lamv = jnp.stack([lambda_q1[l], lambda_k1[l], lambda_q2[l], lambda_k2[l]])
        x = _layer(x, c, w_mod[l], b_mod[l], g_pre_mix[l], g_post_mix[l], w_in[l], lamv,
                   g_subln[l], w_branch_sb[l], w_branch_diff[l], w_out[l], g_pre_ffn[l],
                   g_post_ffn[l], w_router[l], b_router[l], w_gate_up[l], b_gate_up[l],
                   w_down[l], b_down[l],
                   ts_in=512, tq=256, ts_merge=256, tm=512, ts_comb=128)
    return x
```

```python
import functools
import math

import jax
import jax.numpy as jnp
from jax import lax
from jax.experimental import pallas as pl
from jax.experimental.pallas import tpu as pltpu

D_MODEL = 1024
SB_HEADS = 8
SB_HEAD_DIM = 64
SB_WIDTH = SB_HEADS * SB_HEAD_DIM
DIFF_HEADS = 4
DIFF_HEAD_DIM = 64
DIFF_V_DIM = 2 * DIFF_HEAD_DIM
DIFF_QK_WIDTH = DIFF_HEADS * 2 * DIFF_HEAD_DIM
DIFF_V_WIDTH = DIFF_HEADS * DIFF_V_DIM
N_EXPERTS = 32
TOP_K = 4
D_EXPERT = D_MODEL
SWIGLU_LIMIT = 7.0
SWIGLU_ALPHA = 1.702
RMS_EPS = 1e-6
N_MOD = 6
LAM_INIT = 0.8 - 0.6 * math.exp(-0.3 * 0)

LANES = 128
SUBLANES = 8
NEG_BIG = -1e30

MAIN_WIDTH = 2 * SB_WIDTH + 2 * DIFF_QK_WIDTH + 2 * D_MODEL
VT_ROWS = SB_WIDTH + DIFF_V_WIDTH
COLBLK_K_SB = SB_WIDTH // LANES
COLBLK_Q_DF = 2 * SB_WIDTH // LANES
COLBLK_K_DF = COLBLK_Q_DF + DIFF_QK_WIDTH // LANES
GATE_COL0 = 2 * SB_WIDTH + 2 * DIFF_QK_WIDTH

VMEM_LIMIT = 56 * 1024 * 1024


def _cparams(sem, vmem=VMEM_LIMIT):
    return pltpu.CompilerParams(dimension_semantics=sem, vmem_limit_bytes=vmem)


def _rms(x):
    return x * lax.rsqrt(jnp.mean(x * x, axis=-1, keepdims=True) + RMS_EPS)


def _mod_kernel(c_ref, w_ref, b_ref, lamv_ref, mod_ref, lam_ref):
    c = c_ref[...]
    ca = c * jax.nn.sigmoid(c)
    mod_ref[...] = jnp.dot(ca, w_ref[...], preferred_element_type=jnp.float32,
                           precision=lax.Precision.HIGHEST) + b_ref[...]
    lv = lamv_ref[...]
    s1 = jnp.sum(lv[0:1] * lv[1:2], axis=-1, keepdims=True)
    s2 = jnp.sum(lv[2:3] * lv[3:4], axis=-1, keepdims=True)
    lam = jnp.exp(s1) - jnp.exp(s2) + LAM_INIT
    lam_ref[...] = jnp.broadcast_to(lam, lam_ref.shape)


def _mod_proj(c, w_mod, b_mod, lamv):
    bsz = c.shape[0]
    tn = 1536
    n = w_mod.shape[1]
    return pl.pallas_call(
        _mod_kernel,
        grid=(n // tn,),
        in_specs=[
            pl.BlockSpec((bsz, D_MODEL), lambda j: (0, 0)),
            pl.BlockSpec((D_MODEL, tn), lambda j: (0, j)),
            pl.BlockSpec((1, tn), lambda j: (0, j)),
            pl.BlockSpec((4, DIFF_HEAD_DIM), lambda j: (0, 0)),
        ],
        out_specs=[
            pl.BlockSpec((bsz, tn), lambda j: (0, j)),
            pl.BlockSpec((SUBLANES, LANES), lambda j: (0, 0)),
        ],
        out_shape=[
            jax.ShapeDtypeStruct((bsz, n), jnp.float32),
            jax.ShapeDtypeStruct((SUBLANES, LANES), jnp.float32),
        ],
        compiler_params=_cparams(("arbitrary",)),
        name="mod_proj",
    )(c, w_mod, b_mod.reshape(1, n), lamv)


IN_CHUNK = 1024


def _in_proj_kernel(x_ref, mod_ref, g_ref, wm_ref, wvt_ref, main_ref, vt_ref):
    x = x_ref[0]
    mod = mod_ref[0]
    h = _rms(x) * g_ref[...]
    h = h * (1.0 + mod[1:2]) + mod[0:1]
    hb = h.astype(jnp.bfloat16)
    for ci in range(MAIN_WIDTH // IN_CHUNK):
        c0 = ci * IN_CHUNK
        p = jnp.dot(hb, wm_ref[:, c0:c0 + IN_CHUNK], preferred_element_type=jnp.float32)
        if c0 < GATE_COL0:
            half = IN_CHUNK // 2
            qscale = 0.0625 if c0 == 0 else 0.125
            main_ref[0, :, c0:c0 + half] = (p[:, :half] * qscale).astype(jnp.bfloat16)
            main_ref[0, :, c0 + half:c0 + IN_CHUNK] = p[:, half:].astype(jnp.bfloat16)
        else:
            main_ref[0, :, c0:c0 + IN_CHUNK] = jax.nn.sigmoid(p).astype(jnp.bfloat16)
    vt = lax.dot_general(wvt_ref[...], hb, (((1,), (1,)), ((), ())),
                         preferred_element_type=jnp.float32)
    vt_ref[0] = vt.astype(jnp.bfloat16)


def _in_proj(x, mod3, g_pre, w_main, w_vt, ts):
    bsz, seq, _ = x.shape
    return pl.pallas_call(
        _in_proj_kernel,
        grid=(bsz, seq // ts),
        in_specs=[
            pl.BlockSpec((1, ts, D_MODEL), lambda b, i: (b, i, 0)),
            pl.BlockSpec((1, N_MOD, D_MODEL), lambda b, i: (b, 0, 0)),
            pl.BlockSpec((1, D_MODEL), lambda b, i: (0, 0)),
            pl.BlockSpec((D_MODEL, MAIN_WIDTH), lambda b, i: (0, 0)),
            pl.BlockSpec((VT_ROWS, D_MODEL), lambda b, i: (0, 0)),
        ],
        out_specs=[
            pl.BlockSpec((1, ts, MAIN_WIDTH), lambda b, i: (b, i, 0)),
            pl.BlockSpec((1, VT_ROWS, ts), lambda b, i: (b, 0, i)),
        ],
        out_shape=[
            jax.ShapeDtypeStruct((bsz, seq, MAIN_WIDTH), jnp.bfloat16),
            jax.ShapeDtypeStruct((bsz, VT_ROWS, seq), jnp.bfloat16),
        ],
        compiler_params=_cparams(("arbitrary", "arbitrary")),
        name="in_proj",
    )(x, mod3, g_pre, w_main, w_vt)


def _suffix_excl_prod8(tot):
    sub = lax.broadcasted_iota(jnp.int32, tot.shape, 0)
    x = jnp.where(sub < SUBLANES - 1, pltpu.roll(tot, SUBLANES - 1, 0), 1.0)
    for sh in (1, 2, 4):
        x = x * jnp.where(sub + sh < SUBLANES, pltpu.roll(x, SUBLANES - sh, 0), 1.0)
    return x


def _sb_scores(k_ref, q_heads, s_ref, slot, j, tk):
    kb = k_ref[0, pl.ds(pl.multiple_of(j * tk, tk), tk), :]
    for h in range(2):
        s_ref[slot, h] = lax.dot_general(kb, q_heads[h], (((1,), (1,)), ((), ())),
                                         preferred_element_type=jnp.float32)


def _sb_weights(zt, c8, ok, groups):
    tq = zt.shape[1]
    r = 0.5 - 0.5 * jnp.tanh(zt)
    if ok is not None:
        r = jnp.where(ok, r, 1.0)
    rg = [r[g * SUBLANES:(g + 1) * SUBLANES, :] for g in range(groups)]
    tot = rg[0]
    for g in range(1, groups):
        tot = tot * rg[g]
    p = c8 * _suffix_excl_prod8(tot)
    pieces = [None] * groups
    for g in range(groups - 1, -1, -1):
        pn = p * rg[g]
        pieces[g] = p - pn
        p = pn
    a = jnp.concatenate(pieces, axis=0).astype(jnp.bfloat16)
    return a, jnp.broadcast_to(p[0:1, :], (SUBLANES, tq))


def _sb_attn_kernel(q_ref, k_ref, v_ref, o_ref, acc_ref, c_ref, s_ref, *, tq, tk):
    i = pl.program_id(2)
    groups = tk // SUBLANES
    q2 = q_ref[0]
    lane = lax.broadcasted_iota(jnp.int32, q2.shape, 1)
    zero = jnp.zeros_like(q2)
    q_heads = (jnp.where(lane < SB_HEAD_DIM, q2, zero), jnp.where(lane < SB_HEAD_DIM, zero, q2))

    def step(j, slot, masked):
        _sb_scores(k_ref, q_heads, s_ref, 1 - slot, jnp.maximum(j - 1, 0), tk)
        if masked:
            row = lax.broadcasted_iota(jnp.int32, (tk, tq), 0)
            col = lax.broadcasted_iota(jnp.int32, (tk, tq), 1)
            ok = (row % SUBLANES) * groups + row // SUBLANES < col
        else:
            ok = None
        off = pl.multiple_of(j * tk, tk)
        ws = []
        for h in range(2):
            a, c_new = _sb_weights(s_ref[slot, h], c_ref[h], ok, groups)
            c_ref[h] = c_new
            ws.append(a)
        for h in range(2):
            vt_h = v_ref[0, h * SB_HEAD_DIM:(h + 1) * SB_HEAD_DIM, pl.ds(off, tk)]
            acc_ref[h] += jnp.dot(vt_h, ws[h], preferred_element_type=jnp.float32)

    acc_ref[...] = jnp.zeros_like(acc_ref)
    c_ref[...] = jnp.ones_like(c_ref)
    _sb_scores(k_ref, q_heads, s_ref, 0, i, tk)
    step(i, 0, True)

    def pair(m, carry):
        j = i - 1 - 2 * m
        step(j, 1, False)
        step(j - 1, 0, False)
        return carry

    lax.fori_loop(0, i // 2, pair, 0)

    @pl.when(i % 2 == 1)
    def _():
        step(0, 1, False)

    ot = jnp.concatenate([acc_ref[0], acc_ref[1]], axis=0)
    o_ref[0] = ot.T.astype(jnp.bfloat16)


def _sb_attn(main, k_perm, vt_perm, tq, tk):
    bsz, seq, _ = main.shape
    assert tq == tk
    kern = functools.partial(_sb_attn_kernel, tq=tq, tk=tk)
    return pl.pallas_call(
        kern,
        grid=(bsz, SB_WIDTH // LANES, seq // tq),
        in_specs=[
            pl.BlockSpec((1, tq, LANES), lambda b, p, i: (b, i, p)),
            pl.BlockSpec((1, seq, LANES), lambda b, p, i: (b, 0, p)),
            pl.BlockSpec((1, LANES, seq), lambda b, p, i: (b, p, 0)),
        ],
        out_specs=pl.BlockSpec((1, tq, LANES), lambda b, p, i: (b, i, p)),
        out_shape=jax.ShapeDtypeStruct((bsz, seq, SB_WIDTH), jnp.bfloat16),
        scratch_shapes=[
            pltpu.VMEM((2, SB_HEAD_DIM, tq), jnp.float32),
            pltpu.VMEM((2, SUBLANES, tq), jnp.float32),
            pltpu.VMEM((2, 2, tk, tq), jnp.float32),
        ],
        compiler_params=_cparams(("arbitrary", "arbitrary", "arbitrary")),
        name="sb_attn",
    )(main, k_perm, vt_perm)


def _diff_attn_kernel(slopes_ref, q_ref, k_ref, v_ref, lam_ref, g_ref, o_ref,
                      acc_ref, m_ref, l_ref, s_ref, *, tq, tk):
    hd = pl.program_id(1)
    i = pl.program_id(2)
    slope = slopes_ref[hd]
    q2 = q_ref[0]
    lane = lax.broadcasted_iota(jnp.int32, q2.shape, 1)
    zero = jnp.zeros_like(q2)
    q_maps = (jnp.where(lane < DIFF_HEAD_DIM, q2, zero), jnp.where(lane < DIFF_HEAD_DIM, zero, q2))

    row = lax.broadcasted_iota(jnp.int32, (tk, tq), 0)
    col = lax.broadcasted_iota(jnp.int32, (tk, tq), 1)
    bias = slope * (row - col).astype(jnp.float32)

    def scores(slot, j):
        kb = k_ref[0, pl.ds(pl.multiple_of(j * tk, tk), tk), :]
        for m in range(2):
            s_ref[slot, m] = lax.dot_general(kb, q_maps[m], (((1,), (1,)), ((), ())),
                                             preferred_element_type=jnp.float32)

    def step(j, slot, masked):
        scores(1 - slot, jnp.maximum(j - 1, 0))
        off = pl.multiple_of(j * tk, tk)
        vtb = v_ref[0, :, pl.ds(off, tk)]
        cb = slope * ((j - i) * tk).astype(jnp.float32)
        ps, alphas = [], []
        for m in range(2):
            s = s_ref[slot, m] + bias
            if masked:
                s = jnp.where(row <= col, s, NEG_BIG)
            m_old = m_ref[m]
            m_new = jnp.maximum(m_old, jnp.max(s, axis=0, keepdims=True) + cb)
            alpha = jnp.exp(m_old - m_new)
            p = jnp.exp(s - (m_new - cb))
            l_ref[m] = alpha * l_ref[m] + jnp.sum(p, axis=0, keepdims=True)
            m_ref[m] = m_new
            ps.append(p.astype(jnp.bfloat16))
            alphas.append(alpha)
        for m in range(2):
            acc_ref[m] = alphas[m] * acc_ref[m] + jnp.dot(
                vtb, ps[m], preferred_element_type=jnp.float32)

    acc_ref[...] = jnp.zeros_like(acc_ref)
    m_ref[...] = jnp.full_like(m_ref, NEG_BIG)
    l_ref[...] = jnp.zeros_like(l_ref)
    scores(0, i)
    step(i, 0, True)

    def pair(n, carry):
        j = i - 1 - 2 * n
        step(j, 1, False)
        step(j - 1, 0, False)
        return carry

    lax.fori_loop(0, i // 2, pair, 0)

    @pl.when(i % 2 == 1)
    def _():
        step(0, 1, False)

    lam = lam_ref[0:1, 0:1]
    o = acc_ref[0] / l_ref[0] - lam * (acc_ref[1] / l_ref[1])
    ms = jnp.mean(o * o, axis=0, keepdims=True)
    y = o * lax.rsqrt(ms + RMS_EPS) * g_ref[...] * (1.0 - LAM_INIT)
    o_ref[0] = y.T.astype(jnp.bfloat16)


def _diff_attn(main, vt, slopes, lam, g_col, tq, tk):
    bsz, seq, _ = main.shape
    assert tq == tk
    kern = functools.partial(_diff_attn_kernel, tq=tq, tk=tk)
    vrow0 = SB_WIDTH // LANES
    return pl.pallas_call(
        kern,
        grid=(bsz, DIFF_HEADS, seq // tq),
        in_specs=[
            pl.BlockSpec(memory_space=pltpu.SMEM),
            pl.BlockSpec((1, tq, LANES), lambda b, h, i: (b, i, COLBLK_Q_DF + h)),
            pl.BlockSpec((1, seq, LANES), lambda b, h, i: (b, 0, COLBLK_K_DF + h)),
            pl.BlockSpec((1, DIFF_V_DIM, seq), lambda b, h, i: (b, vrow0 + h, 0)),
            pl.BlockSpec((SUBLANES, LANES), lambda b, h, i: (0, 0)),
            pl.BlockSpec((DIFF_V_DIM, 1), lambda b, h, i: (0, 0)),
        ],
        out_specs=pl.BlockSpec((1, tq, DIFF_V_DIM), lambda b, h, i: (b, i, h)),
        out_shape=jax.ShapeDtypeStruct((bsz, seq, DIFF_V_WIDTH), jnp.bfloat16),
        scratch_shapes=[
            pltpu.VMEM((2, DIFF_V_DIM, tq), jnp.float32),
            pltpu.VMEM((2, 1, tq), jnp.float32),
            pltpu.VMEM((2, 1, tq), jnp.float32),
            pltpu.VMEM((2, 2, tk, tq), jnp.float32),
        ],
        compiler_params=_cparams(("arbitrary", "arbitrary", "arbitrary")),
        name="diff_attn",
    )(slopes, main, main, vt, lam, g_col)


def _pack_bf16_pair(a, b):
    ab = pltpu.bitcast(a.astype(jnp.bfloat16).astype(jnp.float32), jnp.uint32)
    bb = pltpu.bitcast(b.astype(jnp.bfloat16).astype(jnp.float32), jnp.uint32)
    return ab | (bb >> 16)


def _unpack_bf16_pair(w):
    hi = pltpu.bitcast(w & jnp.uint32(0xFFFF0000), jnp.float32)
    lo = pltpu.bitcast(w << 16, jnp.float32)
    return jnp.concatenate([hi, lo], axis=1)


def _merge_router_kernel(ysb_ref, ydf_ref, gates_ref, x_ref, mod_ref, wsb_ref, wdf_ref, wout_ref,
                         gpost_ref, gpre_ref, wrh_ref, wrl_ref, br_ref,
                         x1_ref, h2_ref, idx_ref, wgt_ref, rank_ref, cnt_ref, base_ref):
    first = jnp.logical_and(pl.program_id(0) == 0, pl.program_id(1) == 0)

    @pl.when(first)
    def _():
        base_ref[...] = jnp.zeros_like(base_ref)

    mod = mod_ref[0]
    a = jnp.dot(ysb_ref[0], wsb_ref[...], preferred_element_type=jnp.float32)
    b = jnp.dot(ydf_ref[0], wdf_ref[...], preferred_element_type=jnp.float32)
    g = gates_ref[0].astype(jnp.float32)
    merged = g[:, :D_MODEL] * a + g[:, D_MODEL:] * b
    mix = jnp.dot(merged.astype(jnp.bfloat16), wout_ref[...], preferred_element_type=jnp.float32)
    x1 = x_ref[0] + mod[2:3] * (_rms(mix) * gpost_ref[...])
    x1_ref[0] = x1
    h2 = _rms(x1) * gpre_ref[...]
    h2 = h2 * (1.0 + mod[4:5]) + mod[3:4]
    half = D_MODEL // 2
    h2_ref[0] = _pack_bf16_pair(h2[:, :half], h2[:, half:])

    hh = h2.astype(jnp.bfloat16)
    hl = (h2 - hh.astype(jnp.float32)).astype(jnp.bfloat16)
    logits = (jnp.dot(hh, wrh_ref[...], preferred_element_type=jnp.float32)
              + jnp.dot(hh, wrl_ref[...], preferred_element_type=jnp.float32)
              + jnp.dot(hl, wrh_ref[...], preferred_element_type=jnp.float32)
              + br_ref[...])
    lane = lax.broadcasted_iota(jnp.int32, logits.shape, 1)
    lanef = lane.astype(jnp.float32)
    vals, idxs = [], []
    cur = logits
    for _ in range(TOP_K):
        mx = jnp.max(cur, axis=-1, keepdims=True)
        ix = jnp.min(jnp.where(cur == mx, lanef, float(LANES)), axis=-1, keepdims=True)
        cur = jnp.where(lanef == ix, -jnp.inf, cur)
        vals.append(mx)
        idxs.append(ix)
    es = [jnp.exp(v - vals[0]) for v in vals]
    den = es[0] + es[1] + es[2] + es[3]
    oi = jnp.zeros(logits.shape, jnp.float32)
    ow = jnp.zeros(logits.shape, jnp.float32)
    for k in range(TOP_K):
        oi = jnp.where(lane == k, idxs[k], oi)
        ow = jnp.where(lane == k, es[k] / den, ow)
    idx_ref[0] = oi.astype(jnp.int32)
    wgt_ref[0] = ow

    ts = logits.shape[0]
    member = jnp.zeros(logits.shape, jnp.float32)
    for k in range(TOP_K):
        member = member + (lanef == idxs[k]).astype(jnp.float32)
    rr = lax.broadcasted_iota(jnp.int32, (ts, ts), 0)
    cc = lax.broadcasted_iota(jnp.int32, (ts, ts), 1)
    lower = jnp.where(cc < rr, 1.0, 0.0).astype(jnp.bfloat16)
    before = jnp.dot(lower, member.astype(jnp.bfloat16), preferred_element_type=jnp.float32)
    base = base_ref[0:1, :]
    rank_all = before + base
    orank = jnp.zeros(logits.shape, jnp.float32)
    for k in range(TOP_K):
        rk = jnp.sum(jnp.where(lanef == idxs[k], rank_all, 0.0), axis=-1, keepdims=True)
        orank = jnp.where(lane == k, rk, orank)
    rank_ref[0] = orank.astype(jnp.int32)
    new_base = base + jnp.sum(member, axis=0, keepdims=True)
    base_ref[...] = jnp.broadcast_to(new_base, base_ref.shape)
    cnt_ref[...] = jnp.broadcast_to(new_base, cnt_ref.shape).astype(jnp.int32)


def _merge_router(ysb, ydf, main, x, mod3, wsb, wdf, wout, gpost, gpre, wrh, wrl, br, ts):
    bsz, seq, _ = x.shape
    const = lambda b, i: (0, 0)
    return pl.pallas_call(
        _merge_router_kernel,
        grid=(bsz, seq // ts),
        in_specs=[
            pl.BlockSpec((1, ts, SB_WIDTH), lambda b, i: (b, i, 0)),
            pl.BlockSpec((1, ts, DIFF_V_WIDTH), lambda b, i: (b, i, 0)),
            pl.BlockSpec((1, ts, 2 * D_MODEL), lambda b, i: (b, i, GATE_COL0 // (2 * D_MODEL))),
            pl.BlockSpec((1, ts, D_MODEL), lambda b, i: (b, i, 0)),
            pl.BlockSpec((1, N_MOD, D_MODEL), lambda b, i: (b, 0, 0)),
            pl.BlockSpec((SB_WIDTH, D_MODEL), const),
            pl.BlockSpec((DIFF_V_WIDTH, D_MODEL), const),
            pl.BlockSpec((D_MODEL, D_MODEL), const),
            pl.BlockSpec((1, D_MODEL), const),
            pl.BlockSpec((1, D_MODEL), const),
            pl.BlockSpec((D_MODEL, LANES), const),
            pl.BlockSpec((D_MODEL, LANES), const),
            pl.BlockSpec((1, LANES), const),
        ],
        out_specs=[
            pl.BlockSpec((1, ts, D_MODEL), lambda b, i: (b, i, 0)),
            pl.BlockSpec((1, ts, D_MODEL // 2), lambda b, i: (b, i, 0)),
            pl.BlockSpec((1, ts, LANES), lambda b, i: (b, i, 0)),
            pl.BlockSpec((1, ts, LANES), lambda b, i: (b, i, 0)),
            pl.BlockSpec((1, ts, LANES), lambda b, i: (b, i, 0)),
            pl.BlockSpec((SUBLANES, LANES), const),
        ],
        out_shape=[
            jax.ShapeDtypeStruct((bsz, seq, D_MODEL), jnp.float32),
            jax.ShapeDtypeStruct((bsz, seq, D_MODEL // 2), jnp.uint32),
            jax.ShapeDtypeStruct((bsz, seq, LANES), jnp.int32),
            jax.ShapeDtypeStruct((bsz, seq, LANES), jnp.float32),
            jax.ShapeDtypeStruct((bsz, seq, LANES), jnp.int32),
            jax.ShapeDtypeStruct((SUBLANES, LANES), jnp.int32),
        ],
        scratch_shapes=[pltpu.VMEM((SUBLANES, LANES), jnp.float32)],
        compiler_params=_cparams(("arbitrary", "arbitrary")),
        name="merge_router",
    )(ysb, ydf, main, x, mod3, wsb, wdf, wout, gpost, gpre, wrh, wrl, br)


def _row_gather_start(src_hbm, idx_ref, buf_ref, sem, n_rows):
    for r in range(n_rows):
        pltpu.make_async_copy(src_hbm.at[pl.ds(idx_ref[0, 0, r], 1), :],
                              buf_ref.at[pl.ds(r, 1), :], sem).start()


def _row_gather_wait(src_hbm, buf_ref, sem, n_rows):
    pltpu.make_async_copy(src_hbm.at[pl.ds(0, n_rows), :], buf_ref, sem).wait()


def _moe_ffn_kernel(te_ref, nt_ref, rows0_ref, rows_next_ref, h2_hbm, wgu_ref, bgu_ref, wd_ref,
                    bd_ref, o_ref, buf_a, buf_b, sem_ref, *, tm):
    i = pl.program_id(0)
    n_valid = nt_ref[0]

    @pl.when(i == 0)
    def _():
        _row_gather_start(h2_hbm, rows0_ref, buf_a, sem_ref.at[0], tm)

    def tile(cur, cur_sem, nxt, nxt_sem):
        _row_gather_wait(h2_hbm, cur, cur_sem, tm)
        _row_gather_start(h2_hbm, rows_next_ref, nxt, nxt_sem, tm)
        xb = _unpack_bf16_pair(cur[...]).astype(jnp.bfloat16)
        gu = jnp.dot(xb, wgu_ref[0].astype(jnp.bfloat16),
                     preferred_element_type=jnp.float32) + bgu_ref[0]
        gate = jnp.minimum(gu[:, :D_EXPERT], SWIGLU_LIMIT)
        up = jnp.clip(gu[:, D_EXPERT:], -SWIGLU_LIMIT, SWIGLU_LIMIT)
        act = (up + 1.0) * (gate * jax.nn.sigmoid(SWIGLU_ALPHA * gate))
        out = jnp.dot(act.astype(jnp.bfloat16), wd_ref[0].astype(jnp.bfloat16),
                      preferred_element_type=jnp.float32) + bd_ref[0]
        half = D_MODEL // 2
        o_ref[...] = _pack_bf16_pair(out[:, :half], out[:, half:])

        @pl.when(i == n_valid - 1)
        def _():
            _row_gather_wait(h2_hbm, nxt, nxt_sem, tm)

    @pl.when(jnp.logical_and(i < n_valid, i % 2 == 0))
    def _():
        tile(buf_a, sem_ref.at[0], buf_b, sem_ref.at[1])

    @pl.when(jnp.logical_and(i < n_valid, i % 2 == 1))
    def _():
        tile(buf_b, sem_ref.at[1], buf_a, sem_ref.at[0])

    @pl.when(i >= n_valid)
    def _():
        o_ref[...] = jnp.zeros_like(o_ref)


def _moe_ffn(tile_expert, n_valid, row_token, h2p, wgu, bgu, wd, bd, tm):
    n_tiles = row_token.shape[0]
    kern = functools.partial(_moe_ffn_kernel, tm=tm)
    grid_spec = pltpu.PrefetchScalarGridSpec(
        num_scalar_prefetch=2,
        grid=(n_tiles,),
        in_specs=[
            pl.BlockSpec((1, 1, tm), lambda i, te, nt: (0, 0, 0), memory_space=pltpu.SMEM),
            pl.BlockSpec((1, 1, tm), lambda i, te, nt: (jnp.minimum(i + 1, nt[0] - 1), 0, 0),
                         memory_space=pltpu.SMEM),
            pl.BlockSpec(memory_space=pl.ANY),
            pl.BlockSpec((1, D_MODEL, 2 * D_EXPERT), lambda i, te, nt: (te[i], 0, 0)),
            pl.BlockSpec((1, 1, 2 * D_EXPERT), lambda i, te, nt: (te[i], 0, 0)),
            pl.BlockSpec((1, D_EXPERT, D_MODEL), lambda i, te, nt: (te[i], 0, 0)),
            pl.BlockSpec((1, 1, D_MODEL), lambda i, te, nt: (te[i], 0, 0)),
        ],
        out_specs=pl.BlockSpec((tm, D_MODEL // 2), lambda i, te, nt: (i, 0)),
        scratch_shapes=[
            pltpu.VMEM((tm, D_MODEL // 2), jnp.uint32),
            pltpu.VMEM((tm, D_MODEL // 2), jnp.uint32),
            pltpu.SemaphoreType.DMA((2,)),
        ],
    )
    return pl.pallas_call(
        kern,
        grid_spec=grid_spec,
        out_shape=jax.ShapeDtypeStruct((n_tiles * tm, D_MODEL // 2), jnp.uint32),
        compiler_params=_cparams(("arbitrary",)),
        name="moe_ffn",
    )(tile_expert, n_valid, row_token, row_token, h2p, wgu, bgu.reshape(N_EXPERTS, 1, -1), wd,
      bd.reshape(N_EXPERTS, 1, -1))


def _moe_combine_kernel(pos0_ref, pos_next_ref, rows_hbm, wgt_ref, x1_ref, mod_ref, g_ref, o_ref,
                        buf_a, buf_b, sem_ref, *, ts, n_steps):
    i = pl.program_id(0)
    nrow = TOP_K * ts

    @pl.when(i == 0)
    def _():
        _row_gather_start(rows_hbm, pos0_ref, buf_a, sem_ref.at[0], nrow)

    def step(cur, cur_sem, nxt, nxt_sem):
        _row_gather_wait(rows_hbm, cur, cur_sem, nrow)
        _row_gather_start(rows_hbm, pos_next_ref, nxt, nxt_sem, nrow)
        w = wgt_ref[0]
        y = jnp.zeros((ts, D_MODEL), jnp.float32)
        for k in range(TOP_K):
            y = y + w[:, k:k + 1] * _unpack_bf16_pair(cur[k * ts:(k + 1) * ts, :])
        mod = mod_ref[0]
        o_ref[0] = x1_ref[0] + mod[5:6] * (_rms(y) * g_ref[...])

        @pl.when(i == n_steps - 1)
        def _():
            _row_gather_wait(rows_hbm, nxt, nxt_sem, nrow)

    @pl.when(i % 2 == 0)
    def _():
        step(buf_a, sem_ref.at[0], buf_b, sem_ref.at[1])

    @pl.when(i % 2 == 1)
    def _():
        step(buf_b, sem_ref.at[1], buf_a, sem_ref.at[0])


def _moe_combine(pos_steps, rows, wgt, x1, mod3, g_post, ts):
    bsz, seq, _ = x1.shape
    per_b = seq // ts
    n_steps = bsz * per_b
    kern = functools.partial(_moe_combine_kernel, ts=ts, n_steps=n_steps)
    grid_spec = pltpu.PrefetchScalarGridSpec(
        num_scalar_prefetch=0,
        grid=(n_steps,),
        in_specs=[
            pl.BlockSpec((1, 1, TOP_K * ts), lambda i: (0, 0, 0), memory_space=pltpu.SMEM),
            pl.BlockSpec((1, 1, TOP_K * ts), lambda i: (jnp.minimum(i + 1, n_steps - 1), 0, 0),
                         memory_space=pltpu.SMEM),
            pl.BlockSpec(memory_space=pl.ANY),
            pl.BlockSpec((1, ts, LANES), lambda i: (i // per_b, i % per_b, 0)),
            pl.BlockSpec((1, ts, D_MODEL), lambda i: (i // per_b, i % per_b, 0)),
            pl.BlockSpec((1, N_MOD, D_MODEL), lambda i: (i // per_b, 0, 0)),
            pl.BlockSpec((1, D_MODEL), lambda i: (0, 0)),
        ],
        out_specs=pl.BlockSpec((1, ts, D_MODEL), lambda i: (i // per_b, i % per_b, 0)),
        scratch_shapes=[
            pltpu.VMEM((TOP_K * ts, D_MODEL // 2), jnp.uint32),
            pltpu.VMEM((TOP_K * ts, D_MODEL // 2), jnp.uint32),
            pltpu.SemaphoreType.DMA((2,)),
        ],
    )
    return pl.pallas_call(
        kern,
        grid_spec=grid_spec,
        out_shape=jax.ShapeDtypeStruct((bsz, seq, D_MODEL), jnp.float32),
        compiler_params=_cparams(("arbitrary",)),
        name="moe_combine",
    )(pos_steps, pos_steps, rows, wgt, x1, mod3, g_post)


def _perm_keys_rows(a, tk):
    bsz, seq, w = a.shape
    g = tk // SUBLANES
    return a.reshape(bsz, seq // tk, SUBLANES, g, w).swapaxes(2, 3).reshape(bsz, seq, w)


def _perm_keys_cols(a, tk):
    bsz, r, seq = a.shape
    g = tk // SUBLANES
    return a.reshape(bsz, r, seq // tk, SUBLANES, g).swapaxes(3, 4).reshape(bsz, r, seq)


def _routing(top_idx, rank, counts, tm, n_tiles):
    n_tok = top_idx.shape[0]
    padded = ((counts + tm - 1) // tm) * tm
    start = jnp.cumsum(counts) - counts
    pend = jnp.cumsum(padded)
    pstart = pend - padded
    onehot = top_idx[:, :, None] == jnp.arange(N_EXPERTS, dtype=jnp.int32)[None, None, :]
    pos = rank + jnp.sum(jnp.where(onehot, pstart[None, None, :], 0), axis=-1)
    pair_id = jnp.arange(n_tok * TOP_K, dtype=jnp.int32)
    _, order = lax.sort_key_val(pos.reshape(-1), pair_id)
    n_valid = (pend[-1] // tm).astype(jnp.int32)
    tile_row0 = jnp.arange(n_tiles, dtype=jnp.int32) * tm
    tile_expert = jnp.minimum(
        jnp.sum((tile_row0[:, None] >= pend[None, :]).astype(jnp.int32), axis=1), N_EXPERTS - 1)
    te_oh = tile_expert[:, None] == jnp.arange(N_EXPERTS, dtype=jnp.int32)[None, :]
    t_pstart = jnp.sum(jnp.where(te_oh, pstart[None, :], 0), axis=1)
    t_start = jnp.sum(jnp.where(te_oh, start[None, :], 0), axis=1)
    t_count = jnp.sum(jnp.where(te_oh, counts[None, :], 0), axis=1)
    off = (tile_row0 - t_pstart)[:, None] + jnp.arange(tm, dtype=jnp.int32)[None, :]
    valid = off < t_count[:, None]
    src = jnp.clip(t_start[:, None] + off, 0, n_tok * TOP_K - 1)
    row_token = jnp.where(valid, order[src] // TOP_K, 0).astype(jnp.int32)
    tile_expert = tile_expert[jnp.minimum(jnp.arange(n_tiles), jnp.maximum(n_valid - 1, 0))]
    return (pos.astype(jnp.int32), row_token.reshape(n_tiles, 1, tm), tile_expert.astype(jnp.int32),
            n_valid.reshape(1))


def _alibi_slopes(n_heads):
    return 2.0 ** (-8.0 * jnp.arange(1, n_heads + 1, dtype=jnp.float32) / n_heads)


def _layer(x, c, w_mod, b_mod, g_pre_mix, g_post_mix, w_in, lamv, g_subln, w_branch_sb,
           w_branch_diff, w_out, g_pre_ffn, g_post_ffn, w_router, b_router, w_gate_up,
           b_gate_up, w_down, b_down, *, ts_in, tq, ts_merge, tm, ts_comb):
    bsz, seq, d = x.shape
    n_tok = bsz * seq
    bf = jnp.bfloat16

    mod, lam = _mod_proj(c, w_mod, b_mod, lamv)
    mod3 = mod.reshape(bsz, N_MOD, d)

    o_vsb = 2 * SB_WIDTH
    o_qdf = 3 * SB_WIDTH
    o_vdf = o_qdf + 2 * DIFF_QK_WIDTH
    o_g = o_vdf + DIFF_V_WIDTH
    w_main = jnp.concatenate([w_in[:, :o_vsb], w_in[:, o_qdf:o_vdf], w_in[:, o_g:]], axis=1).astype(bf)
    w_vt = jnp.concatenate([w_in[:, o_vsb:o_qdf], w_in[:, o_vdf:o_g]], axis=1).T.astype(bf)

    main, vt = _in_proj(x, mod3, g_pre_mix.reshape(1, d), w_main, w_vt, ts_in)

    k_perm = _perm_keys_rows(main[:, :, SB_WIDTH:2 * SB_WIDTH], tq)
    vt_perm = _perm_keys_cols(vt[:, :SB_WIDTH, :], tq)
    y_sb = _sb_attn(main, k_perm, vt_perm, tq, tq)
    y_df = _diff_attn(main, vt, _alibi_slopes(DIFF_HEADS), lam,
                      g_subln.reshape(DIFF_V_DIM, 1), tq, tq)

    wr = jnp.zeros((d, LANES), jnp.float32).at[:, :N_EXPERTS].set(w_router)
    wrh = wr.astype(bf)
    wrl = (wr - wrh.astype(jnp.float32)).astype(bf)
    br = jnp.full((1, LANES), NEG_BIG, jnp.float32).at[0, :N_EXPERTS].set(b_router)
    x1, h2p, top_idx, top_w, rank, counts = _merge_router(
        y_sb, y_df, main, x, mod3, w_branch_sb.astype(bf), w_branch_diff.astype(bf),
        w_out.astype(bf), g_post_mix.reshape(1, d), g_pre_ffn.reshape(1, d), wrh, wrl, br, ts_merge)

    n_tiles = (n_tok * TOP_K) // tm + N_EXPERTS
    pos, row_token, tile_expert, n_valid = _routing(
        top_idx.reshape(n_tok, LANES)[:, :TOP_K], rank.reshape(n_tok, LANES)[:, :TOP_K],
        counts[0, :N_EXPERTS], tm, n_tiles)
    rows = _moe_ffn(tile_expert, n_valid, row_token, h2p.reshape(n_tok, d // 2),
                    w_gate_up, b_gate_up, w_down, b_down, tm)

    n_steps = n_tok // ts_comb
    pos_steps = pos.reshape(n_steps, ts_comb, TOP_K).swapaxes(1, 2).reshape(n_steps, 1, TOP_K * ts_comb)
    return _moe_combine(pos_steps, rows, top_w, x1, mod3, g_post_ffn.reshape(1, d), ts_comb)


def kernel(x, c, w_mod, b_mod, g_pre_mix, g_post_mix, w_in, lambda_q1, lambda_k1, lambda_q2,
           lambda_k2, g_subln, w_branch_sb, w_branch_diff, w_out, g_pre_ffn, g_post_ffn,
           w_router, b_router, w_gate_up, b_gate_up, w_down, b_down):
    depth = w_mod.shape[0]
    for l in range(depth):
        lamv = jnp.stack([lambda_q1[l], lambda_k1[l], lambda_q2[l], lambda_k2[l]])
        x = _layer(x, c, w_mod[l], b_mod[l], g_pre_mix[l], g_post_mix[l], w_in[l], lamv,
                   g_subln[l], w_branch_sb[l], w_branch_diff[l], w_out[l], g_pre_ffn[l],
                   g_post_ffn[l], w_router[l], b_router[l], w_gate_up[l], b_gate_up[l],
                   w_down[l], b_down[l],
                   ts_in=512, tq=256, ts_merge=256, tm=512, ts_comb=128)
    return x
```

```python
import functools
import math

import jax
import jax.numpy as jnp
from jax import lax
from jax.experimental import pallas as pl
from jax.experimental.pallas import tpu as pltpu
from jax.experimental.pallas import tpu_sc as plsc

D_MODEL = 1024
SB_HEADS = 8
SB_HEAD_DIM = 64
SB_WIDTH = SB_HEADS * SB_HEAD_DIM
DIFF_HEADS = 4
DIFF_HEAD_DIM = 64
DIFF_V_DIM = 2 * DIFF_HEAD_DIM
DIFF_QK_WIDTH = DIFF_HEADS * 2 * DIFF_HEAD_DIM
DIFF_V_WIDTH = DIFF_HEADS * DIFF_V_DIM
N_EXPERTS = 32
TOP_K = 4
D_EXPERT = D_MODEL
SWIGLU_LIMIT = 7.0
SWIGLU_ALPHA = 1.702
RMS_EPS = 1e-6
N_MOD = 6
LAM_INIT = 0.8 - 0.6 * math.exp(-0.3 * 0)

LANES = 128
SUBLANES = 8
NEG_BIG = -1e30
SC_CORES = 2
SC_SUBCORES = 16
SC_GATHER_ROWS = 128

MAIN_WIDTH = 2 * SB_WIDTH + 2 * DIFF_QK_WIDTH + 2 * D_MODEL
VT_ROWS = SB_WIDTH + DIFF_V_WIDTH
COLBLK_K_SB = SB_WIDTH // LANES
COLBLK_Q_DF = 2 * SB_WIDTH // LANES
COLBLK_K_DF = COLBLK_Q_DF + DIFF_QK_WIDTH // LANES
GATE_COL0 = 2 * SB_WIDTH + 2 * DIFF_QK_WIDTH

VMEM_LIMIT = 56 * 1024 * 1024


def _cparams(sem, vmem=VMEM_LIMIT):
    return pltpu.CompilerParams(dimension_semantics=sem, vmem_limit_bytes=vmem)


def _rms(x):
    return x * lax.rsqrt(jnp.mean(x * x, axis=-1, keepdims=True) + RMS_EPS)


def _mod_kernel(c_ref, w_ref, b_ref, lamv_ref, mod_ref, lam_ref):
    c = c_ref[...]
    ca = c * jax.nn.sigmoid(c)
    mod_ref[...] = jnp.dot(ca, w_ref[...], preferred_element_type=jnp.float32,
                           precision=lax.Precision.HIGHEST) + b_ref[...]
    lv = lamv_ref[...]
    s1 = jnp.sum(lv[0:1] * lv[1:2], axis=-1, keepdims=True)
    s2 = jnp.sum(lv[2:3] * lv[3:4], axis=-1, keepdims=True)
    lam = jnp.exp(s1) - jnp.exp(s2) + LAM_INIT
    lam_ref[...] = jnp.broadcast_to(lam, lam_ref.shape)


def _mod_proj(c, w_mod, b_mod, lamv):
    bsz = c.shape[0]
    tn = 1536
    n = w_mod.shape[1]
    return pl.pallas_call(
        _mod_kernel,
        grid=(n // tn,),
        in_specs=[
            pl.BlockSpec((bsz, D_MODEL), lambda j: (0, 0)),
            pl.BlockSpec((D_MODEL, tn), lambda j: (0, j)),
            pl.BlockSpec((1, tn), lambda j: (0, j)),
            pl.BlockSpec((4, DIFF_HEAD_DIM), lambda j: (0, 0)),
        ],
        out_specs=[
            pl.BlockSpec((bsz, tn), lambda j: (0, j)),
            pl.BlockSpec((SUBLANES, LANES), lambda j: (0, 0)),
        ],
        out_shape=[
            jax.ShapeDtypeStruct((bsz, n), jnp.float32),
            jax.ShapeDtypeStruct((SUBLANES, LANES), jnp.float32),
        ],
        compiler_params=_cparams(("arbitrary",)),
        name="mod_proj",
    )(c, w_mod, b_mod.reshape(1, n), lamv)


IN_CHUNK = 1024


def _in_proj_kernel(x_ref, mod_ref, g_ref, wm_ref, wvt_ref, main_ref, vt_ref):
    x = x_ref[0]
    mod = mod_ref[0]
    h = _rms(x) * g_ref[...]
    h = h * (1.0 + mod[1:2]) + mod[0:1]
    hb = h.astype(jnp.bfloat16)
    for ci in range(MAIN_WIDTH // IN_CHUNK):
        c0 = ci * IN_CHUNK
        p = jnp.dot(hb, wm_ref[:, c0:c0 + IN_CHUNK], preferred_element_type=jnp.float32)
        if c0 < GATE_COL0:
            half = IN_CHUNK // 2
            qscale = 0.0625 if c0 == 0 else 0.125
            main_ref[0, :, c0:c0 + half] = (p[:, :half] * qscale).astype(jnp.bfloat16)
            main_ref[0, :, c0 + half:c0 + IN_CHUNK] = p[:, half:].astype(jnp.bfloat16)
        else:
            main_ref[0, :, c0:c0 + IN_CHUNK] = jax.nn.sigmoid(p).astype(jnp.bfloat16)
    vt = lax.dot_general(wvt_ref[...], hb, (((1,), (1,)), ((), ())),
                         preferred_element_type=jnp.float32)
    vt_ref[0] = vt.astype(jnp.bfloat16)


def _in_proj(x, mod3, g_pre, w_main, w_vt, ts):
    bsz, seq, _ = x.shape
    return pl.pallas_call(
        _in_proj_kernel,
        grid=(bsz, seq // ts),
        in_specs=[
            pl.BlockSpec((1, ts, D_MODEL), lambda b, i: (b, i, 0)),
            pl.BlockSpec((1, N_MOD, D_MODEL), lambda b, i: (b, 0, 0)),
            pl.BlockSpec((1, D_MODEL), lambda b, i: (0, 0)),
            pl.BlockSpec((D_MODEL, MAIN_WIDTH), lambda b, i: (0, 0)),
            pl.BlockSpec((VT_ROWS, D_MODEL), lambda b, i: (0, 0)),
        ],
        out_specs=[
            pl.BlockSpec((1, ts, MAIN_WIDTH), lambda b, i: (b, i, 0)),
            pl.BlockSpec((1, VT_ROWS, ts), lambda b, i: (b, 0, i)),
        ],
        out_shape=[
            jax.ShapeDtypeStruct((bsz, seq, MAIN_WIDTH), jnp.bfloat16),
            jax.ShapeDtypeStruct((bsz, VT_ROWS, seq), jnp.bfloat16),
        ],
        compiler_params=_cparams(("arbitrary", "arbitrary")),
        name="in_proj",
    )(x, mod3, g_pre, w_main, w_vt)


def _suffix_excl_prod8(tot):
    sub = lax.broadcasted_iota(jnp.int32, tot.shape, 0)
    x = jnp.where(sub < SUBLANES - 1, pltpu.roll(tot, SUBLANES - 1, 0), 1.0)
    for sh in (1, 2, 4):
        x = x * jnp.where(sub + sh < SUBLANES, pltpu.roll(x, SUBLANES - sh, 0), 1.0)
    return x


def _sb_scores(k_ref, q_heads, s_ref, slot, j, tk):
    kb = k_ref[0, pl.ds(pl.multiple_of(j * tk, tk), tk), :]
    for h in range(2):
        s_ref[slot, h] = lax.dot_general(kb, q_heads[h], (((1,), (1,)), ((), ())),
                                         preferred_element_type=jnp.float32)


def _sb_weights(zt, c8, ok, groups):
    tq = zt.shape[1]
    r = 0.5 - 0.5 * jnp.tanh(zt)
    if ok is not None:
        r = jnp.where(ok, r, 1.0)
    rg = [r[g * SUBLANES:(g + 1) * SUBLANES, :] for g in range(groups)]
    tot = rg[0]
    for g in range(1, groups):
        tot = tot * rg[g]
    p = c8 * _suffix_excl_prod8(tot)
    pieces = [None] * groups
    for g in range(groups - 1, -1, -1):
        pn = p * rg[g]
        pieces[g] = p - pn
        p = pn
    a = jnp.concatenate(pieces, axis=0).astype(jnp.bfloat16)
    return a, jnp.broadcast_to(p[0:1, :], (SUBLANES, tq))


def _sb_attn_kernel(q_ref, k_ref, v_ref, o_ref, acc_ref, c_ref, s_ref, *, tq, tk):
    i = pl.program_id(2)
    groups = tk // SUBLANES
    q2 = q_ref[0]
    lane = lax.broadcasted_iota(jnp.int32, q2.shape, 1)
    zero = jnp.zeros_like(q2)
    q_heads = (jnp.where(lane < SB_HEAD_DIM, q2, zero), jnp.where(lane < SB_HEAD_DIM, zero, q2))

    def step(j, slot, masked):
        _sb_scores(k_ref, q_heads, s_ref, 1 - slot, jnp.maximum(j - 1, 0), tk)
        if masked:
            row = lax.broadcasted_iota(jnp.int32, (tk, tq), 0)
            col = lax.broadcasted_iota(jnp.int32, (tk, tq), 1)
            ok = (row % SUBLANES) * groups + row // SUBLANES < col
        else:
            ok = None
        off = pl.multiple_of(j * tk, tk)
        ws = []
        for h in range(2):
            a, c_new = _sb_weights(s_ref[slot, h], c_ref[h], ok, groups)
            c_ref[h] = c_new
            ws.append(a)
        for h in range(2):
            vt_h = v_ref[0, h * SB_HEAD_DIM:(h + 1) * SB_HEAD_DIM, pl.ds(off, tk)]
            acc_ref[h] += jnp.dot(vt_h, ws[h], preferred_element_type=jnp.float32)

    acc_ref[...] = jnp.zeros_like(acc_ref)
    c_ref[...] = jnp.ones_like(c_ref)
    _sb_scores(k_ref, q_heads, s_ref, 0, i, tk)
    step(i, 0, True)

    def pair(m, carry):
        j = i - 1 - 2 * m
        step(j, 1, False)
        step(j - 1, 0, False)
        return carry

    lax.fori_loop(0, i // 2, pair, 0)

    @pl.when(i % 2 == 1)
    def _():
        step(0, 1, False)

    ot = jnp.concatenate([acc_ref[0], acc_ref[1]], axis=0)
    o_ref[0] = ot.T.astype(jnp.bfloat16)


def _sb_attn(main, k_perm, vt_perm, tq, tk):
    bsz, seq, _ = main.shape
    assert tq == tk
    kern = functools.partial(_sb_attn_kernel, tq=tq, tk=tk)
    return pl.pallas_call(
        kern,
        grid=(bsz, SB_WIDTH // LANES, seq // tq),
        in_specs=[
            pl.BlockSpec((1, tq, LANES), lambda b, p, i: (b, i, p)),
            pl.BlockSpec((1, seq, LANES), lambda b, p, i: (b, 0, p)),
            pl.BlockSpec((1, LANES, seq), lambda b, p, i: (b, p, 0)),
        ],
        out_specs=pl.BlockSpec((1, tq, LANES), lambda b, p, i: (b, i, p)),
        out_shape=jax.ShapeDtypeStruct((bsz, seq, SB_WIDTH), jnp.bfloat16),
        scratch_shapes=[
            pltpu.VMEM((2, SB_HEAD_DIM, tq), jnp.float32),
            pltpu.VMEM((2, SUBLANES, tq), jnp.float32),
            pltpu.VMEM((2, 2, tk, tq), jnp.float32),
        ],
        compiler_params=_cparams(("arbitrary", "arbitrary", "arbitrary")),
        name="sb_attn",
    )(main, k_perm, vt_perm)


def _diff_attn_kernel(slopes_ref, q_ref, k_ref, v_ref, lam_ref, g_ref, o_ref,
                      acc_ref, m_ref, l_ref, s_ref, *, tq, tk):
    hd = pl.program_id(1)
    i = pl.program_id(2)
    slope = slopes_ref[hd]
    q2 = q_ref[0]
    lane = lax.broadcasted_iota(jnp.int32, q2.shape, 1)
    zero = jnp.zeros_like(q2)
    q_maps = (jnp.where(lane < DIFF_HEAD_DIM, q2, zero), jnp.where(lane < DIFF_HEAD_DIM, zero, q2))

    row = lax.broadcasted_iota(jnp.int32, (tk, tq), 0)
    col = lax.broadcasted_iota(jnp.int32, (tk, tq), 1)
    bias = slope * (row - col).astype(jnp.float32)

    def scores(slot, j):
        kb = k_ref[0, pl.ds(pl.multiple_of(j * tk, tk), tk), :]
        for m in range(2):
            s_ref[slot, m] = lax.dot_general(kb, q_maps[m], (((1,), (1,)), ((), ())),
                                             preferred_element_type=jnp.float32)

    def step(j, slot, masked):
        scores(1 - slot, jnp.maximum(j - 1, 0))
        off = pl.multiple_of(j * tk, tk)
        vtb = v_ref[0, :, pl.ds(off, tk)]
        cb = slope * ((j - i) * tk).astype(jnp.float32)
        ps, alphas = [], []
        for m in range(2):
            s = s_ref[slot, m] + bias
            if masked:
                s = jnp.where(row <= col, s, NEG_BIG)
            m_old = m_ref[m]
            m_new = jnp.maximum(m_old, jnp.max(s, axis=0, keepdims=True) + cb)
            alpha = jnp.exp(m_old - m_new)
            p = jnp.exp(s - (m_new - cb))
            l_ref[m] = alpha * l_ref[m] + jnp.sum(p, axis=0, keepdims=True)
            m_ref[m] = m_new
            ps.append(p.astype(jnp.bfloat16))
            alphas.append(alpha)
        for m in range(2):
            acc_ref[m] = alphas[m] * acc_ref[m] + jnp.dot(
                vtb, ps[m], preferred_element_type=jnp.float32)

    acc_ref[...] = jnp.zeros_like(acc_ref)
    m_ref[...] = jnp.full_like(m_ref, NEG_BIG)
    l_ref[...] = jnp.zeros_like(l_ref)
    scores(0, i)
    step(i, 0, True)

    def pair(n, carry):
        j = i - 1 - 2 * n
        step(j, 1, False)
        step(j - 1, 0, False)
        return carry

    lax.fori_loop(0, i // 2, pair, 0)

    @pl.when(i % 2 == 1)
    def _():
        step(0, 1, False)

    lam = lam_ref[0:1, 0:1]
    o = acc_ref[0] / l_ref[0] - lam * (acc_ref[1] / l_ref[1])
    ms = jnp.mean(o * o, axis=0, keepdims=True)
    y = o * lax.rsqrt(ms + RMS_EPS) * g_ref[...] * (1.0 - LAM_INIT)
    o_ref[0] = y.T.astype(jnp.bfloat16)


def _diff_attn(main, vt, slopes, lam, g_col, tq, tk):
    bsz, seq, _ = main.shape
    assert tq == tk
    kern = functools.partial(_diff_attn_kernel, tq=tq, tk=tk)
    vrow0 = SB_WIDTH // LANES
    return pl.pallas_call(
        kern,
        grid=(bsz, DIFF_HEADS, seq // tq),
        in_specs=[
            pl.BlockSpec(memory_space=pltpu.SMEM),
            pl.BlockSpec((1, tq, LANES), lambda b, h, i: (b, i, COLBLK_Q_DF + h)),
            pl.BlockSpec((1, seq, LANES), lambda b, h, i: (b, 0, COLBLK_K_DF + h)),
            pl.BlockSpec((1, DIFF_V_DIM, seq), lambda b, h, i: (b, vrow0 + h, 0)),
            pl.BlockSpec((SUBLANES, LANES), lambda b, h, i: (0, 0)),
            pl.BlockSpec((DIFF_V_DIM, 1), lambda b, h, i: (0, 0)),
        ],
        out_specs=pl.BlockSpec((1, tq, DIFF_V_DIM), lambda b, h, i: (b, i, h)),
        out_shape=jax.ShapeDtypeStruct((bsz, seq, DIFF_V_WIDTH), jnp.bfloat16),
        scratch_shapes=[
            pltpu.VMEM((2, DIFF_V_DIM, tq), jnp.float32),
            pltpu.VMEM((2, 1, tq), jnp.float32),
            pltpu.VMEM((2, 1, tq), jnp.float32),
            pltpu.VMEM((2, 2, tk, tq), jnp.float32),
        ],
        compiler_params=_cparams(("arbitrary", "arbitrary", "arbitrary")),
        name="diff_attn",
    )(slopes, main, main, vt, lam, g_col)


def _pack_bf16_pair(a, b):
    ab = pltpu.bitcast(a.astype(jnp.bfloat16).astype(jnp.float32), jnp.uint32)
    bb = pltpu.bitcast(b.astype(jnp.bfloat16).astype(jnp.float32), jnp.uint32)
    return ab | (bb >> 16)


def _unpack_bf16_pair(w):
    hi = pltpu.bitcast(w & jnp.uint32(0xFFFF0000), jnp.float32)
    lo = pltpu.bitcast(w << 16, jnp.float32)
    return jnp.concatenate([hi, lo], axis=1)


def _merge_router_kernel(ysb_ref, ydf_ref, gates_ref, x_ref, mod_ref, wsb_ref, wdf_ref, wout_ref,
                         gpost_ref, gpre_ref, wrh_ref, wrl_ref, br_ref,
                         x1_ref, h2_ref, idx_ref, wgt_ref, rank_ref, cnt_ref, base_ref):
    first = jnp.logical_and(pl.program_id(0) == 0, pl.program_id(1) == 0)

    @pl.when(first)
    def _():
        base_ref[...] = jnp.zeros_like(base_ref)

    mod = mod_ref[0]
    a = jnp.dot(ysb_ref[0], wsb_ref[...], preferred_element_type=jnp.float32)
    b = jnp.dot(ydf_ref[0], wdf_ref[...], preferred_element_type=jnp.float32)
    g = gates_ref[0].astype(jnp.float32)
    merged = g[:, :D_MODEL] * a + g[:, D_MODEL:] * b
    mix = jnp.dot(merged.astype(jnp.bfloat16), wout_ref[...], preferred_element_type=jnp.float32)
    x1 = x_ref[0] + mod[2:3] * (_rms(mix) * gpost_ref[...])
    x1_ref[0] = x1
    h2 = _rms(x1) * gpre_ref[...]
    h2 = h2 * (1.0 + mod[4:5]) + mod[3:4]
    half = D_MODEL // 2
    h2_ref[0] = _pack_bf16_pair(h2[:, :half], h2[:, half:])

    hh = h2.astype(jnp.bfloat16)
    hl = (h2 - hh.astype(jnp.float32)).astype(jnp.bfloat16)
    logits = (jnp.dot(hh, wrh_ref[...], preferred_element_type=jnp.float32)
              + jnp.dot(hh, wrl_ref[...], preferred_element_type=jnp.float32)
              + jnp.dot(hl, wrh_ref[...], preferred_element_type=jnp.float32)
              + br_ref[...])
    lane = lax.broadcasted_iota(jnp.int32, logits.shape, 1)
    lanef = lane.astype(jnp.float32)
    vals, idxs = [], []
    cur = logits
    for _ in range(TOP_K):
        mx = jnp.max(cur, axis=-1, keepdims=True)
        ix = jnp.min(jnp.where(cur == mx, lanef, float(LANES)), axis=-1, keepdims=True)
        cur = jnp.where(lanef == ix, -jnp.inf, cur)
        vals.append(mx)
        idxs.append(ix)
    es = [jnp.exp(v - vals[0]) for v in vals]
    den = es[0] + es[1] + es[2] + es[3]
    oi = jnp.zeros(logits.shape, jnp.float32)
    ow = jnp.zeros(logits.shape, jnp.float32)
    for k in range(TOP_K):
        oi = jnp.where(lane == k, idxs[k], oi)
        ow = jnp.where(lane == k, es[k] / den, ow)
    idx_ref[0] = oi.astype(jnp.int32)
    wgt_ref[0] = ow

    ts = logits.shape[0]
    member = jnp.zeros(logits.shape, jnp.float32)
    for k in range(TOP_K):
        member = member + (lanef == idxs[k]).astype(jnp.float32)
    rr = lax.broadcasted_iota(jnp.int32, (ts, ts), 0)
    cc = lax.broadcasted_iota(jnp.int32, (ts, ts), 1)
    lower = jnp.where(cc < rr, 1.0, 0.0).astype(jnp.bfloat16)
    before = jnp.dot(lower, member.astype(jnp.bfloat16), preferred_element_type=jnp.float32)
    base = base_ref[0:1, :]
    rank_all = before + base
    orank = jnp.zeros(logits.shape, jnp.float32)
    for k in range(TOP_K):
        rk = jnp.sum(jnp.where(lanef == idxs[k], rank_all, 0.0), axis=-1, keepdims=True)
        orank = jnp.where(lane == k, rk, orank)
    rank_ref[0] = orank.astype(jnp.int32)
    new_base = base + jnp.sum(member, axis=0, keepdims=True)
    base_ref[...] = jnp.broadcast_to(new_base, base_ref.shape)
    cnt_ref[...] = jnp.broadcast_to(new_base, cnt_ref.shape).astype(jnp.int32)


def _merge_router(ysb, ydf, main, x, mod3, wsb, wdf, wout, gpost, gpre, wrh, wrl, br, ts):
    bsz, seq, _ = x.shape
    const = lambda b, i: (0, 0)
    return pl.pallas_call(
        _merge_router_kernel,
        grid=(bsz, seq // ts),
        in_specs=[
            pl.BlockSpec((1, ts, SB_WIDTH), lambda b, i: (b, i, 0)),
            pl.BlockSpec((1, ts, DIFF_V_WIDTH), lambda b, i: (b, i, 0)),
            pl.BlockSpec((1, ts, 2 * D_MODEL), lambda b, i: (b, i, GATE_COL0 // (2 * D_MODEL))),
            pl.BlockSpec((1, ts, D_MODEL), lambda b, i: (b, i, 0)),
            pl.BlockSpec((1, N_MOD, D_MODEL), lambda b, i: (b, 0, 0)),
            pl.BlockSpec((SB_WIDTH, D_MODEL), const),
            pl.BlockSpec((DIFF_V_WIDTH, D_MODEL), const),
            pl.BlockSpec((D_MODEL, D_MODEL), const),
            pl.BlockSpec((1, D_MODEL), const),
            pl.BlockSpec((1, D_MODEL), const),
            pl.BlockSpec((D_MODEL, LANES), const),
            pl.BlockSpec((D_MODEL, LANES), const),
            pl.BlockSpec((1, LANES), const),
        ],
        out_specs=[
            pl.BlockSpec((1, ts, D_MODEL), lambda b, i: (b, i, 0)),
            pl.BlockSpec((1, ts, D_MODEL // 2), lambda b, i: (b, i, 0)),
            pl.BlockSpec((1, ts, LANES), lambda b, i: (b, i, 0)),
            pl.BlockSpec((1, ts, LANES), lambda b, i: (b, i, 0)),
            pl.BlockSpec((1, ts, LANES), lambda b, i: (b, i, 0)),
            pl.BlockSpec((SUBLANES, LANES), const),
        ],
        out_shape=[
            jax.ShapeDtypeStruct((bsz, seq, D_MODEL), jnp.float32),
            jax.ShapeDtypeStruct((bsz, seq, D_MODEL // 2), jnp.uint32),
            jax.ShapeDtypeStruct((bsz, seq, LANES), jnp.int32),
            jax.ShapeDtypeStruct((bsz, seq, LANES), jnp.float32),
            jax.ShapeDtypeStruct((bsz, seq, LANES), jnp.int32),
            jax.ShapeDtypeStruct((SUBLANES, LANES), jnp.int32),
        ],
        scratch_shapes=[pltpu.VMEM((SUBLANES, LANES), jnp.float32)],
        compiler_params=_cparams(("arbitrary", "arbitrary")),
        name="merge_router",
    )(ysb, ydf, main, x, mod3, wsb, wdf, wout, gpost, gpre, wrh, wrl, br)


def _sc_gather_rows(table, idx):
    n = idx.shape[0]
    width = table.shape[1]
    n_workers = SC_CORES * SC_SUBCORES
    per_worker = n // n_workers
    n_chunks = per_worker // SC_GATHER_ROWS
    assert n_chunks * SC_GATHER_ROWS * n_workers == n
    mesh = plsc.VectorSubcoreMesh(core_axis_name="c", subcore_axis_name="s",
                                  num_cores=SC_CORES, num_subcores=SC_SUBCORES)

    def body(table_hbm, idx_hbm, out_hbm, idx_v, rows_v, sem):
        wid = lax.axis_index("s") * SC_CORES + lax.axis_index("c")
        base = wid * per_worker

        @pl.loop(0, n_chunks)
        def _(ci):
            off = pl.multiple_of(base + ci * SC_GATHER_ROWS, SC_GATHER_ROWS)
            pltpu.sync_copy(idx_hbm.at[pl.ds(off, SC_GATHER_ROWS)], idx_v)
            pltpu.async_copy(table_hbm.at[idx_v], rows_v, sem).wait()
            pltpu.sync_copy(rows_v, out_hbm.at[pl.ds(off, SC_GATHER_ROWS)])

    return pl.kernel(
        body,
        out_type=jax.ShapeDtypeStruct((n, width), table.dtype),
        mesh=mesh,
        scratch_types=[
            pltpu.VMEM((SC_GATHER_ROWS,), jnp.int32),
            pltpu.VMEM((SC_GATHER_ROWS, width), table.dtype),
            pltpu.SemaphoreType.DMA,
        ],
        name="sc_gather_rows",
    )(table, idx)


def _moe_ffn_kernel(te_ref, nt_ref, x_ref, wgu_ref, bgu_ref, wd_ref, bd_ref, o_ref):
    i = pl.program_id(0)
    n_valid = nt_ref[0]

    @pl.when(i < n_valid)
    def _():
        xb = _unpack_bf16_pair(x_ref[...]).astype(jnp.bfloat16)
        gu = jnp.dot(xb, wgu_ref[0].astype(jnp.bfloat16),
                     preferred_element_type=jnp.float32) + bgu_ref[0]
        gate = jnp.minimum(gu[:, :D_EXPERT], SWIGLU_LIMIT)
        up = jnp.clip(gu[:, D_EXPERT:], -SWIGLU_LIMIT, SWIGLU_LIMIT)
        act = (up + 1.0) * (gate * jax.nn.sigmoid(SWIGLU_ALPHA * gate))
        out = jnp.dot(act.astype(jnp.bfloat16), wd_ref[0].astype(jnp.bfloat16),
                      preferred_element_type=jnp.float32) + bd_ref[0]
        half = D_MODEL // 2
        o_ref[...] = _pack_bf16_pair(out[:, :half], out[:, half:])

    @pl.when(i >= n_valid)
    def _():
        o_ref[...] = jnp.zeros_like(o_ref)


def _moe_ffn(tile_expert, n_valid, xg, wgu, bgu, wd, bd, tm):
    n_tiles = xg.shape[0] // tm
    grid_spec = pltpu.PrefetchScalarGridSpec(
        num_scalar_prefetch=2,
        grid=(n_tiles,),
        in_specs=[
            pl.BlockSpec((tm, D_MODEL // 2), lambda i, te, nt: (jnp.minimum(i, nt[0] - 1), 0)),
            pl.BlockSpec((1, D_MODEL, 2 * D_EXPERT), lambda i, te, nt: (te[i], 0, 0)),
            pl.BlockSpec((1, 1, 2 * D_EXPERT), lambda i, te, nt: (te[i], 0, 0)),
            pl.BlockSpec((1, D_EXPERT, D_MODEL), lambda i, te, nt: (te[i], 0, 0)),
            pl.BlockSpec((1, 1, D_MODEL), lambda i, te, nt: (te[i], 0, 0)),
        ],
        out_specs=pl.BlockSpec((tm, D_MODEL // 2), lambda i, te, nt: (i, 0)),
    )
    return pl.pallas_call(
        _moe_ffn_kernel,
        grid_spec=grid_spec,
        out_shape=jax.ShapeDtypeStruct((n_tiles * tm, D_MODEL // 2), jnp.uint32),
        compiler_params=_cparams(("arbitrary",)),
        name="moe_ffn",
    )(tile_expert, n_valid, xg, wgu, bgu.reshape(N_EXPERTS, 1, -1), wd,
      bd.reshape(N_EXPERTS, 1, -1))


def _moe_combine_kernel(rows_ref, wgt_ref, x1_ref, mod_ref, g_ref, o_ref):
    half = D_MODEL // 2
    w = wgt_ref[0]
    y = jnp.zeros(x1_ref.shape[1:], jnp.float32)
    for k in range(TOP_K):
        y = y + w[:, k:k + 1] * _unpack_bf16_pair(rows_ref[0, :, k * half:(k + 1) * half])
    mod = mod_ref[0]
    o_ref[0] = x1_ref[0] + mod[5:6] * (_rms(y) * g_ref[...])


def _moe_combine(rows, wgt, x1, mod3, g_post, ts):
    bsz, seq, _ = x1.shape
    return pl.pallas_call(
        _moe_combine_kernel,
        grid=(bsz, seq // ts),
        in_specs=[
            pl.BlockSpec((1, ts, TOP_K * D_MODEL // 2), lambda b, i: (b, i, 0)),
            pl.BlockSpec((1, ts, LANES), lambda b, i: (b, i, 0)),
            pl.BlockSpec((1, ts, D_MODEL), lambda b, i: (b, i, 0)),
            pl.BlockSpec((1, N_MOD, D_MODEL), lambda b, i: (b, 0, 0)),
            pl.BlockSpec((1, D_MODEL), lambda b, i: (0, 0)),
        ],
        out_specs=pl.BlockSpec((1, ts, D_MODEL), lambda b, i: (b, i, 0)),
        out_shape=jax.ShapeDtypeStruct((bsz, seq, D_MODEL), jnp.float32),
        compiler_params=_cparams(("arbitrary", "arbitrary")),
        name="moe_combine",
    )(rows, wgt, x1, mod3, g_post)


def _perm_keys_rows(a, tk):
    bsz, seq, w = a.shape
    g = tk // SUBLANES
    return a.reshape(bsz, seq // tk, SUBLANES, g, w).swapaxes(2, 3).reshape(bsz, seq, w)


def _perm_keys_cols(a, tk):
    bsz, r, seq = a.shape
    g = tk // SUBLANES
    return a.reshape(bsz, r, seq // tk, SUBLANES, g).swapaxes(3, 4).reshape(bsz, r, seq)


def _routing(top_idx, rank, counts, tm, n_tiles):
    n_tok = top_idx.shape[0]
    padded = ((counts + tm - 1) // tm) * tm
    start = jnp.cumsum(counts) - counts
    pend = jnp.cumsum(padded)
    pstart = pend - padded
    onehot = top_idx[:, :, None] == jnp.arange(N_EXPERTS, dtype=jnp.int32)[None, None, :]
    pos = rank + jnp.sum(jnp.where(onehot, pstart[None, None, :], 0), axis=-1)
    pair_id = jnp.arange(n_tok * TOP_K, dtype=jnp.int32)
    _, order = lax.sort_key_val(pos.reshape(-1), pair_id)
    n_valid = (pend[-1] // tm).astype(jnp.int32)
    tile_row0 = jnp.arange(n_tiles, dtype=jnp.int32) * tm
    tile_expert = jnp.minimum(
        jnp.sum((tile_row0[:, None] >= pend[None, :]).astype(jnp.int32), axis=1), N_EXPERTS - 1)
    te_oh = tile_expert[:, None] == jnp.arange(N_EXPERTS, dtype=jnp.int32)[None, :]
    t_pstart = jnp.sum(jnp.where(te_oh, pstart[None, :], 0), axis=1)
    t_start = jnp.sum(jnp.where(te_oh, start[None, :], 0), axis=1)
    t_count = jnp.sum(jnp.where(te_oh, counts[None, :], 0), axis=1)
    off = (tile_row0 - t_pstart)[:, None] + jnp.arange(tm, dtype=jnp.int32)[None, :]
    valid = off < t_count[:, None]
    src = jnp.clip(t_start[:, None] + off, 0, n_tok * TOP_K - 1)
    row_token = jnp.where(valid, order[src] // TOP_K, 0).astype(jnp.int32)
    tile_expert = tile_expert[jnp.minimum(jnp.arange(n_tiles), jnp.maximum(n_valid - 1, 0))]
    return (pos.astype(jnp.int32), row_token.reshape(n_tiles * tm), tile_expert.astype(jnp.int32),
            n_valid.reshape(1))


def _alibi_slopes(n_heads):
    return 2.0 ** (-8.0 * jnp.arange(1, n_heads + 1, dtype=jnp.float32) / n_heads)


def _layer(x, c, w_mod, b_mod, g_pre_mix, g_post_mix, w_in, lamv, g_subln, w_branch_sb,
           w_branch_diff, w_out, g_pre_ffn, g_post_ffn, w_router, b_router, w_gate_up,
           b_gate_up, w_down, b_down, *, ts_in, tq, ts_merge, tm, ts_comb):
    bsz, seq, d = x.shape
    n_tok = bsz * seq
    bf = jnp.bfloat16

    mod, lam = _mod_proj(c, w_mod, b_mod, lamv)
    mod3 = mod.reshape(bsz, N_MOD, d)

    o_vsb = 2 * SB_WIDTH
    o_qdf = 3 * SB_WIDTH
    o_vdf = o_qdf + 2 * DIFF_QK_WIDTH
    o_g = o_vdf + DIFF_V_WIDTH
    w_main = jnp.concatenate([w_in[:, :o_vsb], w_in[:, o_qdf:o_vdf], w_in[:, o_g:]], axis=1).astype(bf)
    w_vt = jnp.concatenate([w_in[:, o_vsb:o_qdf], w_in[:, o_vdf:o_g]], axis=1).T.astype(bf)

    main, vt = _in_proj(x, mod3, g_pre_mix.reshape(1, d), w_main, w_vt, ts_in)

    k_perm = _perm_keys_rows(main[:, :, SB_WIDTH:2 * SB_WIDTH], tq)
    vt_perm = _perm_keys_cols(vt[:, :SB_WIDTH, :], tq)
    y_sb = _sb_attn(main, k_perm, vt_perm, tq, tq)
    y_df = _diff_attn(main, vt, _alibi_slopes(DIFF_HEADS), lam,
                      g_subln.reshape(DIFF_V_DIM, 1), tq, tq)

    wr = jnp.zeros((d, LANES), jnp.float32).at[:, :N_EXPERTS].set(w_router)
    wrh = wr.astype(bf)
    wrl = (wr - wrh.astype(jnp.float32)).astype(bf)
    br = jnp.full((1, LANES), NEG_BIG, jnp.float32).at[0, :N_EXPERTS].set(b_router)
    x1, h2p, top_idx, top_w, rank, counts = _merge_router(
        y_sb, y_df, main, x, mod3, w_branch_sb.astype(bf), w_branch_diff.astype(bf),
        w_out.astype(bf), g_post_mix.reshape(1, d), g_pre_ffn.reshape(1, d), wrh, wrl, br, ts_merge)

    n_tiles = (n_tok * TOP_K) // tm + N_EXPERTS
    pos, row_token, tile_expert, n_valid = _routing(
        top_idx.reshape(n_tok, LANES)[:, :TOP_K], rank.reshape(n_tok, LANES)[:, :TOP_K],
        counts[0, :N_EXPERTS], tm, n_tiles)
    xg = _sc_gather_rows(h2p.reshape(n_tok, d // 2), row_token)
    rows = _moe_ffn(tile_expert, n_valid, xg, w_gate_up, b_gate_up, w_down, b_down, tm)
    tok_rows = _sc_gather_rows(rows, pos.reshape(n_tok * TOP_K)).reshape(bsz, seq, TOP_K * d // 2)
    return _moe_combine(tok_rows, top_w, x1, mod3, g_post_ffn.reshape(1, d), ts_comb)


def kernel(x, c, w_mod, b_mod, g_pre_mix, g_post_mix, w_in, lambda_q1, lambda_k1, lambda_q2,
           lambda_k2, g_subln, w_branch_sb, w_branch_diff, w_out, g_pre_ffn, g_post_ffn,
           w_router, b_router, w_gate_up, b_gate_up, w_down, b_down):
    depth = w_mod.shape[0]
    for l in range(depth):
        lamv = jnp.stack([lambda_q1[l], lambda_k1[l], lambda_q2[l], lambda_k2[l]])
        x = _layer(x, c, w_mod[l], b_mod[l], g_pre_mix[l], g_post_mix[l], w_in[l], lamv,
                   g_subln[l], w_branch_sb[l], w_branch_diff[l], w_out[l], g_pre_ffn[l],
                   g_post_ffn[l], w_router[l], b_router[l], w_gate_up[l], b_gate_up[l],
                   w_down[l], b_down[l],
                   ts_in=512, tq=256, ts_merge=256, tm=512, ts_comb=256)
    return x
```

```python
import functools
import math

import jax
import jax.numpy as jnp
from jax import lax
from jax.experimental import pallas as pl
from jax.experimental.pallas import tpu as pltpu
from jax.experimental.pallas import tpu_sc as plsc

D_MODEL = 1024
SB_HEADS = 8
SB_HEAD_DIM = 64
SB_WIDTH = SB_HEADS * SB_HEAD_DIM
DIFF_HEADS = 4
DIFF_HEAD_DIM = 64
DIFF_V_DIM = 2 * DIFF_HEAD_DIM
DIFF_QK_WIDTH = DIFF_HEADS * 2 * DIFF_HEAD_DIM
DIFF_V_WIDTH = DIFF_HEADS * DIFF_V_DIM
N_EXPERTS = 32
TOP_K = 4
D_EXPERT = D_MODEL
SWIGLU_LIMIT = 7.0
SWIGLU_ALPHA = 1.702
RMS_EPS = 1e-6
N_MOD = 6
LAM_INIT = 0.8 - 0.6 * math.exp(-0.3 * 0)

LANES = 128
SUBLANES = 8
NEG_BIG = -1e30
SC_CORES = 2
SC_SUBCORES = 16
SC_GATHER_ROWS = 128
MERGE_SUB = 256

MAIN_WIDTH = 2 * SB_WIDTH + 2 * DIFF_QK_WIDTH + 2 * D_MODEL
VT_ROWS = SB_WIDTH + DIFF_V_WIDTH
COLBLK_K_SB = SB_WIDTH // LANES
COLBLK_Q_DF = 2 * SB_WIDTH // LANES
COLBLK_K_DF = COLBLK_Q_DF + DIFF_QK_WIDTH // LANES
GATE_COL0 = 2 * SB_WIDTH + 2 * DIFF_QK_WIDTH

VMEM_LIMIT = 56 * 1024 * 1024


def _cparams(sem, vmem=VMEM_LIMIT):
    return pltpu.CompilerParams(dimension_semantics=sem, vmem_limit_bytes=vmem)


def _rms(x):
    return x * lax.rsqrt(jnp.mean(x * x, axis=-1, keepdims=True) + RMS_EPS)


def _mod_kernel(c_ref, w_ref, b_ref, lamv_ref, mod_ref, lam_ref):
    c = c_ref[...]
    ca = c * jax.nn.sigmoid(c)
    mod_ref[...] = jnp.dot(ca, w_ref[...], preferred_element_type=jnp.float32,
                           precision=lax.Precision.HIGHEST) + b_ref[...]
    lv = lamv_ref[...]
    s1 = jnp.sum(lv[0:1] * lv[1:2], axis=-1, keepdims=True)
    s2 = jnp.sum(lv[2:3] * lv[3:4], axis=-1, keepdims=True)
    lam = jnp.exp(s1) - jnp.exp(s2) + LAM_INIT
    lam_ref[...] = jnp.broadcast_to(lam, lam_ref.shape)


def _mod_proj(c, w_mod, b_mod, lamv):
    bsz = c.shape[0]
    tn = 1536
    n = w_mod.shape[1]
    return pl.pallas_call(
        _mod_kernel,
        grid=(n // tn,),
        in_specs=[
            pl.BlockSpec((bsz, D_MODEL), lambda j: (0, 0)),
            pl.BlockSpec((D_MODEL, tn), lambda j: (0, j)),
            pl.BlockSpec((1, tn), lambda j: (0, j)),
            pl.BlockSpec((4, DIFF_HEAD_DIM), lambda j: (0, 0)),
        ],
        out_specs=[
            pl.BlockSpec((bsz, tn), lambda j: (0, j)),
            pl.BlockSpec((SUBLANES, LANES), lambda j: (0, 0)),
        ],
        out_shape=[
            jax.ShapeDtypeStruct((bsz, n), jnp.float32),
            jax.ShapeDtypeStruct((SUBLANES, LANES), jnp.float32),
        ],
        compiler_params=_cparams(("arbitrary",)),
        name="mod_proj",
    )(c, w_mod, b_mod.reshape(1, n), lamv)


IN_CHUNK = 1024


def _in_proj_kernel(x_ref, mod_ref, g_ref, wm_ref, wvt_ref, main_ref, vt_ref):
    x = x_ref[0]
    mod = mod_ref[0]
    h = _rms(x) * g_ref[...]
    h = h * (1.0 + mod[1:2]) + mod[0:1]
    hb = h.astype(jnp.bfloat16)
    for ci in range(MAIN_WIDTH // IN_CHUNK):
        c0 = ci * IN_CHUNK
        p = jnp.dot(hb, wm_ref[:, c0:c0 + IN_CHUNK], preferred_element_type=jnp.float32)
        if c0 < GATE_COL0:
            half = IN_CHUNK // 2
            qscale = 0.0625 if c0 == 0 else 0.125
            main_ref[0, :, c0:c0 + half] = (p[:, :half] * qscale).astype(jnp.bfloat16)
            main_ref[0, :, c0 + half:c0 + IN_CHUNK] = p[:, half:].astype(jnp.bfloat16)
        else:
            main_ref[0, :, c0:c0 + IN_CHUNK] = jax.nn.sigmoid(p).astype(jnp.bfloat16)
    vt = lax.dot_general(wvt_ref[...], hb, (((1,), (1,)), ((), ())),
                         preferred_element_type=jnp.float32)
    vt_ref[0] = vt.astype(jnp.bfloat16)


def _in_proj(x, mod3, g_pre, w_main, w_vt, ts):
    bsz, seq, _ = x.shape
    return pl.pallas_call(
        _in_proj_kernel,
        grid=(bsz, seq // ts),
        in_specs=[
            pl.BlockSpec((1, ts, D_MODEL), lambda b, i: (b, i, 0)),
            pl.BlockSpec((1, N_MOD, D_MODEL), lambda b, i: (b, 0, 0)),
            pl.BlockSpec((1, D_MODEL), lambda b, i: (0, 0)),
            pl.BlockSpec((D_MODEL, MAIN_WIDTH), lambda b, i: (0, 0)),
            pl.BlockSpec((VT_ROWS, D_MODEL), lambda b, i: (0, 0)),
        ],
        out_specs=[
            pl.BlockSpec((1, ts, MAIN_WIDTH), lambda b, i: (b, i, 0)),
            pl.BlockSpec((1, VT_ROWS, ts), lambda b, i: (b, 0, i)),
        ],
        out_shape=[
            jax.ShapeDtypeStruct((bsz, seq, MAIN_WIDTH), jnp.bfloat16),
            jax.ShapeDtypeStruct((bsz, VT_ROWS, seq), jnp.bfloat16),
        ],
        compiler_params=_cparams(("arbitrary", "arbitrary")),
        name="in_proj",
    )(x, mod3, g_pre, w_main, w_vt)


def _suffix_excl_prod8(tot):
    sub = lax.broadcasted_iota(jnp.int32, tot.shape, 0)
    x = jnp.where(sub < SUBLANES - 1, pltpu.roll(tot, SUBLANES - 1, 0), 1.0)
    for sh in (1, 2, 4):
        x = x * jnp.where(sub + sh < SUBLANES, pltpu.roll(x, SUBLANES - sh, 0), 1.0)
    return x


def _sb_scores(k_ref, q_heads, s_ref, slot, j, tk):
    kb = k_ref[0, pl.ds(pl.multiple_of(j * tk, tk), tk), :]
    for h in range(2):
        s_ref[slot, h] = lax.dot_general(kb, q_heads[h], (((1,), (1,)), ((), ())),
                                         preferred_element_type=jnp.float32)


def _sb_weights(zt, c8, ok, groups):
    tq = zt.shape[1]
    r = 0.5 - 0.5 * jnp.tanh(zt)
    if ok is not None:
        r = jnp.where(ok, r, 1.0)
    rg = [r[g * SUBLANES:(g + 1) * SUBLANES, :] for g in range(groups)]
    tot = rg[0]
    for g in range(1, groups):
        tot = tot * rg[g]
    p = c8 * _suffix_excl_prod8(tot)
    pieces = [None] * groups
    for g in range(groups - 1, -1, -1):
        pn = p * rg[g]
        pieces[g] = p - pn
        p = pn
    a = jnp.concatenate(pieces, axis=0).astype(jnp.bfloat16)
    return a, jnp.broadcast_to(p[0:1, :], (SUBLANES, tq))


def _sb_attn_kernel(q_ref, k_ref, v_ref, o_ref, acc_ref, c_ref, s_ref, *, tq, tk):
    i = pl.program_id(2)
    groups = tk // SUBLANES
    q2 = q_ref[0]
    lane = lax.broadcasted_iota(jnp.int32, q2.shape, 1)
    zero = jnp.zeros_like(q2)
    q_heads = (jnp.where(lane < SB_HEAD_DIM, q2, zero), jnp.where(lane < SB_HEAD_DIM, zero, q2))

    def step(j, slot, masked):
        _sb_scores(k_ref, q_heads, s_ref, 1 - slot, jnp.maximum(j - 1, 0), tk)
        if masked:
            row = lax.broadcasted_iota(jnp.int32, (tk, tq), 0)
            col = lax.broadcasted_iota(jnp.int32, (tk, tq), 1)
            ok = (row % SUBLANES) * groups + row // SUBLANES < col
        else:
            ok = None
        off = pl.multiple_of(j * tk, tk)
        ws = []
        for h in range(2):
            a, c_new = _sb_weights(s_ref[slot, h], c_ref[h], ok, groups)
            c_ref[h] = c_new
            ws.append(a)
        for h in range(2):
            vt_h = v_ref[0, h * SB_HEAD_DIM:(h + 1) * SB_HEAD_DIM, pl.ds(off, tk)]
            acc_ref[h] += jnp.dot(vt_h, ws[h], preferred_element_type=jnp.float32)

    acc_ref[...] = jnp.zeros_like(acc_ref)
    c_ref[...] = jnp.ones_like(c_ref)
    _sb_scores(k_ref, q_heads, s_ref, 0, i, tk)
    step(i, 0, True)

    def pair(m, carry):
        j = i - 1 - 2 * m
        step(j, 1, False)
        step(j - 1, 0, False)
        return carry

    lax.fori_loop(0, i // 2, pair, 0)

    @pl.when(i % 2 == 1)
    def _():
        step(0, 1, False)

    ot = jnp.concatenate([acc_ref[0], acc_ref[1]], axis=0)
    o_ref[0] = ot.T.astype(jnp.bfloat16)


def _sb_attn(main, k_perm, vt_perm, tq, tk):
    bsz, seq, _ = main.shape
    assert tq == tk
    kern = functools.partial(_sb_attn_kernel, tq=tq, tk=tk)
    return pl.pallas_call(
        kern,
        grid=(bsz, SB_WIDTH // LANES, seq // tq),
        in_specs=[
            pl.BlockSpec((1, tq, LANES), lambda b, p, i: (b, i, p)),
            pl.BlockSpec((1, seq, LANES), lambda b, p, i: (b, 0, p)),
            pl.BlockSpec((1, LANES, seq), lambda b, p, i: (b, p, 0)),
        ],
        out_specs=pl.BlockSpec((1, tq, LANES), lambda b, p, i: (b, i, p)),
        out_shape=jax.ShapeDtypeStruct((bsz, seq, SB_WIDTH), jnp.bfloat16),
        scratch_shapes=[
            pltpu.VMEM((2, SB_HEAD_DIM, tq), jnp.float32),
            pltpu.VMEM((2, SUBLANES, tq), jnp.float32),
            pltpu.VMEM((2, 2, tk, tq), jnp.float32),
        ],
        compiler_params=_cparams(("arbitrary", "arbitrary", "arbitrary")),
        name="sb_attn",
    )(main, k_perm, vt_perm)


def _diff_attn_kernel(slopes_ref, q_ref, k_ref, v_ref, lam_ref, g_ref, o_ref,
                      acc_ref, m_ref, l_ref, s_ref, *, tq, tk):
    hd = pl.program_id(1)
    i = pl.program_id(2)
    slope = slopes_ref[hd]
    q2 = q_ref[0]
    lane = lax.broadcasted_iota(jnp.int32, q2.shape, 1)
    zero = jnp.zeros_like(q2)
    q_maps = (jnp.where(lane < DIFF_HEAD_DIM, q2, zero), jnp.where(lane < DIFF_HEAD_DIM, zero, q2))

    row = lax.broadcasted_iota(jnp.int32, (tk, tq), 0)
    col = lax.broadcasted_iota(jnp.int32, (tk, tq), 1)
    bias = slope * (row - col).astype(jnp.float32)

    def scores(slot, j):
        kb = k_ref[0, pl.ds(pl.multiple_of(j * tk, tk), tk), :]
        for m in range(2):
            s_ref[slot, m] = lax.dot_general(kb, q_maps[m], (((1,), (1,)), ((), ())),
                                             preferred_element_type=jnp.float32)

    def step(j, slot, masked):
        scores(1 - slot, jnp.maximum(j - 1, 0))
        off = pl.multiple_of(j * tk, tk)
        vtb = v_ref[0, :, pl.ds(off, tk)]
        cb = slope * ((j - i) * tk).astype(jnp.float32)
        ps, alphas = [], []
        for m in range(2):
            s = s_ref[slot, m] + bias
            if masked:
                s = jnp.where(row <= col, s, NEG_BIG)
            m_old = m_ref[m]
            m_new = jnp.maximum(m_old, jnp.max(s, axis=0, keepdims=True) + cb)
            alpha = jnp.exp(m_old - m_new)
            p = jnp.exp(s - (m_new - cb))
            l_ref[m] = alpha * l_ref[m] + jnp.sum(p, axis=0, keepdims=True)
            m_ref[m] = m_new
            ps.append(p.astype(jnp.bfloat16))
            alphas.append(alpha)
        for m in range(2):
            acc_ref[m] = alphas[m] * acc_ref[m] + jnp.dot(
                vtb, ps[m], preferred_element_type=jnp.float32)

    acc_ref[...] = jnp.zeros_like(acc_ref)
    m_ref[...] = jnp.full_like(m_ref, NEG_BIG)
    l_ref[...] = jnp.zeros_like(l_ref)
    scores(0, i)
    step(i, 0, True)

    def pair(n, carry):
        j = i - 1 - 2 * n
        step(j, 1, False)
        step(j - 1, 0, False)
        return carry

    lax.fori_loop(0, i // 2, pair, 0)

    @pl.when(i % 2 == 1)
    def _():
        step(0, 1, False)

    lam = lam_ref[0:1, 0:1]
    o = acc_ref[0] / l_ref[0] - lam * (acc_ref[1] / l_ref[1])
    ms = jnp.mean(o * o, axis=0, keepdims=True)
    y = o * lax.rsqrt(ms + RMS_EPS) * g_ref[...] * (1.0 - LAM_INIT)
    o_ref[0] = y.T.astype(jnp.bfloat16)


def _diff_attn(main, vt, slopes, lam, g_col, tq, tk):
    bsz, seq, _ = main.shape
    assert tq == tk
    kern = functools.partial(_diff_attn_kernel, tq=tq, tk=tk)
    vrow0 = SB_WIDTH // LANES
    return pl.pallas_call(
        kern,
        grid=(bsz, DIFF_HEADS, seq // tq),
        in_specs=[
            pl.BlockSpec(memory_space=pltpu.SMEM),
            pl.BlockSpec((1, tq, LANES), lambda b, h, i: (b, i, COLBLK_Q_DF + h)),
            pl.BlockSpec((1, seq, LANES), lambda b, h, i: (b, 0, COLBLK_K_DF + h)),
            pl.BlockSpec((1, DIFF_V_DIM, seq), lambda b, h, i: (b, vrow0 + h, 0)),
            pl.BlockSpec((SUBLANES, LANES), lambda b, h, i: (0, 0)),
            pl.BlockSpec((DIFF_V_DIM, 1), lambda b, h, i: (0, 0)),
        ],
        out_specs=pl.BlockSpec((1, tq, DIFF_V_DIM), lambda b, h, i: (b, i, h)),
        out_shape=jax.ShapeDtypeStruct((bsz, seq, DIFF_V_WIDTH), jnp.bfloat16),
        scratch_shapes=[
            pltpu.VMEM((2, DIFF_V_DIM, tq), jnp.float32),
            pltpu.VMEM((2, 1, tq), jnp.float32),
            pltpu.VMEM((2, 1, tq), jnp.float32),
            pltpu.VMEM((2, 2, tk, tq), jnp.float32),
        ],
        compiler_params=_cparams(("arbitrary", "arbitrary", "arbitrary")),
        name="diff_attn",
    )(slopes, main, main, vt, lam, g_col)


def _pack_bf16_pair(a, b):
    ab = pltpu.bitcast(a.astype(jnp.bfloat16).astype(jnp.float32), jnp.uint32)
    bb = pltpu.bitcast(b.astype(jnp.bfloat16).astype(jnp.float32), jnp.uint32)
    return ab | (bb >> 16)


def _unpack_bf16_pair(w):
    hi = pltpu.bitcast(w & jnp.uint32(0xFFFF0000), jnp.float32)
    lo = pltpu.bitcast(w << 16, jnp.float32)
    return jnp.concatenate([hi, lo], axis=1)


def _merge_router_kernel(ysb_ref, ydf_ref, gates_ref, x_ref, mod_ref, wsb_ref, wdf_ref, wout_ref,
                         gpost_ref, gpre_ref, wrh_ref, wrl_ref, br_ref,
                         x1_ref, h2_ref, idx_ref, wgt_ref, rank_ref, cnt_ref, base_ref):
    first = jnp.logical_and(pl.program_id(0) == 0, pl.program_id(1) == 0)

    @pl.when(first)
    def _():
        base_ref[...] = jnp.zeros_like(base_ref)

    mod = mod_ref[0]
    for sub in range(x_ref.shape[1] // MERGE_SUB):
        rows = slice(sub * MERGE_SUB, (sub + 1) * MERGE_SUB)
        _merge_router_rows(rows, mod, ysb_ref, ydf_ref, gates_ref, x_ref, wsb_ref, wdf_ref, wout_ref,
                           gpost_ref, gpre_ref, wrh_ref, wrl_ref, br_ref,
                           x1_ref, h2_ref, idx_ref, wgt_ref, rank_ref, cnt_ref, base_ref)


def _merge_router_rows(rows, mod, ysb_ref, ydf_ref, gates_ref, x_ref, wsb_ref, wdf_ref, wout_ref,
                       gpost_ref, gpre_ref, wrh_ref, wrl_ref, br_ref,
                       x1_ref, h2_ref, idx_ref, wgt_ref, rank_ref, cnt_ref, base_ref):
    a = jnp.dot(ysb_ref[0, rows, :], wsb_ref[...], preferred_element_type=jnp.float32)
    b = jnp.dot(ydf_ref[0, rows, :], wdf_ref[...], preferred_element_type=jnp.float32)
    g = gates_ref[0, rows, :].astype(jnp.float32)
    merged = g[:, :D_MODEL] * a + g[:, D_MODEL:] * b
    mix = jnp.dot(merged.astype(jnp.bfloat16), wout_ref[...], preferred_element_type=jnp.float32)
    x1 = x_ref[0, rows, :] + mod[2:3] * (_rms(mix) * gpost_ref[...])
    x1_ref[0, rows, :] = x1
    h2 = _rms(x1) * gpre_ref[...]
    h2 = h2 * (1.0 + mod[4:5]) + mod[3:4]
    half = D_MODEL // 2
    h2_ref[0, rows, :] = _pack_bf16_pair(h2[:, :half], h2[:, half:])

    hh = h2.astype(jnp.bfloat16)
    hl = (h2 - hh.astype(jnp.float32)).astype(jnp.bfloat16)
    logits = (jnp.dot(hh, wrh_ref[...], preferred_element_type=jnp.float32)
              + jnp.dot(hh, wrl_ref[...], preferred_element_type=jnp.float32)
              + jnp.dot(hl, wrh_ref[...], preferred_element_type=jnp.float32)
              + br_ref[...])
    lane = lax.broadcasted_iota(jnp.int32, logits.shape, 1)
    lanef = lane.astype(jnp.float32)
    vals, idxs = [], []
    cur = logits
    for _ in range(TOP_K):
        mx = jnp.max(cur, axis=-1, keepdims=True)
        ix = jnp.min(jnp.where(cur == mx, lanef, float(LANES)), axis=-1, keepdims=True)
        cur = jnp.where(lanef == ix, -jnp.inf, cur)
        vals.append(mx)
        idxs.append(ix)
    es = [jnp.exp(v - vals[0]) for v in vals]
    den = es[0] + es[1] + es[2] + es[3]
    oi = jnp.zeros(logits.shape, jnp.float32)
    ow = jnp.zeros(logits.shape, jnp.float32)
    for k in range(TOP_K):
        oi = jnp.where(lane == k, idxs[k], oi)
        ow = jnp.where(lane == k, es[k] / den, ow)
    idx_ref[0, rows, :] = oi.astype(jnp.int32)
    wgt_ref[0, rows, :] = ow

    ts = logits.shape[0]
    member = jnp.zeros(logits.shape, jnp.float32)
    for k in range(TOP_K):
        member = member + (lanef == idxs[k]).astype(jnp.float32)
    rr = lax.broadcasted_iota(jnp.int32, (ts, ts), 0)
    cc = lax.broadcasted_iota(jnp.int32, (ts, ts), 1)
    lower = jnp.where(cc < rr, 1.0, 0.0).astype(jnp.bfloat16)
    before = jnp.dot(lower, member.astype(jnp.bfloat16), preferred_element_type=jnp.float32)
    base = base_ref[0:1, :]
    rank_all = before + base
    orank = jnp.zeros(logits.shape, jnp.float32)
    for k in range(TOP_K):
        rk = jnp.sum(jnp.where(lanef == idxs[k], rank_all, 0.0), axis=-1, keepdims=True)
        orank = jnp.where(lane == k, rk, orank)
    rank_ref[0, rows, :] = orank.astype(jnp.int32)
    new_base = base + jnp.sum(member, axis=0, keepdims=True)
    base_ref[...] = jnp.broadcast_to(new_base, base_ref.shape)
    cnt_ref[...] = jnp.broadcast_to(new_base, cnt_ref.shape).astype(jnp.int32)


def _merge_router(ysb, ydf, main, x, mod3, wsb, wdf, wout, gpost, gpre, wrh, wrl, br, ts):
    bsz, seq, _ = x.shape
    const = lambda b, i: (0, 0)
    return pl.pallas_call(
        _merge_router_kernel,
        grid=(bsz, seq // ts),
        in_specs=[
            pl.BlockSpec((1, ts, SB_WIDTH), lambda b, i: (b, i, 0)),
            pl.BlockSpec((1, ts, DIFF_V_WIDTH), lambda b, i: (b, i, 0)),
            pl.BlockSpec((1, ts, 2 * D_MODEL), lambda b, i: (b, i, GATE_COL0 // (2 * D_MODEL))),
            pl.BlockSpec((1, ts, D_MODEL), lambda b, i: (b, i, 0)),
            pl.BlockSpec((1, N_MOD, D_MODEL), lambda b, i: (b, 0, 0)),
            pl.BlockSpec((SB_WIDTH, D_MODEL), const),
            pl.BlockSpec((DIFF_V_WIDTH, D_MODEL), const),
            pl.BlockSpec((D_MODEL, D_MODEL), const),
            pl.BlockSpec((1, D_MODEL), const),
            pl.BlockSpec((1, D_MODEL), const),
            pl.BlockSpec((D_MODEL, LANES), const),
            pl.BlockSpec((D_MODEL, LANES), const),
            pl.BlockSpec((1, LANES), const),
        ],
        out_specs=[
            pl.BlockSpec((1, ts, D_MODEL), lambda b, i: (b, i, 0)),
            pl.BlockSpec((1, ts, D_MODEL // 2), lambda b, i: (b, i, 0)),
            pl.BlockSpec((1, ts, LANES), lambda b, i: (b, i, 0)),
            pl.BlockSpec((1, ts, LANES), lambda b, i: (b, i, 0)),
            pl.BlockSpec((1, ts, LANES), lambda b, i: (b, i, 0)),
            pl.BlockSpec((SUBLANES, LANES), const),
        ],
        out_shape=[
            jax.ShapeDtypeStruct((bsz, seq, D_MODEL), jnp.float32),
            jax.ShapeDtypeStruct((bsz, seq, D_MODEL // 2), jnp.uint32),
            jax.ShapeDtypeStruct((bsz, seq, LANES), jnp.int32),
            jax.ShapeDtypeStruct((bsz, seq, LANES), jnp.float32),
            jax.ShapeDtypeStruct((bsz, seq, LANES), jnp.int32),
            jax.ShapeDtypeStruct((SUBLANES, LANES), jnp.int32),
        ],
        scratch_shapes=[pltpu.VMEM((SUBLANES, LANES), jnp.float32)],
        compiler_params=_cparams(("arbitrary", "arbitrary")),
        name="merge_router",
    )(ysb, ydf, main, x, mod3, wsb, wdf, wout, gpost, gpre, wrh, wrl, br)


def _sc_gather_rows(table, idx):
    n = idx.shape[0]
    width = table.shape[1]
    n_workers = SC_CORES * SC_SUBCORES
    per_worker = n // n_workers
    n_chunks = per_worker // SC_GATHER_ROWS
    assert n_chunks * SC_GATHER_ROWS * n_workers == n
    mesh = plsc.VectorSubcoreMesh(core_axis_name="c", subcore_axis_name="s",
                                  num_cores=SC_CORES, num_subcores=SC_SUBCORES)

    def body(table_hbm, idx_hbm, out_hbm, idx_v, rows_v, sem):
        wid = lax.axis_index("s") * SC_CORES + lax.axis_index("c")
        base = wid * per_worker

        @pl.loop(0, n_chunks)
        def _(ci):
            off = pl.multiple_of(base + ci * SC_GATHER_ROWS, SC_GATHER_ROWS)
            pltpu.sync_copy(idx_hbm.at[pl.ds(off, SC_GATHER_ROWS)], idx_v)
            pltpu.async_copy(table_hbm.at[idx_v], rows_v, sem).wait()
            pltpu.sync_copy(rows_v, out_hbm.at[pl.ds(off, SC_GATHER_ROWS)])

    return pl.kernel(
        body,
        out_type=jax.ShapeDtypeStruct((n, width), table.dtype),
        mesh=mesh,
        scratch_types=[
            pltpu.VMEM((SC_GATHER_ROWS,), jnp.int32),
            pltpu.VMEM((SC_GATHER_ROWS, width), table.dtype),
            pltpu.SemaphoreType.DMA,
        ],
        name="sc_gather_rows",
    )(table, idx)


def _moe_ffn_kernel(te_ref, nt_ref, x_ref, wgu_ref, bgu_ref, wd_ref, bd_ref, o_ref):
    i = pl.program_id(0)
    n_valid = nt_ref[0]

    @pl.when(i < n_valid)
    def _():
        xb = _unpack_bf16_pair(x_ref[...]).astype(jnp.bfloat16)
        gu = jnp.dot(xb, wgu_ref[0].astype(jnp.bfloat16),
                     preferred_element_type=jnp.float32) + bgu_ref[0]
        gate = jnp.minimum(gu[:, :D_EXPERT], SWIGLU_LIMIT)
        up = jnp.clip(gu[:, D_EXPERT:], -SWIGLU_LIMIT, SWIGLU_LIMIT)
        act = (up + 1.0) * (gate * jax.nn.sigmoid(SWIGLU_ALPHA * gate))
        out = jnp.dot(act.astype(jnp.bfloat16), wd_ref[0].astype(jnp.bfloat16),
                      preferred_element_type=jnp.float32) + bd_ref[0]
        half = D_MODEL // 2
        o_ref[...] = _pack_bf16_pair(out[:, :half], out[:, half:])

    @pl.when(i >= n_valid)
    def _():
        o_ref[...] = jnp.zeros_like(o_ref)


def _moe_ffn(tile_expert, n_valid, xg, wgu, bgu, wd, bd, tm):
    n_tiles = xg.shape[0] // tm
    grid_spec = pltpu.PrefetchScalarGridSpec(
        num_scalar_prefetch=2,
        grid=(n_tiles,),
        in_specs=[
            pl.BlockSpec((tm, D_MODEL // 2), lambda i, te, nt: (jnp.minimum(i, nt[0] - 1), 0)),
            pl.BlockSpec((1, D_MODEL, 2 * D_EXPERT), lambda i, te, nt: (te[i], 0, 0)),
            pl.BlockSpec((1, 1, 2 * D_EXPERT), lambda i, te, nt: (te[i], 0, 0)),
            pl.BlockSpec((1, D_EXPERT, D_MODEL), lambda i, te, nt: (te[i], 0, 0)),
            pl.BlockSpec((1, 1, D_MODEL), lambda i, te, nt: (te[i], 0, 0)),
        ],
        out_specs=pl.BlockSpec((tm, D_MODEL // 2), lambda i, te, nt: (i, 0)),
    )
    return pl.pallas_call(
        _moe_ffn_kernel,
        grid_spec=grid_spec,
        out_shape=jax.ShapeDtypeStruct((n_tiles * tm, D_MODEL // 2), jnp.uint32),
        compiler_params=_cparams(("arbitrary",)),
        name="moe_ffn",
    )(tile_expert, n_valid, xg, wgu, bgu.reshape(N_EXPERTS, 1, -1), wd,
      bd.reshape(N_EXPERTS, 1, -1))


def _moe_combine_kernel(rows_ref, wgt_ref, x1_ref, mod_ref, g_ref, o_ref):
    ts = x1_ref.shape[1]
    w = wgt_ref[0]
    y = jnp.zeros(x1_ref.shape[1:], jnp.float32)
    for k in range(TOP_K):
        y = y + w[:, k:k + 1] * _unpack_bf16_pair(rows_ref[k * ts:(k + 1) * ts, :])
    mod = mod_ref[0]
    o_ref[0] = x1_ref[0] + mod[5:6] * (_rms(y) * g_ref[...])


def _moe_combine(rows, wgt, x1, mod3, g_post, ts):
    bsz, seq, _ = x1.shape
    per_b = seq // ts
    return pl.pallas_call(
        _moe_combine_kernel,
        grid=(bsz, per_b),
        in_specs=[
            pl.BlockSpec((TOP_K * ts, D_MODEL // 2), lambda b, i: (b * per_b + i, 0)),
            pl.BlockSpec((1, ts, LANES), lambda b, i: (b, i, 0)),
            pl.BlockSpec((1, ts, D_MODEL), lambda b, i: (b, i, 0)),
            pl.BlockSpec((1, N_MOD, D_MODEL), lambda b, i: (b, 0, 0)),
            pl.BlockSpec((1, D_MODEL), lambda b, i: (0, 0)),
        ],
        out_specs=pl.BlockSpec((1, ts, D_MODEL), lambda b, i: (b, i, 0)),
        out_shape=jax.ShapeDtypeStruct((bsz, seq, D_MODEL), jnp.float32),
        compiler_params=_cparams(("arbitrary", "arbitrary")),
        name="moe_combine",
    )(rows, wgt, x1, mod3, g_post)


def _perm_keys_rows(a, tk):
    bsz, seq, w = a.shape
    g = tk // SUBLANES
    return a.reshape(bsz, seq // tk, SUBLANES, g, w).swapaxes(2, 3).reshape(bsz, seq, w)


def _perm_keys_cols(a, tk):
    bsz, r, seq = a.shape
    g = tk // SUBLANES
    return a.reshape(bsz, r, seq // tk, SUBLANES, g).swapaxes(3, 4).reshape(bsz, r, seq)


def _routing(top_idx, rank, counts, tm, n_tiles):
    n_tok = top_idx.shape[0]
    padded = ((counts + tm - 1) // tm) * tm
    start = jnp.cumsum(counts) - counts
    pend = jnp.cumsum(padded)
    pstart = pend - padded
    onehot = top_idx[:, :, None] == jnp.arange(N_EXPERTS, dtype=jnp.int32)[None, None, :]
    pos = rank + jnp.sum(jnp.where(onehot, pstart[None, None, :], 0), axis=-1)
    pair_id = jnp.arange(n_tok * TOP_K, dtype=jnp.int32)
    _, order = lax.sort_key_val(pos.reshape(-1), pair_id)
    n_valid = (pend[-1] // tm).astype(jnp.int32)
    tile_row0 = jnp.arange(n_tiles, dtype=jnp.int32) * tm
    tile_expert = jnp.minimum(
        jnp.sum((tile_row0[:, None] >= pend[None, :]).astype(jnp.int32), axis=1), N_EXPERTS - 1)
    te_oh = tile_expert[:, None] == jnp.arange(N_EXPERTS, dtype=jnp.int32)[None, :]
    t_pstart = jnp.sum(jnp.where(te_oh, pstart[None, :], 0), axis=1)
    t_start = jnp.sum(jnp.where(te_oh, start[None, :], 0), axis=1)
    t_count = jnp.sum(jnp.where(te_oh, counts[None, :], 0), axis=1)
    off = (tile_row0 - t_pstart)[:, None] + jnp.arange(tm, dtype=jnp.int32)[None, :]
    valid = off < t_count[:, None]
    src = jnp.clip(t_start[:, None] + off, 0, n_tok * TOP_K - 1)
    spread = (tile_row0[:, None] + jnp.arange(tm, dtype=jnp.int32)[None, :]) % n_tok
    row_token = jnp.where(valid, order[src] // TOP_K, spread).astype(jnp.int32)
    tile_expert = tile_expert[jnp.minimum(jnp.arange(n_tiles), jnp.maximum(n_valid - 1, 0))]
    return (pos.astype(jnp.int32), row_token.reshape(n_tiles * tm), tile_expert.astype(jnp.int32),
            n_valid.reshape(1))


def _alibi_slopes(n_heads):
    return 2.0 ** (-8.0 * jnp.arange(1, n_heads + 1, dtype=jnp.float32) / n_heads)


def _layer(x, c, w_mod, b_mod, g_pre_mix, g_post_mix, w_in, lamv, g_subln, w_branch_sb,
           w_branch_diff, w_out, g_pre_ffn, g_post_ffn, w_router, b_router, w_gate_up,
           b_gate_up, w_down, b_down, *, ts_in, tq, ts_merge, tm, ts_comb):
    bsz, seq, d = x.shape
    n_tok = bsz * seq
    bf = jnp.bfloat16

    mod, lam = _mod_proj(c, w_mod, b_mod, lamv)
    mod3 = mod.reshape(bsz, N_MOD, d)

    o_vsb = 2 * SB_WIDTH
    o_qdf = 3 * SB_WIDTH
    o_vdf = o_qdf + 2 * DIFF_QK_WIDTH
    o_g = o_vdf + DIFF_V_WIDTH
    w_main = jnp.concatenate([w_in[:, :o_vsb], w_in[:, o_qdf:o_vdf], w_in[:, o_g:]], axis=1).astype(bf)
    w_vt = jnp.concatenate([w_in[:, o_vsb:o_qdf], w_in[:, o_vdf:o_g]], axis=1).T.astype(bf)

    main, vt = _in_proj(x, mod3, g_pre_mix.reshape(1, d), w_main, w_vt, ts_in)

    k_perm = _perm_keys_rows(main[:, :, SB_WIDTH:2 * SB_WIDTH], tq)
    vt_perm = _perm_keys_cols(vt[:, :SB_WIDTH, :], tq)
    y_sb = _sb_attn(main, k_perm, vt_perm, tq, tq)
    y_df = _diff_attn(main, vt, _alibi_slopes(DIFF_HEADS), lam,
                      g_subln.reshape(DIFF_V_DIM, 1), tq, tq)

    wr = jnp.zeros((d, LANES), jnp.float32).at[:, :N_EXPERTS].set(w_router)
    wrh = wr.astype(bf)
    wrl = (wr - wrh.astype(jnp.float32)).astype(bf)
    br = jnp.full((1, LANES), NEG_BIG, jnp.float32).at[0, :N_EXPERTS].set(b_router)
    x1, h2p, top_idx, top_w, rank, counts = _merge_router(
        y_sb, y_df, main, x, mod3, w_branch_sb.astype(bf), w_branch_diff.astype(bf),
        w_out.astype(bf), g_post_mix.reshape(1, d), g_pre_ffn.reshape(1, d), wrh, wrl, br, ts_merge)

    n_tiles = (n_tok * TOP_K) // tm + N_EXPERTS
    pos, row_token, tile_expert, n_valid = _routing(
        top_idx.reshape(n_tok, LANES)[:, :TOP_K], rank.reshape(n_tok, LANES)[:, :TOP_K],
        counts[0, :N_EXPERTS], tm, n_tiles)
    xg = _sc_gather_rows(h2p.reshape(n_tok, d // 2), row_token)
    rows = _moe_ffn(tile_expert, n_valid, xg, w_gate_up, b_gate_up, w_down, b_down, tm)
    pos_steps = pos.reshape(n_tok // ts_comb, ts_comb, TOP_K).swapaxes(1, 2).reshape(n_tok * TOP_K)
    tok_rows = _sc_gather_rows(rows, pos_steps)
    return _moe_combine(tok_rows, top_w, x1, mod3, g_post_ffn.reshape(1, d), ts_comb)


def kernel(x, c, w_mod, b_mod, g_pre_mix, g_post_mix, w_in, lambda_q1, lambda_k1, lambda_q2,
           lambda_k2, g_subln, w_branch_sb, w_branch_diff, w_out, g_pre_ffn, g_post_ffn,
           w_router, b_router, w_gate_up, b_gate_up, w_down, b_down):
    depth = w_mod.shape[0]
    for l in range(depth):
        lamv = jnp.stack([lambda_q1[l], lambda_k1[l], lambda_q2[l], lambda_k2[l]])
        x = _layer(x, c, w_mod[l], b_mod[l], g_pre_mix[l], g_post_mix[l], w_in[l], lamv,
                   g_subln[l], w_branch_sb[l], w_branch_diff[l], w_out[l], g_pre_ffn[l],
                   g_post_ffn[l], w_router[l], b_router[l], w_gate_up[l], b_gate_up[l],
                   w_down[l], b_down[l],
                   ts_in=512, tq=256, ts_merge=512, tm=512, ts_comb=256)
    return x
```

```python
import functools
import math

import jax
import jax.numpy as jnp
from jax import lax
from jax.experimental import pallas as pl
from jax.experimental.pallas import tpu as pltpu
from jax.experimental.pallas import tpu_sc as plsc

D_MODEL = 1024
SB_HEADS = 8
SB_HEAD_DIM = 64
SB_WIDTH = SB_HEADS * SB_HEAD_DIM
DIFF_HEADS = 4
DIFF_HEAD_DIM = 64
DIFF_V_DIM = 2 * DIFF_HEAD_DIM
DIFF_QK_WIDTH = DIFF_HEADS * 2 * DIFF_HEAD_DIM
DIFF_V_WIDTH = DIFF_HEADS * DIFF_V_DIM
N_EXPERTS = 32
TOP_K = 4
D_EXPERT = D_MODEL
SWIGLU_LIMIT = 7.0
SWIGLU_ALPHA = 1.702
RMS_EPS = 1e-6
N_MOD = 6
LAM_INIT = 0.8 - 0.6 * math.exp(-0.3 * 0)

LANES = 128
SUBLANES = 8
NEG_BIG = -1e30
SC_CORES = 2
SC_SUBCORES = 16
SC_GATHER_ROWS = 128
MERGE_SUB = 256

MAIN_WIDTH = 2 * SB_WIDTH + 2 * DIFF_QK_WIDTH + 2 * D_MODEL
VT_ROWS = SB_WIDTH + DIFF_V_WIDTH
COLBLK_K_SB = SB_WIDTH // LANES
COLBLK_Q_DF = 2 * SB_WIDTH // LANES
COLBLK_K_DF = COLBLK_Q_DF + DIFF_QK_WIDTH // LANES
GATE_COL0 = 2 * SB_WIDTH + 2 * DIFF_QK_WIDTH

VMEM_LIMIT = 56 * 1024 * 1024


def _cparams(sem, vmem=VMEM_LIMIT):
    return pltpu.CompilerParams(dimension_semantics=sem, vmem_limit_bytes=vmem)


def _rms(x):
    return x * lax.rsqrt(jnp.mean(x * x, axis=-1, keepdims=True) + RMS_EPS)


def _mod_kernel(c_ref, w_ref, b_ref, lamv_ref, mod_ref, lam_ref):
    c = c_ref[...]
    ca = c * jax.nn.sigmoid(c)
    mod_ref[...] = jnp.dot(ca, w_ref[...], preferred_element_type=jnp.float32,
                           precision=lax.Precision.HIGHEST) + b_ref[...]
    lv = lamv_ref[...]
    s1 = jnp.sum(lv[0:1] * lv[1:2], axis=-1, keepdims=True)
    s2 = jnp.sum(lv[2:3] * lv[3:4], axis=-1, keepdims=True)
    lam = jnp.exp(s1) - jnp.exp(s2) + LAM_INIT
    lam_ref[...] = jnp.broadcast_to(lam, lam_ref.shape)


def _mod_proj(c, w_mod, b_mod, lamv):
    bsz = c.shape[0]
    tn = 1536
    n = w_mod.shape[1]
    return pl.pallas_call(
        _mod_kernel,
        grid=(n // tn,),
        in_specs=[
            pl.BlockSpec((bsz, D_MODEL), lambda j: (0, 0)),
            pl.BlockSpec((D_MODEL, tn), lambda j: (0, j)),
            pl.BlockSpec((1, tn), lambda j: (0, j)),
            pl.BlockSpec((4, DIFF_HEAD_DIM), lambda j: (0, 0)),
        ],
        out_specs=[
            pl.BlockSpec((bsz, tn), lambda j: (0, j)),
            pl.BlockSpec((SUBLANES, LANES), lambda j: (0, 0)),
        ],
        out_shape=[
            jax.ShapeDtypeStruct((bsz, n), jnp.float32),
            jax.ShapeDtypeStruct((SUBLANES, LANES), jnp.float32),
        ],
        compiler_params=_cparams(("arbitrary",)),
        name="mod_proj",
    )(c, w_mod, b_mod.reshape(1, n), lamv)


IN_CHUNK = 1024


def _in_proj_kernel(x_ref, mod_ref, g_ref, wm_ref, wvt_ref, main_ref, vt_ref, h_scr, *, tk):
    x = x_ref[0]
    mod = mod_ref[0]
    h = _rms(x) * g_ref[...]
    h = h * (1.0 + mod[1:2]) + mod[0:1]
    hb = h.astype(jnp.bfloat16)
    groups = tk // SUBLANES
    cols = []
    for ct in range(D_MODEL // LANES):
        h_scr[ct] = h[:, ct * LANES:(ct + 1) * LANES]
        pieces = []
        for blk in range(h.shape[0] // tk):
            for g in range(groups):
                pieces.append(h_scr[ct, pl.ds(blk * tk + g, SUBLANES, stride=groups), :])
        cols.append(jnp.concatenate(pieces, axis=0))
    hpb = jnp.concatenate(cols, axis=1).astype(jnp.bfloat16)

    half = IN_CHUNK // 2
    for ci in range(MAIN_WIDTH // IN_CHUNK):
        c0 = ci * IN_CHUNK
        if c0 == 0:
            q = jnp.dot(hb, wm_ref[:, :half], preferred_element_type=jnp.float32)
            k = jnp.dot(hpb, wm_ref[:, half:IN_CHUNK], preferred_element_type=jnp.float32)
            main_ref[0, :, :half] = (q * 0.0625).astype(jnp.bfloat16)
            main_ref[0, :, half:IN_CHUNK] = k.astype(jnp.bfloat16)
            continue
        p = jnp.dot(hb, wm_ref[:, c0:c0 + IN_CHUNK], preferred_element_type=jnp.float32)
        if c0 < GATE_COL0:
            main_ref[0, :, c0:c0 + half] = (p[:, :half] * 0.125).astype(jnp.bfloat16)
            main_ref[0, :, c0 + half:c0 + IN_CHUNK] = p[:, half:].astype(jnp.bfloat16)
        else:
            main_ref[0, :, c0:c0 + IN_CHUNK] = jax.nn.sigmoid(p).astype(jnp.bfloat16)
    nt = (((1,), (1,)), ((), ()))
    vt_sb = lax.dot_general(wvt_ref[:SB_WIDTH, :], hpb, nt, preferred_element_type=jnp.float32)
    vt_df = lax.dot_general(wvt_ref[SB_WIDTH:, :], hb, nt, preferred_element_type=jnp.float32)
    vt_ref[0, :SB_WIDTH, :] = vt_sb.astype(jnp.bfloat16)
    vt_ref[0, SB_WIDTH:, :] = vt_df.astype(jnp.bfloat16)


def _in_proj(x, mod3, g_pre, w_main, w_vt, ts, tk):
    bsz, seq, _ = x.shape
    assert ts % tk == 0
    return pl.pallas_call(
        functools.partial(_in_proj_kernel, tk=tk),
        grid=(bsz, seq // ts),
        in_specs=[
            pl.BlockSpec((1, ts, D_MODEL), lambda b, i: (b, i, 0)),
            pl.BlockSpec((1, N_MOD, D_MODEL), lambda b, i: (b, 0, 0)),
            pl.BlockSpec((1, D_MODEL), lambda b, i: (0, 0)),
            pl.BlockSpec((D_MODEL, MAIN_WIDTH), lambda b, i: (0, 0)),
            pl.BlockSpec((VT_ROWS, D_MODEL), lambda b, i: (0, 0)),
        ],
        out_specs=[
            pl.BlockSpec((1, ts, MAIN_WIDTH), lambda b, i: (b, i, 0)),
            pl.BlockSpec((1, VT_ROWS, ts), lambda b, i: (b, 0, i)),
        ],
        out_shape=[
            jax.ShapeDtypeStruct((bsz, seq, MAIN_WIDTH), jnp.bfloat16),
            jax.ShapeDtypeStruct((bsz, VT_ROWS, seq), jnp.bfloat16),
        ],
        scratch_shapes=[pltpu.VMEM((D_MODEL // LANES, ts, LANES), jnp.float32)],
        compiler_params=_cparams(("arbitrary", "arbitrary")),
        name="in_proj",
    )(x, mod3, g_pre, w_main, w_vt)


def _suffix_excl_prod8(tot):
    sub = lax.broadcasted_iota(jnp.int32, tot.shape, 0)
    x = jnp.where(sub < SUBLANES - 1, pltpu.roll(tot, SUBLANES - 1, 0), 1.0)
    for sh in (1, 2, 4):
        x = x * jnp.where(sub + sh < SUBLANES, pltpu.roll(x, SUBLANES - sh, 0), 1.0)
    return x


def _sb_scores(k_ref, q_heads, s_ref, slot, j, tk):
    kb = k_ref[0, pl.ds(pl.multiple_of(j * tk, tk), tk), :]
    for h in range(2):
        s_ref[slot, h] = lax.dot_general(kb, q_heads[h], (((1,), (1,)), ((), ())),
                                         preferred_element_type=jnp.float32)


def _sb_weights(zt, c8, ok, groups):
    tq = zt.shape[1]
    r = 0.5 - 0.5 * jnp.tanh(zt)
    if ok is not None:
        r = jnp.where(ok, r, 1.0)
    rg = [r[g * SUBLANES:(g + 1) * SUBLANES, :] for g in range(groups)]
    tot = rg[0]
    for g in range(1, groups):
        tot = tot * rg[g]
    p = c8 * _suffix_excl_prod8(tot)
    pieces = [None] * groups
    for g in range(groups - 1, -1, -1):
        pn = p * rg[g]
        pieces[g] = p - pn
        p = pn
    a = jnp.concatenate(pieces, axis=0).astype(jnp.bfloat16)
    return a, jnp.broadcast_to(p[0:1, :], (SUBLANES, tq))


def _sb_attn_kernel(q_ref, k_ref, v_ref, o_ref, acc_ref, c_ref, s_ref, *, tq, tk):
    i = pl.program_id(2)
    groups = tk // SUBLANES
    q2 = q_ref[0]
    lane = lax.broadcasted_iota(jnp.int32, q2.shape, 1)
    zero = jnp.zeros_like(q2)
    q_heads = (jnp.where(lane < SB_HEAD_DIM, q2, zero), jnp.where(lane < SB_HEAD_DIM, zero, q2))

    def step(j, slot, masked):
        _sb_scores(k_ref, q_heads, s_ref, 1 - slot, jnp.maximum(j - 1, 0), tk)
        if masked:
            row = lax.broadcasted_iota(jnp.int32, (tk, tq), 0)
            col = lax.broadcasted_iota(jnp.int32, (tk, tq), 1)
            ok = (row % SUBLANES) * groups + row // SUBLANES < col
        else:
            ok = None
        off = pl.multiple_of(j * tk, tk)
        ws = []
        for h in range(2):
            a, c_new = _sb_weights(s_ref[slot, h], c_ref[h], ok, groups)
            c_ref[h] = c_new
            ws.append(a)
        for h in range(2):
            vt_h = v_ref[0, h * SB_HEAD_DIM:(h + 1) * SB_HEAD_DIM, pl.ds(off, tk)]
            acc_ref[h] += jnp.dot(vt_h, ws[h], preferred_element_type=jnp.float32)

    acc_ref[...] = jnp.zeros_like(acc_ref)
    c_ref[...] = jnp.ones_like(c_ref)
    _sb_scores(k_ref, q_heads, s_ref, 0, i, tk)
    step(i, 0, True)

    def stick_left():
        return jnp.max(c_ref[...]) > 0.0

    def more(state):
        m, go = state
        return jnp.logical_and(m < i // 2, go)

    def pair(state):
        m, _ = state
        j = i - 1 - 2 * m
        step(j, 1, False)
        step(j - 1, 0, False)
        return m + 1, stick_left()

    m_done, go = lax.while_loop(more, pair, (jnp.int32(0), stick_left()))

    @pl.when(jnp.logical_and(jnp.logical_and(i % 2 == 1, m_done == i // 2), go))
    def _():
        step(0, 1, False)

    ot = jnp.concatenate([acc_ref[0], acc_ref[1]], axis=0)
    o_ref[0] = ot.T.astype(jnp.bfloat16)


def _sb_attn(main, vt, tq, tk):
    bsz, seq, _ = main.shape
    assert tq == tk
    kern = functools.partial(_sb_attn_kernel, tq=tq, tk=tk)
    return pl.pallas_call(
        kern,
        grid=(bsz, SB_WIDTH // LANES, seq // tq),
        in_specs=[
            pl.BlockSpec((1, tq, LANES), lambda b, p, i: (b, i, p)),
            pl.BlockSpec((1, seq, LANES), lambda b, p, i: (b, 0, COLBLK_K_SB + p)),
            pl.BlockSpec((1, LANES, seq), lambda b, p, i: (b, p, 0)),
        ],
        out_specs=pl.BlockSpec((1, tq, LANES), lambda b, p, i: (b, i, p)),
        out_shape=jax.ShapeDtypeStruct((bsz, seq, SB_WIDTH), jnp.bfloat16),
        scratch_shapes=[
            pltpu.VMEM((2, SB_HEAD_DIM, tq), jnp.float32),
            pltpu.VMEM((2, SUBLANES, tq), jnp.float32),
            pltpu.VMEM((2, 2, tk, tq), jnp.float32),
        ],
        compiler_params=_cparams(("arbitrary", "arbitrary", "arbitrary")),
        name="sb_attn",
    )(main, main, vt)


def _diff_attn_kernel(slopes_ref, q_ref, k_ref, v_ref, lam_ref, g_ref, o_ref,
                      acc_ref, m_ref, l_ref, s_ref, *, tq, tk):
    hd = pl.program_id(1)
    i = pl.program_id(2)
    slope = slopes_ref[hd]
    q2 = q_ref[0]
    lane = lax.broadcasted_iota(jnp.int32, q2.shape, 1)
    zero = jnp.zeros_like(q2)
    q_maps = (jnp.where(lane < DIFF_HEAD_DIM, q2, zero), jnp.where(lane < DIFF_HEAD_DIM, zero, q2))

    row = lax.broadcasted_iota(jnp.int32, (tk, tq), 0)
    col = lax.broadcasted_iota(jnp.int32, (tk, tq), 1)
    bias = slope * (row - col).astype(jnp.float32)

    def scores(slot, j):
        kb = k_ref[0, pl.ds(pl.multiple_of(j * tk, tk), tk), :]
        for m in range(2):
            s_ref[slot, m] = lax.dot_general(kb, q_maps[m], (((1,), (1,)), ((), ())),
                                             preferred_element_type=jnp.float32)

    def step(j, slot, masked):
        scores(1 - slot, jnp.maximum(j - 1, 0))
        off = pl.multiple_of(j * tk, tk)
        vtb = v_ref[0, :, pl.ds(off, tk)]
        cb = slope * ((j - i) * tk).astype(jnp.float32)
        ps, alphas = [], []
        for m in range(2):
            s = s_ref[slot, m] + bias
            if masked:
                s = jnp.where(row <= col, s, NEG_BIG)
            m_old = m_ref[m]
            m_new = jnp.maximum(m_old, jnp.max(s, axis=0, keepdims=True) + cb)
            alpha = jnp.exp(m_old - m_new)
            p = jnp.exp(s - (m_new - cb))
            l_ref[m] = alpha * l_ref[m] + jnp.sum(p, axis=0, keepdims=True)
            m_ref[m] = m_new
            ps.append(p.astype(jnp.bfloat16))
            alphas.append(alpha)
        for m in range(2):
            acc_ref[m] = alphas[m] * acc_ref[m] + jnp.dot(
                vtb, ps[m], preferred_element_type=jnp.float32)

    acc_ref[...] = jnp.zeros_like(acc_ref)
    m_ref[...] = jnp.full_like(m_ref, NEG_BIG)
    l_ref[...] = jnp.zeros_like(l_ref)
    scores(0, i)
    step(i, 0, True)

    def pair(n, carry):
        j = i - 1 - 2 * n
        step(j, 1, False)
        step(j - 1, 0, False)
        return carry

    lax.fori_loop(0, i // 2, pair, 0)

    @pl.when(i % 2 == 1)
    def _():
        step(0, 1, False)

    lam = lam_ref[0:1, 0:1]
    o = acc_ref[0] / l_ref[0] - lam * (acc_ref[1] / l_ref[1])
    ms = jnp.mean(o * o, axis=0, keepdims=True)
    y = o * lax.rsqrt(ms + RMS_EPS) * g_ref[...] * (1.0 - LAM_INIT)
    o_ref[0] = y.T.astype(jnp.bfloat16)


def _diff_attn(main, vt, slopes, lam, g_col, tq, tk):
    bsz, seq, _ = main.shape
    assert tq == tk
    kern = functools.partial(_diff_attn_kernel, tq=tq, tk=tk)
    vrow0 = SB_WIDTH // LANES
    return pl.pallas_call(
        kern,
        grid=(bsz, DIFF_HEADS, seq // tq),
        in_specs=[
            pl.BlockSpec(memory_space=pltpu.SMEM),
            pl.BlockSpec((1, tq, LANES), lambda b, h, i: (b, i, COLBLK_Q_DF + h)),
            pl.BlockSpec((1, seq, LANES), lambda b, h, i: (b, 0, COLBLK_K_DF + h)),
            pl.BlockSpec((1, DIFF_V_DIM, seq), lambda b, h, i: (b, vrow0 + h, 0)),
            pl.BlockSpec((SUBLANES, LANES), lambda b, h, i: (0, 0)),
            pl.BlockSpec((DIFF_V_DIM, 1), lambda b, h, i: (0, 0)),
        ],
        out_specs=pl.BlockSpec((1, tq, DIFF_V_DIM), lambda b, h, i: (b, i, h)),
        out_shape=jax.ShapeDtypeStruct((bsz, seq, DIFF_V_WIDTH), jnp.bfloat16),
        scratch_shapes=[
            pltpu.VMEM((2, DIFF_V_DIM, tq), jnp.float32),
            pltpu.VMEM((2, 1, tq), jnp.float32),
            pltpu.VMEM((2, 1, tq), jnp.float32),
            pltpu.VMEM((2, 2, tk, tq), jnp.float32),
        ],
        compiler_params=_cparams(("arbitrary", "arbitrary", "arbitrary")),
        name="diff_attn",
    )(slopes, main, main, vt, lam, g_col)


def _pack_bf16_pair(a, b):
    ab = pltpu.bitcast(a.astype(jnp.bfloat16).astype(jnp.float32), jnp.uint32)
    bb = pltpu.bitcast(b.astype(jnp.bfloat16).astype(jnp.float32), jnp.uint32)
    return ab | (bb >> 16)


def _unpack_bf16_pair(w):
    hi = pltpu.bitcast(w & jnp.uint32(0xFFFF0000), jnp.float32)
    lo = pltpu.bitcast(w << 16, jnp.float32)
    return jnp.concatenate([hi, lo], axis=1)


def _merge_router_kernel(ysb_ref, ydf_ref, gates_ref, x_ref, mod_ref, wsb_ref, wdf_ref, wout_ref,
                         gpost_ref, gpre_ref, wrh_ref, wrl_ref, br_ref,
                         x1_ref, h2_ref, idx_ref, wgt_ref, rank_ref, cnt_ref, base_ref):
    first = jnp.logical_and(pl.program_id(0) == 0, pl.program_id(1) == 0)

    @pl.when(first)
    def _():
        base_ref[...] = jnp.zeros_like(base_ref)

    mod = mod_ref[0]
    for sub in range(x_ref.shape[1] // MERGE_SUB):
        rows = slice(sub * MERGE_SUB, (sub + 1) * MERGE_SUB)
        _merge_router_rows(rows, mod, ysb_ref, ydf_ref, gates_ref, x_ref, wsb_ref, wdf_ref, wout_ref,
                           gpost_ref, gpre_ref, wrh_ref, wrl_ref, br_ref,
                           x1_ref, h2_ref, idx_ref, wgt_ref, rank_ref, cnt_ref, base_ref)


def _merge_router_rows(rows, mod, ysb_ref, ydf_ref, gates_ref, x_ref, wsb_ref, wdf_ref, wout_ref,
                       gpost_ref, gpre_ref, wrh_ref, wrl_ref, br_ref,
                       x1_ref, h2_ref, idx_ref, wgt_ref, rank_ref, cnt_ref, base_ref):
    a = jnp.dot(ysb_ref[0, rows, :], wsb_ref[...], preferred_element_type=jnp.float32)
    b = jnp.dot(ydf_ref[0, rows, :], wdf_ref[...], preferred_element_type=jnp.float32)
    g = gates_ref[0, rows, :].astype(jnp.float32)
    merged = g[:, :D_MODEL] * a + g[:, D_MODEL:] * b
    mix = jnp.dot(merged.astype(jnp.bfloat16), wout_ref[...], preferred_element_type=jnp.float32)
    x1 = x_ref[0, rows, :] + mod[2:3] * (_rms(mix) * gpost_ref[...])
    x1_ref[0, rows, :] = x1
    h2 = _rms(x1) * gpre_ref[...]
    h2 = h2 * (1.0 + mod[4:5]) + mod[3:4]
    half = D_MODEL // 2
    h2_ref[0, rows, :] = _pack_bf16_pair(h2[:, :half], h2[:, half:])

    hh = h2.astype(jnp.bfloat16)
    hl = (h2 - hh.astype(jnp.float32)).astype(jnp.bfloat16)
    logits = (jnp.dot(hh, wrh_ref[...], preferred_element_type=jnp.float32)
              + jnp.dot(hh, wrl_ref[...], preferred_element_type=jnp.float32)
              + jnp.dot(hl, wrh_ref[...], preferred_element_type=jnp.float32)
              + br_ref[...])
    lane = lax.broadcasted_iota(jnp.int32, logits.shape, 1)
    lanef = lane.astype(jnp.float32)
    vals, idxs = [], []
    cur = logits
    for _ in range(TOP_K):
        mx = jnp.max(cur, axis=-1, keepdims=True)
        ix = jnp.min(jnp.where(cur == mx, lanef, float(LANES)), axis=-1, keepdims=True)
        cur = jnp.where(lanef == ix, -jnp.inf, cur)
        vals.append(mx)
        idxs.append(ix)
    es = [jnp.exp(v - vals[0]) for v in vals]
    den = es[0] + es[1] + es[2] + es[3]
    oi = jnp.zeros(logits.shape, jnp.float32)
    ow = jnp.zeros(logits.shape, jnp.float32)
    for k in range(TOP_K):
        oi = jnp.where(lane == k, idxs[k], oi)
        ow = jnp.where(lane == k, es[k] / den, ow)
    idx_ref[0, rows, :] = oi.astype(jnp.int32)
    wgt_ref[0, rows, :] = ow

    ts = logits.shape[0]
    member = jnp.zeros(logits.shape, jnp.float32)
    for k in range(TOP_K):
        member = member + (lanef == idxs[k]).astype(jnp.float32)
    rr = lax.broadcasted_iota(jnp.int32, (ts, ts), 0)
    cc = lax.broadcasted_iota(jnp.int32, (ts, ts), 1)
    lower = jnp.where(cc < rr, 1.0, 0.0).astype(jnp.bfloat16)
    before = jnp.dot(lower, member.astype(jnp.bfloat16), preferred_element_type=jnp.float32)
    base = base_ref[0:1, :]
    rank_all = before + base
    orank = jnp.zeros(logits.shape, jnp.float32)
    for k in range(TOP_K):
        rk = jnp.sum(jnp.where(lanef == idxs[k], rank_all, 0.0), axis=-1, keepdims=True)
        orank = jnp.where(lane == k, rk, orank)
    rank_ref[0, rows, :] = orank.astype(jnp.int32)
    new_base = base + jnp.sum(member, axis=0, keepdims=True)
    base_ref[...] = jnp.broadcast_to(new_base, base_ref.shape)
    cnt_ref[...] = jnp.broadcast_to(new_base, cnt_ref.shape).astype(jnp.int32)


def _merge_router(ysb, ydf, main, x, mod3, wsb, wdf, wout, gpost, gpre, wrh, wrl, br, ts):
    bsz, seq, _ = x.shape
    const = lambda b, i: (0, 0)
    return pl.pallas_call(
        _merge_router_kernel,
        grid=(bsz, seq // ts),
        in_specs=[
            pl.BlockSpec((1, ts, SB_WIDTH), lambda b, i: (b, i, 0)),
            pl.BlockSpec((1, ts, DIFF_V_WIDTH), lambda b, i: (b, i, 0)),
            pl.BlockSpec((1, ts, 2 * D_MODEL), lambda b, i: (b, i, GATE_COL0 // (2 * D_MODEL))),
            pl.BlockSpec((1, ts, D_MODEL), lambda b, i: (b, i, 0)),
            pl.BlockSpec((1, N_MOD, D_MODEL), lambda b, i: (b, 0, 0)),
            pl.BlockSpec((SB_WIDTH, D_MODEL), const),
            pl.BlockSpec((DIFF_V_WIDTH, D_MODEL), const),
            pl.BlockSpec((D_MODEL, D_MODEL), const),
            pl.BlockSpec((1, D_MODEL), const),
            pl.BlockSpec((1, D_MODEL), const),
            pl.BlockSpec((D_MODEL, LANES), const),
            pl.BlockSpec((D_MODEL, LANES), const),
            pl.BlockSpec((1, LANES), const),
        ],
        out_specs=[
            pl.BlockSpec((1, ts, D_MODEL), lambda b, i: (b, i, 0)),
            pl.BlockSpec((1, ts, D_MODEL // 2), lambda b, i: (b, i, 0)),
            pl.BlockSpec((1, ts, LANES), lambda b, i: (b, i, 0)),
            pl.BlockSpec((1, ts, LANES), lambda b, i: (b, i, 0)),
            pl.BlockSpec((1, ts, LANES), lambda b, i: (b, i, 0)),
            pl.BlockSpec((SUBLANES, LANES), const),
        ],
        out_shape=[
            jax.ShapeDtypeStruct((bsz, seq, D_MODEL), jnp.float32),
            jax.ShapeDtypeStruct((bsz, seq, D_MODEL // 2), jnp.uint32),
            jax.ShapeDtypeStruct((bsz, seq, LANES), jnp.int32),
            jax.ShapeDtypeStruct((bsz, seq, LANES), jnp.float32),
            jax.ShapeDtypeStruct((bsz, seq, LANES), jnp.int32),
            jax.ShapeDtypeStruct((SUBLANES, LANES), jnp.int32),
        ],
        scratch_shapes=[pltpu.VMEM((SUBLANES, LANES), jnp.float32)],
        compiler_params=_cparams(("arbitrary", "arbitrary")),
        name="merge_router",
    )(ysb, ydf, main, x, mod3, wsb, wdf, wout, gpost, gpre, wrh, wrl, br)


def _sc_gather_rows(table, idx):
    n = idx.shape[0]
    width = table.shape[1]
    n_workers = SC_CORES * SC_SUBCORES
    per_worker = n // n_workers
    n_chunks = per_worker // SC_GATHER_ROWS
    assert n_chunks * SC_GATHER_ROWS * n_workers == n
    mesh = plsc.VectorSubcoreMesh(core_axis_name="c", subcore_axis_name="s",
                                  num_cores=SC_CORES, num_subcores=SC_SUBCORES)

    def body(table_hbm, idx_hbm, out_hbm, idx_v, rows_v, sem):
        wid = lax.axis_index("s") * SC_CORES + lax.axis_index("c")
        base = wid * per_worker

        @pl.loop(0, n_chunks)
        def _(ci):
            off = pl.multiple_of(base + ci * SC_GATHER_ROWS, SC_GATHER_ROWS)
            pltpu.sync_copy(idx_hbm.at[pl.ds(off, SC_GATHER_ROWS)], idx_v)
            pltpu.async_copy(table_hbm.at[idx_v], rows_v, sem).wait()
            pltpu.sync_copy(rows_v, out_hbm.at[pl.ds(off, SC_GATHER_ROWS)])

    return pl.kernel(
        body,
        out_type=jax.ShapeDtypeStruct((n, width), table.dtype),
        mesh=mesh,
        scratch_types=[
            pltpu.VMEM((SC_GATHER_ROWS,), jnp.int32),
            pltpu.VMEM((SC_GATHER_ROWS, width), table.dtype),
            pltpu.SemaphoreType.DMA,
        ],
        name="sc_gather_rows",
    )(table, idx)


def _moe_ffn_kernel(te_ref, nt_ref, x_ref, wgu_ref, bgu_ref, wd_ref, bd_ref, o_ref):
    i = pl.program_id(0)
    n_valid = nt_ref[0]

    @pl.when(i < n_valid)
    def _():
        xb = _unpack_bf16_pair(x_ref[...]).astype(jnp.bfloat16)
        gu = jnp.dot(xb, wgu_ref[0].astype(jnp.bfloat16),
                     preferred_element_type=jnp.float32) + bgu_ref[0]
        gate = jnp.minimum(gu[:, :D_EXPERT], SWIGLU_LIMIT)
        up = jnp.clip(gu[:, D_EXPERT:], -SWIGLU_LIMIT, SWIGLU_LIMIT)
        act = (up + 1.0) * (gate * jax.nn.sigmoid(SWIGLU_ALPHA * gate))
        out = jnp.dot(act.astype(jnp.bfloat16), wd_ref[0].astype(jnp.bfloat16),
                      preferred_element_type=jnp.float32) + bd_ref[0]
        half = D_MODEL // 2
        o_ref[...] = _pack_bf16_pair(out[:, :half], out[:, half:])

    @pl.when(i >= n_valid)
    def _():
        o_ref[...] = jnp.zeros_like(o_ref)


def _moe_ffn(tile_expert, n_valid, xg, wgu, bgu, wd, bd, tm):
    n_tiles = xg.shape[0] // tm
    grid_spec = pltpu.PrefetchScalarGridSpec(
        num_scalar_prefetch=2,
        grid=(n_tiles,),
        in_specs=[
            pl.BlockSpec((tm, D_MODEL // 2), lambda i, te, nt: (jnp.minimum(i, nt[0] - 1), 0)),
            pl.BlockSpec((1, D_MODEL, 2 * D_EXPERT), lambda i, te, nt: (te[i], 0, 0)),
            pl.BlockSpec((1, 1, 2 * D_EXPERT), lambda i, te, nt: (te[i], 0, 0)),
            pl.BlockSpec((1, D_EXPERT, D_MODEL), lambda i, te, nt: (te[i], 0, 0)),
            pl.BlockSpec((1, 1, D_MODEL), lambda i, te, nt: (te[i], 0, 0)),
        ],
        out_specs=pl.BlockSpec((tm, D_MODEL // 2), lambda i, te, nt: (i, 0)),
    )
    return pl.pallas_call(
        _moe_ffn_kernel,
        grid_spec=grid_spec,
        out_shape=jax.ShapeDtypeStruct((n_tiles * tm, D_MODEL // 2), jnp.uint32),
        compiler_params=_cparams(("arbitrary",)),
        name="moe_ffn",
    )(tile_expert, n_valid, xg, wgu, bgu.reshape(N_EXPERTS, 1, -1), wd,
      bd.reshape(N_EXPERTS, 1, -1))


def _moe_combine_kernel(rows_ref, wgt_ref, x1_ref, mod_ref, g_ref, o_ref):
    ts = x1_ref.shape[1]
    w = wgt_ref[0]
    y = jnp.zeros(x1_ref.shape[1:], jnp.float32)
    for k in range(TOP_K):
        y = y + w[:, k:k + 1] * _unpack_bf16_pair(rows_ref[k * ts:(k + 1) * ts, :])
    mod = mod_ref[0]
    o_ref[0] = x1_ref[0] + mod[5:6] * (_rms(y) * g_ref[...])


def _moe_combine(rows, wgt, x1, mod3, g_post, ts):
    bsz, seq, _ = x1.shape
    per_b = seq // ts
    return pl.pallas_call(
        _moe_combine_kernel,
        grid=(bsz, per_b),
        in_specs=[
            pl.BlockSpec((TOP_K * ts, D_MODEL // 2), lambda b, i: (b * per_b + i, 0)),
            pl.BlockSpec((1, ts, LANES), lambda b, i: (b, i, 0)),
            pl.BlockSpec((1, ts, D_MODEL), lambda b, i: (b, i, 0)),
            pl.BlockSpec((1, N_MOD, D_MODEL), lambda b, i: (b, 0, 0)),
            pl.BlockSpec((1, D_MODEL), lambda b, i: (0, 0)),
        ],
        out_specs=pl.BlockSpec((1, ts, D_MODEL), lambda b, i: (b, i, 0)),
        out_shape=jax.ShapeDtypeStruct((bsz, seq, D_MODEL), jnp.float32),
        compiler_params=_cparams(("arbitrary", "arbitrary")),
        name="moe_combine",
    )(rows, wgt, x1, mod3, g_post)


def _routing(top_idx, rank, counts, tm, n_tiles):
    n_tok = top_idx.shape[0]
    padded = ((counts + tm - 1) // tm) * tm
    start = jnp.cumsum(counts) - counts
    pend = jnp.cumsum(padded)
    pstart = pend - padded
    onehot = top_idx[:, :, None] == jnp.arange(N_EXPERTS, dtype=jnp.int32)[None, None, :]
    pos = rank + jnp.sum(jnp.where(onehot, pstart[None, None, :], 0), axis=-1)
    pair_id = jnp.arange(n_tok * TOP_K, dtype=jnp.int32)
    _, order = lax.sort_key_val(pos.reshape(-1), pair_id)
    n_valid = (pend[-1] // tm).astype(jnp.int32)
    tile_row0 = jnp.arange(n_tiles, dtype=jnp.int32) * tm
    tile_expert = jnp.minimum(
        jnp.sum((tile_row0[:, None] >= pend[None, :]).astype(jnp.int32), axis=1), N_EXPERTS - 1)
    te_oh = tile_expert[:, None] == jnp.arange(N_EXPERTS, dtype=jnp.int32)[None, :]
    t_pstart = jnp.sum(jnp.where(te_oh, pstart[None, :], 0), axis=1)
    t_start = jnp.sum(jnp.where(te_oh, start[None, :], 0), axis=1)
    t_count = jnp.sum(jnp.where(te_oh, counts[None, :], 0), axis=1)
    off = (tile_row0 - t_pstart)[:, None] + jnp.arange(tm, dtype=jnp.int32)[None, :]
    valid = off < t_count[:, None]
    src = jnp.clip(t_start[:, None] + off, 0, n_tok * TOP_K - 1)
    spread = (tile_row0[:, None] + jnp.arange(tm, dtype=jnp.int32)[None, :]) % n_tok
    row_token = jnp.where(valid, order[src] // TOP_K, spread).astype(jnp.int32)
    tile_expert = tile_expert[jnp.minimum(jnp.arange(n_tiles), jnp.maximum(n_valid - 1, 0))]
    return (pos.astype(jnp.int32), row_token.reshape(n_tiles * tm), tile_expert.astype(jnp.int32),
            n_valid.reshape(1))


def _alibi_slopes(n_heads):
    return 2.0 ** (-8.0 * jnp.arange(1, n_heads + 1, dtype=jnp.float32) / n_heads)


def _layer(x, c, w_mod, b_mod, g_pre_mix, g_post_mix, w_in, lamv, g_subln, w_branch_sb,
           w_branch_diff, w_out, g_pre_ffn, g_post_ffn, w_router, b_router, w_gate_up,
           b_gate_up, w_down, b_down, *, ts_in, tq, ts_merge, tm, ts_comb):
    bsz, seq, d = x.shape
    n_tok = bsz * seq
    bf = jnp.bfloat16

    mod, lam = _mod_proj(c, w_mod, b_mod, lamv)
    mod3 = mod.reshape(bsz, N_MOD, d)

    o_vsb = 2 * SB_WIDTH
    o_qdf = 3 * SB_WIDTH
    o_vdf = o_qdf + 2 * DIFF_QK_WIDTH
    o_g = o_vdf + DIFF_V_WIDTH
    w_main = jnp.concatenate([w_in[:, :o_vsb], w_in[:, o_qdf:o_vdf], w_in[:, o_g:]], axis=1).astype(bf)
    w_vt = jnp.concatenate([w_in[:, o_vsb:o_qdf], w_in[:, o_vdf:o_g]], axis=1).T.astype(bf)

    main, vt = _in_proj(x, mod3, g_pre_mix.reshape(1, d), w_main, w_vt, ts_in, tq)
    y_sb = _sb_attn(main, vt, tq, tq)
    y_df = _diff_attn(main, vt, _alibi_slopes(DIFF_HEADS), lam,
                      g_subln.reshape(DIFF_V_DIM, 1), tq, tq)

    wr = jnp.zeros((d, LANES), jnp.float32).at[:, :N_EXPERTS].set(w_router)
    wrh = wr.astype(bf)
    wrl = (wr - wrh.astype(jnp.float32)).astype(bf)
    br = jnp.full((1, LANES), NEG_BIG, jnp.float32).at[0, :N_EXPERTS].set(b_router)
    x1, h2p, top_idx, top_w, rank, counts = _merge_router(
        y_sb, y_df, main, x, mod3, w_branch_sb.astype(bf), w_branch_diff.astype(bf),
        w_out.astype(bf), g_post_mix.reshape(1, d), g_pre_ffn.reshape(1, d), wrh, wrl, br, ts_merge)

    n_tiles = (n_tok * TOP_K) // tm + N_EXPERTS
    pos, row_token, tile_expert, n_valid = _routing(
        top_idx.reshape(n_tok, LANES)[:, :TOP_K], rank.reshape(n_tok, LANES)[:, :TOP_K],
        counts[0, :N_EXPERTS], tm, n_tiles)
    xg = _sc_gather_rows(h2p.reshape(n_tok, d // 2), row_token)
    rows = _moe_ffn(tile_expert, n_valid, xg, w_gate_up, b_gate_up, w_down, b_down, tm)
    pos_steps = pos.reshape(n_tok // ts_comb, ts_comb, TOP_K).swapaxes(1, 2).reshape(n_tok * TOP_K)
    tok_rows = _sc_gather_rows(rows, pos_steps)
    return _moe_combine(tok_rows, top_w, x1, mod3, g_post_ffn.reshape(1, d), ts_comb)


def kernel(x, c, w_mod, b_mod, g_pre_mix, g_post_mix, w_in, lambda_q1, lambda_k1, lambda_q2,
           lambda_k2, g_subln, w_branch_sb, w_branch_diff, w_out, g_pre_ffn, g_post_ffn,
           w_router, b_router, w_gate_up, b_gate_up, w_down, b_down):
    depth = w_mod.shape[0]
    for l in range(depth):
        lamv = jnp.stack([lambda_q1[l], lambda_k1[l], lambda_q2[l], lambda_k2[l]])
        x = _layer(x, c, w_mod[l], b_mod[l], g_pre_mix[l], g_post_mix[l], w_in[l], lamv,
                   g_subln[l], w_branch_sb[l], w_branch_diff[l], w_out[l], g_pre_ffn[l],
                   g_post_ffn[l], w_router[l], b_router[l], w_gate_up[l], b_gate_up[l],
                   w_down[l], b_down[l],
                   ts_in=512, tq=256, ts_merge=512, tm=512, ts_comb=256)
    return x
```

```python
import functools
import math

import jax
import jax.numpy as jnp
from jax import lax
from jax.experimental import pallas as pl
from jax.experimental.pallas import tpu as pltpu
from jax.experimental.pallas import tpu_sc as plsc

D_MODEL = 1024
SB_HEADS = 8
SB_HEAD_DIM = 64
SB_WIDTH = SB_HEADS * SB_HEAD_DIM
DIFF_HEADS = 4
DIFF_HEAD_DIM = 64
DIFF_V_DIM = 2 * DIFF_HEAD_DIM
DIFF_QK_WIDTH = DIFF_HEADS * 2 * DIFF_HEAD_DIM
DIFF_V_WIDTH = DIFF_HEADS * DIFF_V_DIM
N_EXPERTS = 32
TOP_K = 4
D_EXPERT = D_MODEL
SWIGLU_LIMIT = 7.0
SWIGLU_ALPHA = 1.702
RMS_EPS = 1e-6
N_MOD = 6
LAM_INIT = 0.8 - 0.6 * math.exp(-0.3 * 0)

LANES = 128
SUBLANES = 8
NEG_BIG = -1e30
EXP_ZERO_MARGIN = 110.0
NORM_SLACK = 1.01
SC_CORES = 2
SC_SUBCORES = 16
SC_GATHER_ROWS = 128
MERGE_SUB = 256

MAIN_WIDTH = 2 * SB_WIDTH + 2 * DIFF_QK_WIDTH + 2 * D_MODEL
VT_ROWS = SB_WIDTH + DIFF_V_WIDTH
COLBLK_K_SB = SB_WIDTH // LANES
COLBLK_Q_DF = 2 * SB_WIDTH // LANES
COLBLK_K_DF = COLBLK_Q_DF + DIFF_QK_WIDTH // LANES
GATE_COL0 = 2 * SB_WIDTH + 2 * DIFF_QK_WIDTH

VMEM_LIMIT = 56 * 1024 * 1024


def _cparams(sem, vmem=VMEM_LIMIT):
    return pltpu.CompilerParams(dimension_semantics=sem, vmem_limit_bytes=vmem)


def _rms(x):
    return x * lax.rsqrt(jnp.mean(x * x, axis=-1, keepdims=True) + RMS_EPS)


def _mod_kernel(c_ref, w_ref, b_ref, lamv_ref, mod_ref, lam_ref):
    c = c_ref[...]
    ca = c * jax.nn.sigmoid(c)
    mod_ref[...] = jnp.dot(ca, w_ref[...], preferred_element_type=jnp.float32,
                           precision=lax.Precision.HIGHEST) + b_ref[...]
    lv = lamv_ref[...]
    s1 = jnp.sum(lv[0:1] * lv[1:2], axis=-1, keepdims=True)
    s2 = jnp.sum(lv[2:3] * lv[3:4], axis=-1, keepdims=True)
    lam = jnp.exp(s1) - jnp.exp(s2) + LAM_INIT
    lam_ref[...] = jnp.broadcast_to(lam, lam_ref.shape)


def _mod_proj(c, w_mod, b_mod, lamv):
    bsz = c.shape[0]
    tn = 1536
    n = w_mod.shape[1]
    return pl.pallas_call(
        _mod_kernel,
        grid=(n // tn,),
        in_specs=[
            pl.BlockSpec((bsz, D_MODEL), lambda j: (0, 0)),
            pl.BlockSpec((D_MODEL, tn), lambda j: (0, j)),
            pl.BlockSpec((1, tn), lambda j: (0, j)),
            pl.BlockSpec((4, DIFF_HEAD_DIM), lambda j: (0, 0)),
        ],
        out_specs=[
            pl.BlockSpec((bsz, tn), lambda j: (0, j)),
            pl.BlockSpec((SUBLANES, LANES), lambda j: (0, 0)),
        ],
        out_shape=[
            jax.ShapeDtypeStruct((bsz, n), jnp.float32),
            jax.ShapeDtypeStruct((SUBLANES, LANES), jnp.float32),
        ],
        compiler_params=_cparams(("arbitrary",)),
        name="mod_proj",
    )(c, w_mod, b_mod.reshape(1, n), lamv)


IN_CHUNK = 1024


def _in_proj_kernel(x_ref, mod_ref, g_ref, wm_ref, wvt_ref, main_ref, vt_ref, h_scr, *, tk):
    x = x_ref[0]
    mod = mod_ref[0]
    h = _rms(x) * g_ref[...]
    h = h * (1.0 + mod[1:2]) + mod[0:1]
    hb = h.astype(jnp.bfloat16)
    groups = tk // SUBLANES
    cols = []
    for ct in range(D_MODEL // LANES):
        h_scr[ct] = h[:, ct * LANES:(ct + 1) * LANES]
        pieces = []
        for blk in range(h.shape[0] // tk):
            for g in range(groups):
                pieces.append(h_scr[ct, pl.ds(blk * tk + g, SUBLANES, stride=groups), :])
        cols.append(jnp.concatenate(pieces, axis=0))
    hpb = jnp.concatenate(cols, axis=1).astype(jnp.bfloat16)

    half = IN_CHUNK // 2
    for ci in range(MAIN_WIDTH // IN_CHUNK):
        c0 = ci * IN_CHUNK
        if c0 == 0:
            q = jnp.dot(hb, wm_ref[:, :half], preferred_element_type=jnp.float32)
            k = jnp.dot(hpb, wm_ref[:, half:IN_CHUNK], preferred_element_type=jnp.float32)
            main_ref[0, :, :half] = (q * 0.0625).astype(jnp.bfloat16)
            main_ref[0, :, half:IN_CHUNK] = k.astype(jnp.bfloat16)
            continue
        p = jnp.dot(hb, wm_ref[:, c0:c0 + IN_CHUNK], preferred_element_type=jnp.float32)
        if c0 < GATE_COL0:
            main_ref[0, :, c0:c0 + half] = (p[:, :half] * 0.125).astype(jnp.bfloat16)
            main_ref[0, :, c0 + half:c0 + IN_CHUNK] = p[:, half:].astype(jnp.bfloat16)
        else:
            main_ref[0, :, c0:c0 + IN_CHUNK] = jax.nn.sigmoid(p).astype(jnp.bfloat16)
    nt = (((1,), (1,)), ((), ()))
    vt_sb = lax.dot_general(wvt_ref[:SB_WIDTH, :], hpb, nt, preferred_element_type=jnp.float32)
    vt_df = lax.dot_general(wvt_ref[SB_WIDTH:, :], hb, nt, preferred_element_type=jnp.float32)
    vt_ref[0, :SB_WIDTH, :] = vt_sb.astype(jnp.bfloat16)
    vt_ref[0, SB_WIDTH:, :] = vt_df.astype(jnp.bfloat16)


def _in_proj(x, mod3, g_pre, w_main, w_vt, ts, tk):
    bsz, seq, _ = x.shape
    assert ts % tk == 0
    return pl.pallas_call(
        functools.partial(_in_proj_kernel, tk=tk),
        grid=(bsz, seq // ts),
        in_specs=[
            pl.BlockSpec((1, ts, D_MODEL), lambda b, i: (b, i, 0)),
            pl.BlockSpec((1, N_MOD, D_MODEL), lambda b, i: (b, 0, 0)),
            pl.BlockSpec((1, D_MODEL), lambda b, i: (0, 0)),
            pl.BlockSpec((D_MODEL, MAIN_WIDTH), lambda b, i: (0, 0)),
            pl.BlockSpec((VT_ROWS, D_MODEL), lambda b, i: (0, 0)),
        ],
        out_specs=[
            pl.BlockSpec((1, ts, MAIN_WIDTH), lambda b, i: (b, i, 0)),
            pl.BlockSpec((1, VT_ROWS, ts), lambda b, i: (b, 0, i)),
        ],
        out_shape=[
            jax.ShapeDtypeStruct((bsz, seq, MAIN_WIDTH), jnp.bfloat16),
            jax.ShapeDtypeStruct((bsz, VT_ROWS, seq), jnp.bfloat16),
        ],
        scratch_shapes=[pltpu.VMEM((D_MODEL // LANES, ts, LANES), jnp.float32)],
        compiler_params=_cparams(("arbitrary", "arbitrary")),
        name="in_proj",
    )(x, mod3, g_pre, w_main, w_vt)


def _suffix_excl_prod8(tot):
    sub = lax.broadcasted_iota(jnp.int32, tot.shape, 0)
    x = jnp.where(sub < SUBLANES - 1, pltpu.roll(tot, SUBLANES - 1, 0), 1.0)
    for sh in (1, 2, 4):
        x = x * jnp.where(sub + sh < SUBLANES, pltpu.roll(x, SUBLANES - sh, 0), 1.0)
    return x


def _sb_scores(k_ref, q_heads, s_ref, slot, j, tk):
    kb = k_ref[0, pl.ds(pl.multiple_of(j * tk, tk), tk), :]
    for h in range(2):
        s_ref[slot, h] = lax.dot_general(kb, q_heads[h], (((1,), (1,)), ((), ())),
                                         preferred_element_type=jnp.float32)


def _sb_weights(zt, c8, ok, groups):
    tq = zt.shape[1]
    r = 0.5 - 0.5 * jnp.tanh(zt)
    if ok is not None:
        r = jnp.where(ok, r, 1.0)
    rg = [r[g * SUBLANES:(g + 1) * SUBLANES, :] for g in range(groups)]
    tot = rg[0]
    for g in range(1, groups):
        tot = tot * rg[g]
    p = c8 * _suffix_excl_prod8(tot)
    pieces = [None] * groups
    for g in range(groups - 1, -1, -1):
        pn = p * rg[g]
        pieces[g] = p - pn
        p = pn
    a = jnp.concatenate(pieces, axis=0).astype(jnp.bfloat16)
    return a, jnp.broadcast_to(p[0:1, :], (SUBLANES, tq))


def _sb_attn_kernel(q_ref, k_ref, v_ref, o_ref, acc_ref, c_ref, s_ref, *, tq, tk):
    i = pl.program_id(2)
    groups = tk // SUBLANES
    q2 = q_ref[0]
    lane = lax.broadcasted_iota(jnp.int32, q2.shape, 1)
    zero = jnp.zeros_like(q2)
    q_heads = (jnp.where(lane < SB_HEAD_DIM, q2, zero), jnp.where(lane < SB_HEAD_DIM, zero, q2))

    def step(j, slot, masked):
        _sb_scores(k_ref, q_heads, s_ref, 1 - slot, jnp.maximum(j - 1, 0), tk)
        if masked:
            row = lax.broadcasted_iota(jnp.int32, (tk, tq), 0)
            col = lax.broadcasted_iota(jnp.int32, (tk, tq), 1)
            ok = (row % SUBLANES) * groups + row // SUBLANES < col
        else:
            ok = None
        off = pl.multiple_of(j * tk, tk)
        ws = []
        for h in range(2):
            a, c_new = _sb_weights(s_ref[slot, h], c_ref[h], ok, groups)
            c_ref[h] = c_new
            ws.append(a)
        for h in range(2):
            vt_h = v_ref[0, h * SB_HEAD_DIM:(h + 1) * SB_HEAD_DIM, pl.ds(off, tk)]
            acc_ref[h] += jnp.dot(vt_h, ws[h], preferred_element_type=jnp.float32)

    acc_ref[...] = jnp.zeros_like(acc_ref)
    c_ref[...] = jnp.ones_like(c_ref)
    _sb_scores(k_ref, q_heads, s_ref, 0, i, tk)
    step(i, 0, True)

    def stick_left():
        return jnp.max(c_ref[...]) > 0.0

    def more(state):
        m, go = state
        return jnp.logical_and(m < i // 2, go)

    def pair(state):
        m, _ = state
        j = i - 1 - 2 * m
        step(j, 1, False)
        step(j - 1, 0, False)
        return m + 1, stick_left()

    m_done, go = lax.while_loop(more, pair, (jnp.int32(0), stick_left()))

    @pl.when(jnp.logical_and(jnp.logical_and(i % 2 == 1, m_done == i // 2), go))
    def _():
        step(0, 1, False)

    ot = jnp.concatenate([acc_ref[0], acc_ref[1]], axis=0)
    o_ref[0] = ot.T.astype(jnp.bfloat16)


def _sb_attn(main, vt, tq, tk):
    bsz, seq, _ = main.shape
    assert tq == tk
    kern = functools.partial(_sb_attn_kernel, tq=tq, tk=tk)
    return pl.pallas_call(
        kern,
        grid=(bsz, SB_WIDTH // LANES, seq // tq),
        in_specs=[
            pl.BlockSpec((1, tq, LANES), lambda b, p, i: (b, i, p)),
            pl.BlockSpec((1, seq, LANES), lambda b, p, i: (b, 0, COLBLK_K_SB + p)),
            pl.BlockSpec((1, LANES, seq), lambda b, p, i: (b, p, 0)),
        ],
        out_specs=pl.BlockSpec((1, tq, LANES), lambda b, p, i: (b, i, p)),
        out_shape=jax.ShapeDtypeStruct((bsz, seq, SB_WIDTH), jnp.bfloat16),
        scratch_shapes=[
            pltpu.VMEM((2, SB_HEAD_DIM, tq), jnp.float32),
            pltpu.VMEM((2, SUBLANES, tq), jnp.float32),
            pltpu.VMEM((2, 2, tk, tq), jnp.float32),
        ],
        compiler_params=_cparams(("arbitrary", "arbitrary", "arbitrary")),
        name="sb_attn",
    )(main, main, vt)


def _diff_attn_kernel(slopes_ref, inv_slopes_ref, q_ref, k_ref, v_ref, lam_ref, g_ref, o_ref,
                      acc_ref, m_ref, l_ref, s_ref, kn_ref, *, tq, tk):
    hd = pl.program_id(1)
    i = pl.program_id(2)
    slope = slopes_ref[hd]

    @pl.when(i == 0)
    def _():
        kf = k_ref[0].astype(jnp.float32)
        kn2 = jnp.max(jnp.sum(kf * kf, axis=-1, keepdims=True), axis=0, keepdims=True)
        kn_ref[...] = jnp.broadcast_to(kn2, kn_ref.shape)

    q2 = q_ref[0]
    lane = lax.broadcasted_iota(jnp.int32, q2.shape, 1)
    zero = jnp.zeros_like(q2)
    q_maps = (jnp.where(lane < DIFF_HEAD_DIM, q2, zero), jnp.where(lane < DIFF_HEAD_DIM, zero, q2))

    row = lax.broadcasted_iota(jnp.int32, (tk, tq), 0)
    col = lax.broadcasted_iota(jnp.int32, (tk, tq), 1)
    bias = slope * (row - col).astype(jnp.float32)

    def scores(slot, j):
        kb = k_ref[0, pl.ds(pl.multiple_of(j * tk, tk), tk), :]
        for m in range(2):
            s_ref[slot, m] = lax.dot_general(kb, q_maps[m], (((1,), (1,)), ((), ())),
                                             preferred_element_type=jnp.float32)

    def step(j, slot, masked):
        scores(1 - slot, jnp.maximum(j - 1, 0))
        off = pl.multiple_of(j * tk, tk)
        vtb = v_ref[0, :, pl.ds(off, tk)]
        cb = slope * ((j - i) * tk).astype(jnp.float32)
        ps, alphas = [], []
        for m in range(2):
            s = s_ref[slot, m] + bias
            if masked:
                s = jnp.where(row <= col, s, NEG_BIG)
            m_old = m_ref[m]
            m_new = jnp.maximum(m_old, jnp.max(s, axis=0, keepdims=True) + cb)
            alpha = jnp.exp(m_old - m_new)
            p = jnp.exp(s - (m_new - cb))
            l_ref[m] = alpha * l_ref[m] + jnp.sum(p, axis=0, keepdims=True)
            m_ref[m] = m_new
            ps.append(p.astype(jnp.bfloat16))
            alphas.append(alpha)
        for m in range(2):
            acc_ref[m] = alphas[m] * acc_ref[m] + jnp.dot(
                vtb, ps[m], preferred_element_type=jnp.float32)

    acc_ref[...] = jnp.zeros_like(acc_ref)
    m_ref[...] = jnp.full_like(m_ref, NEG_BIG)
    l_ref[...] = jnp.zeros_like(l_ref)
    scores(0, i)
    step(i, 0, True)

    qf = q2.astype(jnp.float32)
    qn2 = jnp.max(jnp.sum(qf * qf, axis=-1, keepdims=True), axis=0, keepdims=True)
    zabs = jnp.sqrt(qn2 * kn_ref[0:1, 0:1]) * NORM_SLACK
    reach = (EXP_ZERO_MARGIN + 2.0 * zabs) * inv_slopes_ref[hd]
    n_need = jnp.floor(jnp.minimum((reach - 1.0) * (1.0 / tk), 1e6)) + 1.0
    n_back = jnp.minimum(i, jnp.max(n_need).astype(jnp.int32))

    def pair(n, carry):
        j = i - 1 - 2 * n
        step(j, 1, False)
        step(j - 1, 0, False)
        return carry

    lax.fori_loop(0, n_back // 2, pair, 0)

    @pl.when(n_back % 2 == 1)
    def _():
        step(i - n_back, 1, False)

    lam = lam_ref[0:1, 0:1]
    o = acc_ref[0] / l_ref[0] - lam * (acc_ref[1] / l_ref[1])
    ms = jnp.mean(o * o, axis=0, keepdims=True)
    y = o * lax.rsqrt(ms + RMS_EPS) * g_ref[...] * (1.0 - LAM_INIT)
    o_ref[0] = y.T.astype(jnp.bfloat16)


def _diff_attn(main, vt, slopes, lam, g_col, tq, tk):
    bsz, seq, _ = main.shape
    assert tq == tk
    kern = functools.partial(_diff_attn_kernel, tq=tq, tk=tk)
    vrow0 = SB_WIDTH // LANES
    return pl.pallas_call(
        kern,
        grid=(bsz, DIFF_HEADS, seq // tq),
        in_specs=[
            pl.BlockSpec(memory_space=pltpu.SMEM),
            pl.BlockSpec(memory_space=pltpu.SMEM),
            pl.BlockSpec((1, tq, LANES), lambda b, h, i: (b, i, COLBLK_Q_DF + h)),
            pl.BlockSpec((1, seq, LANES), lambda b, h, i: (b, 0, COLBLK_K_DF + h)),
            pl.BlockSpec((1, DIFF_V_DIM, seq), lambda b, h, i: (b, vrow0 + h, 0)),
            pl.BlockSpec((SUBLANES, LANES), lambda b, h, i: (0, 0)),
            pl.BlockSpec((DIFF_V_DIM, 1), lambda b, h, i: (0, 0)),
        ],
        out_specs=pl.BlockSpec((1, tq, DIFF_V_DIM), lambda b, h, i: (b, i, h)),
        out_shape=jax.ShapeDtypeStruct((bsz, seq, DIFF_V_WIDTH), jnp.bfloat16),
        scratch_shapes=[
            pltpu.VMEM((2, DIFF_V_DIM, tq), jnp.float32),
            pltpu.VMEM((2, 1, tq), jnp.float32),
            pltpu.VMEM((2, 1, tq), jnp.float32),
            pltpu.VMEM((2, 2, tk, tq), jnp.float32),
            pltpu.VMEM((SUBLANES, LANES), jnp.float32),
        ],
        compiler_params=_cparams(("arbitrary", "arbitrary", "arbitrary")),
        name="diff_attn",
    )(slopes, 1.0 / slopes, main, main, vt, lam, g_col)


def _pack_bf16_pair(a, b):
    ab = pltpu.bitcast(a.astype(jnp.bfloat16).astype(jnp.float32), jnp.uint32)
    bb = pltpu.bitcast(b.astype(jnp.bfloat16).astype(jnp.float32), jnp.uint32)
    return ab | (bb >> 16)


def _unpack_bf16_pair(w):
    hi = pltpu.bitcast(w & jnp.uint32(0xFFFF0000), jnp.float32)
    lo = pltpu.bitcast(w << 16, jnp.float32)
    return jnp.concatenate([hi, lo], axis=1)


def _merge_router_kernel(ysb_ref, ydf_ref, gates_ref, x_ref, mod_ref, wsb_ref, wdf_ref, wout_ref,
                         gpost_ref, gpre_ref, wrh_ref, wrl_ref, br_ref,
                         x1_ref, h2_ref, idx_ref, wgt_ref, rank_ref, cnt_ref, base_ref):
    first = jnp.logical_and(pl.program_id(0) == 0, pl.program_id(1) == 0)

    @pl.when(first)
    def _():
        base_ref[...] = jnp.zeros_like(base_ref)

    mod = mod_ref[0]
    for sub in range(x_ref.shape[1] // MERGE_SUB):
        rows = slice(sub * MERGE_SUB, (sub + 1) * MERGE_SUB)
        _merge_router_rows(rows, mod, ysb_ref, ydf_ref, gates_ref, x_ref, wsb_ref, wdf_ref, wout_ref,
                           gpost_ref, gpre_ref, wrh_ref, wrl_ref, br_ref,
                           x1_ref, h2_ref, idx_ref, wgt_ref, rank_ref, cnt_ref, base_ref)


def _merge_router_rows(rows, mod, ysb_ref, ydf_ref, gates_ref, x_ref, wsb_ref, wdf_ref, wout_ref,
                       gpost_ref, gpre_ref, wrh_ref, wrl_ref, br_ref,
                       x1_ref, h2_ref, idx_ref, wgt_ref, rank_ref, cnt_ref, base_ref):
    a = jnp.dot(ysb_ref[0, rows, :], wsb_ref[...], preferred_element_type=jnp.float32)
    b = jnp.dot(ydf_ref[0, rows, :], wdf_ref[...], preferred_element_type=jnp.float32)
    g = gates_ref[0, rows, :].astype(jnp.float32)
    merged = g[:, :D_MODEL] * a + g[:, D_MODEL:] * b
    mix = jnp.dot(merged.astype(jnp.bfloat16), wout_ref[...], preferred_element_type=jnp.float32)
    x1 = x_ref[0, rows, :] + mod[2:3] * (_rms(mix) * gpost_ref[...])
    x1_ref[0, rows, :] = x1
    h2 = _rms(x1) * gpre_ref[...]
    h2 = h2 * (1.0 + mod[4:5]) + mod[3:4]
    half = D_MODEL // 2
    h2_ref[0, rows, :] = _pack_bf16_pair(h2[:, :half], h2[:, half:])

    hh = h2.astype(jnp.bfloat16)
    hl = (h2 - hh.astype(jnp.float32)).astype(jnp.bfloat16)
    logits = (jnp.dot(hh, wrh_ref[...], preferred_element_type=jnp.float32)
              + jnp.dot(hh, wrl_ref[...], preferred_element_type=jnp.float32)
              + jnp.dot(hl, wrh_ref[...], preferred_element_type=jnp.float32)
              + br_ref[...])
    lane = lax.broadcasted_iota(jnp.int32, logits.shape, 1)
    lanef = lane.astype(jnp.float32)
    vals, idxs = [], []
    cur = logits
    for _ in range(TOP_K):
        mx = jnp.max(cur, axis=-1, keepdims=True)
        ix = jnp.min(jnp.where(cur == mx, lanef, float(LANES)), axis=-1, keepdims=True)
        cur = jnp.where(lanef == ix, -jnp.inf, cur)
        vals.append(mx)
        idxs.append(ix)
    es = [jnp.exp(v - vals[0]) for v in vals]
    den = es[0] + es[1] + es[2] + es[3]
    oi = jnp.zeros(logits.shape, jnp.float32)
    ow = jnp.zeros(logits.shape, jnp.float32)
    for k in range(TOP_K):
        oi = jnp.where(lane == k, idxs[k], oi)
        ow = jnp.where(lane == k, es[k] / den, ow)
    idx_ref[0, rows, :] = oi.astype(jnp.int32)
    wgt_ref[0, rows, :] = ow

    ts = logits.shape[0]
    member = jnp.zeros(logits.shape, jnp.float32)
    for k in range(TOP_K):
        member = member + (lanef == idxs[k]).astype(jnp.float32)
    rr = lax.broadcasted_iota(jnp.int32, (ts, ts), 0)
    cc = lax.broadcasted_iota(jnp.int32, (ts, ts), 1)
    lower = jnp.where(cc < rr, 1.0, 0.0).astype(jnp.bfloat16)
    before = jnp.dot(lower, member.astype(jnp.bfloat16), preferred_element_type=jnp.float32)
    base = base_ref[0:1, :]
    rank_all = before + base
    orank = jnp.zeros(logits.shape, jnp.float32)
    for k in range(TOP_K):
        rk = jnp.sum(jnp.where(lanef == idxs[k], rank_all, 0.0), axis=-1, keepdims=True)
        orank = jnp.where(lane == k, rk, orank)
    rank_ref[0, rows, :] = orank.astype(jnp.int32)
    new_base = base + jnp.sum(member, axis=0, keepdims=True)
    base_ref[...] = jnp.broadcast_to(new_base, base_ref.shape)
    cnt_ref[...] = jnp.broadcast_to(new_base, cnt_ref.shape).astype(jnp.int32)


def _merge_router(ysb, ydf, main, x, mod3, wsb, wdf, wout, gpost, gpre, wrh, wrl, br, ts):
    bsz, seq, _ = x.shape
    const = lambda b, i: (0, 0)
    return pl.pallas_call(
        _merge_router_kernel,
        grid=(bsz, seq // ts),
        in_specs=[
            pl.BlockSpec((1, ts, SB_WIDTH), lambda b, i: (b, i, 0)),
            pl.BlockSpec((1, ts, DIFF_V_WIDTH), lambda b, i: (b, i, 0)),
            pl.BlockSpec((1, ts, 2 * D_MODEL), lambda b, i: (b, i, GATE_COL0 // (2 * D_MODEL))),
            pl.BlockSpec((1, ts, D_MODEL), lambda b, i: (b, i, 0)),
            pl.BlockSpec((1, N_MOD, D_MODEL), lambda b, i: (b, 0, 0)),
            pl.BlockSpec((SB_WIDTH, D_MODEL), const),
            pl.BlockSpec((DIFF_V_WIDTH, D_MODEL), const),
            pl.BlockSpec((D_MODEL, D_MODEL), const),
            pl.BlockSpec((1, D_MODEL), const),
            pl.BlockSpec((1, D_MODEL), const),
            pl.BlockSpec((D_MODEL, LANES), const),
            pl.BlockSpec((D_MODEL, LANES), const),
            pl.BlockSpec((1, LANES), const),
        ],
        out_specs=[
            pl.BlockSpec((1, ts, D_MODEL), lambda b, i: (b, i, 0)),
            pl.BlockSpec((1, ts, D_MODEL // 2), lambda b, i: (b, i, 0)),
            pl.BlockSpec((1, ts, LANES), lambda b, i: (b, i, 0)),
            pl.BlockSpec((1, ts, LANES), lambda b, i: (b, i, 0)),
            pl.BlockSpec((1, ts, LANES), lambda b, i: (b, i, 0)),
            pl.BlockSpec((SUBLANES, LANES), const),
        ],
        out_shape=[
            jax.ShapeDtypeStruct((bsz, seq, D_MODEL), jnp.float32),
            jax.ShapeDtypeStruct((bsz, seq, D_MODEL // 2), jnp.uint32),
            jax.ShapeDtypeStruct((bsz, seq, LANES), jnp.int32),
            jax.ShapeDtypeStruct((bsz, seq, LANES), jnp.float32),
            jax.ShapeDtypeStruct((bsz, seq, LANES), jnp.int32),
            jax.ShapeDtypeStruct((SUBLANES, LANES), jnp.int32),
        ],
        scratch_shapes=[pltpu.VMEM((SUBLANES, LANES), jnp.float32)],
        compiler_params=_cparams(("arbitrary", "arbitrary")),
        name="merge_router",
    )(ysb, ydf, main, x, mod3, wsb, wdf, wout, gpost, gpre, wrh, wrl, br)


def _sc_gather_rows(table, idx):
    n = idx.shape[0]
    width = table.shape[1]
    n_workers = SC_CORES * SC_SUBCORES
    per_worker = n // n_workers
    n_chunks = per_worker // SC_GATHER_ROWS
    assert n_chunks * SC_GATHER_ROWS * n_workers == n
    mesh = plsc.VectorSubcoreMesh(core_axis_name="c", subcore_axis_name="s",
                                  num_cores=SC_CORES, num_subcores=SC_SUBCORES)

    def body(table_hbm, idx_hbm, out_hbm, idx_v, rows_v, sem):
        wid = lax.axis_index("s") * SC_CORES + lax.axis_index("c")
        base = wid * per_worker

        @pl.loop(0, n_chunks)
        def _(ci):
            off = pl.multiple_of(base + ci * SC_GATHER_ROWS, SC_GATHER_ROWS)
            pltpu.sync_copy(idx_hbm.at[pl.ds(off, SC_GATHER_ROWS)], idx_v)
            pltpu.async_copy(table_hbm.at[idx_v], rows_v, sem).wait()
            pltpu.sync_copy(rows_v, out_hbm.at[pl.ds(off, SC_GATHER_ROWS)])

    return pl.kernel(
        body,
        out_type=jax.ShapeDtypeStruct((n, width), table.dtype),
        mesh=mesh,
        scratch_types=[
            pltpu.VMEM((SC_GATHER_ROWS,), jnp.int32),
            pltpu.VMEM((SC_GATHER_ROWS, width), table.dtype),
            pltpu.SemaphoreType.DMA,
        ],
        name="sc_gather_rows",
    )(table, idx)


def _moe_ffn_kernel(te_ref, nt_ref, x_ref, wgu_ref, bgu_ref, wd_ref, bd_ref, o_ref,
                    wgu_bf, wd_bf):
    i = pl.program_id(0)
    n_valid = nt_ref[0]

    new_expert = jnp.logical_or(i == 0, te_ref[i] != te_ref[jnp.maximum(i - 1, 0)])

    @pl.when(jnp.logical_and(i < n_valid, new_expert))
    def _():
        wgu_bf[...] = wgu_ref[0].astype(jnp.bfloat16)
        wd_bf[...] = wd_ref[0].astype(jnp.bfloat16)

    @pl.when(i < n_valid)
    def _():
        xb = _unpack_bf16_pair(x_ref[...]).astype(jnp.bfloat16)
        gu = jnp.dot(xb, wgu_bf[...], preferred_element_type=jnp.float32) + bgu_ref[0]
        gate = jnp.minimum(gu[:, :D_EXPERT], SWIGLU_LIMIT)
        up = jnp.clip(gu[:, D_EXPERT:], -SWIGLU_LIMIT, SWIGLU_LIMIT)
        act = (up + 1.0) * (gate * jax.nn.sigmoid(SWIGLU_ALPHA * gate))
        out = jnp.dot(act.astype(jnp.bfloat16), wd_bf[...],
                      preferred_element_type=jnp.float32) + bd_ref[0]
        half = D_MODEL // 2
        o_ref[...] = _pack_bf16_pair(out[:, :half], out[:, half:])

    @pl.when(i >= n_valid)
    def _():
        o_ref[...] = jnp.zeros_like(o_ref)


def _moe_ffn(tile_expert, n_valid, xg, wgu, bgu, wd, bd, tm):
    n_tiles = xg.shape[0] // tm
    grid_spec = pltpu.PrefetchScalarGridSpec(
        num_scalar_prefetch=2,
        grid=(n_tiles,),
        in_specs=[
            pl.BlockSpec((tm, D_MODEL // 2), lambda i, te, nt: (jnp.minimum(i, nt[0] - 1), 0)),
            pl.BlockSpec((1, D_MODEL, 2 * D_EXPERT), lambda i, te, nt: (te[i], 0, 0)),
            pl.BlockSpec((1, 1, 2 * D_EXPERT), lambda i, te, nt: (te[i], 0, 0)),
            pl.BlockSpec((1, D_EXPERT, D_MODEL), lambda i, te, nt: (te[i], 0, 0)),
            pl.BlockSpec((1, 1, D_MODEL), lambda i, te, nt: (te[i], 0, 0)),
        ],
        out_specs=pl.BlockSpec((tm, D_MODEL // 2), lambda i, te, nt: (i, 0)),
        scratch_shapes=[
            pltpu.VMEM((D_MODEL, 2 * D_EXPERT), jnp.bfloat16),
            pltpu.VMEM((D_EXPERT, D_MODEL), jnp.bfloat16),
        ],
    )
    return pl.pallas_call(
        _moe_ffn_kernel,
        grid_spec=grid_spec,
        out_shape=jax.ShapeDtypeStruct((n_tiles * tm, D_MODEL // 2), jnp.uint32),
        compiler_params=_cparams(("arbitrary",)),
        name="moe_ffn",
    )(tile_expert, n_valid, xg, wgu, bgu.reshape(N_EXPERTS, 1, -1), wd,
      bd.reshape(N_EXPERTS, 1, -1))


def _moe_combine_kernel(rows_ref, wgt_ref, x1_ref, mod_ref, g_ref, o_ref):
    ts = x1_ref.shape[1]
    w = wgt_ref[0]
    y = jnp.zeros(x1_ref.shape[1:], jnp.float32)
    for k in range(TOP_K):
        y = y + w[:, k:k + 1] * _unpack_bf16_pair(rows_ref[k * ts:(k + 1) * ts, :])
    mod = mod_ref[0]
    o_ref[0] = x1_ref[0] + mod[5:6] * (_rms(y) * g_ref[...])


def _moe_combine(rows, wgt, x1, mod3, g_post, ts):
    bsz, seq, _ = x1.shape
    per_b = seq // ts
    return pl.pallas_call(
        _moe_combine_kernel,
        grid=(bsz, per_b),
        in_specs=[
            pl.BlockSpec((TOP_K * ts, D_MODEL // 2), lambda b, i: (b * per_b + i, 0)),
            pl.BlockSpec((1, ts, LANES), lambda b, i: (b, i, 0)),
            pl.BlockSpec((1, ts, D_MODEL), lambda b, i: (b, i, 0)),
            pl.BlockSpec((1, N_MOD, D_MODEL), lambda b, i: (b, 0, 0)),
            pl.BlockSpec((1, D_MODEL), lambda b, i: (0, 0)),
        ],
        out_specs=pl.BlockSpec((1, ts, D_MODEL), lambda b, i: (b, i, 0)),
        out_shape=jax.ShapeDtypeStruct((bsz, seq, D_MODEL), jnp.float32),
        compiler_params=_cparams(("arbitrary", "arbitrary")),
        name="moe_combine",
    )(rows, wgt, x1, mod3, g_post)


def _routing(top_idx, rank, counts, tm, n_tiles):
    n_tok = top_idx.shape[0]
    padded = ((counts + tm - 1) // tm) * tm
    start = jnp.cumsum(counts) - counts
    pend = jnp.cumsum(padded)
    pstart = pend - padded
    onehot = top_idx[:, :, None] == jnp.arange(N_EXPERTS, dtype=jnp.int32)[None, None, :]
    pos = rank + jnp.sum(jnp.where(onehot, pstart[None, None, :], 0), axis=-1)
    pair_id = jnp.arange(n_tok * TOP_K, dtype=jnp.int32)
    _, order = lax.sort_key_val(pos.reshape(-1), pair_id)
    n_valid = (pend[-1] // tm).astype(jnp.int32)
    tile_row0 = jnp.arange(n_tiles, dtype=jnp.int32) * tm
    tile_expert = jnp.minimum(
        jnp.sum((tile_row0[:, None] >= pend[None, :]).astype(jnp.int32), axis=1), N_EXPERTS - 1)
    te_oh = tile_expert[:, None] == jnp.arange(N_EXPERTS, dtype=jnp.int32)[None, :]
    t_pstart = jnp.sum(jnp.where(te_oh, pstart[None, :], 0), axis=1)
    t_start = jnp.sum(jnp.where(te_oh, start[None, :], 0), axis=1)
    t_count = jnp.sum(jnp.where(te_oh, counts[None, :], 0), axis=1)
    off = (tile_row0 - t_pstart)[:, None] + jnp.arange(tm, dtype=jnp.int32)[None, :]
    valid = off < t_count[:, None]
    src = jnp.clip(t_start[:, None] + off, 0, n_tok * TOP_K - 1)
    spread = (tile_row0[:, None] + jnp.arange(tm, dtype=jnp.int32)[None, :]) % n_tok
    row_token = jnp.where(valid, order[src] // TOP_K, spread).astype(jnp.int32)
    tile_expert = tile_expert[jnp.minimum(jnp.arange(n_tiles), jnp.maximum(n_valid - 1, 0))]
    return (pos.astype(jnp.int32), row_token.reshape(n_tiles * tm), tile_expert.astype(jnp.int32),
            n_valid.reshape(1))


def _alibi_slopes(n_heads):
    return 2.0 ** (-8.0 * jnp.arange(1, n_heads + 1, dtype=jnp.float32) / n_heads)


def _layer(x, c, w_mod, b_mod, g_pre_mix, g_post_mix, w_in, lamv, g_subln, w_branch_sb,
           w_branch_diff, w_out, g_pre_ffn, g_post_ffn, w_router, b_router, w_gate_up,
           b_gate_up, w_down, b_down, *, ts_in, tq, ts_merge, tm, ts_comb):
    bsz, seq, d = x.shape
    n_tok = bsz * seq
    bf = jnp.bfloat16

    mod, lam = _mod_proj(c, w_mod, b_mod, lamv)
    mod3 = mod.reshape(bsz, N_MOD, d)

    o_vsb = 2 * SB_WIDTH
    o_qdf = 3 * SB_WIDTH
    o_vdf = o_qdf + 2 * DIFF_QK_WIDTH
    o_g = o_vdf + DIFF_V_WIDTH
    w_main = jnp.concatenate([w_in[:, :o_vsb], w_in[:, o_qdf:o_vdf], w_in[:, o_g:]], axis=1).astype(bf)
    w_vt = jnp.concatenate([w_in[:, o_vsb:o_qdf], w_in[:, o_vdf:o_g]], axis=1).T.astype(bf)

    main, vt = _in_proj(x, mod3, g_pre_mix.reshape(1, d), w_main, w_vt, ts_in, tq)
    y_sb = _sb_attn(main, vt, tq, tq)
    y_df = _diff_attn(main, vt, _alibi_slopes(DIFF_HEADS), lam,
                      g_subln.reshape(DIFF_V_DIM, 1), tq, tq)

    wr = jnp.zeros((d, LANES), jnp.float32).at[:, :N_EXPERTS].set(w_router)
    wrh = wr.astype(bf)
    wrl = (wr - wrh.astype(jnp.float32)).astype(bf)
    br = jnp.full((1, LANES), NEG_BIG, jnp.float32).at[0, :N_EXPERTS].set(b_router)
    x1, h2p, top_idx, top_w, rank, counts = _merge_router(
        y_sb, y_df, main, x, mod3, w_branch_sb.astype(bf), w_branch_diff.astype(bf),
        w_out.astype(bf), g_post_mix.reshape(1, d), g_pre_ffn.reshape(1, d), wrh, wrl, br, ts_merge)

    n_tiles = (n_tok * TOP_K) // tm + N_EXPERTS
    pos, row_token, tile_expert, n_valid = _routing(
        top_idx.reshape(n_tok, LANES)[:, :TOP_K], rank.reshape(n_tok, LANES)[:, :TOP_K],
        counts[0, :N_EXPERTS], tm, n_tiles)
    xg = _sc_gather_rows(h2p.reshape(n_tok, d // 2), row_token)
    rows = _moe_ffn(tile_expert, n_valid, xg, w_gate_up, b_gate_up, w_down, b_down, tm)
    pos_steps = pos.reshape(n_tok // ts_comb, ts_comb, TOP_K).swapaxes(1, 2).reshape(n_tok * TOP_K)
    tok_rows = _sc_gather_rows(rows, pos_steps)
    return _moe_combine(tok_rows, top_w, x1, mod3, g_post_ffn.reshape(1, d), ts_comb)


def kernel(x, c, w_mod, b_mod, g_pre_mix, g_post_mix, w_in, lambda_q1, lambda_k1, lambda_q2,
           lambda_k2, g_subln, w_branch_sb, w_branch_diff, w_out, g_pre_ffn, g_post_ffn,
           w_router, b_router, w_gate_up, b_gate_up, w_down, b_down):
    depth = w_mod.shape[0]
    for l in range(depth):
        lamv = jnp.stack([lambda_q1[l], lambda_k1[l], lambda_q2[l], lambda_k2[l]])
        x = _layer(x, c, w_mod[l], b_mod[l], g_pre_mix[l], g_post_mix[l], w_in[l], lamv,
                   g_subln[l], w_branch_sb[l], w_branch_diff[l], w_out[l], g_pre_ffn[l],
                   g_post_ffn[l], w_router[l], b_router[l], w_gate_up[l], b_gate_up[l],
                   w_down[l], b_down[l],
                   ts_in=512, tq=256, ts_merge=512, tm=512, ts_comb=256)
    return x
```

```python
import functools
import math

import jax
import jax.numpy as jnp
from jax import lax
from jax.experimental import pallas as pl
from jax.experimental.pallas import tpu as pltpu
from jax.experimental.pallas import tpu_sc as plsc

D_MODEL = 1024
SB_HEADS = 8
SB_HEAD_DIM = 64
SB_WIDTH = SB_HEADS * SB_HEAD_DIM
DIFF_HEADS = 4
DIFF_HEAD_DIM = 64
DIFF_V_DIM = 2 * DIFF_HEAD_DIM
DIFF_QK_WIDTH = DIFF_HEADS * 2 * DIFF_HEAD_DIM
DIFF_V_WIDTH = DIFF_HEADS * DIFF_V_DIM
N_EXPERTS = 32
TOP_K = 4
D_EXPERT = D_MODEL
SWIGLU_LIMIT = 7.0
SWIGLU_ALPHA = 1.702
RMS_EPS = 1e-6
N_MOD = 6
LAM_INIT = 0.8 - 0.6 * math.exp(-0.3 * 0)

LANES = 128
SUBLANES = 8
NEG_BIG = -1e30
EXP_ZERO_MARGIN = 110.0
NORM_SLACK = 1.01
SC_CORES = 2
SC_SUBCORES = 16
SC_GATHER_ROWS = 128
MERGE_SUB = 256

MAIN_WIDTH = 2 * SB_WIDTH + 2 * DIFF_QK_WIDTH + 2 * D_MODEL
VT_ROWS = SB_WIDTH + DIFF_V_WIDTH
COLBLK_K_SB = SB_WIDTH // LANES
COLBLK_Q_DF = 2 * SB_WIDTH // LANES
COLBLK_K_DF = COLBLK_Q_DF + DIFF_QK_WIDTH // LANES
GATE_COL0 = 2 * SB_WIDTH + 2 * DIFF_QK_WIDTH

VMEM_LIMIT = 56 * 1024 * 1024


def _cparams(sem, vmem=VMEM_LIMIT):
    return pltpu.CompilerParams(dimension_semantics=sem, vmem_limit_bytes=vmem)


def _rms(x):
    return x * lax.rsqrt(jnp.mean(x * x, axis=-1, keepdims=True) + RMS_EPS)


def _mod_kernel(c_ref, w_ref, b_ref, lamv_ref, mod_ref, lam_ref):
    c = c_ref[...]
    ca = c * jax.nn.sigmoid(c)
    mod_ref[...] = jnp.dot(ca, w_ref[...], preferred_element_type=jnp.float32,
                           precision=lax.Precision.HIGHEST) + b_ref[...]
    lv = lamv_ref[...]
    s1 = jnp.sum(lv[0:1] * lv[1:2], axis=-1, keepdims=True)
    s2 = jnp.sum(lv[2:3] * lv[3:4], axis=-1, keepdims=True)
    lam = jnp.exp(s1) - jnp.exp(s2) + LAM_INIT
    lam_ref[...] = jnp.broadcast_to(lam, lam_ref.shape)


def _mod_proj(c, w_mod, b_mod, lamv):
    bsz = c.shape[0]
    tn = 1536
    n = w_mod.shape[1]
    return pl.pallas_call(
        _mod_kernel,
        grid=(n // tn,),
        in_specs=[
            pl.BlockSpec((bsz, D_MODEL), lambda j: (0, 0)),
            pl.BlockSpec((D_MODEL, tn), lambda j: (0, j)),
            pl.BlockSpec((1, tn), lambda j: (0, j)),
            pl.BlockSpec((4, DIFF_HEAD_DIM), lambda j: (0, 0)),
        ],
        out_specs=[
            pl.BlockSpec((bsz, tn), lambda j: (0, j)),
            pl.BlockSpec((SUBLANES, LANES), lambda j: (0, 0)),
        ],
        out_shape=[
            jax.ShapeDtypeStruct((bsz, n), jnp.float32),
            jax.ShapeDtypeStruct((SUBLANES, LANES), jnp.float32),
        ],
        compiler_params=_cparams(("arbitrary",)),
        name="mod_proj",
    )(c, w_mod, b_mod.reshape(1, n), lamv)


IN_CHUNK = 1024


def _in_proj_kernel(x_ref, mod_ref, g_ref, wm_ref, wvt_ref, main_ref, vt_ref, h_scr, *, tk):
    x = x_ref[0]
    mod = mod_ref[0]
    h = _rms(x) * g_ref[...]
    h = h * (1.0 + mod[1:2]) + mod[0:1]
    hb = h.astype(jnp.bfloat16)
    groups = tk // SUBLANES
    cols = []
    for ct in range(D_MODEL // LANES):
        h_scr[ct] = h[:, ct * LANES:(ct + 1) * LANES]
        pieces = []
        for blk in range(h.shape[0] // tk):
            for g in range(groups):
                pieces.append(h_scr[ct, pl.ds(blk * tk + g, SUBLANES, stride=groups), :])
        cols.append(jnp.concatenate(pieces, axis=0))
    hpb = jnp.concatenate(cols, axis=1).astype(jnp.bfloat16)

    half = IN_CHUNK // 2
    for ci in range(MAIN_WIDTH // IN_CHUNK):
        c0 = ci * IN_CHUNK
        if c0 == 0:
            q = jnp.dot(hb, wm_ref[:, :half], preferred_element_type=jnp.float32)
            k = jnp.dot(hpb, wm_ref[:, half:IN_CHUNK], preferred_element_type=jnp.float32)
            main_ref[0, :, :half] = (q * 0.0625).astype(jnp.bfloat16)
            main_ref[0, :, half:IN_CHUNK] = k.astype(jnp.bfloat16)
            continue
        p = jnp.dot(hb, wm_ref[:, c0:c0 + IN_CHUNK], preferred_element_type=jnp.float32)
        if c0 < GATE_COL0:
            main_ref[0, :, c0:c0 + half] = (p[:, :half] * 0.125).astype(jnp.bfloat16)
            main_ref[0, :, c0 + half:c0 + IN_CHUNK] = p[:, half:].astype(jnp.bfloat16)
        else:
            main_ref[0, :, c0:c0 + IN_CHUNK] = jax.nn.sigmoid(p).astype(jnp.bfloat16)
    nt = (((1,), (1,)), ((), ()))
    vt_sb = lax.dot_general(wvt_ref[:SB_WIDTH, :], hpb, nt, preferred_element_type=jnp.float32)
    vt_df = lax.dot_general(wvt_ref[SB_WIDTH:, :], hb, nt, preferred_element_type=jnp.float32)
    vt_ref[0, :SB_WIDTH, :] = vt_sb.astype(jnp.bfloat16)
    vt_ref[0, SB_WIDTH:, :] = vt_df.astype(jnp.bfloat16)


def _in_proj(x, mod3, g_pre, w_main, w_vt, ts, tk):
    bsz, seq, _ = x.shape
    assert ts % tk == 0
    return pl.pallas_call(
        functools.partial(_in_proj_kernel, tk=tk),
        grid=(bsz, seq // ts),
        in_specs=[
            pl.BlockSpec((1, ts, D_MODEL), lambda b, i: (b, i, 0)),
            pl.BlockSpec((1, N_MOD, D_MODEL), lambda b, i: (b, 0, 0)),
            pl.BlockSpec((1, D_MODEL), lambda b, i: (0, 0)),
            pl.BlockSpec((D_MODEL, MAIN_WIDTH), lambda b, i: (0, 0)),
            pl.BlockSpec((VT_ROWS, D_MODEL), lambda b, i: (0, 0)),
        ],
        out_specs=[
            pl.BlockSpec((1, ts, MAIN_WIDTH), lambda b, i: (b, i, 0)),
            pl.BlockSpec((1, VT_ROWS, ts), lambda b, i: (b, 0, i)),
        ],
        out_shape=[
            jax.ShapeDtypeStruct((bsz, seq, MAIN_WIDTH), jnp.bfloat16),
            jax.ShapeDtypeStruct((bsz, VT_ROWS, seq), jnp.bfloat16),
        ],
        scratch_shapes=[pltpu.VMEM((D_MODEL // LANES, ts, LANES), jnp.float32)],
        compiler_params=_cparams(("arbitrary", "arbitrary")),
        name="in_proj",
    )(x, mod3, g_pre, w_main, w_vt)


def _suffix_excl_prod8(tot):
    sub = lax.broadcasted_iota(jnp.int32, tot.shape, 0)
    x = jnp.where(sub < SUBLANES - 1, pltpu.roll(tot, SUBLANES - 1, 0), 1.0)
    for sh in (1, 2, 4):
        x = x * jnp.where(sub + sh < SUBLANES, pltpu.roll(x, SUBLANES - sh, 0), 1.0)
    return x


def _sb_scores(k_ref, q_heads, s_ref, slot, j, tk):
    kb = k_ref[0, pl.ds(pl.multiple_of(j * tk, tk), tk), :]
    for h in range(2):
        s_ref[slot, h] = lax.dot_general(kb, q_heads[h], (((1,), (1,)), ((), ())),
                                         preferred_element_type=jnp.float32)


def _sb_weights(zt, c8, ok, groups):
    tq = zt.shape[1]
    r = 0.5 - 0.5 * jnp.tanh(zt)
    if ok is not None:
        r = jnp.where(ok, r, 1.0)
    rg = [r[g * SUBLANES:(g + 1) * SUBLANES, :] for g in range(groups)]
    tot = rg[0]
    for g in range(1, groups):
        tot = tot * rg[g]
    p = c8 * _suffix_excl_prod8(tot)
    pieces = [None] * groups
    for g in range(groups - 1, -1, -1):
        pn = p * rg[g]
        pieces[g] = p - pn
        p = pn
    a = jnp.concatenate(pieces, axis=0).astype(jnp.bfloat16)
    return a, jnp.broadcast_to(p[0:1, :], (SUBLANES, tq))


def _sb_attn_kernel(q_ref, k_ref, v_ref, o_ref, acc_ref, c_ref, s_ref, *, tq, tk):
    i = pl.program_id(2)
    groups = tk // SUBLANES
    q2 = q_ref[0]
    lane = lax.broadcasted_iota(jnp.int32, q2.shape, 1)
    zero = jnp.zeros_like(q2)
    q_heads = (jnp.where(lane < SB_HEAD_DIM, q2, zero), jnp.where(lane < SB_HEAD_DIM, zero, q2))

    def step(j, slot, masked):
        _sb_scores(k_ref, q_heads, s_ref, 1 - slot, jnp.maximum(j - 1, 0), tk)
        if masked:
            row = lax.broadcasted_iota(jnp.int32, (tk, tq), 0)
            col = lax.broadcasted_iota(jnp.int32, (tk, tq), 1)
            ok = (row % SUBLANES) * groups + row // SUBLANES < col
        else:
            ok = None
        off = pl.multiple_of(j * tk, tk)
        ws = []
        for h in range(2):
            a, c_new = _sb_weights(s_ref[slot, h], c_ref[h], ok, groups)
            c_ref[h] = c_new
            ws.append(a)
        for h in range(2):
            vt_h = v_ref[0, h * SB_HEAD_DIM:(h + 1) * SB_HEAD_DIM, pl.ds(off, tk)]
            acc_ref[h] += jnp.dot(vt_h, ws[h], preferred_element_type=jnp.float32)

    acc_ref[...] = jnp.zeros_like(acc_ref)
    c_ref[...] = jnp.ones_like(c_ref)
    _sb_scores(k_ref, q_heads, s_ref, 0, i, tk)
    step(i, 0, True)

    def stick_left():
        return jnp.max(c_ref[...]) > 0.0

    def more(state):
        m, go = state
        return jnp.logical_and(m < i // 2, go)

    def pair(state):
        m, _ = state
        j = i - 1 - 2 * m
        step(j, 1, False)
        step(j - 1, 0, False)
        return m + 1, stick_left()

    m_done, go = lax.while_loop(more, pair, (jnp.int32(0), stick_left()))

    @pl.when(jnp.logical_and(jnp.logical_and(i % 2 == 1, m_done == i // 2), go))
    def _():
        step(0, 1, False)

    ot = jnp.concatenate([acc_ref[0], acc_ref[1]], axis=0)
    o_ref[0] = ot.T.astype(jnp.bfloat16)


def _sb_attn(main, vt, tq, tk):
    bsz, seq, _ = main.shape
    assert tq == tk
    kern = functools.partial(_sb_attn_kernel, tq=tq, tk=tk)
    return pl.pallas_call(
        kern,
        grid=(bsz, SB_WIDTH // LANES, seq // tq),
        in_specs=[
            pl.BlockSpec((1, tq, LANES), lambda b, p, i: (b, i, p)),
            pl.BlockSpec((1, seq, LANES), lambda b, p, i: (b, 0, COLBLK_K_SB + p)),
            pl.BlockSpec((1, LANES, seq), lambda b, p, i: (b, p, 0)),
        ],
        out_specs=pl.BlockSpec((1, tq, LANES), lambda b, p, i: (b, i, p)),
        out_shape=jax.ShapeDtypeStruct((bsz, seq, SB_WIDTH), jnp.bfloat16),
        scratch_shapes=[
            pltpu.VMEM((2, SB_HEAD_DIM, tq), jnp.float32),
            pltpu.VMEM((2, SUBLANES, tq), jnp.float32),
            pltpu.VMEM((2, 2, tk, tq), jnp.float32),
        ],
        compiler_params=_cparams(("arbitrary", "arbitrary", "arbitrary")),
        name="sb_attn",
    )(main, main, vt)


def _diff_attn_kernel(slopes_ref, inv_slopes_ref, q_ref, k_ref, v_ref, lam_ref, g_ref, o_ref,
                      acc_ref, m_ref, l_ref, s_ref, kn_ref, *, tq, tk):
    hd = pl.program_id(1)
    i = pl.program_id(2)
    slope = slopes_ref[hd]

    @pl.when(i == 0)
    def _():
        kf = k_ref[0].astype(jnp.float32)
        kn2 = jnp.max(jnp.sum(kf * kf, axis=-1, keepdims=True), axis=0, keepdims=True)
        kn_ref[...] = jnp.broadcast_to(kn2, kn_ref.shape)

    q2 = q_ref[0]
    lane = lax.broadcasted_iota(jnp.int32, q2.shape, 1)
    zero = jnp.zeros_like(q2)
    q_maps = (jnp.where(lane < DIFF_HEAD_DIM, q2, zero), jnp.where(lane < DIFF_HEAD_DIM, zero, q2))

    row = lax.broadcasted_iota(jnp.int32, (tk, tq), 0)
    col = lax.broadcasted_iota(jnp.int32, (tk, tq), 1)
    bias = slope * (row - col).astype(jnp.float32)

    def scores(slot, j):
        kb = k_ref[0, pl.ds(pl.multiple_of(j * tk, tk), tk), :]
        for m in range(2):
            s_ref[slot, m] = lax.dot_general(kb, q_maps[m], (((1,), (1,)), ((), ())),
                                             preferred_element_type=jnp.float32)

    def step(j, slot, masked):
        scores(1 - slot, jnp.maximum(j - 1, 0))
        off = pl.multiple_of(j * tk, tk)
        vtb = v_ref[0, :, pl.ds(off, tk)]
        cb = slope * ((j - i) * tk).astype(jnp.float32)
        ps, alphas = [], []
        for m in range(2):
            s = s_ref[slot, m] + bias
            if masked:
                s = jnp.where(row <= col, s, NEG_BIG)
            m_old = m_ref[m]
            m_new = jnp.maximum(m_old, jnp.max(s, axis=0, keepdims=True) + cb)
            alpha = jnp.exp(m_old - m_new)
            p = jnp.exp(s - (m_new - cb))
            l_ref[m] = alpha * l_ref[m] + jnp.sum(p, axis=0, keepdims=True)
            m_ref[m] = m_new
            ps.append(p.astype(jnp.bfloat16))
            alphas.append(alpha)
        for m in range(2):
            acc_ref[m] = alphas[m] * acc_ref[m] + jnp.dot(
                vtb, ps[m], preferred_element_type=jnp.float32)

    acc_ref[...] = jnp.zeros_like(acc_ref)
    m_ref[...] = jnp.full_like(m_ref, NEG_BIG)
    l_ref[...] = jnp.zeros_like(l_ref)
    scores(0, i)
    step(i, 0, True)

    qf = q2.astype(jnp.float32)
    qn2 = jnp.max(jnp.sum(qf * qf, axis=-1, keepdims=True), axis=0, keepdims=True)
    zabs = jnp.sqrt(qn2 * kn_ref[0:1, 0:1]) * NORM_SLACK
    reach = (EXP_ZERO_MARGIN + 2.0 * zabs) * inv_slopes_ref[hd]
    n_need = jnp.floor(jnp.minimum((reach - 1.0) * (1.0 / tk), 1e6)) + 1.0
    n_back = jnp.minimum(i, jnp.max(n_need).astype(jnp.int32))

    def pair(n, carry):
        j = i - 1 - 2 * n
        step(j, 1, False)
        step(j - 1, 0, False)
        return carry

    lax.fori_loop(0, n_back // 2, pair, 0)

    @pl.when(n_back % 2 == 1)
    def _():
        step(i - n_back, 1, False)

    lam = lam_ref[0:1, 0:1]
    o = acc_ref[0] / l_ref[0] - lam * (acc_ref[1] / l_ref[1])
    ms = jnp.mean(o * o, axis=0, keepdims=True)
    y = o * lax.rsqrt(ms + RMS_EPS) * g_ref[...] * (1.0 - LAM_INIT)
    o_ref[0] = y.T.astype(jnp.bfloat16)


def _diff_attn(main, vt, slopes, lam, g_col, tq, tk):
    bsz, seq, _ = main.shape
    assert tq == tk
    kern = functools.partial(_diff_attn_kernel, tq=tq, tk=tk)
    vrow0 = SB_WIDTH // LANES
    return pl.pallas_call(
        kern,
        grid=(bsz, DIFF_HEADS, seq // tq),
        in_specs=[
            pl.BlockSpec(memory_space=pltpu.SMEM),
            pl.BlockSpec(memory_space=pltpu.SMEM),
            pl.BlockSpec((1, tq, LANES), lambda b, h, i: (b, i, COLBLK_Q_DF + h)),
            pl.BlockSpec((1, seq, LANES), lambda b, h, i: (b, 0, COLBLK_K_DF + h)),
            pl.BlockSpec((1, DIFF_V_DIM, seq), lambda b, h, i: (b, vrow0 + h, 0)),
            pl.BlockSpec((SUBLANES, LANES), lambda b, h, i: (0, 0)),
            pl.BlockSpec((DIFF_V_DIM, 1), lambda b, h, i: (0, 0)),
        ],
        out_specs=pl.BlockSpec((1, tq, DIFF_V_DIM), lambda b, h, i: (b, i, h)),
        out_shape=jax.ShapeDtypeStruct((bsz, seq, DIFF_V_WIDTH), jnp.bfloat16),
        scratch_shapes=[
            pltpu.VMEM((2, DIFF_V_DIM, tq), jnp.float32),
            pltpu.VMEM((2, 1, tq), jnp.float32),
            pltpu.VMEM((2, 1, tq), jnp.float32),
            pltpu.VMEM((2, 2, tk, tq), jnp.float32),
            pltpu.VMEM((SUBLANES, LANES), jnp.float32),
        ],
        compiler_params=_cparams(("arbitrary", "arbitrary", "arbitrary")),
        name="diff_attn",
    )(slopes, 1.0 / slopes, main, main, vt, lam, g_col)


def _pack_bf16_pair(a, b):
    ab = pltpu.bitcast(a.astype(jnp.bfloat16).astype(jnp.float32), jnp.uint32)
    bb = pltpu.bitcast(b.astype(jnp.bfloat16).astype(jnp.float32), jnp.uint32)
    return ab | (bb >> 16)


def _unpack_bf16_pair(w):
    hi = pltpu.bitcast(w & jnp.uint32(0xFFFF0000), jnp.float32)
    lo = pltpu.bitcast(w << 16, jnp.float32)
    return jnp.concatenate([hi, lo], axis=1)


def _merge_router_kernel(ysb_ref, ydf_ref, gates_ref, x_ref, mod_ref, wsb_ref, wdf_ref, wout_ref,
                         gpost_ref, gpre_ref, wrh_ref, wrl_ref, br_ref,
                         x1_ref, h2_ref, idx_ref, wgt_ref, rank_ref, cnt_ref, base_ref):
    first = jnp.logical_and(pl.program_id(0) == 0, pl.program_id(1) == 0)

    @pl.when(first)
    def _():
        base_ref[...] = jnp.zeros_like(base_ref)

    mod = mod_ref[0]
    for sub in range(x_ref.shape[1] // MERGE_SUB):
        rows = slice(sub * MERGE_SUB, (sub + 1) * MERGE_SUB)
        _merge_router_rows(rows, mod, ysb_ref, ydf_ref, gates_ref, x_ref, wsb_ref, wdf_ref, wout_ref,
                           gpost_ref, gpre_ref, wrh_ref, wrl_ref, br_ref,
                           x1_ref, h2_ref, idx_ref, wgt_ref, rank_ref, cnt_ref, base_ref)


def _merge_router_rows(rows, mod, ysb_ref, ydf_ref, gates_ref, x_ref, wsb_ref, wdf_ref, wout_ref,
                       gpost_ref, gpre_ref, wrh_ref, wrl_ref, br_ref,
                       x1_ref, h2_ref, idx_ref, wgt_ref, rank_ref, cnt_ref, base_ref):
    a = jnp.dot(ysb_ref[0, rows, :], wsb_ref[...], preferred_element_type=jnp.float32)
    b = jnp.dot(ydf_ref[0, rows, :], wdf_ref[...], preferred_element_type=jnp.float32)
    g = gates_ref[0, rows, :].astype(jnp.float32)
    merged = g[:, :D_MODEL] * a + g[:, D_MODEL:] * b
    mix = jnp.dot(merged.astype(jnp.bfloat16), wout_ref[...], preferred_element_type=jnp.float32)
    x1 = x_ref[0, rows, :] + mod[2:3] * (_rms(mix) * gpost_ref[...])
    x1_ref[0, rows, :] = x1
    h2 = _rms(x1) * gpre_ref[...]
    h2 = h2 * (1.0 + mod[4:5]) + mod[3:4]
    half = D_MODEL // 2
    h2_ref[0, rows, :] = _pack_bf16_pair(h2[:, :half], h2[:, half:])

    hh = h2.astype(jnp.bfloat16)
    hl = (h2 - hh.astype(jnp.float32)).astype(jnp.bfloat16)
    logits = (jnp.dot(hh, wrh_ref[...], preferred_element_type=jnp.float32)
              + jnp.dot(hh, wrl_ref[...], preferred_element_type=jnp.float32)
              + jnp.dot(hl, wrh_ref[...], preferred_element_type=jnp.float32)
              + br_ref[...])
    lane = lax.broadcasted_iota(jnp.int32, logits.shape, 1)
    lanef = lane.astype(jnp.float32)
    vals, idxs = [], []
    cur = logits
    for _ in range(TOP_K):
        mx = jnp.max(cur, axis=-1, keepdims=True)
        ix = jnp.min(jnp.where(cur == mx, lanef, float(LANES)), axis=-1, keepdims=True)
        cur = jnp.where(lanef == ix, -jnp.inf, cur)
        vals.append(mx)
        idxs.append(ix)
    es = [jnp.exp(v - vals[0]) for v in vals]
    den = es[0] + es[1] + es[2] + es[3]
    oi = jnp.zeros(logits.shape, jnp.float32)
    ow = jnp.zeros(logits.shape, jnp.float32)
    for k in range(TOP_K):
        oi = jnp.where(lane == k, idxs[k], oi)
        ow = jnp.where(lane == k, es[k] / den, ow)
    idx_ref[0, rows, :] = oi.astype(jnp.int32)
    wgt_ref[0, rows, :] = ow

    ts = logits.shape[0]
    member = jnp.zeros(logits.shape, jnp.float32)
    for k in range(TOP_K):
        member = member + (lanef == idxs[k]).astype(jnp.float32)
    rr = lax.broadcasted_iota(jnp.int32, (ts, ts), 0)
    cc = lax.broadcasted_iota(jnp.int32, (ts, ts), 1)
    lower = jnp.where(cc < rr, 1.0, 0.0).astype(jnp.bfloat16)
    before = jnp.dot(lower, member.astype(jnp.bfloat16), preferred_element_type=jnp.float32)
    base = base_ref[0:1, :]
    rank_all = before + base
    orank = jnp.zeros(logits.shape, jnp.float32)
    for k in range(TOP_K):
        rk = jnp.sum(jnp.where(lanef == idxs[k], rank_all, 0.0), axis=-1, keepdims=True)
        orank = jnp.where(lane == k, rk, orank)
    rank_ref[0, rows, :] = orank.astype(jnp.int32)
    new_base = base + jnp.sum(member, axis=0, keepdims=True)
    base_ref[...] = jnp.broadcast_to(new_base, base_ref.shape)
    cnt_ref[...] = jnp.broadcast_to(new_base, cnt_ref.shape).astype(jnp.int32)


def _merge_router(ysb, ydf, main, x, mod3, wsb, wdf, wout, gpost, gpre, wrh, wrl, br, ts):
    bsz, seq, _ = x.shape
    const = lambda b, i: (0, 0)
    return pl.pallas_call(
        _merge_router_kernel,
        grid=(bsz, seq // ts),
        in_specs=[
            pl.BlockSpec((1, ts, SB_WIDTH), lambda b, i: (b, i, 0)),
            pl.BlockSpec((1, ts, DIFF_V_WIDTH), lambda b, i: (b, i, 0)),
            pl.BlockSpec((1, ts, 2 * D_MODEL), lambda b, i: (b, i, GATE_COL0 // (2 * D_MODEL))),
            pl.BlockSpec((1, ts, D_MODEL), lambda b, i: (b, i, 0)),
            pl.BlockSpec((1, N_MOD, D_MODEL), lambda b, i: (b, 0, 0)),
            pl.BlockSpec((SB_WIDTH, D_MODEL), const),
            pl.BlockSpec((DIFF_V_WIDTH, D_MODEL), const),
            pl.BlockSpec((D_MODEL, D_MODEL), const),
            pl.BlockSpec((1, D_MODEL), const),
            pl.BlockSpec((1, D_MODEL), const),
            pl.BlockSpec((D_MODEL, LANES), const),
            pl.BlockSpec((D_MODEL, LANES), const),
            pl.BlockSpec((1, LANES), const),
        ],
        out_specs=[
            pl.BlockSpec((1, ts, D_MODEL), lambda b, i: (b, i, 0)),
            pl.BlockSpec((1, ts, D_MODEL // 2), lambda b, i: (b, i, 0)),
            pl.BlockSpec((1, ts, LANES), lambda b, i: (b, i, 0)),
            pl.BlockSpec((1, ts, LANES), lambda b, i: (b, i, 0)),
            pl.BlockSpec((1, ts, LANES), lambda b, i: (b, i, 0)),
            pl.BlockSpec((SUBLANES, LANES), const),
        ],
        out_shape=[
            jax.ShapeDtypeStruct((bsz, seq, D_MODEL), jnp.float32),
            jax.ShapeDtypeStruct((bsz, seq, D_MODEL // 2), jnp.uint32),
            jax.ShapeDtypeStruct((bsz, seq, LANES), jnp.int32),
            jax.ShapeDtypeStruct((bsz, seq, LANES), jnp.float32),
            jax.ShapeDtypeStruct((bsz, seq, LANES), jnp.int32),
            jax.ShapeDtypeStruct((SUBLANES, LANES), jnp.int32),
        ],
        scratch_shapes=[pltpu.VMEM((SUBLANES, LANES), jnp.float32)],
        compiler_params=_cparams(("arbitrary", "arbitrary")),
        name="merge_router",
    )(ysb, ydf, main, x, mod3, wsb, wdf, wout, gpost, gpre, wrh, wrl, br)


def _sc_gather_rows(table, idx):
    n = idx.shape[0]
    width = table.shape[1]
    n_workers = SC_CORES * SC_SUBCORES
    per_worker = n // n_workers
    n_chunks = per_worker // SC_GATHER_ROWS
    assert n_chunks * SC_GATHER_ROWS * n_workers == n
    mesh = plsc.VectorSubcoreMesh(core_axis_name="c", subcore_axis_name="s",
                                  num_cores=SC_CORES, num_subcores=SC_SUBCORES)

    def body(table_hbm, idx_hbm, out_hbm, idx_v, rows_v, sem):
        wid = lax.axis_index("s") * SC_CORES + lax.axis_index("c")
        base = wid * per_worker

        @pl.loop(0, n_chunks)
        def _(ci):
            off = pl.multiple_of(base + ci * SC_GATHER_ROWS, SC_GATHER_ROWS)
            pltpu.sync_copy(idx_hbm.at[pl.ds(off, SC_GATHER_ROWS)], idx_v)
            pltpu.async_copy(table_hbm.at[idx_v], rows_v, sem).wait()
            pltpu.sync_copy(rows_v, out_hbm.at[pl.ds(off, SC_GATHER_ROWS)])

    return pl.kernel(
        body,
        out_type=jax.ShapeDtypeStruct((n, width), table.dtype),
        mesh=mesh,
        scratch_types=[
            pltpu.VMEM((SC_GATHER_ROWS,), jnp.int32),
            pltpu.VMEM((SC_GATHER_ROWS, width), table.dtype),
            pltpu.SemaphoreType.DMA,
        ],
        name="sc_gather_rows",
    )(table, idx)


def _sc_scatter_rows(src, pos_kmajor, n_rows):
    n_tok, width = src.shape
    n_workers = SC_CORES * SC_SUBCORES
    per_worker = n_tok // n_workers
    n_chunks = per_worker // SC_GATHER_ROWS
    assert n_chunks * SC_GATHER_ROWS * n_workers == n_tok
    mesh = plsc.VectorSubcoreMesh(core_axis_name="c", subcore_axis_name="s",
                                  num_cores=SC_CORES, num_subcores=SC_SUBCORES)

    def body(src_hbm, idx_hbm, out_hbm, idx_v, rows_v):
        wid = lax.axis_index("s") * SC_CORES + lax.axis_index("c")
        base = wid * per_worker

        @pl.loop(0, n_chunks)
        def _(ci):
            off = pl.multiple_of(base + ci * SC_GATHER_ROWS, SC_GATHER_ROWS)
            pltpu.sync_copy(src_hbm.at[pl.ds(off, SC_GATHER_ROWS)], rows_v)
            for k in range(TOP_K):
                koff = pl.multiple_of(k * n_tok + off, SC_GATHER_ROWS)
                pltpu.sync_copy(idx_hbm.at[pl.ds(koff, SC_GATHER_ROWS)], idx_v)
                pltpu.sync_copy(rows_v, out_hbm.at[idx_v])

    return pl.kernel(
        body,
        out_type=jax.ShapeDtypeStruct((n_rows, width), src.dtype),
        mesh=mesh,
        scratch_types=[
            pltpu.VMEM((SC_GATHER_ROWS,), jnp.int32),
            pltpu.VMEM((SC_GATHER_ROWS, width), src.dtype),
        ],
        name="sc_scatter_rows",
    )(src, pos_kmajor)


def _moe_ffn_kernel(te_ref, nt_ref, x_ref, wgu_ref, bgu_ref, wd_ref, bd_ref, o_ref,
                    wgu_bf, wd_bf):
    i = pl.program_id(0)
    n_valid = nt_ref[0]

    new_expert = jnp.logical_or(i == 0, te_ref[i] != te_ref[jnp.maximum(i - 1, 0)])

    @pl.when(jnp.logical_and(i < n_valid, new_expert))
    def _():
        wgu_bf[...] = wgu_ref[0].astype(jnp.bfloat16)
        wd_bf[...] = wd_ref[0].astype(jnp.bfloat16)

    @pl.when(i < n_valid)
    def _():
        xb = _unpack_bf16_pair(x_ref[...]).astype(jnp.bfloat16)
        gu = jnp.dot(xb, wgu_bf[...], preferred_element_type=jnp.float32) + bgu_ref[0]
        gate = jnp.minimum(gu[:, :D_EXPERT], SWIGLU_LIMIT)
        up = jnp.clip(gu[:, D_EXPERT:], -SWIGLU_LIMIT, SWIGLU_LIMIT)
        act = (up + 1.0) * (gate * jax.nn.sigmoid(SWIGLU_ALPHA * gate))
        out = jnp.dot(act.astype(jnp.bfloat16), wd_bf[...],
                      preferred_element_type=jnp.float32) + bd_ref[0]
        half = D_MODEL // 2
        o_ref[...] = _pack_bf16_pair(out[:, :half], out[:, half:])

    @pl.when(i >= n_valid)
    def _():
        o_ref[...] = jnp.zeros_like(o_ref)


def _moe_ffn(tile_expert, n_valid, xg, wgu, bgu, wd, bd, tm):
    n_tiles = xg.shape[0] // tm
    grid_spec = pltpu.PrefetchScalarGridSpec(
        num_scalar_prefetch=2,
        grid=(n_tiles,),
        in_specs=[
            pl.BlockSpec((tm, D_MODEL // 2), lambda i, te, nt: (jnp.minimum(i, nt[0] - 1), 0)),
            pl.BlockSpec((1, D_MODEL, 2 * D_EXPERT), lambda i, te, nt: (te[i], 0, 0)),
            pl.BlockSpec((1, 1, 2 * D_EXPERT), lambda i, te, nt: (te[i], 0, 0)),
            pl.BlockSpec((1, D_EXPERT, D_MODEL), lambda i, te, nt: (te[i], 0, 0)),
            pl.BlockSpec((1, 1, D_MODEL), lambda i, te, nt: (te[i], 0, 0)),
        ],
        out_specs=pl.BlockSpec((tm, D_MODEL // 2), lambda i, te, nt: (i, 0)),
        scratch_shapes=[
            pltpu.VMEM((D_MODEL, 2 * D_EXPERT), jnp.bfloat16),
            pltpu.VMEM((D_EXPERT, D_MODEL), jnp.bfloat16),
        ],
    )
    return pl.pallas_call(
        _moe_ffn_kernel,
        grid_spec=grid_spec,
        out_shape=jax.ShapeDtypeStruct((n_tiles * tm, D_MODEL // 2), jnp.uint32),
        compiler_params=_cparams(("arbitrary",)),
        name="moe_ffn",
    )(tile_expert, n_valid, xg, wgu, bgu.reshape(N_EXPERTS, 1, -1), wd,
      bd.reshape(N_EXPERTS, 1, -1))


def _moe_combine_kernel(rows_ref, wgt_ref, x1_ref, mod_ref, g_ref, o_ref):
    ts = x1_ref.shape[1]
    w = wgt_ref[0]
    y = jnp.zeros(x1_ref.shape[1:], jnp.float32)
    for k in range(TOP_K):
        y = y + w[:, k:k + 1] * _unpack_bf16_pair(rows_ref[k * ts:(k + 1) * ts, :])
    mod = mod_ref[0]
    o_ref[0] = x1_ref[0] + mod[5:6] * (_rms(y) * g_ref[...])


def _moe_combine(rows, wgt, x1, mod3, g_post, ts):
    bsz, seq, _ = x1.shape
    per_b = seq // ts
    return pl.pallas_call(
        _moe_combine_kernel,
        grid=(bsz, per_b),
        in_specs=[
            pl.BlockSpec((TOP_K * ts, D_MODEL // 2), lambda b, i: (b * per_b + i, 0)),
            pl.BlockSpec((1, ts, LANES), lambda b, i: (b, i, 0)),
            pl.BlockSpec((1, ts, D_MODEL), lambda b, i: (b, i, 0)),
            pl.BlockSpec((1, N_MOD, D_MODEL), lambda b, i: (b, 0, 0)),
            pl.BlockSpec((1, D_MODEL), lambda b, i: (0, 0)),
        ],
        out_specs=pl.BlockSpec((1, ts, D_MODEL), lambda b, i: (b, i, 0)),
        out_shape=jax.ShapeDtypeStruct((bsz, seq, D_MODEL), jnp.float32),
        compiler_params=_cparams(("arbitrary", "arbitrary")),
        name="moe_combine",
    )(rows, wgt, x1, mod3, g_post)


def _routing(top_idx, rank, counts, tm, n_tiles):
    padded = ((counts + tm - 1) // tm) * tm
    pend = jnp.cumsum(padded)
    pstart = pend - padded
    onehot = top_idx[:, :, None] == jnp.arange(N_EXPERTS, dtype=jnp.int32)[None, None, :]
    pos = rank + jnp.sum(jnp.where(onehot, pstart[None, None, :], 0), axis=-1)
    n_valid = (pend[-1] // tm).astype(jnp.int32)
    tile_row0 = jnp.arange(n_tiles, dtype=jnp.int32) * tm
    tile_expert = jnp.minimum(
        jnp.sum((tile_row0[:, None] >= pend[None, :]).astype(jnp.int32), axis=1), N_EXPERTS - 1)
    last_oh = jnp.arange(n_tiles, dtype=jnp.int32) == jnp.maximum(n_valid - 1, 0)
    last_expert = jnp.sum(jnp.where(last_oh, tile_expert, 0))
    tile_expert = jnp.where(jnp.arange(n_tiles, dtype=jnp.int32) < n_valid, tile_expert, last_expert)
    return pos.astype(jnp.int32), tile_expert.astype(jnp.int32), n_valid.reshape(1)


def _alibi_slopes(n_heads):
    return 2.0 ** (-8.0 * jnp.arange(1, n_heads + 1, dtype=jnp.float32) / n_heads)


def _layer(x, c, w_mod, b_mod, g_pre_mix, g_post_mix, w_in, lamv, g_subln, w_branch_sb,
           w_branch_diff, w_out, g_pre_ffn, g_post_ffn, w_router, b_router, w_gate_up,
           b_gate_up, w_down, b_down, *, ts_in, tq, ts_merge, tm, ts_comb):
    bsz, seq, d = x.shape
    n_tok = bsz * seq
    bf = jnp.bfloat16

    mod, lam = _mod_proj(c, w_mod, b_mod, lamv)
    mod3 = mod.reshape(bsz, N_MOD, d)

    o_vsb = 2 * SB_WIDTH
    o_qdf = 3 * SB_WIDTH
    o_vdf = o_qdf + 2 * DIFF_QK_WIDTH
    o_g = o_vdf + DIFF_V_WIDTH
    w_main = jnp.concatenate([w_in[:, :o_vsb], w_in[:, o_qdf:o_vdf], w_in[:, o_g:]], axis=1).astype(bf)
    w_vt = jnp.concatenate([w_in[:, o_vsb:o_qdf], w_in[:, o_vdf:o_g]], axis=1).T.astype(bf)

    main, vt = _in_proj(x, mod3, g_pre_mix.reshape(1, d), w_main, w_vt, ts_in, tq)
    y_sb = _sb_attn(main, vt, tq, tq)
    y_df = _diff_attn(main, vt, _alibi_slopes(DIFF_HEADS), lam,
                      g_subln.reshape(DIFF_V_DIM, 1), tq, tq)

    wr = jnp.zeros((d, LANES), jnp.float32).at[:, :N_EXPERTS].set(w_router)
    wrh = wr.astype(bf)
    wrl = (wr - wrh.astype(jnp.float32)).astype(bf)
    br = jnp.full((1, LANES), NEG_BIG, jnp.float32).at[0, :N_EXPERTS].set(b_router)
    x1, h2p, top_idx, top_w, rank, counts = _merge_router(
        y_sb, y_df, main, x, mod3, w_branch_sb.astype(bf), w_branch_diff.astype(bf),
        w_out.astype(bf), g_post_mix.reshape(1, d), g_pre_ffn.reshape(1, d), wrh, wrl, br, ts_merge)

    n_tiles = (n_tok * TOP_K) // tm + N_EXPERTS
    pos, tile_expert, n_valid = _routing(
        top_idx.reshape(n_tok, LANES)[:, :TOP_K], rank.reshape(n_tok, LANES)[:, :TOP_K],
        counts[0, :N_EXPERTS], tm, n_tiles)
    xg = _sc_scatter_rows(h2p.reshape(n_tok, d // 2), pos.T.reshape(TOP_K * n_tok), n_tiles * tm)
    rows = _moe_ffn(tile_expert, n_valid, xg, w_gate_up, b_gate_up, w_down, b_down, tm)
    pos_steps = pos.reshape(n_tok // ts_comb, ts_comb, TOP_K).swapaxes(1, 2).reshape(n_tok * TOP_K)
    tok_rows = _sc_gather_rows(rows, pos_steps)
    return _moe_combine(tok_rows, top_w, x1, mod3, g_post_ffn.reshape(1, d), ts_comb)


def kernel(x, c, w_mod, b_mod, g_pre_mix, g_post_mix, w_in, lambda_q1, lambda_k1, lambda_q2,
           lambda_k2, g_subln, w_branch_sb, w_branch_diff, w_out, g_pre_ffn, g_post_ffn,
           w_router, b_router, w_gate_up, b_gate_up, w_down, b_down):
    depth = w_mod.shape[0]
    for l in range(depth):
        lamv = jnp.stack([lambda_q1[l], lambda_k1[l], lambda_q2[l], lambda_k2[l]])
        x = _layer(x, c, w_mod[l], b_mod[l], g_pre_mix[l], g_post_mix[l], w_in[l], lamv,
                   g_subln[l], w_branch_sb[l], w_branch_diff[l], w_out[l], g_pre_ffn[l],
                   g_post_ffn[l], w_router[l], b_router[l], w_gate_up[l], b_gate_up[l],
                   w_down[l], b_down[l],
                   ts_in=512, tq=256, ts_merge=512, tm=512, ts_comb=256)
    return x
```

```python
import functools
import math

import jax
import jax.numpy as jnp
from jax import lax
from jax.experimental import pallas as pl
from jax.experimental.pallas import tpu as pltpu
from jax.experimental.pallas import tpu_sc as plsc

D_MODEL = 1024
SB_HEADS = 8
SB_HEAD_DIM = 64
SB_WIDTH = SB_HEADS * SB_HEAD_DIM
DIFF_HEADS = 4
DIFF_HEAD_DIM = 64
DIFF_V_DIM = 2 * DIFF_HEAD_DIM
DIFF_QK_WIDTH = DIFF_HEADS * 2 * DIFF_HEAD_DIM
DIFF_V_WIDTH = DIFF_HEADS * DIFF_V_DIM
N_EXPERTS = 32
TOP_K = 4
D_EXPERT = D_MODEL
SWIGLU_LIMIT = 7.0
SWIGLU_ALPHA = 1.702
RMS_EPS = 1e-6
N_MOD = 6
LAM_INIT = 0.8 - 0.6 * math.exp(-0.3 * 0)

LANES = 128
SUBLANES = 8
NEG_BIG = -1e30
EXP_ZERO_MARGIN = 110.0
NORM_SLACK = 1.01
SC_CORES = 2
SC_SUBCORES = 16
SC_GATHER_ROWS = 128
MERGE_SUB = 256

MAIN_WIDTH = 2 * SB_WIDTH + 2 * DIFF_QK_WIDTH + 2 * D_MODEL
VT_ROWS = SB_WIDTH + DIFF_V_WIDTH
COLBLK_K_SB = SB_WIDTH // LANES
COLBLK_Q_DF = 2 * SB_WIDTH // LANES
COLBLK_K_DF = COLBLK_Q_DF + DIFF_QK_WIDTH // LANES
GATE_COL0 = 2 * SB_WIDTH + 2 * DIFF_QK_WIDTH

VMEM_LIMIT = 56 * 1024 * 1024


def _cparams(sem, vmem=VMEM_LIMIT):
    return pltpu.CompilerParams(dimension_semantics=sem, vmem_limit_bytes=vmem)


def _rms(x):
    return x * lax.rsqrt(jnp.mean(x * x, axis=-1, keepdims=True) + RMS_EPS)


def _mod_kernel(c_ref, w_ref, b_ref, lamv_ref, mod_ref, lam_ref):
    c = c_ref[...]
    ca = c * jax.nn.sigmoid(c)
    mod_ref[...] = jnp.dot(ca, w_ref[...], preferred_element_type=jnp.float32,
                           precision=lax.Precision.HIGHEST) + b_ref[...]
    lv = lamv_ref[...]
    s1 = jnp.sum(lv[0:1] * lv[1:2], axis=-1, keepdims=True)
    s2 = jnp.sum(lv[2:3] * lv[3:4], axis=-1, keepdims=True)
    lam = jnp.exp(s1) - jnp.exp(s2) + LAM_INIT
    lam_ref[...] = jnp.broadcast_to(lam, lam_ref.shape)


def _mod_proj(c, w_mod, b_mod, lamv):
    bsz = c.shape[0]
    tn = 1536
    n = w_mod.shape[1]
    return pl.pallas_call(
        _mod_kernel,
        grid=(n // tn,),
        in_specs=[
            pl.BlockSpec((bsz, D_MODEL), lambda j: (0, 0)),
            pl.BlockSpec((D_MODEL, tn), lambda j: (0, j)),
            pl.BlockSpec((1, tn), lambda j: (0, j)),
            pl.BlockSpec((4, DIFF_HEAD_DIM), lambda j: (0, 0)),
        ],
        out_specs=[
            pl.BlockSpec((bsz, tn), lambda j: (0, j)),
            pl.BlockSpec((SUBLANES, LANES), lambda j: (0, 0)),
        ],
        out_shape=[
            jax.ShapeDtypeStruct((bsz, n), jnp.float32),
            jax.ShapeDtypeStruct((SUBLANES, LANES), jnp.float32),
        ],
        compiler_params=_cparams(("arbitrary",)),
        name="mod_proj",
    )(c, w_mod, b_mod.reshape(1, n), lamv)


IN_CHUNK = 1024


def _in_proj_kernel(x_ref, mod_ref, g_ref, wm_ref, wvt_ref, main_ref, vt_ref, h_scr, *, tk):
    x = x_ref[0]
    mod = mod_ref[0]
    h = _rms(x) * g_ref[...]
    h = h * (1.0 + mod[1:2]) + mod[0:1]
    hb = h.astype(jnp.bfloat16)
    groups = tk // SUBLANES
    cols = []
    for ct in range(D_MODEL // LANES):
        h_scr[ct] = h[:, ct * LANES:(ct + 1) * LANES]
        pieces = []
        for blk in range(h.shape[0] // tk):
            for g in range(groups):
                pieces.append(h_scr[ct, pl.ds(blk * tk + g, SUBLANES, stride=groups), :])
        cols.append(jnp.concatenate(pieces, axis=0))
    hpb = jnp.concatenate(cols, axis=1).astype(jnp.bfloat16)

    half = IN_CHUNK // 2
    for ci in range(MAIN_WIDTH // IN_CHUNK):
        c0 = ci * IN_CHUNK
        if c0 == 0:
            q = jnp.dot(hb, wm_ref[:, :half], preferred_element_type=jnp.float32)
            k = jnp.dot(hpb, wm_ref[:, half:IN_CHUNK], preferred_element_type=jnp.float32)
            main_ref[0, :, :half] = (q * 0.0625).astype(jnp.bfloat16)
            main_ref[0, :, half:IN_CHUNK] = k.astype(jnp.bfloat16)
            continue
        p = jnp.dot(hb, wm_ref[:, c0:c0 + IN_CHUNK], preferred_element_type=jnp.float32)
        if c0 < GATE_COL0:
            main_ref[0, :, c0:c0 + half] = (p[:, :half] * 0.125).astype(jnp.bfloat16)
            main_ref[0, :, c0 + half:c0 + IN_CHUNK] = p[:, half:].astype(jnp.bfloat16)
        else:
            main_ref[0, :, c0:c0 + IN_CHUNK] = jax.nn.sigmoid(p).astype(jnp.bfloat16)
    nt = (((1,), (1,)), ((), ()))
    vt_sb = lax.dot_general(wvt_ref[:SB_WIDTH, :], hpb, nt, preferred_element_type=jnp.float32)
    vt_df = lax.dot_general(wvt_ref[SB_WIDTH:, :], hb, nt, preferred_element_type=jnp.float32)
    vt_ref[0, :SB_WIDTH, :] = vt_sb.astype(jnp.bfloat16)
    vt_ref[0, SB_WIDTH:, :] = vt_df.astype(jnp.bfloat16)


def _in_proj(x, mod3, g_pre, w_main, w_vt, ts, tk):
    bsz, seq, _ = x.shape
    assert ts % tk == 0
    return pl.pallas_call(
        functools.partial(_in_proj_kernel, tk=tk),
        grid=(bsz, seq // ts),
        in_specs=[
            pl.BlockSpec((1, ts, D_MODEL), lambda b, i: (b, i, 0)),
            pl.BlockSpec((1, N_MOD, D_MODEL), lambda b, i: (b, 0, 0)),
            pl.BlockSpec((1, D_MODEL), lambda b, i: (0, 0)),
            pl.BlockSpec((D_MODEL, MAIN_WIDTH), lambda b, i: (0, 0)),
            pl.BlockSpec((VT_ROWS, D_MODEL), lambda b, i: (0, 0)),
        ],
        out_specs=[
            pl.BlockSpec((1, ts, MAIN_WIDTH), lambda b, i: (b, i, 0)),
            pl.BlockSpec((1, VT_ROWS, ts), lambda b, i: (b, 0, i)),
        ],
        out_shape=[
            jax.ShapeDtypeStruct((bsz, seq, MAIN_WIDTH), jnp.bfloat16),
            jax.ShapeDtypeStruct((bsz, VT_ROWS, seq), jnp.bfloat16),
        ],
        scratch_shapes=[pltpu.VMEM((D_MODEL // LANES, ts, LANES), jnp.float32)],
        compiler_params=_cparams(("arbitrary", "arbitrary")),
        name="in_proj",
    )(x, mod3, g_pre, w_main, w_vt)


def _suffix_excl_prod8(tot):
    sub = lax.broadcasted_iota(jnp.int32, tot.shape, 0)
    x = jnp.where(sub < SUBLANES - 1, pltpu.roll(tot, SUBLANES - 1, 0), 1.0)
    for sh in (1, 2, 4):
        x = x * jnp.where(sub + sh < SUBLANES, pltpu.roll(x, SUBLANES - sh, 0), 1.0)
    return x


def _sb_scores(k_ref, q_heads, s_ref, slot, j, tk):
    kb = k_ref[0, pl.ds(pl.multiple_of(j * tk, tk), tk), :]
    for h in range(2):
        s_ref[slot, h] = lax.dot_general(kb, q_heads[h], (((1,), (1,)), ((), ())),
                                         preferred_element_type=jnp.float32)


def _sb_weights(zt, c8, ok, groups):
    tq = zt.shape[1]
    r = 0.5 - 0.5 * jnp.tanh(zt)
    if ok is not None:
        r = jnp.where(ok, r, 1.0)
    rg = [r[g * SUBLANES:(g + 1) * SUBLANES, :] for g in range(groups)]
    tot = rg[0]
    for g in range(1, groups):
        tot = tot * rg[g]
    p = c8 * _suffix_excl_prod8(tot)
    pieces = [None] * groups
    for g in range(groups - 1, -1, -1):
        pn = p * rg[g]
        pieces[g] = p - pn
        p = pn
    a = jnp.concatenate(pieces, axis=0).astype(jnp.bfloat16)
    return a, jnp.broadcast_to(p[0:1, :], (SUBLANES, tq))


def _sb_attn_kernel(q_ref, k_ref, v_ref, o_ref, acc_ref, c_ref, s_ref, *, tq, tk):
    i = pl.program_id(2)
    n_q = pl.num_programs(2)
    groups = tk // SUBLANES

    def heads_of(qi):
        q2 = q_ref[0, pl.ds(pl.multiple_of(qi * tq, tq), tq), :]
        lane = lax.broadcasted_iota(jnp.int32, q2.shape, 1)
        zero = jnp.zeros_like(q2)
        return (jnp.where(lane < SB_HEAD_DIM, q2, zero), jnp.where(lane < SB_HEAD_DIM, zero, q2))

    q_heads = heads_of(i)

    def step(j, slot, masked):
        _sb_scores(k_ref, q_heads, s_ref, 1 - slot, jnp.maximum(j - 1, 0), tk)
        if masked:
            row = lax.broadcasted_iota(jnp.int32, (tk, tq), 0)
            col = lax.broadcasted_iota(jnp.int32, (tk, tq), 1)
            ok = (row % SUBLANES) * groups + row // SUBLANES < col
        else:
            ok = None
        off = pl.multiple_of(j * tk, tk)
        ws = []
        for h in range(2):
            a, c_new = _sb_weights(s_ref[slot, h], c_ref[h], ok, groups)
            c_ref[h] = c_new
            ws.append(a)
        for h in range(2):
            vt_h = v_ref[0, h * SB_HEAD_DIM:(h + 1) * SB_HEAD_DIM, pl.ds(off, tk)]
            acc_ref[h] += jnp.dot(vt_h, ws[h], preferred_element_type=jnp.float32)

    acc_ref[...] = jnp.zeros_like(acc_ref)
    c_ref[...] = jnp.ones_like(c_ref)

    @pl.when(i == 0)
    def _():
        _sb_scores(k_ref, q_heads, s_ref, 0, i, tk)

    step(i, 0, True)

    def stick_left():
        return jnp.max(c_ref[...]) > 0.0

    def more(state):
        m, go = state
        return jnp.logical_and(m < i // 2, go)

    def pair(state):
        m, _ = state
        j = i - 1 - 2 * m
        step(j, 1, False)
        step(j - 1, 0, False)
        return m + 1, stick_left()

    m_done, go = lax.while_loop(more, pair, (jnp.int32(0), stick_left()))

    @pl.when(jnp.logical_and(jnp.logical_and(i % 2 == 1, m_done == i // 2), go))
    def _():
        step(0, 1, False)

    i_next = jnp.minimum(i + 1, n_q - 1)
    _sb_scores(k_ref, heads_of(i_next), s_ref, 0, i_next, tk)

    ot = jnp.concatenate([acc_ref[0], acc_ref[1]], axis=0)
    o_ref[0] = ot.T.astype(jnp.bfloat16)


def _sb_attn(main, vt, tq, tk):
    bsz, seq, _ = main.shape
    assert tq == tk
    kern = functools.partial(_sb_attn_kernel, tq=tq, tk=tk)
    return pl.pallas_call(
        kern,
        grid=(bsz, SB_WIDTH // LANES, seq // tq),
        in_specs=[
            pl.BlockSpec((1, seq, LANES), lambda b, p, i: (b, 0, p)),
            pl.BlockSpec((1, seq, LANES), lambda b, p, i: (b, 0, COLBLK_K_SB + p)),
            pl.BlockSpec((1, LANES, seq), lambda b, p, i: (b, p, 0)),
        ],
        out_specs=pl.BlockSpec((1, tq, LANES), lambda b, p, i: (b, i, p)),
        out_shape=jax.ShapeDtypeStruct((bsz, seq, SB_WIDTH), jnp.bfloat16),
        scratch_shapes=[
            pltpu.VMEM((2, SB_HEAD_DIM, tq), jnp.float32),
            pltpu.VMEM((2, SUBLANES, tq), jnp.float32),
            pltpu.VMEM((2, 2, tk, tq), jnp.float32),
        ],
        compiler_params=_cparams(("arbitrary", "arbitrary", "arbitrary")),
        name="sb_attn",
    )(main, main, vt)


def _diff_attn_kernel(slopes_ref, inv_slopes_ref, q_ref, k_ref, v_ref, lam_ref, g_ref, o_ref,
                      acc_ref, m_ref, l_ref, s_ref, kn_ref, *, tq, tk):
    hd = pl.program_id(1)
    i = pl.program_id(2)
    n_q = pl.num_programs(2)
    slope = slopes_ref[hd]

    @pl.when(i == 0)
    def _():
        kf = k_ref[0].astype(jnp.float32)
        kn2 = jnp.max(jnp.sum(kf * kf, axis=-1, keepdims=True), axis=0, keepdims=True)
        kn_ref[...] = jnp.broadcast_to(kn2, kn_ref.shape)

    def maps_of(qi):
        qt = q_ref[0, pl.ds(pl.multiple_of(qi * tq, tq), tq), :]
        lane = lax.broadcasted_iota(jnp.int32, qt.shape, 1)
        zero = jnp.zeros_like(qt)
        return qt, (jnp.where(lane < DIFF_HEAD_DIM, qt, zero), jnp.where(lane < DIFF_HEAD_DIM, zero, qt))

    q2, q_maps = maps_of(i)

    row = lax.broadcasted_iota(jnp.int32, (tk, tq), 0)
    col = lax.broadcasted_iota(jnp.int32, (tk, tq), 1)
    bias = slope * (row - col).astype(jnp.float32)

    def scores(slot, j, qm=q_maps):
        kb = k_ref[0, pl.ds(pl.multiple_of(j * tk, tk), tk), :]
        for m in range(2):
            s_ref[slot, m] = lax.dot_general(kb, qm[m], (((1,), (1,)), ((), ())),
                                             preferred_element_type=jnp.float32)

    def step(j, slot, masked):
        scores(1 - slot, jnp.maximum(j - 1, 0))
        off = pl.multiple_of(j * tk, tk)
        vtb = v_ref[0, :, pl.ds(off, tk)]
        cb = slope * ((j - i) * tk).astype(jnp.float32)
        ps, alphas = [], []
        for m in range(2):
            s = s_ref[slot, m] + bias
            if masked:
                s = jnp.where(row <= col, s, NEG_BIG)
            m_old = m_ref[m]
            m_new = jnp.maximum(m_old, jnp.max(s, axis=0, keepdims=True) + cb)
            alpha = jnp.exp(m_old - m_new)
            p = jnp.exp(s - (m_new - cb))
            l_ref[m] = alpha * l_ref[m] + jnp.sum(p, axis=0, keepdims=True)
            m_ref[m] = m_new
            ps.append(p.astype(jnp.bfloat16))
            alphas.append(alpha)
        for m in range(2):
            acc_ref[m] = alphas[m] * acc_ref[m] + jnp.dot(
                vtb, ps[m], preferred_element_type=jnp.float32)

    acc_ref[...] = jnp.zeros_like(acc_ref)
    m_ref[...] = jnp.full_like(m_ref, NEG_BIG)
    l_ref[...] = jnp.zeros_like(l_ref)

    @pl.when(i == 0)
    def _():
        scores(0, i)

    step(i, 0, True)

    qf = q2.astype(jnp.float32)
    qn2 = jnp.max(jnp.sum(qf * qf, axis=-1, keepdims=True), axis=0, keepdims=True)
    zabs = jnp.sqrt(qn2 * kn_ref[0:1, 0:1]) * NORM_SLACK
    reach = (EXP_ZERO_MARGIN + 2.0 * zabs) * inv_slopes_ref[hd]
    n_need = jnp.floor(jnp.minimum((reach - 1.0) * (1.0 / tk), 1e6)) + 1.0
    n_back = jnp.minimum(i, jnp.max(n_need).astype(jnp.int32))

    def pair(n, carry):
        j = i - 1 - 2 * n
        step(j, 1, False)
        step(j - 1, 0, False)
        return carry

    lax.fori_loop(0, n_back // 2, pair, 0)

    @pl.when(n_back % 2 == 1)
    def _():
        step(i - n_back, 1, False)

    i_next = jnp.minimum(i + 1, n_q - 1)
    scores(0, i_next, maps_of(i_next)[1])

    lam = lam_ref[0:1, 0:1]
    o = acc_ref[0] / l_ref[0] - lam * (acc_ref[1] / l_ref[1])
    ms = jnp.mean(o * o, axis=0, keepdims=True)
    y = o * lax.rsqrt(ms + RMS_EPS) * g_ref[...] * (1.0 - LAM_INIT)
    o_ref[0] = y.T.astype(jnp.bfloat16)


def _diff_attn(main, vt, slopes, lam, g_col, tq, tk):
    bsz, seq, _ = main.shape
    assert tq == tk
    kern = functools.partial(_diff_attn_kernel, tq=tq, tk=tk)
    vrow0 = SB_WIDTH // LANES
    return pl.pallas_call(
        kern,
        grid=(bsz, DIFF_HEADS, seq // tq),
        in_specs=[
            pl.BlockSpec(memory_space=pltpu.SMEM),
            pl.BlockSpec(memory_space=pltpu.SMEM),
            pl.BlockSpec((1, seq, LANES), lambda b, h, i: (b, 0, COLBLK_Q_DF + h)),
            pl.BlockSpec((1, seq, LANES), lambda b, h, i: (b, 0, COLBLK_K_DF + h)),
            pl.BlockSpec((1, DIFF_V_DIM, seq), lambda b, h, i: (b, vrow0 + h, 0)),
            pl.BlockSpec((SUBLANES, LANES), lambda b, h, i: (0, 0)),
            pl.BlockSpec((DIFF_V_DIM, 1), lambda b, h, i: (0, 0)),
        ],
        out_specs=pl.BlockSpec((1, tq, DIFF_V_DIM), lambda b, h, i: (b, i, h)),
        out_shape=jax.ShapeDtypeStruct((bsz, seq, DIFF_V_WIDTH), jnp.bfloat16),
        scratch_shapes=[
            pltpu.VMEM((2, DIFF_V_DIM, tq), jnp.float32),
            pltpu.VMEM((2, 1, tq), jnp.float32),
            pltpu.VMEM((2, 1, tq), jnp.float32),
            pltpu.VMEM((2, 2, tk, tq), jnp.float32),
            pltpu.VMEM((SUBLANES, LANES), jnp.float32),
        ],
        compiler_params=_cparams(("arbitrary", "arbitrary", "arbitrary")),
        name="diff_attn",
    )(slopes, 1.0 / slopes, main, main, vt, lam, g_col)


def _pack_bf16_pair(a, b):
    ab = pltpu.bitcast(a.astype(jnp.bfloat16).astype(jnp.float32), jnp.uint32)
    bb = pltpu.bitcast(b.astype(jnp.bfloat16).astype(jnp.float32), jnp.uint32)
    return ab | (bb >> 16)


def _unpack_bf16_pair(w):
    hi = pltpu.bitcast(w & jnp.uint32(0xFFFF0000), jnp.float32)
    lo = pltpu.bitcast(w << 16, jnp.float32)
    return jnp.concatenate([hi, lo], axis=1)


def _merge_router_kernel(ysb_ref, ydf_ref, gates_ref, x_ref, mod_ref, wsb_ref, wdf_ref, wout_ref,
                         gpost_ref, gpre_ref, wrh_ref, wrl_ref, br_ref,
                         x1_ref, h2_ref, idx_ref, wgt_ref, rank_ref, cnt_ref, base_ref):
    first = jnp.logical_and(pl.program_id(0) == 0, pl.program_id(1) == 0)

    @pl.when(first)
    def _():
        base_ref[...] = jnp.zeros_like(base_ref)

    mod = mod_ref[0]
    subs = [slice(s * MERGE_SUB, (s + 1) * MERGE_SUB) for s in range(x_ref.shape[1] // MERGE_SUB)]
    half = D_MODEL // 2
    branch = [(jnp.dot(ysb_ref[0, rows, :], wsb_ref[...], preferred_element_type=jnp.float32),
               jnp.dot(ydf_ref[0, rows, :], wdf_ref[...], preferred_element_type=jnp.float32))
              for rows in subs]
    merged = []
    for rows, (a, b) in zip(subs, branch):
        g = gates_ref[0, rows, :].astype(jnp.float32)
        merged.append((g[:, :D_MODEL] * a + g[:, D_MODEL:] * b).astype(jnp.bfloat16))
    mixes = [jnp.dot(m, wout_ref[...], preferred_element_type=jnp.float32) for m in merged]
    h2s = []
    for rows, mix in zip(subs, mixes):
        x1 = x_ref[0, rows, :] + mod[2:3] * (_rms(mix) * gpost_ref[...])
        x1_ref[0, rows, :] = x1
        h2 = _rms(x1) * gpre_ref[...]
        h2 = h2 * (1.0 + mod[4:5]) + mod[3:4]
        h2_ref[0, rows, :] = _pack_bf16_pair(h2[:, :half], h2[:, half:])
        h2s.append(h2)
    logit_list = []
    for h2 in h2s:
        hh = h2.astype(jnp.bfloat16)
        hl = (h2 - hh.astype(jnp.float32)).astype(jnp.bfloat16)
        logit_list.append(jnp.dot(hh, wrh_ref[...], preferred_element_type=jnp.float32)
                          + jnp.dot(hh, wrl_ref[...], preferred_element_type=jnp.float32)
                          + jnp.dot(hl, wrh_ref[...], preferred_element_type=jnp.float32)
                          + br_ref[...])
    for rows, logits in zip(subs, logit_list):
        _route_rows(rows, logits, idx_ref, wgt_ref, rank_ref, cnt_ref, base_ref)


def _route_rows(rows, logits, idx_ref, wgt_ref, rank_ref, cnt_ref, base_ref):
    lane = lax.broadcasted_iota(jnp.int32, logits.shape, 1)
    lanef = lane.astype(jnp.float32)
    vals, idxs = [], []
    cur = logits
    for _ in range(TOP_K):
        mx = jnp.max(cur, axis=-1, keepdims=True)
        ix = jnp.min(jnp.where(cur == mx, lanef, float(LANES)), axis=-1, keepdims=True)
        cur = jnp.where(lanef == ix, -jnp.inf, cur)
        vals.append(mx)
        idxs.append(ix)
    es = [jnp.exp(v - vals[0]) for v in vals]
    den = es[0] + es[1] + es[2] + es[3]
    oi = jnp.zeros(logits.shape, jnp.float32)
    ow = jnp.zeros(logits.shape, jnp.float32)
    for k in range(TOP_K):
        oi = jnp.where(lane == k, idxs[k], oi)
        ow = jnp.where(lane == k, es[k] / den, ow)
    idx_ref[0, rows, :] = oi.astype(jnp.int32)
    wgt_ref[0, rows, :] = ow

    ts = logits.shape[0]
    member = jnp.zeros(logits.shape, jnp.float32)
    for k in range(TOP_K):
        member = member + (lanef == idxs[k]).astype(jnp.float32)
    rr = lax.broadcasted_iota(jnp.int32, (ts, ts), 0)
    cc = lax.broadcasted_iota(jnp.int32, (ts, ts), 1)
    lower = jnp.where(cc < rr, 1.0, 0.0).astype(jnp.bfloat16)
    before = jnp.dot(lower, member.astype(jnp.bfloat16), preferred_element_type=jnp.float32)
    base = base_ref[0:1, :]
    rank_all = before + base
    orank = jnp.zeros(logits.shape, jnp.float32)
    for k in range(TOP_K):
        rk = jnp.sum(jnp.where(lanef == idxs[k], rank_all, 0.0), axis=-1, keepdims=True)
        orank = jnp.where(lane == k, rk, orank)
    rank_ref[0, rows, :] = orank.astype(jnp.int32)
    new_base = base + jnp.sum(member, axis=0, keepdims=True)
    base_ref[...] = jnp.broadcast_to(new_base, base_ref.shape)
    cnt_ref[...] = jnp.broadcast_to(new_base, cnt_ref.shape).astype(jnp.int32)


def _merge_router(ysb, ydf, main, x, mod3, wsb, wdf, wout, gpost, gpre, wrh, wrl, br, ts):
    bsz, seq, _ = x.shape
    const = lambda b, i: (0, 0)
    return pl.pallas_call(
        _merge_router_kernel,
        grid=(bsz, seq // ts),
        in_specs=[
            pl.BlockSpec((1, ts, SB_WIDTH), lambda b, i: (b, i, 0)),
            pl.BlockSpec((1, ts, DIFF_V_WIDTH), lambda b, i: (b, i, 0)),
            pl.BlockSpec((1, ts, 2 * D_MODEL), lambda b, i: (b, i, GATE_COL0 // (2 * D_MODEL))),
            pl.BlockSpec((1, ts, D_MODEL), lambda b, i: (b, i, 0)),
            pl.BlockSpec((1, N_MOD, D_MODEL), lambda b, i: (b, 0, 0)),
            pl.BlockSpec((SB_WIDTH, D_MODEL), const),
            pl.BlockSpec((DIFF_V_WIDTH, D_MODEL), const),
            pl.BlockSpec((D_MODEL, D_MODEL), const),
            pl.BlockSpec((1, D_MODEL), const),
            pl.BlockSpec((1, D_MODEL), const),
            pl.BlockSpec((D_MODEL, LANES), const),
            pl.BlockSpec((D_MODEL, LANES), const),
            pl.BlockSpec((1, LANES), const),
        ],
        out_specs=[
            pl.BlockSpec((1, ts, D_MODEL), lambda b, i: (b, i, 0)),
            pl.BlockSpec((1, ts, D_MODEL // 2), lambda b, i: (b, i, 0)),
            pl.BlockSpec((1, ts, LANES), lambda b, i: (b, i, 0)),
            pl.BlockSpec((1, ts, LANES), lambda b, i: (b, i, 0)),
            pl.BlockSpec((1, ts, LANES), lambda b, i: (b, i, 0)),
            pl.BlockSpec((SUBLANES, LANES), const),
        ],
        out_shape=[
            jax.ShapeDtypeStruct((bsz, seq, D_MODEL), jnp.float32),
            jax.ShapeDtypeStruct((bsz, seq, D_MODEL // 2), jnp.uint32),
            jax.ShapeDtypeStruct((bsz, seq, LANES), jnp.int32),
            jax.ShapeDtypeStruct((bsz, seq, LANES), jnp.float32),
            jax.ShapeDtypeStruct((bsz, seq, LANES), jnp.int32),
            jax.ShapeDtypeStruct((SUBLANES, LANES), jnp.int32),
        ],
        scratch_shapes=[pltpu.VMEM((SUBLANES, LANES), jnp.float32)],
        compiler_params=_cparams(("arbitrary", "arbitrary")),
        name="merge_router",
    )(ysb, ydf, main, x, mod3, wsb, wdf, wout, gpost, gpre, wrh, wrl, br)


def _sc_gather_rows(table, idx):
    n = idx.shape[0]
    width = table.shape[1]
    n_workers = SC_CORES * SC_SUBCORES
    per_worker = n // n_workers
    n_chunks = per_worker // SC_GATHER_ROWS
    assert n_chunks * SC_GATHER_ROWS * n_workers == n
    mesh = plsc.VectorSubcoreMesh(core_axis_name="c", subcore_axis_name="s",
                                  num_cores=SC_CORES, num_subcores=SC_SUBCORES)

    def body(table_hbm, idx_hbm, out_hbm, idx_v, rows_v, sem):
        wid = lax.axis_index("s") * SC_CORES + lax.axis_index("c")
        base = wid * per_worker

        @pl.loop(0, n_chunks)
        def _(ci):
            off = pl.multiple_of(base + ci * SC_GATHER_ROWS, SC_GATHER_ROWS)
            pltpu.sync_copy(idx_hbm.at[pl.ds(off, SC_GATHER_ROWS)], idx_v)
            pltpu.async_copy(table_hbm.at[idx_v], rows_v, sem).wait()
            pltpu.sync_copy(rows_v, out_hbm.at[pl.ds(off, SC_GATHER_ROWS)])

    return pl.kernel(
        body,
        out_type=jax.ShapeDtypeStruct((n, width), table.dtype),
        mesh=mesh,
        scratch_types=[
            pltpu.VMEM((SC_GATHER_ROWS,), jnp.int32),
            pltpu.VMEM((SC_GATHER_ROWS, width), table.dtype),
            pltpu.SemaphoreType.DMA,
        ],
        name="sc_gather_rows",
    )(table, idx)


def _sc_scatter_rows(src, pos_kmajor, n_rows):
    n_tok, width = src.shape
    n_workers = SC_CORES * SC_SUBCORES
    per_worker = n_tok // n_workers
    n_chunks = per_worker // SC_GATHER_ROWS
    assert n_chunks * SC_GATHER_ROWS * n_workers == n_tok
    mesh = plsc.VectorSubcoreMesh(core_axis_name="c", subcore_axis_name="s",
                                  num_cores=SC_CORES, num_subcores=SC_SUBCORES)

    def body(src_hbm, idx_hbm, out_hbm, idx_v, rows_v):
        wid = lax.axis_index("s") * SC_CORES + lax.axis_index("c")
        base = wid * per_worker

        @pl.loop(0, n_chunks)
        def _(ci):
            off = pl.multiple_of(base + ci * SC_GATHER_ROWS, SC_GATHER_ROWS)
            pltpu.sync_copy(src_hbm.at[pl.ds(off, SC_GATHER_ROWS)], rows_v)
            for k in range(TOP_K):
                koff = pl.multiple_of(k * n_tok + off, SC_GATHER_ROWS)
                pltpu.sync_copy(idx_hbm.at[pl.ds(koff, SC_GATHER_ROWS)], idx_v)
                pltpu.sync_copy(rows_v, out_hbm.at[idx_v])

    return pl.kernel(
        body,
        out_type=jax.ShapeDtypeStruct((n_rows, width), src.dtype),
        mesh=mesh,
        scratch_types=[
            pltpu.VMEM((SC_GATHER_ROWS,), jnp.int32),
            pltpu.VMEM((SC_GATHER_ROWS, width), src.dtype),
        ],
        name="sc_scatter_rows",
    )(src, pos_kmajor)


def _moe_ffn_kernel(te_ref, nt_ref, x_ref, wgu_ref, bgu_ref, wd_ref, bd_ref, o_ref,
                    wgu_bf, wd_bf):
    i = pl.program_id(0)
    n_valid = nt_ref[0]

    new_expert = jnp.logical_or(i == 0, te_ref[i] != te_ref[jnp.maximum(i - 1, 0)])

    @pl.when(jnp.logical_and(i < n_valid, new_expert))
    def _():
        wgu_bf[...] = wgu_ref[0].astype(jnp.bfloat16)
        wd_bf[...] = wd_ref[0].astype(jnp.bfloat16)

    @pl.when(i < n_valid)
    def _():
        xb = _unpack_bf16_pair(x_ref[...]).astype(jnp.bfloat16)
        gu = jnp.dot(xb, wgu_bf[...], preferred_element_type=jnp.float32) + bgu_ref[0]
        gate = jnp.minimum(gu[:, :D_EXPERT], SWIGLU_LIMIT)
        up = jnp.clip(gu[:, D_EXPERT:], -SWIGLU_LIMIT, SWIGLU_LIMIT)
        act = (up + 1.0) * (gate * jax.nn.sigmoid(SWIGLU_ALPHA * gate))
        out = jnp.dot(act.astype(jnp.bfloat16), wd_bf[...],
                      preferred_element_type=jnp.float32) + bd_ref[0]
        half = D_MODEL // 2
        o_ref[...] = _pack_bf16_pair(out[:, :half], out[:, half:])

    @pl.when(i >= n_valid)
    def _():
        o_ref[...] = jnp.zeros_like(o_ref)


def _moe_ffn(tile_expert, n_valid, xg, wgu, bgu, wd, bd, tm):
    n_tiles = xg.shape[0] // tm
    grid_spec = pltpu.PrefetchScalarGridSpec(
        num_scalar_prefetch=2,
        grid=(n_tiles,),
        in_specs=[
            pl.BlockSpec((tm, D_MODEL // 2), lambda i, te, nt: (jnp.minimum(i, nt[0] - 1), 0)),
            pl.BlockSpec((1, D_MODEL, 2 * D_EXPERT), lambda i, te, nt: (te[i], 0, 0)),
            pl.BlockSpec((1, 1, 2 * D_EXPERT), lambda i, te, nt: (te[i], 0, 0)),
            pl.BlockSpec((1, D_EXPERT, D_MODEL), lambda i, te, nt: (te[i], 0, 0)),
            pl.BlockSpec((1, 1, D_MODEL), lambda i, te, nt: (te[i], 0, 0)),
        ],
        out_specs=pl.BlockSpec((tm, D_MODEL // 2), lambda i, te, nt: (i, 0)),
        scratch_shapes=[
            pltpu.VMEM((D_MODEL, 2 * D_EXPERT), jnp.bfloat16),
            pltpu.VMEM((D_EXPERT, D_MODEL), jnp.bfloat16),
        ],
    )
    return pl.pallas_call(
        _moe_ffn_kernel,
        grid_spec=grid_spec,
        out_shape=jax.ShapeDtypeStruct((n_tiles * tm, D_MODEL // 2), jnp.uint32),
        compiler_params=_cparams(("arbitrary",)),
        name="moe_ffn",
    )(tile_expert, n_valid, xg, wgu, bgu.reshape(N_EXPERTS, 1, -1), wd,
      bd.reshape(N_EXPERTS, 1, -1))


def _moe_combine_kernel(rows_ref, wgt_ref, x1_ref, mod_ref, g_ref, o_ref):
    ts = x1_ref.shape[1]
    w = wgt_ref[0]
    y = jnp.zeros(x1_ref.shape[1:], jnp.float32)
    for k in range(TOP_K):
        y = y + w[:, k:k + 1] * _unpack_bf16_pair(rows_ref[k * ts:(k + 1) * ts, :])
    mod = mod_ref[0]
    o_ref[0] = x1_ref[0] + mod[5:6] * (_rms(y) * g_ref[...])


def _moe_combine(rows, wgt, x1, mod3, g_post, ts):
    bsz, seq, _ = x1.shape
    per_b = seq // ts
    return pl.pallas_call(
        _moe_combine_kernel,
        grid=(bsz, per_b),
        in_specs=[
            pl.BlockSpec((TOP_K * ts, D_MODEL // 2), lambda b, i: (b * per_b + i, 0)),
            pl.BlockSpec((1, ts, LANES), lambda b, i: (b, i, 0)),
            pl.BlockSpec((1, ts, D_MODEL), lambda b, i: (b, i, 0)),
            pl.BlockSpec((1, N_MOD, D_MODEL), lambda b, i: (b, 0, 0)),
            pl.BlockSpec((1, D_MODEL), lambda b, i: (0, 0)),
        ],
        out_specs=pl.BlockSpec((1, ts, D_MODEL), lambda b, i: (b, i, 0)),
        out_shape=jax.ShapeDtypeStruct((bsz, seq, D_MODEL), jnp.float32),
        compiler_params=_cparams(("arbitrary", "arbitrary")),
        name="moe_combine",
    )(rows, wgt, x1, mod3, g_post)


def _routing(top_idx, rank, counts, tm, n_tiles):
    padded = ((counts + tm - 1) // tm) * tm
    pend = jnp.cumsum(padded)
    pstart = pend - padded
    onehot = top_idx[:, :, None] == jnp.arange(N_EXPERTS, dtype=jnp.int32)[None, None, :]
    pos = rank + jnp.sum(jnp.where(onehot, pstart[None, None, :], 0), axis=-1)
    n_valid = (pend[-1] // tm).astype(jnp.int32)
    tile_row0 = jnp.arange(n_tiles, dtype=jnp.int32) * tm
    tile_expert = jnp.minimum(
        jnp.sum((tile_row0[:, None] >= pend[None, :]).astype(jnp.int32), axis=1), N_EXPERTS - 1)
    last_oh = jnp.arange(n_tiles, dtype=jnp.int32) == jnp.maximum(n_valid - 1, 0)
    last_expert = jnp.sum(jnp.where(last_oh, tile_expert, 0))
    tile_expert = jnp.where(jnp.arange(n_tiles, dtype=jnp.int32) < n_valid, tile_expert, last_expert)
    return pos.astype(jnp.int32), tile_expert.astype(jnp.int32), n_valid.reshape(1)


def _alibi_slopes(n_heads):
    return 2.0 ** (-8.0 * jnp.arange(1, n_heads + 1, dtype=jnp.float32) / n_heads)


def _layer(x, c, w_mod, b_mod, g_pre_mix, g_post_mix, w_in, lamv, g_subln, w_branch_sb,
           w_branch_diff, w_out, g_pre_ffn, g_post_ffn, w_router, b_router, w_gate_up,
           b_gate_up, w_down, b_down, *, ts_in, tq, ts_merge, tm, ts_comb):
    bsz, seq, d = x.shape
    n_tok = bsz * seq
    bf = jnp.bfloat16

    mod, lam = _mod_proj(c, w_mod, b_mod, lamv)
    mod3 = mod.reshape(bsz, N_MOD, d)

    o_vsb = 2 * SB_WIDTH
    o_qdf = 3 * SB_WIDTH
    o_vdf = o_qdf + 2 * DIFF_QK_WIDTH
    o_g = o_vdf + DIFF_V_WIDTH
    w_main = jnp.concatenate([w_in[:, :o_vsb], w_in[:, o_qdf:o_vdf], w_in[:, o_g:]], axis=1).astype(bf)
    w_vt = jnp.concatenate([w_in[:, o_vsb:o_qdf], w_in[:, o_vdf:o_g]], axis=1).T.astype(bf)

    main, vt = _in_proj(x, mod3, g_pre_mix.reshape(1, d), w_main, w_vt, ts_in, tq)
    y_sb = _sb_attn(main, vt, tq, tq)
    y_df = _diff_attn(main, vt, _alibi_slopes(DIFF_HEADS), lam,
                      g_subln.reshape(DIFF_V_DIM, 1), tq, tq)

    wr = jnp.zeros((d, LANES), jnp.float32).at[:, :N_EXPERTS].set(w_router)
    wrh = wr.astype(bf)
    wrl = (wr - wrh.astype(jnp.float32)).astype(bf)
    br = jnp.full((1, LANES), NEG_BIG, jnp.float32).at[0, :N_EXPERTS].set(b_router)
    x1, h2p, top_idx, top_w, rank, counts = _merge_router(
        y_sb, y_df, main, x, mod3, w_branch_sb.astype(bf), w_branch_diff.astype(bf),
        w_out.astype(bf), g_post_mix.reshape(1, d), g_pre_ffn.reshape(1, d), wrh, wrl, br, ts_merge)

    n_tiles = (n_tok * TOP_K) // tm + N_EXPERTS
    pos, tile_expert, n_valid = _routing(
        top_idx.reshape(n_tok, LANES)[:, :TOP_K], rank.reshape(n_tok, LANES)[:, :TOP_K],
        counts[0, :N_EXPERTS], tm, n_tiles)
    xg = _sc_scatter_rows(h2p.reshape(n_tok, d // 2), pos.T.reshape(TOP_K * n_tok), n_tiles * tm)
    rows = _moe_ffn(tile_expert, n_valid, xg, w_gate_up, b_gate_up, w_down, b_down, tm)
    pos_steps = pos.reshape(n_tok // ts_comb, ts_comb, TOP_K).swapaxes(1, 2).reshape(n_tok * TOP_K)
    tok_rows = _sc_gather_rows(rows, pos_steps)
    return _moe_combine(tok_rows, top_w, x1, mod3, g_post_ffn.reshape(1, d), ts_comb)


def kernel(x, c, w_mod, b_mod, g_pre_mix, g_post_mix, w_in, lambda_q1, lambda_k1, lambda_q2,
           lambda_k2, g_subln, w_branch_sb, w_branch_diff, w_out, g_pre_ffn, g_post_ffn,
           w_router, b_router, w_gate_up, b_gate_up, w_down, b_down):
    depth = w_mod.shape[0]
    for l in range(depth):
        lamv = jnp.stack([lambda_q1[l], lambda_k1[l], lambda_q2[l], lambda_k2[l]])
        x = _layer(x, c, w_mod[l], b_mod[l], g_pre_mix[l], g_post_mix[l], w_in[l], lamv,
                   g_subln[l], w_branch_sb[l], w_branch_diff[l], w_out[l], g_pre_ffn[l],
                   g_post_ffn[l], w_router[l], b_router[l], w_gate_up[l], b_gate_up[l],
                   w_down[l], b_down[l],
                   ts_in=512, tq=256, ts_merge=512, tm=512, ts_comb=256)
    return x
```

```python
import functools
import math

import jax
import jax.numpy as jnp
from jax import lax
from jax.experimental import pallas as pl
from jax.experimental.pallas import tpu as pltpu
from jax.experimental.pallas import tpu_sc as plsc

D_MODEL = 1024
SB_HEADS = 8
SB_HEAD_DIM = 64
SB_WIDTH = SB_HEADS * SB_HEAD_DIM
DIFF_HEADS = 4
DIFF_HEAD_DIM = 64
DIFF_V_DIM = 2 * DIFF_HEAD_DIM
DIFF_QK_WIDTH = DIFF_HEADS * 2 * DIFF_HEAD_DIM
DIFF_V_WIDTH = DIFF_HEADS * DIFF_V_DIM
N_EXPERTS = 32
TOP_K = 4
D_EXPERT = D_MODEL
SWIGLU_LIMIT = 7.0
SWIGLU_ALPHA = 1.702
RMS_EPS = 1e-6
N_MOD = 6
LAM_INIT = 0.8 - 0.6 * math.exp(-0.3 * 0)

LANES = 128
SUBLANES = 8
NEG_BIG = -1e30
EXP_ZERO_MARGIN = 110.0
NORM_SLACK = 1.01
SC_CORES = 2
SC_SUBCORES = 16
SC_GATHER_ROWS = 128
MERGE_SUB = 256

MAIN_WIDTH = 2 * SB_WIDTH + 2 * DIFF_QK_WIDTH + 2 * D_MODEL
VT_ROWS = SB_WIDTH + DIFF_V_WIDTH
COLBLK_K_SB = SB_WIDTH // LANES
COLBLK_Q_DF = 2 * SB_WIDTH // LANES
COLBLK_K_DF = COLBLK_Q_DF + DIFF_QK_WIDTH // LANES
GATE_COL0 = 2 * SB_WIDTH + 2 * DIFF_QK_WIDTH

VMEM_LIMIT = 56 * 1024 * 1024


def _cparams(sem, vmem=VMEM_LIMIT):
    return pltpu.CompilerParams(dimension_semantics=sem, vmem_limit_bytes=vmem)


def _rms(x):
    return x * lax.rsqrt(jnp.mean(x * x, axis=-1, keepdims=True) + RMS_EPS)


def _mod_kernel(c_ref, w_ref, b_ref, lamv_ref, mod_ref, lam_ref):
    c = c_ref[...]
    ca = c * jax.nn.sigmoid(c)
    mod_ref[...] = jnp.dot(ca, w_ref[...], preferred_element_type=jnp.float32,
                           precision=lax.Precision.HIGHEST) + b_ref[...]
    lv = lamv_ref[...]
    s1 = jnp.sum(lv[0:1] * lv[1:2], axis=-1, keepdims=True)
    s2 = jnp.sum(lv[2:3] * lv[3:4], axis=-1, keepdims=True)
    lam = jnp.exp(s1) - jnp.exp(s2) + LAM_INIT
    lam_ref[...] = jnp.broadcast_to(lam, lam_ref.shape)


def _mod_proj(c, w_mod, b_mod, lamv):
    bsz = c.shape[0]
    tn = 1536
    n = w_mod.shape[1]
    return pl.pallas_call(
        _mod_kernel,
        grid=(n // tn,),
        in_specs=[
            pl.BlockSpec((bsz, D_MODEL), lambda j: (0, 0)),
            pl.BlockSpec((D_MODEL, tn), lambda j: (0, j)),
            pl.BlockSpec((1, tn), lambda j: (0, j)),
            pl.BlockSpec((4, DIFF_HEAD_DIM), lambda j: (0, 0)),
        ],
        out_specs=[
            pl.BlockSpec((bsz, tn), lambda j: (0, j)),
            pl.BlockSpec((SUBLANES, LANES), lambda j: (0, 0)),
        ],
        out_shape=[
            jax.ShapeDtypeStruct((bsz, n), jnp.float32),
            jax.ShapeDtypeStruct((SUBLANES, LANES), jnp.float32),
        ],
        compiler_params=_cparams(("arbitrary",)),
        name="mod_proj",
    )(c, w_mod, b_mod.reshape(1, n), lamv)


IN_CHUNK = 1024


def _in_proj_kernel(x_ref, mod_ref, g_ref, wm_ref, wvt_ref, main_ref, vt_ref, h_scr, *, tk):
    x = x_ref[0]
    mod = mod_ref[0]
    h = _rms(x) * g_ref[...]
    h = h * (1.0 + mod[1:2]) + mod[0:1]
    hb = h.astype(jnp.bfloat16)
    groups = tk // SUBLANES
    cols = []
    for ct in range(D_MODEL // LANES):
        h_scr[ct] = h[:, ct * LANES:(ct + 1) * LANES]
        pieces = []
        for blk in range(h.shape[0] // tk):
            for g in range(groups):
                pieces.append(h_scr[ct, pl.ds(blk * tk + g, SUBLANES, stride=groups), :])
        cols.append(jnp.concatenate(pieces, axis=0))
    hpb = jnp.concatenate(cols, axis=1).astype(jnp.bfloat16)

    half = IN_CHUNK // 2
    for ci in range(MAIN_WIDTH // IN_CHUNK):
        c0 = ci * IN_CHUNK
        if c0 == 0:
            q = jnp.dot(hb, wm_ref[:, :half], preferred_element_type=jnp.float32)
            k = jnp.dot(hpb, wm_ref[:, half:IN_CHUNK], preferred_element_type=jnp.float32)
            main_ref[0, :, :half] = (q * 0.0625).astype(jnp.bfloat16)
            main_ref[0, :, half:IN_CHUNK] = k.astype(jnp.bfloat16)
            continue
        p = jnp.dot(hb, wm_ref[:, c0:c0 + IN_CHUNK], preferred_element_type=jnp.float32)
        if c0 < GATE_COL0:
            main_ref[0, :, c0:c0 + half] = (p[:, :half] * 0.125).astype(jnp.bfloat16)
            main_ref[0, :, c0 + half:c0 + IN_CHUNK] = p[:, half:].astype(jnp.bfloat16)
        else:
            main_ref[0, :, c0:c0 + IN_CHUNK] = jax.nn.sigmoid(p).astype(jnp.bfloat16)
    nt = (((1,), (1,)), ((), ()))
    vt_sb = lax.dot_general(wvt_ref[:SB_WIDTH, :], hpb, nt, preferred_element_type=jnp.float32)
    vt_df = lax.dot_general(wvt_ref[SB_WIDTH:, :], hb, nt, preferred_element_type=jnp.float32)
    vt_ref[0, :SB_WIDTH, :] = vt_sb.astype(jnp.bfloat16)
    vt_ref[0, SB_WIDTH:, :] = vt_df.astype(jnp.bfloat16)


def _in_proj(x, mod3, g_pre, w_main, w_vt, ts, tk):
    bsz, seq, _ = x.shape
    assert ts % tk == 0
    return pl.pallas_call(
        functools.partial(_in_proj_kernel, tk=tk),
        grid=(bsz, seq // ts),
        in_specs=[
            pl.BlockSpec((1, ts, D_MODEL), lambda b, i: (b, i, 0)),
            pl.BlockSpec((1, N_MOD, D_MODEL), lambda b, i: (b, 0, 0)),
            pl.BlockSpec((1, D_MODEL), lambda b, i: (0, 0)),
            pl.BlockSpec((D_MODEL, MAIN_WIDTH), lambda b, i: (0, 0)),
            pl.BlockSpec((VT_ROWS, D_MODEL), lambda b, i: (0, 0)),
        ],
        out_specs=[
            pl.BlockSpec((1, ts, MAIN_WIDTH), lambda b, i: (b, i, 0)),
            pl.BlockSpec((1, VT_ROWS, ts), lambda b, i: (b, 0, i)),
        ],
        out_shape=[
            jax.ShapeDtypeStruct((bsz, seq, MAIN_WIDTH), jnp.bfloat16),
            jax.ShapeDtypeStruct((bsz, VT_ROWS, seq), jnp.bfloat16),
        ],
        scratch_shapes=[pltpu.VMEM((D_MODEL // LANES, ts, LANES), jnp.float32)],
        compiler_params=_cparams(("arbitrary", "arbitrary")),
        name="in_proj",
    )(x, mod3, g_pre, w_main, w_vt)


def _suffix_excl_prod8(tot):
    sub = lax.broadcasted_iota(jnp.int32, tot.shape, 0)
    x = jnp.where(sub < SUBLANES - 1, pltpu.roll(tot, SUBLANES - 1, 0), 1.0)
    for sh in (1, 2, 4):
        x = x * jnp.where(sub + sh < SUBLANES, pltpu.roll(x, SUBLANES - sh, 0), 1.0)
    return x


def _sb_scores(k_ref, q_heads, s_ref, slot, j, tk):
    kb = k_ref[0, pl.ds(pl.multiple_of(j * tk, tk), tk), :]
    for h in range(2):
        s_ref[slot, h] = lax.dot_general(kb, q_heads[h], (((1,), (1,)), ((), ())),
                                         preferred_element_type=jnp.float32)


def _sb_weights(zt, c8, ok, groups):
    tq = zt.shape[1]
    r = 0.5 - 0.5 * jnp.tanh(zt)
    if ok is not None:
        r = jnp.where(ok, r, 1.0)
    rg = [r[g * SUBLANES:(g + 1) * SUBLANES, :] for g in range(groups)]
    tot = rg[0]
    for g in range(1, groups):
        tot = tot * rg[g]
    p = c8 * _suffix_excl_prod8(tot)
    pieces = [None] * groups
    for g in range(groups - 1, -1, -1):
        pn = p * rg[g]
        pieces[g] = p - pn
        p = pn
    a = jnp.concatenate(pieces, axis=0).astype(jnp.bfloat16)
    return a, jnp.broadcast_to(p[0:1, :], (SUBLANES, tq))


def _sb_attn_kernel(q_ref, k_ref, v_ref, o_ref, acc_ref, c_ref, s_ref, *, tq, tk):
    i = pl.program_id(2)
    groups = tk // SUBLANES
    q2 = q_ref[0]
    lane = lax.broadcasted_iota(jnp.int32, q2.shape, 1)
    zero = jnp.zeros_like(q2)
    q_heads = (jnp.where(lane < SB_HEAD_DIM, q2, zero), jnp.where(lane < SB_HEAD_DIM, zero, q2))

    def step(j, slot, masked):
        _sb_scores(k_ref, q_heads, s_ref, 1 - slot, jnp.maximum(j - 1, 0), tk)
        if masked:
            row = lax.broadcasted_iota(jnp.int32, (tk, tq), 0)
            col = lax.broadcasted_iota(jnp.int32, (tk, tq), 1)
            ok = (row % SUBLANES) * groups + row // SUBLANES < col
        else:
            ok = None
        off = pl.multiple_of(j * tk, tk)
        ws = []
        for h in range(2):
            a, c_new = _sb_weights(s_ref[slot, h], c_ref[h], ok, groups)
            c_ref[h] = c_new
            ws.append(a)
        for h in range(2):
            vt_h = v_ref[0, h * SB_HEAD_DIM:(h + 1) * SB_HEAD_DIM, pl.ds(off, tk)]
            acc_ref[h] += jnp.dot(vt_h, ws[h], preferred_element_type=jnp.float32)

    acc_ref[...] = jnp.zeros_like(acc_ref)
    c_ref[...] = jnp.ones_like(c_ref)
    _sb_scores(k_ref, q_heads, s_ref, 0, i, tk)
    step(i, 0, True)

    def stick_left():
        return jnp.max(c_ref[...]) > 0.0

    def more(state):
        m, go = state
        return jnp.logical_and(m < i // 2, go)

    def pair(state):
        m, _ = state
        j = i - 1 - 2 * m
        step(j, 1, False)
        go_on = stick_left()

        @pl.when(go_on)
        def _():
            step(j - 1, 0, False)

        return m + 1, jnp.logical_and(go_on, stick_left())

    m_done, go = lax.while_loop(more, pair, (jnp.int32(0), stick_left()))

    @pl.when(jnp.logical_and(jnp.logical_and(i % 2 == 1, m_done == i // 2), go))
    def _():
        step(0, 1, False)

    ot = jnp.concatenate([acc_ref[0], acc_ref[1]], axis=0)
    o_ref[0] = ot.T.astype(jnp.bfloat16)


def _sb_attn(main, vt, tq, tk):
    bsz, seq, _ = main.shape
    assert tq == tk
    kern = functools.partial(_sb_attn_kernel, tq=tq, tk=tk)
    return pl.pallas_call(
        kern,
        grid=(bsz, SB_WIDTH // LANES, seq // tq),
        in_specs=[
            pl.BlockSpec((1, tq, LANES), lambda b, p, i: (b, i, p)),
            pl.BlockSpec((1, seq, LANES), lambda b, p, i: (b, 0, COLBLK_K_SB + p)),
            pl.BlockSpec((1, LANES, seq), lambda b, p, i: (b, p, 0)),
        ],
        out_specs=pl.BlockSpec((1, tq, LANES), lambda b, p, i: (b, i, p)),
        out_shape=jax.ShapeDtypeStruct((bsz, seq, SB_WIDTH), jnp.bfloat16),
        scratch_shapes=[
            pltpu.VMEM((2, SB_HEAD_DIM, tq), jnp.float32),
            pltpu.VMEM((2, SUBLANES, tq), jnp.float32),
            pltpu.VMEM((2, 2, tk, tq), jnp.float32),
        ],
        compiler_params=_cparams(("arbitrary", "arbitrary", "arbitrary")),
        name="sb_attn",
    )(main, main, vt)


def _diff_attn_kernel(slopes_ref, inv_slopes_ref, q_ref, k_ref, v_ref, lam_ref, g_ref, o_ref,
                      acc_ref, m_ref, l_ref, s_ref, kn_ref, *, tq, tk):
    hd = pl.program_id(1)
    i = pl.program_id(2)
    slope = slopes_ref[hd]

    @pl.when(i == 0)
    def _():
        kf = k_ref[0].astype(jnp.float32)
        kn2 = jnp.max(jnp.sum(kf * kf, axis=-1, keepdims=True), axis=0, keepdims=True)
        kn_ref[...] = jnp.broadcast_to(kn2, kn_ref.shape)

    q2 = q_ref[0]
    lane = lax.broadcasted_iota(jnp.int32, q2.shape, 1)
    zero = jnp.zeros_like(q2)
    q_maps = (jnp.where(lane < DIFF_HEAD_DIM, q2, zero), jnp.where(lane < DIFF_HEAD_DIM, zero, q2))

    row = lax.broadcasted_iota(jnp.int32, (tk, tq), 0)
    col = lax.broadcasted_iota(jnp.int32, (tk, tq), 1)
    bias = slope * (row - col).astype(jnp.float32)

    def scores(slot, j):
        kb = k_ref[0, pl.ds(pl.multiple_of(j * tk, tk), tk), :]
        for m in range(2):
            s_ref[slot, m] = lax.dot_general(kb, q_maps[m], (((1,), (1,)), ((), ())),
                                             preferred_element_type=jnp.float32)

    def step(j, slot, masked):
        scores(1 - slot, jnp.maximum(j - 1, 0))
        off = pl.multiple_of(j * tk, tk)
        vtb = v_ref[0, :, pl.ds(off, tk)]
        cb = slope * ((j - i) * tk).astype(jnp.float32)
        ps, alphas = [], []
        for m in range(2):
            s = s_ref[slot, m] + bias
            if masked:
                s = jnp.where(row <= col, s, NEG_BIG)
            m_old = m_ref[m]
            m_new = jnp.maximum(m_old, jnp.max(s, axis=0, keepdims=True) + cb)
            alpha = jnp.exp(m_old - m_new)
            p = jnp.exp(s - (m_new - cb))
            l_ref[m] = alpha * l_ref[m] + jnp.sum(p, axis=0, keepdims=True)
            m_ref[m] = m_new
            ps.append(p.astype(jnp.bfloat16))
            alphas.append(alpha)
        for m in range(2):
            acc_ref[m] = alphas[m] * acc_ref[m] + jnp.dot(
                vtb, ps[m], preferred_element_type=jnp.float32)

    acc_ref[...] = jnp.zeros_like(acc_ref)
    m_ref[...] = jnp.full_like(m_ref, NEG_BIG)
    l_ref[...] = jnp.zeros_like(l_ref)
    scores(0, i)
    step(i, 0, True)

    qf = q2.astype(jnp.float32)
    qn2 = jnp.max(jnp.sum(qf * qf, axis=-1, keepdims=True), axis=0, keepdims=True)
    zabs = jnp.sqrt(qn2 * kn_ref[0:1, 0:1]) * NORM_SLACK
    reach = (EXP_ZERO_MARGIN + 2.0 * zabs) * inv_slopes_ref[hd]
    n_need = jnp.floor(jnp.minimum((reach - 1.0) * (1.0 / tk), 1e6)) + 1.0
    n_back = jnp.minimum(i, jnp.max(n_need).astype(jnp.int32))

    def pair(n, carry):
        j = i - 1 - 2 * n
        step(j, 1, False)
        step(j - 1, 0, False)
        return carry

    lax.fori_loop(0, n_back // 2, pair, 0)

    @pl.when(n_back % 2 == 1)
    def _():
        step(i - n_back, 1, False)

    lam = lam_ref[0:1, 0:1]
    o = acc_ref[0] / l_ref[0] - lam * (acc_ref[1] / l_ref[1])
    ms = jnp.mean(o * o, axis=0, keepdims=True)
    y = o * lax.rsqrt(ms + RMS_EPS) * g_ref[...] * (1.0 - LAM_INIT)
    o_ref[0] = y.T.astype(jnp.bfloat16)


def _diff_attn(main, vt, slopes, lam, g_col, tq, tk):
    bsz, seq, _ = main.shape
    assert tq == tk
    kern = functools.partial(_diff_attn_kernel, tq=tq, tk=tk)
    vrow0 = SB_WIDTH // LANES
    return pl.pallas_call(
        kern,
        grid=(bsz, DIFF_HEADS, seq // tq),
        in_specs=[
            pl.BlockSpec(memory_space=pltpu.SMEM),
            pl.BlockSpec(memory_space=pltpu.SMEM),
            pl.BlockSpec((1, tq, LANES), lambda b, h, i: (b, i, COLBLK_Q_DF + h)),
            pl.BlockSpec((1, seq, LANES), lambda b, h, i: (b, 0, COLBLK_K_DF + h)),
            pl.BlockSpec((1, DIFF_V_DIM, seq), lambda b, h, i: (b, vrow0 + h, 0)),
            pl.BlockSpec((SUBLANES, LANES), lambda b, h, i: (0, 0)),
            pl.BlockSpec((DIFF_V_DIM, 1), lambda b, h, i: (0, 0)),
        ],
        out_specs=pl.BlockSpec((1, tq, DIFF_V_DIM), lambda b, h, i: (b, i, h)),
        out_shape=jax.ShapeDtypeStruct((bsz, seq, DIFF_V_WIDTH), jnp.bfloat16),
        scratch_shapes=[
            pltpu.VMEM((2, DIFF_V_DIM, tq), jnp.float32),
            pltpu.VMEM((2, 1, tq), jnp.float32),
            pltpu.VMEM((2, 1, tq), jnp.float32),
            pltpu.VMEM((2, 2, tk, tq), jnp.float32),
            pltpu.VMEM((SUBLANES, LANES), jnp.float32),
        ],
        compiler_params=_cparams(("arbitrary", "arbitrary", "arbitrary")),
        name="diff_attn",
    )(slopes, 1.0 / slopes, main, main, vt, lam, g_col)


def _pack_bf16_pair(a, b):
    ab = pltpu.bitcast(a.astype(jnp.bfloat16).astype(jnp.float32), jnp.uint32)
    bb = pltpu.bitcast(b.astype(jnp.bfloat16).astype(jnp.float32), jnp.uint32)
    return ab | (bb >> 16)


def _unpack_bf16_pair(w):
    hi = pltpu.bitcast(w & jnp.uint32(0xFFFF0000), jnp.float32)
    lo = pltpu.bitcast(w << 16, jnp.float32)
    return jnp.concatenate([hi, lo], axis=1)


def _merge_router_kernel(ysb_ref, ydf_ref, gates_ref, x_ref, mod_ref, wsb_ref, wdf_ref, wout_ref,
                         gpost_ref, gpre_ref, wrh_ref, wrl_ref, br_ref,
                         x1_ref, h2_ref, idx_ref, wgt_ref, rank_ref, cnt_ref, base_ref):
    first = jnp.logical_and(pl.program_id(0) == 0, pl.program_id(1) == 0)

    @pl.when(first)
    def _():
        base_ref[...] = jnp.zeros_like(base_ref)

    mod = mod_ref[0]
    subs = [slice(s * MERGE_SUB, (s + 1) * MERGE_SUB) for s in range(x_ref.shape[1] // MERGE_SUB)]
    half = D_MODEL // 2
    branch = [(jnp.dot(ysb_ref[0, rows, :], wsb_ref[...], preferred_element_type=jnp.float32),
               jnp.dot(ydf_ref[0, rows, :], wdf_ref[...], preferred_element_type=jnp.float32))
              for rows in subs]
    merged = []
    for rows, (a, b) in zip(subs, branch):
        g = gates_ref[0, rows, :].astype(jnp.float32)
        merged.append((g[:, :D_MODEL] * a + g[:, D_MODEL:] * b).astype(jnp.bfloat16))
    mixes = [jnp.dot(m, wout_ref[...], preferred_element_type=jnp.float32) for m in merged]
    h2s = []
    for rows, mix in zip(subs, mixes):
        x1 = x_ref[0, rows, :] + mod[2:3] * (_rms(mix) * gpost_ref[...])
        x1_ref[0, rows, :] = x1
        h2 = _rms(x1) * gpre_ref[...]
        h2 = h2 * (1.0 + mod[4:5]) + mod[3:4]
        h2_ref[0, rows, :] = _pack_bf16_pair(h2[:, :half], h2[:, half:])
        h2s.append(h2)
    logit_list = []
    for h2 in h2s:
        hh = h2.astype(jnp.bfloat16)
        hl = (h2 - hh.astype(jnp.float32)).astype(jnp.bfloat16)
        logit_list.append(jnp.dot(hh, wrh_ref[...], preferred_element_type=jnp.float32)
                          + jnp.dot(hh, wrl_ref[...], preferred_element_type=jnp.float32)
                          + jnp.dot(hl, wrh_ref[...], preferred_element_type=jnp.float32)
                          + br_ref[...])
    for rows, logits in zip(subs, logit_list):
        _route_rows(rows, logits, idx_ref, wgt_ref, rank_ref, cnt_ref, base_ref)


def _route_rows(rows, logits, idx_ref, wgt_ref, rank_ref, cnt_ref, base_ref):
    lane = lax.broadcasted_iota(jnp.int32, logits.shape, 1)
    lanef = lane.astype(jnp.float32)
    vals, idxs = [], []
    cur = logits
    for _ in range(TOP_K):
        mx = jnp.max(cur, axis=-1, keepdims=True)
        ix = jnp.min(jnp.where(cur == mx, lanef, float(LANES)), axis=-1, keepdims=True)
        cur = jnp.where(lanef == ix, -jnp.inf, cur)
        vals.append(mx)
        idxs.append(ix)
    es = [jnp.exp(v - vals[0]) for v in vals]
    den = es[0] + es[1] + es[2] + es[3]
    oi = jnp.zeros(logits.shape, jnp.float32)
    ow = jnp.zeros(logits.shape, jnp.float32)
    for k in range(TOP_K):
        oi = jnp.where(lane == k, idxs[k], oi)
        ow = jnp.where(lane == k, es[k] / den, ow)
    idx_ref[0, rows, :] = oi.astype(jnp.int32)
    wgt_ref[0, rows, :] = ow

    ts = logits.shape[0]
    member = jnp.zeros(logits.shape, jnp.float32)
    for k in range(TOP_K):
        member = member + (lanef == idxs[k]).astype(jnp.float32)
    rr = lax.broadcasted_iota(jnp.int32, (ts, ts), 0)
    cc = lax.broadcasted_iota(jnp.int32, (ts, ts), 1)
    lower = jnp.where(cc < rr, 1.0, 0.0).astype(jnp.bfloat16)
    before = jnp.dot(lower, member.astype(jnp.bfloat16), preferred_element_type=jnp.float32)
    base = base_ref[0:1, :]
    rank_all = before + base
    orank = jnp.zeros(logits.shape, jnp.float32)
    for k in range(TOP_K):
        rk = jnp.sum(jnp.where(lanef == idxs[k], rank_all, 0.0), axis=-1, keepdims=True)
        orank = jnp.where(lane == k, rk, orank)
    rank_ref[0, rows, :] = orank.astype(jnp.int32)
    new_base = base + jnp.sum(member, axis=0, keepdims=True)
    base_ref[...] = jnp.broadcast_to(new_base, base_ref.shape)
    cnt_ref[...] = jnp.broadcast_to(new_base, cnt_ref.shape).astype(jnp.int32)


def _merge_router(ysb, ydf, main, x, mod3, wsb, wdf, wout, gpost, gpre, wrh, wrl, br, ts):
    bsz, seq, _ = x.shape
    const = lambda b, i: (0, 0)
    return pl.pallas_call(
        _merge_router_kernel,
        grid=(bsz, seq // ts),
        in_specs=[
            pl.BlockSpec((1, ts, SB_WIDTH), lambda b, i: (b, i, 0)),
            pl.BlockSpec((1, ts, DIFF_V_WIDTH), lambda b, i: (b, i, 0)),
            pl.BlockSpec((1, ts, 2 * D_MODEL), lambda b, i: (b, i, GATE_COL0 // (2 * D_MODEL))),
            pl.BlockSpec((1, ts, D_MODEL), lambda b, i: (b, i, 0)),
            pl.BlockSpec((1, N_MOD, D_MODEL), lambda b, i: (b, 0, 0)),
            pl.BlockSpec((SB_WIDTH, D_MODEL), const),
            pl.BlockSpec((DIFF_V_WIDTH, D_MODEL), const),
            pl.BlockSpec((D_MODEL, D_MODEL), const),
            pl.BlockSpec((1, D_MODEL), const),
            pl.BlockSpec((1, D_MODEL), const),
            pl.BlockSpec((D_MODEL, LANES), const),
            pl.BlockSpec((D_MODEL, LANES), const),
            pl.BlockSpec((1, LANES), const),
        ],
        out_specs=[
            pl.BlockSpec((1, ts, D_MODEL), lambda b, i: (b, i, 0)),
            pl.BlockSpec((1, ts, D_MODEL // 2), lambda b, i: (b, i, 0)),
            pl.BlockSpec((1, ts, LANES), lambda b, i: (b, i, 0)),
            pl.BlockSpec((1, ts, LANES), lambda b, i: (b, i, 0)),
            pl.BlockSpec((1, ts, LANES), lambda b, i: (b, i, 0)),
            pl.BlockSpec((SUBLANES, LANES), const),
        ],
        out_shape=[
            jax.ShapeDtypeStruct((bsz, seq, D_MODEL), jnp.float32),
            jax.ShapeDtypeStruct((bsz, seq, D_MODEL // 2), jnp.uint32),
            jax.ShapeDtypeStruct((bsz, seq, LANES), jnp.int32),
            jax.ShapeDtypeStruct((bsz, seq, LANES), jnp.float32),
            jax.ShapeDtypeStruct((bsz, seq, LANES), jnp.int32),
            jax.ShapeDtypeStruct((SUBLANES, LANES), jnp.int32),
        ],
        scratch_shapes=[pltpu.VMEM((SUBLANES, LANES), jnp.float32)],
        compiler_params=_cparams(("arbitrary", "arbitrary")),
        name="merge_router",
    )(ysb, ydf, main, x, mod3, wsb, wdf, wout, gpost, gpre, wrh, wrl, br)


def _sc_gather_rows(table, idx):
    n = idx.shape[0]
    width = table.shape[1]
    n_workers = SC_CORES * SC_SUBCORES
    per_worker = n // n_workers
    n_chunks = per_worker // SC_GATHER_ROWS
    assert n_chunks * SC_GATHER_ROWS * n_workers == n
    mesh = plsc.VectorSubcoreMesh(core_axis_name="c", subcore_axis_name="s",
                                  num_cores=SC_CORES, num_subcores=SC_SUBCORES)

    def body(table_hbm, idx_hbm, out_hbm, idx_v, rows_v, sem):
        wid = lax.axis_index("s") * SC_CORES + lax.axis_index("c")
        base = wid * per_worker

        @pl.loop(0, n_chunks)
        def _(ci):
            off = pl.multiple_of(base + ci * SC_GATHER_ROWS, SC_GATHER_ROWS)
            pltpu.sync_copy(idx_hbm.at[pl.ds(off, SC_GATHER_ROWS)], idx_v)
            pltpu.async_copy(table_hbm.at[idx_v], rows_v, sem).wait()
            pltpu.sync_copy(rows_v, out_hbm.at[pl.ds(off, SC_GATHER_ROWS)])

    return pl.kernel(
        body,
        out_type=jax.ShapeDtypeStruct((n, width), table.dtype),
        mesh=mesh,
        scratch_types=[
            pltpu.VMEM((SC_GATHER_ROWS,), jnp.int32),
            pltpu.VMEM((SC_GATHER_ROWS, width), table.dtype),
            pltpu.SemaphoreType.DMA,
        ],
        name="sc_gather_rows",
    )(table, idx)


def _sc_scatter_rows(src, pos_kmajor, n_rows):
    n_tok, width = src.shape
    n_workers = SC_CORES * SC_SUBCORES
    per_worker = n_tok // n_workers
    n_chunks = per_worker // SC_GATHER_ROWS
    assert n_chunks * SC_GATHER_ROWS * n_workers == n_tok
    mesh = plsc.VectorSubcoreMesh(core_axis_name="c", subcore_axis_name="s",
                                  num_cores=SC_CORES, num_subcores=SC_SUBCORES)

    def body(src_hbm, idx_hbm, out_hbm, idx_v, rows_v):
        wid = lax.axis_index("s") * SC_CORES + lax.axis_index("c")
        base = wid * per_worker

        @pl.loop(0, n_chunks)
        def _(ci):
            off = pl.multiple_of(base + ci * SC_GATHER_ROWS, SC_GATHER_ROWS)
            pltpu.sync_copy(src_hbm.at[pl.ds(off, SC_GATHER_ROWS)], rows_v)
            for k in range(TOP_K):
                koff = pl.multiple_of(k * n_tok + off, SC_GATHER_ROWS)
                pltpu.sync_copy(idx_hbm.at[pl.ds(koff, SC_GATHER_ROWS)], idx_v)
                pltpu.sync_copy(rows_v, out_hbm.at[idx_v])

    return pl.kernel(
        body,
        out_type=jax.ShapeDtypeStruct((n_rows, width), src.dtype),
        mesh=mesh,
        scratch_types=[
            pltpu.VMEM((SC_GATHER_ROWS,), jnp.int32),
            pltpu.VMEM((SC_GATHER_ROWS, width), src.dtype),
        ],
        name="sc_scatter_rows",
    )(src, pos_kmajor)


def _moe_ffn_kernel(te_ref, nt_ref, x_ref, wgu_ref, bgu_ref, wd_ref, bd_ref, o_ref,
                    wgu_bf, wd_bf):
    i = pl.program_id(0)
    n_valid = nt_ref[0]

    new_expert = jnp.logical_or(i == 0, te_ref[i] != te_ref[jnp.maximum(i - 1, 0)])

    @pl.when(jnp.logical_and(i < n_valid, new_expert))
    def _():
        wgu_bf[...] = wgu_ref[0].astype(jnp.bfloat16)
        wd_bf[...] = wd_ref[0].astype(jnp.bfloat16)

    @pl.when(i < n_valid)
    def _():
        xb = _unpack_bf16_pair(x_ref[...]).astype(jnp.bfloat16)
        gu = jnp.dot(xb, wgu_bf[...], preferred_element_type=jnp.float32) + bgu_ref[0]
        gate = jnp.minimum(gu[:, :D_EXPERT], SWIGLU_LIMIT)
        up = jnp.clip(gu[:, D_EXPERT:], -SWIGLU_LIMIT, SWIGLU_LIMIT)
        act = (up + 1.0) * (gate * jax.nn.sigmoid(SWIGLU_ALPHA * gate))
        out = jnp.dot(act.astype(jnp.bfloat16), wd_bf[...],
                      preferred_element_type=jnp.float32) + bd_ref[0]
        half = D_MODEL // 2
        o_ref[...] = _pack_bf16_pair(out[:, :half], out[:, half:])

    @pl.when(i >= n_valid)
    def _():
        o_ref[...] = jnp.zeros_like(o_ref)


def _moe_ffn(tile_expert, n_valid, xg, wgu, bgu, wd, bd, tm):
    n_tiles = xg.shape[0] // tm
    grid_spec = pltpu.PrefetchScalarGridSpec(
        num_scalar_prefetch=2,
        grid=(n_tiles,),
        in_specs=[
            pl.BlockSpec((tm, D_MODEL // 2), lambda i, te, nt: (jnp.minimum(i, nt[0] - 1), 0)),
            pl.BlockSpec((1, D_MODEL, 2 * D_EXPERT), lambda i, te, nt: (te[i], 0, 0)),
            pl.BlockSpec((1, 1, 2 * D_EXPERT), lambda i, te, nt: (te[i], 0, 0)),
            pl.BlockSpec((1, D_EXPERT, D_MODEL), lambda i, te, nt: (te[i], 0, 0)),
            pl.BlockSpec((1, 1, D_MODEL), lambda i, te, nt: (te[i], 0, 0)),
        ],
        out_specs=pl.BlockSpec((tm, D_MODEL // 2), lambda i, te, nt: (i, 0)),
        scratch_shapes=[
            pltpu.VMEM((D_MODEL, 2 * D_EXPERT), jnp.bfloat16),
            pltpu.VMEM((D_EXPERT, D_MODEL), jnp.bfloat16),
        ],
    )
    return pl.pallas_call(
        _moe_ffn_kernel,
        grid_spec=grid_spec,
        out_shape=jax.ShapeDtypeStruct((n_tiles * tm, D_MODEL // 2), jnp.uint32),
        compiler_params=_cparams(("arbitrary",)),
        name="moe_ffn",
    )(tile_expert, n_valid, xg, wgu, bgu.reshape(N_EXPERTS, 1, -1), wd,
      bd.reshape(N_EXPERTS, 1, -1))


def _moe_combine_kernel(rows_ref, wgt_ref, x1_ref, mod_ref, g_ref, o_ref):
    ts = x1_ref.shape[1]
    w = wgt_ref[0]
    y = jnp.zeros(x1_ref.shape[1:], jnp.float32)
    for k in range(TOP_K):
        y = y + w[:, k:k + 1] * _unpack_bf16_pair(rows_ref[k * ts:(k + 1) * ts, :])
    mod = mod_ref[0]
    o_ref[0] = x1_ref[0] + mod[5:6] * (_rms(y) * g_ref[...])


def _moe_combine(rows, wgt, x1, mod3, g_post, ts):
    bsz, seq, _ = x1.shape
    per_b = seq // ts
    return pl.pallas_call(
        _moe_combine_kernel,
        grid=(bsz, per_b),
        in_specs=[
            pl.BlockSpec((TOP_K * ts, D_MODEL // 2), lambda b, i: (b * per_b + i, 0)),
            pl.BlockSpec((1, ts, LANES), lambda b, i: (b, i, 0)),
            pl.BlockSpec((1, ts, D_MODEL), lambda b, i: (b, i, 0)),
            pl.BlockSpec((1, N_MOD, D_MODEL), lambda b, i: (b, 0, 0)),
            pl.BlockSpec((1, D_MODEL), lambda b, i: (0, 0)),
        ],
        out_specs=pl.BlockSpec((1, ts, D_MODEL), lambda b, i: (b, i, 0)),
        out_shape=jax.ShapeDtypeStruct((bsz, seq, D_MODEL), jnp.float32),
        compiler_params=_cparams(("arbitrary", "arbitrary")),
        name="moe_combine",
    )(rows, wgt, x1, mod3, g_post)


def _routing(top_idx, rank, counts, tm, n_tiles):
    padded = ((counts + tm - 1) // tm) * tm
    pend = jnp.cumsum(padded)
    pstart = pend - padded
    onehot = top_idx[:, :, None] == jnp.arange(N_EXPERTS, dtype=jnp.int32)[None, None, :]
    pos = rank + jnp.sum(jnp.where(onehot, pstart[None, None, :], 0), axis=-1)
    n_valid = (pend[-1] // tm).astype(jnp.int32)
    tile_row0 = jnp.arange(n_tiles, dtype=jnp.int32) * tm
    tile_expert = jnp.minimum(
        jnp.sum((tile_row0[:, None] >= pend[None, :]).astype(jnp.int32), axis=1), N_EXPERTS - 1)
    last_oh = jnp.arange(n_tiles, dtype=jnp.int32) == jnp.maximum(n_valid - 1, 0)
    last_expert = jnp.sum(jnp.where(last_oh, tile_expert, 0))
    tile_expert = jnp.where(jnp.arange(n_tiles, dtype=jnp.int32) < n_valid, tile_expert, last_expert)
    return pos.astype(jnp.int32), tile_expert.astype(jnp.int32), n_valid.reshape(1)


def _alibi_slopes(n_heads):
    return 2.0 ** (-8.0 * jnp.arange(1, n_heads + 1, dtype=jnp.float32) / n_heads)


def _layer(x, c, w_mod, b_mod, g_pre_mix, g_post_mix, w_in, lamv, g_subln, w_branch_sb,
           w_branch_diff, w_out, g_pre_ffn, g_post_ffn, w_router, b_router, w_gate_up,
           b_gate_up, w_down, b_down, *, ts_in, tq, ts_merge, tm, ts_comb):
    bsz, seq, d = x.shape
    n_tok = bsz * seq
    bf = jnp.bfloat16

    mod, lam = _mod_proj(c, w_mod, b_mod, lamv)
    mod3 = mod.reshape(bsz, N_MOD, d)

    o_vsb = 2 * SB_WIDTH
    o_qdf = 3 * SB_WIDTH
    o_vdf = o_qdf + 2 * DIFF_QK_WIDTH
    o_g = o_vdf + DIFF_V_WIDTH
    w_main = jnp.concatenate([w_in[:, :o_vsb], w_in[:, o_qdf:o_vdf], w_in[:, o_g:]], axis=1).astype(bf)
    w_vt = jnp.concatenate([w_in[:, o_vsb:o_qdf], w_in[:, o_vdf:o_g]], axis=1).T.astype(bf)

    main, vt = _in_proj(x, mod3, g_pre_mix.reshape(1, d), w_main, w_vt, ts_in, tq)
    y_sb = _sb_attn(main, vt, tq, tq)
    y_df = _diff_attn(main, vt, _alibi_slopes(DIFF_HEADS), lam,
                      g_subln.reshape(DIFF_V_DIM, 1), tq, tq)

    wr = jnp.zeros((d, LANES), jnp.float32).at[:, :N_EXPERTS].set(w_router)
    wrh = wr.astype(bf)
    wrl = (wr - wrh.astype(jnp.float32)).astype(bf)
    br = jnp.full((1, LANES), NEG_BIG, jnp.float32).at[0, :N_EXPERTS].set(b_router)
    x1, h2p, top_idx, top_w, rank, counts = _merge_router(
        y_sb, y_df, main, x, mod3, w_branch_sb.astype(bf), w_branch_diff.astype(bf),
        w_out.astype(bf), g_post_mix.reshape(1, d), g_pre_ffn.reshape(1, d), wrh, wrl, br, ts_merge)

    n_tiles = (n_tok * TOP_K) // tm + N_EXPERTS
    pos, tile_expert, n_valid = _routing(
        top_idx.reshape(n_tok, LANES)[:, :TOP_K], rank.reshape(n_tok, LANES)[:, :TOP_K],
        counts[0, :N_EXPERTS], tm, n_tiles)
    xg = _sc_scatter_rows(h2p.reshape(n_tok, d // 2), pos.T.reshape(TOP_K * n_tok), n_tiles * tm)
    rows = _moe_ffn(tile_expert, n_valid, xg, w_gate_up, b_gate_up, w_down, b_down, tm)
    pos_steps = pos.reshape(n_tok // ts_comb, ts_comb, TOP_K).swapaxes(1, 2).reshape(n_tok * TOP_K)
    tok_rows = _sc_gather_rows(rows, pos_steps)
    return _moe_combine(tok_rows, top_w, x1, mod3, g_post_ffn.reshape(1, d), ts_comb)


def kernel(x, c, w_mod, b_mod, g_pre_mix, g_post_mix, w_in, lambda_q1, lambda_k1, lambda_q2,
           lambda_k2, g_subln, w_branch_sb, w_branch_diff, w_out, g_pre_ffn, g_post_ffn,
           w_router, b_router, w_gate_up, b_gate_up, w_down, b_down):
    depth = w_mod.shape[0]
    for l in range(depth):
        lamv = jnp.stack([lambda_q1[l], lambda_k1[l], lambda_q2[l], lambda_k2[l]])
        x = _layer(x, c, w_mod[l], b_mod[l], g_pre_mix[l], g_post_mix[l], w_in[l], lamv,
                   g_subln[l], w_branch_sb[l], w_branch_diff[l], w_out[l], g_pre_ffn[l],
                   g_post_ffn[l], w_router[l], b_router[l], w_gate_up[l], b_gate_up[l],
                   w_down[l], b_down[l],
                   ts_in=512, tq=256, ts_merge=512, tm=512, ts_comb=256)
    return x
```

```python
import functools
import math

import jax
import jax.numpy as jnp
from jax import lax
from jax.experimental import pallas as pl
from jax.experimental.pallas import tpu as pltpu
from jax.experimental.pallas import tpu_sc as plsc

D_MODEL = 1024
SB_HEADS = 8
SB_HEAD_DIM = 64
SB_WIDTH = SB_HEADS * SB_HEAD_DIM
DIFF_HEADS = 4
DIFF_HEAD_DIM = 64
DIFF_V_DIM = 2 * DIFF_HEAD_DIM
DIFF_QK_WIDTH = DIFF_HEADS * 2 * DIFF_HEAD_DIM
DIFF_V_WIDTH = DIFF_HEADS * DIFF_V_DIM
N_EXPERTS = 32
TOP_K = 4
D_EXPERT = D_MODEL
SWIGLU_LIMIT = 7.0
SWIGLU_ALPHA = 1.702
RMS_EPS = 1e-6
N_MOD = 6
LAM_INIT = 0.8 - 0.6 * math.exp(-0.3 * 0)

LANES = 128
SUBLANES = 8
NEG_BIG = -1e30
EXP_ZERO_MARGIN = 110.0
NORM_SLACK = 1.01
SC_CORES = 2
SC_SUBCORES = 16
SC_GATHER_ROWS = 128
MERGE_SUB = 256

MAIN_WIDTH = 2 * SB_WIDTH + 2 * DIFF_QK_WIDTH + 2 * D_MODEL
VT_ROWS = SB_WIDTH + DIFF_V_WIDTH
COLBLK_K_SB = SB_WIDTH // LANES
COLBLK_Q_DF = 2 * SB_WIDTH // LANES
COLBLK_K_DF = COLBLK_Q_DF + DIFF_QK_WIDTH // LANES
GATE_COL0 = 2 * SB_WIDTH + 2 * DIFF_QK_WIDTH

VMEM_LIMIT = 56 * 1024 * 1024


def _cparams(sem, vmem=VMEM_LIMIT):
    return pltpu.CompilerParams(dimension_semantics=sem, vmem_limit_bytes=vmem)


def _rms(x):
    return x * lax.rsqrt(jnp.mean(x * x, axis=-1, keepdims=True) + RMS_EPS)


def _mod_kernel(c_ref, w_ref, b_ref, lamv_ref, mod_ref, lam_ref):
    c = c_ref[...]
    ca = c * jax.nn.sigmoid(c)
    mod_ref[...] = jnp.dot(ca, w_ref[...], preferred_element_type=jnp.float32,
                           precision=lax.Precision.HIGHEST) + b_ref[...]
    lv = lamv_ref[...]
    s1 = jnp.sum(lv[0:1] * lv[1:2], axis=-1, keepdims=True)
    s2 = jnp.sum(lv[2:3] * lv[3:4], axis=-1, keepdims=True)
    lam = jnp.exp(s1) - jnp.exp(s2) + LAM_INIT
    lam_ref[...] = jnp.broadcast_to(lam, lam_ref.shape)


def _mod_proj(c, w_mod, b_mod, lamv):
    bsz = c.shape[0]
    tn = 1536
    n = w_mod.shape[1]
    return pl.pallas_call(
        _mod_kernel,
        grid=(n // tn,),
        in_specs=[
            pl.BlockSpec((bsz, D_MODEL), lambda j: (0, 0)),
            pl.BlockSpec((D_MODEL, tn), lambda j: (0, j)),
            pl.BlockSpec((1, tn), lambda j: (0, j)),
            pl.BlockSpec((4, DIFF_HEAD_DIM), lambda j: (0, 0)),
        ],
        out_specs=[
            pl.BlockSpec((bsz, tn), lambda j: (0, j)),
            pl.BlockSpec((SUBLANES, LANES), lambda j: (0, 0)),
        ],
        out_shape=[
            jax.ShapeDtypeStruct((bsz, n), jnp.float32),
            jax.ShapeDtypeStruct((SUBLANES, LANES), jnp.float32),
        ],
        compiler_params=_cparams(("arbitrary",)),
        name="mod_proj",
    )(c, w_mod, b_mod.reshape(1, n), lamv)


IN_CHUNK = 1024


def _in_proj_kernel(x_ref, mod_ref, g_ref, wm_ref, wvt_ref, main_ref, vt_ref, h_scr, *, tk):
    x = x_ref[0]
    mod = mod_ref[0]
    h = _rms(x) * g_ref[...]
    h = h * (1.0 + mod[1:2]) + mod[0:1]
    hb = h.astype(jnp.bfloat16)
    groups = tk // SUBLANES
    cols = []
    for ct in range(D_MODEL // LANES):
        h_scr[ct] = h[:, ct * LANES:(ct + 1) * LANES]
        pieces = []
        for blk in range(h.shape[0] // tk):
            for g in range(groups):
                pieces.append(h_scr[ct, pl.ds(blk * tk + g, SUBLANES, stride=groups), :])
        cols.append(jnp.concatenate(pieces, axis=0))
    hpb = jnp.concatenate(cols, axis=1).astype(jnp.bfloat16)

    half = IN_CHUNK // 2
    for ci in range(MAIN_WIDTH // IN_CHUNK):
        c0 = ci * IN_CHUNK
        if c0 == 0:
            q = jnp.dot(hb, wm_ref[:, :half], preferred_element_type=jnp.float32)
            k = jnp.dot(hpb, wm_ref[:, half:IN_CHUNK], preferred_element_type=jnp.float32)
            main_ref[0, :, :half] = (q * 0.0625).astype(jnp.bfloat16)
            main_ref[0, :, half:IN_CHUNK] = k.astype(jnp.bfloat16)
            continue
        p = jnp.dot(hb, wm_ref[:, c0:c0 + IN_CHUNK], preferred_element_type=jnp.float32)
        if c0 < GATE_COL0:
            main_ref[0, :, c0:c0 + half] = (p[:, :half] * 0.125).astype(jnp.bfloat16)
            main_ref[0, :, c0 + half:c0 + IN_CHUNK] = p[:, half:].astype(jnp.bfloat16)
        else:
            main_ref[0, :, c0:c0 + IN_CHUNK] = jax.nn.sigmoid(p).astype(jnp.bfloat16)
    nt = (((1,), (1,)), ((), ()))
    vt_sb = lax.dot_general(wvt_ref[:SB_WIDTH, :], hpb, nt, preferred_element_type=jnp.float32)
    vt_df = lax.dot_general(wvt_ref[SB_WIDTH:, :], hb, nt, preferred_element_type=jnp.float32)
    vt_ref[0, :SB_WIDTH, :] = vt_sb.astype(jnp.bfloat16)
    vt_ref[0, SB_WIDTH:, :] = vt_df.astype(jnp.bfloat16)


def _in_proj(x, mod3, g_pre, w_main, w_vt, ts, tk):
    bsz, seq, _ = x.shape
    assert ts % tk == 0
    return pl.pallas_call(
        functools.partial(_in_proj_kernel, tk=tk),
        grid=(bsz, seq // ts),
        in_specs=[
            pl.BlockSpec((1, ts, D_MODEL), lambda b, i: (b, i, 0)),
            pl.BlockSpec((1, N_MOD, D_MODEL), lambda b, i: (b, 0, 0)),
            pl.BlockSpec((1, D_MODEL), lambda b, i: (0, 0)),
            pl.BlockSpec((D_MODEL, MAIN_WIDTH), lambda b, i: (0, 0)),
            pl.BlockSpec((VT_ROWS, D_MODEL), lambda b, i: (0, 0)),
        ],
        out_specs=[
            pl.BlockSpec((1, ts, MAIN_WIDTH), lambda b, i: (b, i, 0)),
            pl.BlockSpec((1, VT_ROWS, ts), lambda b, i: (b, 0, i)),
        ],
        out_shape=[
            jax.ShapeDtypeStruct((bsz, seq, MAIN_WIDTH), jnp.bfloat16),
            jax.ShapeDtypeStruct((bsz, VT_ROWS, seq), jnp.bfloat16),
        ],
        scratch_shapes=[pltpu.VMEM((D_MODEL // LANES, ts, LANES), jnp.float32)],
        compiler_params=_cparams(("arbitrary", "arbitrary")),
        name="in_proj",
    )(x, mod3, g_pre, w_main, w_vt)


def _suffix_excl_prod8(tot):
    sub = lax.broadcasted_iota(jnp.int32, tot.shape, 0)
    x = jnp.where(sub < SUBLANES - 1, pltpu.roll(tot, SUBLANES - 1, 0), 1.0)
    for sh in (1, 2, 4):
        x = x * jnp.where(sub + sh < SUBLANES, pltpu.roll(x, SUBLANES - sh, 0), 1.0)
    return x


def _sb_scores(k_ref, q_heads, s_ref, slot, j, tk):
    kb = k_ref[0, pl.ds(pl.multiple_of(j * tk, tk), tk), :]
    for h in range(2):
        s_ref[slot, h] = lax.dot_general(kb, q_heads[h], (((1,), (1,)), ((), ())),
                                         preferred_element_type=jnp.float32)


def _sb_weights(zt, c8, ok, groups):
    tq = zt.shape[1]
    r = 0.5 - 0.5 * jnp.tanh(zt)
    if ok is not None:
        r = jnp.where(ok, r, 1.0)
    rg = [r[g * SUBLANES:(g + 1) * SUBLANES, :] for g in range(groups)]
    tot = rg[0]
    for g in range(1, groups):
        tot = tot * rg[g]
    p = c8 * _suffix_excl_prod8(tot)
    pieces = [None] * groups
    for g in range(groups - 1, -1, -1):
        pn = p * rg[g]
        pieces[g] = p - pn
        p = pn
    a = jnp.concatenate(pieces, axis=0).astype(jnp.bfloat16)
    return a, jnp.broadcast_to(p[0:1, :], (SUBLANES, tq))


def _sb_attn_kernel(q_ref, k_ref, v_ref, o_ref, acc_ref, c_ref, s_ref, ok_ref, *, tq, tk):
    i = pl.program_id(2)
    groups = tk // SUBLANES
    q2 = q_ref[0]
    lane = lax.broadcasted_iota(jnp.int32, q2.shape, 1)
    zero = jnp.zeros_like(q2)
    q_heads = (jnp.where(lane < SB_HEAD_DIM, q2, zero), jnp.where(lane < SB_HEAD_DIM, zero, q2))

    def step(j, slot, masked):
        _sb_scores(k_ref, q_heads, s_ref, 1 - slot, jnp.maximum(j - 1, 0), tk)
        ok = (ok_ref[...] > 0.5) if masked else None
        off = pl.multiple_of(j * tk, tk)
        ws = []
        for h in range(2):
            a, c_new = _sb_weights(s_ref[slot, h], c_ref[h], ok, groups)
            c_ref[h] = c_new
            ws.append(a)
        for h in range(2):
            vt_h = v_ref[0, h * SB_HEAD_DIM:(h + 1) * SB_HEAD_DIM, pl.ds(off, tk)]
            acc_ref[h] += jnp.dot(vt_h, ws[h], preferred_element_type=jnp.float32)

    @pl.when(i == 0)
    def _():
        row = lax.broadcasted_iota(jnp.int32, (tk, tq), 0)
        col = lax.broadcasted_iota(jnp.int32, (tk, tq), 1)
        ok_ref[...] = jnp.where((row % SUBLANES) * groups + row // SUBLANES < col, 1.0, 0.0)

    acc_ref[...] = jnp.zeros_like(acc_ref)
    c_ref[...] = jnp.ones_like(c_ref)
    _sb_scores(k_ref, q_heads, s_ref, 0, i, tk)
    step(i, 0, True)

    def stick_left():
        return jnp.max(c_ref[...]) > 0.0

    def more(state):
        m, go = state
        return jnp.logical_and(m < i // 2, go)

    def pair(state):
        m, _ = state
        j = i - 1 - 2 * m
        step(j, 1, False)
        go_on = stick_left()

        @pl.when(go_on)
        def _():
            step(j - 1, 0, False)

        return m + 1, jnp.logical_and(go_on, stick_left())

    m_done, go = lax.while_loop(more, pair, (jnp.int32(0), stick_left()))

    @pl.when(jnp.logical_and(jnp.logical_and(i % 2 == 1, m_done == i // 2), go))
    def _():
        step(0, 1, False)

    ot = jnp.concatenate([acc_ref[0], acc_ref[1]], axis=0)
    o_ref[0] = ot.T.astype(jnp.bfloat16)


def _sb_attn(main, vt, tq, tk):
    bsz, seq, _ = main.shape
    assert tq == tk
    kern = functools.partial(_sb_attn_kernel, tq=tq, tk=tk)
    return pl.pallas_call(
        kern,
        grid=(bsz, SB_WIDTH // LANES, seq // tq),
        in_specs=[
            pl.BlockSpec((1, tq, LANES), lambda b, p, i: (b, i, p)),
            pl.BlockSpec((1, seq, LANES), lambda b, p, i: (b, 0, COLBLK_K_SB + p)),
            pl.BlockSpec((1, LANES, seq), lambda b, p, i: (b, p, 0)),
        ],
        out_specs=pl.BlockSpec((1, tq, LANES), lambda b, p, i: (b, i, p)),
        out_shape=jax.ShapeDtypeStruct((bsz, seq, SB_WIDTH), jnp.bfloat16),
        scratch_shapes=[
            pltpu.VMEM((2, SB_HEAD_DIM, tq), jnp.float32),
            pltpu.VMEM((2, SUBLANES, tq), jnp.float32),
            pltpu.VMEM((2, 2, tk, tq), jnp.float32),
            pltpu.VMEM((tk, tq), jnp.float32),
        ],
        compiler_params=_cparams(("arbitrary", "arbitrary", "arbitrary")),
        name="sb_attn",
    )(main, main, vt)


def _diff_attn_kernel(slopes_ref, inv_slopes_ref, q_ref, k_ref, v_ref, lam_ref, g_ref, o_ref,
                      acc_ref, m_ref, l_ref, s_ref, kn_ref, bias_ref, *, tq, tk):
    hd = pl.program_id(1)
    i = pl.program_id(2)
    slope = slopes_ref[hd]

    @pl.when(i == 0)
    def _():
        kf = k_ref[0].astype(jnp.float32)
        kn2 = jnp.max(jnp.sum(kf * kf, axis=-1, keepdims=True), axis=0, keepdims=True)
        kn_ref[...] = jnp.broadcast_to(kn2, kn_ref.shape)
        row0 = lax.broadcasted_iota(jnp.int32, (tk, tq), 0)
        col0 = lax.broadcasted_iota(jnp.int32, (tk, tq), 1)
        bias_ref[...] = slope * (row0 - col0).astype(jnp.float32)

    q2 = q_ref[0]
    lane = lax.broadcasted_iota(jnp.int32, q2.shape, 1)
    zero = jnp.zeros_like(q2)
    q_maps = (jnp.where(lane < DIFF_HEAD_DIM, q2, zero), jnp.where(lane < DIFF_HEAD_DIM, zero, q2))

    def scores(slot, j):
        kb = k_ref[0, pl.ds(pl.multiple_of(j * tk, tk), tk), :]
        for m in range(2):
            s_ref[slot, m] = lax.dot_general(kb, q_maps[m], (((1,), (1,)), ((), ())),
                                             preferred_element_type=jnp.float32)

    def step(j, slot, masked):
        scores(1 - slot, jnp.maximum(j - 1, 0))
        off = pl.multiple_of(j * tk, tk)
        vtb = v_ref[0, :, pl.ds(off, tk)]
        cb = slope * ((j - i) * tk).astype(jnp.float32)
        ps, alphas = [], []
        for m in range(2):
            s = s_ref[slot, m] + bias_ref[...]
            if masked:
                row = lax.broadcasted_iota(jnp.int32, (tk, tq), 0)
                col = lax.broadcasted_iota(jnp.int32, (tk, tq), 1)
                s = jnp.where(row <= col, s, NEG_BIG)
            m_old = m_ref[m]
            m_new = jnp.maximum(m_old, jnp.max(s, axis=0, keepdims=True) + cb)
            alpha = jnp.exp(m_old - m_new)
            p = jnp.exp(s - (m_new - cb))
            l_ref[m] = alpha * l_ref[m] + jnp.sum(p, axis=0, keepdims=True)
            m_ref[m] = m_new
            ps.append(p.astype(jnp.bfloat16))
            alphas.append(alpha)
        for m in range(2):
            acc_ref[m] = alphas[m] * acc_ref[m] + jnp.dot(
                vtb, ps[m], preferred_element_type=jnp.float32)

    acc_ref[...] = jnp.zeros_like(acc_ref)
    m_ref[...] = jnp.full_like(m_ref, NEG_BIG)
    l_ref[...] = jnp.zeros_like(l_ref)
    scores(0, i)
    step(i, 0, True)

    qf = q2.astype(jnp.float32)
    qn2 = jnp.max(jnp.sum(qf * qf, axis=-1, keepdims=True), axis=0, keepdims=True)
    zabs = jnp.sqrt(qn2 * kn_ref[0:1, 0:1]) * NORM_SLACK
    m_lo = jnp.min(jnp.minimum(m_ref[0], m_ref[1]), axis=1, keepdims=True)
    reach = (EXP_ZERO_MARGIN + zabs - m_lo) * inv_slopes_ref[hd]
    n_need = jnp.floor(jnp.clip((reach - 1.0) * (1.0 / tk), -1.0, 1e6)) + 1.0
    n_back = jnp.minimum(i, jnp.max(n_need).astype(jnp.int32))

    def pair(n, carry):
        j = i - 1 - 2 * n
        step(j, 1, False)
        step(j - 1, 0, False)
        return carry

    lax.fori_loop(0, n_back // 2, pair, 0)

    @pl.when(n_back % 2 == 1)
    def _():
        step(i - n_back, 1, False)

    lam = lam_ref[0:1, 0:1]
    o = acc_ref[0] / l_ref[0] - lam * (acc_ref[1] / l_ref[1])
    ms = jnp.mean(o * o, axis=0, keepdims=True)
    y = o * lax.rsqrt(ms + RMS_EPS) * g_ref[...] * (1.0 - LAM_INIT)
    o_ref[0] = y.T.astype(jnp.bfloat16)


def _diff_attn(main, vt, slopes, lam, g_col, tq, tk):
    bsz, seq, _ = main.shape
    assert tq == tk
    kern = functools.partial(_diff_attn_kernel, tq=tq, tk=tk)
    vrow0 = SB_WIDTH // LANES
    return pl.pallas_call(
        kern,
        grid=(bsz, DIFF_HEADS, seq // tq),
        in_specs=[
            pl.BlockSpec(memory_space=pltpu.SMEM),
            pl.BlockSpec(memory_space=pltpu.SMEM),
            pl.BlockSpec((1, tq, LANES), lambda b, h, i: (b, i, COLBLK_Q_DF + h)),
            pl.BlockSpec((1, seq, LANES), lambda b, h, i: (b, 0, COLBLK_K_DF + h)),
            pl.BlockSpec((1, DIFF_V_DIM, seq), lambda b, h, i: (b, vrow0 + h, 0)),
            pl.BlockSpec((SUBLANES, LANES), lambda b, h, i: (0, 0)),
            pl.BlockSpec((DIFF_V_DIM, 1), lambda b, h, i: (0, 0)),
        ],
        out_specs=pl.BlockSpec((1, tq, DIFF_V_DIM), lambda b, h, i: (b, i, h)),
        out_shape=jax.ShapeDtypeStruct((bsz, seq, DIFF_V_WIDTH), jnp.bfloat16),
        scratch_shapes=[
            pltpu.VMEM((2, DIFF_V_DIM, tq), jnp.float32),
            pltpu.VMEM((2, 1, tq), jnp.float32),
            pltpu.VMEM((2, 1, tq), jnp.float32),
            pltpu.VMEM((2, 2, tk, tq), jnp.float32),
            pltpu.VMEM((SUBLANES, LANES), jnp.float32),
            pltpu.VMEM((tk, tq), jnp.float32),
        ],
        compiler_params=_cparams(("arbitrary", "arbitrary", "arbitrary")),
        name="diff_attn",
    )(slopes, 1.0 / slopes, main, main, vt, lam, g_col)


def _pack_bf16_pair(a, b):
    ab = pltpu.bitcast(a.astype(jnp.bfloat16).astype(jnp.float32), jnp.uint32)
    bb = pltpu.bitcast(b.astype(jnp.bfloat16).astype(jnp.float32), jnp.uint32)
    return ab | (bb >> 16)


def _unpack_bf16_pair(w):
    hi = pltpu.bitcast(w & jnp.uint32(0xFFFF0000), jnp.float32)
    lo = pltpu.bitcast(w << 16, jnp.float32)
    return jnp.concatenate([hi, lo], axis=1)


def _merge_router_kernel(ysb_ref, ydf_ref, gates_ref, x_ref, mod_ref, wsb_ref, wdf_ref, wout_ref,
                         gpost_ref, gpre_ref, wrh_ref, wrl_ref, br_ref,
                         x1_ref, h2_ref, idx_ref, wgt_ref, rank_ref, cnt_ref, base_ref):
    first = jnp.logical_and(pl.program_id(0) == 0, pl.program_id(1) == 0)

    @pl.when(first)
    def _():
        base_ref[...] = jnp.zeros_like(base_ref)

    mod = mod_ref[0]
    subs = [slice(s * MERGE_SUB, (s + 1) * MERGE_SUB) for s in range(x_ref.shape[1] // MERGE_SUB)]
    half = D_MODEL // 2
    branch = [(jnp.dot(ysb_ref[0, rows, :], wsb_ref[...], preferred_element_type=jnp.float32),
               jnp.dot(ydf_ref[0, rows, :], wdf_ref[...], preferred_element_type=jnp.float32))
              for rows in subs]
    merged = []
    for rows, (a, b) in zip(subs, branch):
        g = gates_ref[0, rows, :].astype(jnp.float32)
        merged.append((g[:, :D_MODEL] * a + g[:, D_MODEL:] * b).astype(jnp.bfloat16))
    mixes = [jnp.dot(m, wout_ref[...], preferred_element_type=jnp.float32) for m in merged]
    h2s = []
    for rows, mix in zip(subs, mixes):
        x1 = x_ref[0, rows, :] + mod[2:3] * (_rms(mix) * gpost_ref[...])
        x1_ref[0, rows, :] = x1
        h2 = _rms(x1) * gpre_ref[...]
        h2 = h2 * (1.0 + mod[4:5]) + mod[3:4]
        h2_ref[0, rows, :] = _pack_bf16_pair(h2[:, :half], h2[:, half:])
        h2s.append(h2)
    logit_list = []
    for h2 in h2s:
        hh = h2.astype(jnp.bfloat16)
        hl = (h2 - hh.astype(jnp.float32)).astype(jnp.bfloat16)
        logit_list.append(jnp.dot(hh, wrh_ref[...], preferred_element_type=jnp.float32)
                          + jnp.dot(hh, wrl_ref[...], preferred_element_type=jnp.float32)
                          + jnp.dot(hl, wrh_ref[...], preferred_element_type=jnp.float32)
                          + br_ref[...])
    for rows, logits in zip(subs, logit_list):
        _route_rows(rows, logits, idx_ref, wgt_ref, rank_ref, cnt_ref, base_ref)


def _route_rows(rows, logits, idx_ref, wgt_ref, rank_ref, cnt_ref, base_ref):
    lane = lax.broadcasted_iota(jnp.int32, logits.shape, 1)
    lanef = lane.astype(jnp.float32)
    vals, idxs = [], []
    cur = logits
    for _ in range(TOP_K):
        mx = jnp.max(cur, axis=-1, keepdims=True)
        ix = jnp.min(jnp.where(cur == mx, lanef, float(LANES)), axis=-1, keepdims=True)
        cur = jnp.where(lanef == ix, -jnp.inf, cur)
        vals.append(mx)
        idxs.append(ix)
    es = [jnp.exp(v - vals[0]) for v in vals]
    den = es[0] + es[1] + es[2] + es[3]
    oi = jnp.zeros(logits.shape, jnp.float32)
    ow = jnp.zeros(logits.shape, jnp.float32)
    for k in range(TOP_K):
        oi = jnp.where(lane == k, idxs[k], oi)
        ow = jnp.where(lane == k, es[k] / den, ow)
    idx_ref[0, rows, :] = oi.astype(jnp.int32)
    wgt_ref[0, rows, :] = ow

    ts = logits.shape[0]
    member = jnp.zeros(logits.shape, jnp.float32)
    for k in range(TOP_K):
        member = member + (lanef == idxs[k]).astype(jnp.float32)
    rr = lax.broadcasted_iota(jnp.int32, (ts, ts), 0)
    cc = lax.broadcasted_iota(jnp.int32, (ts, ts), 1)
    lower = jnp.where(cc < rr, 1.0, 0.0).astype(jnp.bfloat16)
    before = jnp.dot(lower, member.astype(jnp.bfloat16), preferred_element_type=jnp.float32)
    base = base_ref[0:1, :]
    rank_all = before + base
    orank = jnp.zeros(logits.shape, jnp.float32)
    for k in range(TOP_K):
        rk = jnp.sum(jnp.where(lanef == idxs[k], rank_all, 0.0), axis=-1, keepdims=True)
        orank = jnp.where(lane == k, rk, orank)
    rank_ref[0, rows, :] = orank.astype(jnp.int32)
    new_base = base + jnp.sum(member, axis=0, keepdims=True)
    base_ref[...] = jnp.broadcast_to(new_base, base_ref.shape)
    cnt_ref[...] = jnp.broadcast_to(new_base, cnt_ref.shape).astype(jnp.int32)


def _merge_router(ysb, ydf, main, x, mod3, wsb, wdf, wout, gpost, gpre, wrh, wrl, br, ts):
    bsz, seq, _ = x.shape
    const = lambda b, i: (0, 0)
    return pl.pallas_call(
        _merge_router_kernel,
        grid=(bsz, seq // ts),
        in_specs=[
            pl.BlockSpec((1, ts, SB_WIDTH), lambda b, i: (b, i, 0)),
            pl.BlockSpec((1, ts, DIFF_V_WIDTH), lambda b, i: (b, i, 0)),
            pl.BlockSpec((1, ts, 2 * D_MODEL), lambda b, i: (b, i, GATE_COL0 // (2 * D_MODEL))),
            pl.BlockSpec((1, ts, D_MODEL), lambda b, i: (b, i, 0)),
            pl.BlockSpec((1, N_MOD, D_MODEL), lambda b, i: (b, 0, 0)),
            pl.BlockSpec((SB_WIDTH, D_MODEL), const),
            pl.BlockSpec((DIFF_V_WIDTH, D_MODEL), const),
            pl.BlockSpec((D_MODEL, D_MODEL), const),
            pl.BlockSpec((1, D_MODEL), const),
            pl.BlockSpec((1, D_MODEL), const),
            pl.BlockSpec((D_MODEL, LANES), const),
            pl.BlockSpec((D_MODEL, LANES), const),
            pl.BlockSpec((1, LANES), const),
        ],
        out_specs=[
            pl.BlockSpec((1, ts, D_MODEL), lambda b, i: (b, i, 0)),
            pl.BlockSpec((1, ts, D_MODEL // 2), lambda b, i: (b, i, 0)),
            pl.BlockSpec((1, ts, LANES), lambda b, i: (b, i, 0)),
            pl.BlockSpec((1, ts, LANES), lambda b, i: (b, i, 0)),
            pl.BlockSpec((1, ts, LANES), lambda b, i: (b, i, 0)),
            pl.BlockSpec((SUBLANES, LANES), const),
        ],
        out_shape=[
            jax.ShapeDtypeStruct((bsz, seq, D_MODEL), jnp.float32),
            jax.ShapeDtypeStruct((bsz, seq, D_MODEL // 2), jnp.uint32),
            jax.ShapeDtypeStruct((bsz, seq, LANES), jnp.int32),
            jax.ShapeDtypeStruct((bsz, seq, LANES), jnp.float32),
            jax.ShapeDtypeStruct((bsz, seq, LANES), jnp.int32),
            jax.ShapeDtypeStruct((SUBLANES, LANES), jnp.int32),
        ],
        scratch_shapes=[pltpu.VMEM((SUBLANES, LANES), jnp.float32)],
        compiler_params=_cparams(("arbitrary", "arbitrary")),
        name="merge_router",
    )(ysb, ydf, main, x, mod3, wsb, wdf, wout, gpost, gpre, wrh, wrl, br)


def _sc_gather_rows(table, idx):
    n = idx.shape[0]
    width = table.shape[1]
    n_workers = SC_CORES * SC_SUBCORES
    per_worker = n // n_workers
    n_chunks = per_worker // SC_GATHER_ROWS
    assert n_chunks * SC_GATHER_ROWS * n_workers == n
    mesh = plsc.VectorSubcoreMesh(core_axis_name="c", subcore_axis_name="s",
                                  num_cores=SC_CORES, num_subcores=SC_SUBCORES)

    def body(table_hbm, idx_hbm, out_hbm, idx_v, rows_v, sem):
        wid = lax.axis_index("s") * SC_CORES + lax.axis_index("c")
        base = wid * per_worker

        @pl.loop(0, n_chunks)
        def _(ci):
            off = pl.multiple_of(base + ci * SC_GATHER_ROWS, SC_GATHER_ROWS)
            pltpu.sync_copy(idx_hbm.at[pl.ds(off, SC_GATHER_ROWS)], idx_v)
            pltpu.async_copy(table_hbm.at[idx_v], rows_v, sem).wait()
            pltpu.sync_copy(rows_v, out_hbm.at[pl.ds(off, SC_GATHER_ROWS)])

    return pl.kernel(
        body,
        out_type=jax.ShapeDtypeStruct((n, width), table.dtype),
        mesh=mesh,
        scratch_types=[
            pltpu.VMEM((SC_GATHER_ROWS,), jnp.int32),
            pltpu.VMEM((SC_GATHER_ROWS, width), table.dtype),
            pltpu.SemaphoreType.DMA,
        ],
        name="sc_gather_rows",
    )(table, idx)


def _sc_scatter_rows(src, pos_kmajor, n_rows):
    n_tok, width = src.shape
    n_workers = SC_CORES * SC_SUBCORES
    per_worker = n_tok // n_workers
    n_chunks = per_worker // SC_GATHER_ROWS
    assert n_chunks * SC_GATHER_ROWS * n_workers == n_tok
    mesh = plsc.VectorSubcoreMesh(core_axis_name="c", subcore_axis_name="s",
                                  num_cores=SC_CORES, num_subcores=SC_SUBCORES)

    def body(src_hbm, idx_hbm, out_hbm, idx_v, rows_v):
        wid = lax.axis_index("s") * SC_CORES + lax.axis_index("c")
        base = wid * per_worker

        @pl.loop(0, n_chunks)
        def _(ci):
            off = pl.multiple_of(base + ci * SC_GATHER_ROWS, SC_GATHER_ROWS)
            pltpu.sync_copy(src_hbm.at[pl.ds(off, SC_GATHER_ROWS)], rows_v)
            for k in range(TOP_K):
                koff = pl.multiple_of(k * n_tok + off, SC_GATHER_ROWS)
                pltpu.sync_copy(idx_hbm.at[pl.ds(koff, SC_GATHER_ROWS)], idx_v)
                pltpu.sync_copy(rows_v, out_hbm.at[idx_v])

    return pl.kernel(
        body,
        out_type=jax.ShapeDtypeStruct((n_rows, width), src.dtype),
        mesh=mesh,
        scratch_types=[
            pltpu.VMEM((SC_GATHER_ROWS,), jnp.int32),
            pltpu.VMEM((SC_GATHER_ROWS, width), src.dtype),
        ],
        name="sc_scatter_rows",
    )(src, pos_kmajor)


def _moe_ffn_kernel(te_ref, nt_ref, x_ref, wgu_ref, bgu_ref, wd_ref, bd_ref, o_ref,
                    wgu_bf, wd_bf):
    i = pl.program_id(0)
    n_valid = nt_ref[0]

    new_expert = jnp.logical_or(i == 0, te_ref[i] != te_ref[jnp.maximum(i - 1, 0)])

    @pl.when(jnp.logical_and(i < n_valid, new_expert))
    def _():
        wgu_bf[...] = wgu_ref[0].astype(jnp.bfloat16)
        wd_bf[...] = wd_ref[0].astype(jnp.bfloat16)

    @pl.when(i < n_valid)
    def _():
        xb = _unpack_bf16_pair(x_ref[...]).astype(jnp.bfloat16)
        gu = jnp.dot(xb, wgu_bf[...], preferred_element_type=jnp.float32) + bgu_ref[0]
        gate = jnp.minimum(gu[:, :D_EXPERT], SWIGLU_LIMIT)
        up = jnp.clip(gu[:, D_EXPERT:], -SWIGLU_LIMIT, SWIGLU_LIMIT)
        act = (up + 1.0) * (gate * jax.nn.sigmoid(SWIGLU_ALPHA * gate))
        out = jnp.dot(act.astype(jnp.bfloat16), wd_bf[...],
                      preferred_element_type=jnp.float32) + bd_ref[0]
        half = D_MODEL // 2
        o_ref[...] = _pack_bf16_pair(out[:, :half], out[:, half:])

    @pl.when(i >= n_valid)
    def _():
        o_ref[...] = jnp.zeros_like(o_ref)


def _moe_ffn(tile_expert, n_valid, xg, wgu, bgu, wd, bd, tm):
    n_tiles = xg.shape[0] // tm
    grid_spec = pltpu.PrefetchScalarGridSpec(
        num_scalar_prefetch=2,
        grid=(n_tiles,),
        in_specs=[
            pl.BlockSpec((tm, D_MODEL // 2), lambda i, te, nt: (jnp.minimum(i, nt[0] - 1), 0)),
            pl.BlockSpec((1, D_MODEL, 2 * D_EXPERT), lambda i, te, nt: (te[i], 0, 0)),
            pl.BlockSpec((1, 1, 2 * D_EXPERT), lambda i, te, nt: (te[i], 0, 0)),
            pl.BlockSpec((1, D_EXPERT, D_MODEL), lambda i, te, nt: (te[i], 0, 0)),
            pl.BlockSpec((1, 1, D_MODEL), lambda i, te, nt: (te[i], 0, 0)),
        ],
        out_specs=pl.BlockSpec((tm, D_MODEL // 2), lambda i, te, nt: (i, 0)),
        scratch_shapes=[
            pltpu.VMEM((D_MODEL, 2 * D_EXPERT), jnp.bfloat16),
            pltpu.VMEM((D_EXPERT, D_MODEL), jnp.bfloat16),
        ],
    )
    return pl.pallas_call(
        _moe_ffn_kernel,
        grid_spec=grid_spec,
        out_shape=jax.ShapeDtypeStruct((n_tiles * tm, D_MODEL // 2), jnp.uint32),
        compiler_params=_cparams(("arbitrary",)),
        name="moe_ffn",
    )(tile_expert, n_valid, xg, wgu, bgu.reshape(N_EXPERTS, 1, -1), wd,
      bd.reshape(N_EXPERTS, 1, -1))


def _moe_combine_kernel(rows_ref, wgt_ref, x1_ref, mod_ref, g_ref, o_ref):
    ts = x1_ref.shape[1]
    w = wgt_ref[0]
    y = jnp.zeros(x1_ref.shape[1:], jnp.float32)
    for k in range(TOP_K):
        y = y + w[:, k:k + 1] * _unpack_bf16_pair(rows_ref[k * ts:(k + 1) * ts, :])
    mod = mod_ref[0]
    o_ref[0] = x1_ref[0] + mod[5:6] * (_rms(y) * g_ref[...])


def _moe_combine(rows, wgt, x1, mod3, g_post, ts):
    bsz, seq, _ = x1.shape
    per_b = seq // ts
    return pl.pallas_call(
        _moe_combine_kernel,
        grid=(bsz, per_b),
        in_specs=[
            pl.BlockSpec((TOP_K * ts, D_MODEL // 2), lambda b, i: (b * per_b + i, 0)),
            pl.BlockSpec((1, ts, LANES), lambda b, i: (b, i, 0)),
            pl.BlockSpec((1, ts, D_MODEL), lambda b, i: (b, i, 0)),
            pl.BlockSpec((1, N_MOD, D_MODEL), lambda b, i: (b, 0, 0)),
            pl.BlockSpec((1, D_MODEL), lambda b, i: (0, 0)),
        ],
        out_specs=pl.BlockSpec((1, ts, D_MODEL), lambda b, i: (b, i, 0)),
        out_shape=jax.ShapeDtypeStruct((bsz, seq, D_MODEL), jnp.float32),
        compiler_params=_cparams(("arbitrary", "arbitrary")),
        name="moe_combine",
    )(rows, wgt, x1, mod3, g_post)


def _routing(top_idx, rank, counts, tm, n_tiles):
    padded = ((counts + tm - 1) // tm) * tm
    pend = jnp.cumsum(padded)
    pstart = pend - padded
    onehot = top_idx[:, :, None] == jnp.arange(N_EXPERTS, dtype=jnp.int32)[None, None, :]
    pos = rank + jnp.sum(jnp.where(onehot, pstart[None, None, :], 0), axis=-1)
    n_valid = (pend[-1] // tm).astype(jnp.int32)
    tile_row0 = jnp.arange(n_tiles, dtype=jnp.int32) * tm
    tile_expert = jnp.minimum(
        jnp.sum((tile_row0[:, None] >= pend[None, :]).astype(jnp.int32), axis=1), N_EXPERTS - 1)
    last_oh = jnp.arange(n_tiles, dtype=jnp.int32) == jnp.maximum(n_valid - 1, 0)
    last_expert = jnp.sum(jnp.where(last_oh, tile_expert, 0))
    tile_expert = jnp.where(jnp.arange(n_tiles, dtype=jnp.int32) < n_valid, tile_expert, last_expert)
    return pos.astype(jnp.int32), tile_expert.astype(jnp.int32), n_valid.reshape(1)


def _alibi_slopes(n_heads):
    return 2.0 ** (-8.0 * jnp.arange(1, n_heads + 1, dtype=jnp.float32) / n_heads)


def _layer(x, c, w_mod, b_mod, g_pre_mix, g_post_mix, w_in, lamv, g_subln, w_branch_sb,
           w_branch_diff, w_out, g_pre_ffn, g_post_ffn, w_router, b_router, w_gate_up,
           b_gate_up, w_down, b_down, *, ts_in, tq, ts_merge, tm, ts_comb):
    bsz, seq, d = x.shape
    n_tok = bsz * seq
    bf = jnp.bfloat16

    mod, lam = _mod_proj(c, w_mod, b_mod, lamv)
    mod3 = mod.reshape(bsz, N_MOD, d)

    o_vsb = 2 * SB_WIDTH
    o_qdf = 3 * SB_WIDTH
    o_vdf = o_qdf + 2 * DIFF_QK_WIDTH
    o_g = o_vdf + DIFF_V_WIDTH
    w_main = jnp.concatenate([w_in[:, :o_vsb], w_in[:, o_qdf:o_vdf], w_in[:, o_g:]], axis=1).astype(bf)
    w_vt = jnp.concatenate([w_in[:, o_vsb:o_qdf], w_in[:, o_vdf:o_g]], axis=1).T.astype(bf)

    main, vt = _in_proj(x, mod3, g_pre_mix.reshape(1, d), w_main, w_vt, ts_in, tq)
    y_sb = _sb_attn(main, vt, tq, tq)
    y_df = _diff_attn(main, vt, _alibi_slopes(DIFF_HEADS), lam,
                      g_subln.reshape(DIFF_V_DIM, 1), tq, tq)

    wr = jnp.zeros((d, LANES), jnp.float32).at[:, :N_EXPERTS].set(w_router)
    wrh = wr.astype(bf)
    wrl = (wr - wrh.astype(jnp.float32)).astype(bf)
    br = jnp.full((1, LANES), NEG_BIG, jnp.float32).at[0, :N_EXPERTS].set(b_router)
    x1, h2p, top_idx, top_w, rank, counts = _merge_router(
        y_sb, y_df, main, x, mod3, w_branch_sb.astype(bf), w_branch_diff.astype(bf),
        w_out.astype(bf), g_post_mix.reshape(1, d), g_pre_ffn.reshape(1, d), wrh, wrl, br, ts_merge)

    n_tiles = (n_tok * TOP_K) // tm + N_EXPERTS
    pos, tile_expert, n_valid = _routing(
        top_idx.reshape(n_tok, LANES)[:, :TOP_K], rank.reshape(n_tok, LANES)[:, :TOP_K],
        counts[0, :N_EXPERTS], tm, n_tiles)
    xg = _sc_scatter_rows(h2p.reshape(n_tok, d // 2), pos.T.reshape(TOP_K * n_tok), n_tiles * tm)
    rows = _moe_ffn(tile_expert, n_valid, xg, w_gate_up, b_gate_up, w_down, b_down, tm)
    pos_steps = pos.reshape(n_tok // ts_comb, ts_comb, TOP_K).swapaxes(1, 2).reshape(n_tok * TOP_K)
    tok_rows = _sc_gather_rows(rows, pos_steps)
    return _moe_combine(tok_rows, top_w, x1, mod3, g_post_ffn.reshape(1, d), ts_comb)


def kernel(x, c, w_mod, b_mod, g_pre_mix, g_post_mix, w_in, lambda_q1, lambda_k1, lambda_q2,
           lambda_k2, g_subln, w_branch_sb, w_branch_diff, w_out, g_pre_ffn, g_post_ffn,
           w_router, b_router, w_gate_up, b_gate_up, w_down, b_down):
    depth = w_mod.shape[0]
    for l in range(depth):
        lamv = jnp.stack([lambda_q1[l], lambda_k1[l], lambda_q2[l], lambda_k2[l]])
        x = _layer(x, c, w_mod[l], b_mod[l], g_pre_mix[l], g_post_mix[l], w_in[l], lamv,
                   g_subln[l], w_branch_sb[l], w_branch_diff[l], w_out[l], g_pre_ffn[l],
                   g_post_ffn[l], w_router[l], b_router[l], w_gate_up[l], b_gate_up[l],
                   w_down[l], b_down[l],
                   ts_in=512, tq=256, ts_merge=512, tm=512, ts_comb=256)
    return x
```

```python
import functools
import math

import jax
import jax.numpy as jnp
from jax import lax
from jax.experimental import pallas as pl
from jax.experimental.pallas import tpu as pltpu
from jax.experimental.pallas import tpu_sc as plsc

D_MODEL = 1024
SB_HEADS = 8
SB_HEAD_DIM = 64
SB_WIDTH = SB_HEADS * SB_HEAD_DIM
DIFF_HEADS = 4
DIFF_HEAD_DIM = 64
DIFF_V_DIM = 2 * DIFF_HEAD_DIM
DIFF_QK_WIDTH = DIFF_HEADS * 2 * DIFF_HEAD_DIM
DIFF_V_WIDTH = DIFF_HEADS * DIFF_V_DIM
N_EXPERTS = 32
TOP_K = 4
D_EXPERT = D_MODEL
SWIGLU_LIMIT = 7.0
SWIGLU_ALPHA = 1.702
RMS_EPS = 1e-6
N_MOD = 6
LAM_INIT = 0.8 - 0.6 * math.exp(-0.3 * 0)

LANES = 128
SUBLANES = 8
NEG_BIG = -1e30
EXP_ZERO_MARGIN = 110.0
NORM_SLACK = 1.01
SC_CORES = 2
SC_SUBCORES = 16
SC_GATHER_ROWS = 128
MERGE_SUB = 256

MAIN_WIDTH = 2 * SB_WIDTH + 2 * DIFF_QK_WIDTH + 2 * D_MODEL
VT_ROWS = SB_WIDTH + DIFF_V_WIDTH
COLBLK_K_SB = SB_WIDTH // LANES
COLBLK_Q_DF = 2 * SB_WIDTH // LANES
COLBLK_K_DF = COLBLK_Q_DF + DIFF_QK_WIDTH // LANES
GATE_COL0 = 2 * SB_WIDTH + 2 * DIFF_QK_WIDTH

VMEM_LIMIT = 56 * 1024 * 1024


def _cparams(sem, vmem=VMEM_LIMIT):
    return pltpu.CompilerParams(dimension_semantics=sem, vmem_limit_bytes=vmem)


def _rms(x):
    return x * lax.rsqrt(jnp.mean(x * x, axis=-1, keepdims=True) + RMS_EPS)


def _mod_kernel(c_ref, w_ref, b_ref, lamv_ref, mod_ref, lam_ref):
    c = c_ref[...]
    ca = c * jax.nn.sigmoid(c)
    mod_ref[...] = jnp.dot(ca, w_ref[...], preferred_element_type=jnp.float32,
                           precision=lax.Precision.HIGHEST) + b_ref[...]
    lv = lamv_ref[...]
    s1 = jnp.sum(lv[0:1] * lv[1:2], axis=-1, keepdims=True)
    s2 = jnp.sum(lv[2:3] * lv[3:4], axis=-1, keepdims=True)
    lam = jnp.exp(s1) - jnp.exp(s2) + LAM_INIT
    lam_ref[...] = jnp.broadcast_to(lam, lam_ref.shape)


def _mod_proj(c, w_mod, b_mod, lamv):
    bsz = c.shape[0]
    tn = 1536
    n = w_mod.shape[1]
    return pl.pallas_call(
        _mod_kernel,
        grid=(n // tn,),
        in_specs=[
            pl.BlockSpec((bsz, D_MODEL), lambda j: (0, 0)),
            pl.BlockSpec((D_MODEL, tn), lambda j: (0, j)),
            pl.BlockSpec((1, tn), lambda j: (0, j)),
            pl.BlockSpec((4, DIFF_HEAD_DIM), lambda j: (0, 0)),
        ],
        out_specs=[
            pl.BlockSpec((bsz, tn), lambda j: (0, j)),
            pl.BlockSpec((SUBLANES, LANES), lambda j: (0, 0)),
        ],
        out_shape=[
            jax.ShapeDtypeStruct((bsz, n), jnp.float32),
            jax.ShapeDtypeStruct((SUBLANES, LANES), jnp.float32),
        ],
        compiler_params=_cparams(("arbitrary",)),
        name="mod_proj",
    )(c, w_mod, b_mod.reshape(1, n), lamv)


IN_CHUNK = 1024


def _in_proj_kernel(x_ref, mod_ref, g_ref, wm_ref, wvt_ref, main_ref, vt_ref, h_scr, *, tk):
    x = x_ref[0]
    mod = mod_ref[0]
    h = _rms(x) * g_ref[...]
    h = h * (1.0 + mod[1:2]) + mod[0:1]
    hb = h.astype(jnp.bfloat16)
    groups = tk // SUBLANES
    cols = []
    for ct in range(D_MODEL // LANES):
        h_scr[ct] = h[:, ct * LANES:(ct + 1) * LANES]
        pieces = []
        for blk in range(h.shape[0] // tk):
            for g in range(groups):
                pieces.append(h_scr[ct, pl.ds(blk * tk + g, SUBLANES, stride=groups), :])
        cols.append(jnp.concatenate(pieces, axis=0))
    hpb = jnp.concatenate(cols, axis=1).astype(jnp.bfloat16)

    half = IN_CHUNK // 2
    for ci in range(MAIN_WIDTH // IN_CHUNK):
        c0 = ci * IN_CHUNK
        if c0 == 0:
            q = jnp.dot(hb, wm_ref[:, :half], preferred_element_type=jnp.float32)
            k = jnp.dot(hpb, wm_ref[:, half:IN_CHUNK], preferred_element_type=jnp.float32)
            main_ref[0, :, :half] = (q * 0.0625).astype(jnp.bfloat16)
            main_ref[0, :, half:IN_CHUNK] = k.astype(jnp.bfloat16)
            continue
        p = jnp.dot(hb, wm_ref[:, c0:c0 + IN_CHUNK], preferred_element_type=jnp.float32)
        if c0 < GATE_COL0:
            main_ref[0, :, c0:c0 + half] = (p[:, :half] * 0.125).astype(jnp.bfloat16)
            main_ref[0, :, c0 + half:c0 + IN_CHUNK] = p[:, half:].astype(jnp.bfloat16)
        else:
            main_ref[0, :, c0:c0 + IN_CHUNK] = jax.nn.sigmoid(p).astype(jnp.bfloat16)
    nt = (((1,), (1,)), ((), ()))
    vt_sb = lax.dot_general(wvt_ref[:SB_WIDTH, :], hpb, nt, preferred_element_type=jnp.float32)
    vt_df = lax.dot_general(wvt_ref[SB_WIDTH:, :], hb, nt, preferred_element_type=jnp.float32)
    vt_ref[0, :SB_WIDTH, :] = vt_sb.astype(jnp.bfloat16)
    vt_ref[0, SB_WIDTH:, :] = vt_df.astype(jnp.bfloat16)


def _in_proj(x, mod3, g_pre, w_main, w_vt, ts, tk):
    bsz, seq, _ = x.shape
    assert ts % tk == 0
    return pl.pallas_call(
        functools.partial(_in_proj_kernel, tk=tk),
        grid=(bsz, seq // ts),
        in_specs=[
            pl.BlockSpec((1, ts, D_MODEL), lambda b, i: (b, i, 0)),
            pl.BlockSpec((1, N_MOD, D_MODEL), lambda b, i: (b, 0, 0)),
            pl.BlockSpec((1, D_MODEL), lambda b, i: (0, 0)),
            pl.BlockSpec((D_MODEL, MAIN_WIDTH), lambda b, i: (0, 0)),
            pl.BlockSpec((VT_ROWS, D_MODEL), lambda b, i: (0, 0)),
        ],
        out_specs=[
            pl.BlockSpec((1, ts, MAIN_WIDTH), lambda b, i: (b, i, 0)),
            pl.BlockSpec((1, VT_ROWS, ts), lambda b, i: (b, 0, i)),
        ],
        out_shape=[
            jax.ShapeDtypeStruct((bsz, seq, MAIN_WIDTH), jnp.bfloat16),
            jax.ShapeDtypeStruct((bsz, VT_ROWS, seq), jnp.bfloat16),
        ],
        scratch_shapes=[pltpu.VMEM((D_MODEL // LANES, ts, LANES), jnp.float32)],
        compiler_params=_cparams(("arbitrary", "arbitrary")),
        name="in_proj",
    )(x, mod3, g_pre, w_main, w_vt)


def _suffix_excl_prod8(tot):
    sub = lax.broadcasted_iota(jnp.int32, tot.shape, 0)
    x = jnp.where(sub < SUBLANES - 1, pltpu.roll(tot, SUBLANES - 1, 0), 1.0)
    for sh in (1, 2, 4):
        x = x * jnp.where(sub + sh < SUBLANES, pltpu.roll(x, SUBLANES - sh, 0), 1.0)
    return x


def _sb_scores(k_ref, q_heads, s_ref, slot, j, tk):
    kb = k_ref[0, pl.ds(pl.multiple_of(j * tk, tk), tk), :]
    for h in range(2):
        s_ref[slot, h] = lax.dot_general(kb, q_heads[h], (((1,), (1,)), ((), ())),
                                         preferred_element_type=jnp.float32)


def _sb_weights(zt, c8, ok, groups):
    tq = zt.shape[1]
    r = 0.5 - 0.5 * jnp.tanh(zt)
    if ok is not None:
        r = jnp.where(ok, r, 1.0)
    rg = [r[g * SUBLANES:(g + 1) * SUBLANES, :] for g in range(groups)]
    tot = rg[0]
    for g in range(1, groups):
        tot = tot * rg[g]
    p = c8 * _suffix_excl_prod8(tot)
    pieces = [None] * groups
    for g in range(groups - 1, -1, -1):
        pn = p * rg[g]
        pieces[g] = p - pn
        p = pn
    a = jnp.concatenate(pieces, axis=0).astype(jnp.bfloat16)
    return a, jnp.broadcast_to(p[0:1, :], (SUBLANES, tq))


def _sb_attn_kernel(q_ref, k_ref, v_ref, o_ref, acc_ref, c_ref, s_ref, ok_ref, *, tq, tk):
    i = pl.program_id(2)
    groups = tk // SUBLANES
    q2 = q_ref[0]
    lane = lax.broadcasted_iota(jnp.int32, q2.shape, 1)
    zero = jnp.zeros_like(q2)
    q_heads = (jnp.where(lane < SB_HEAD_DIM, q2, zero), jnp.where(lane < SB_HEAD_DIM, zero, q2))

    def step(j, slot, masked):
        _sb_scores(k_ref, q_heads, s_ref, 1 - slot, jnp.maximum(j - 1, 0), tk)
        ok = (ok_ref[...] > 0.5) if masked else None
        off = pl.multiple_of(j * tk, tk)
        ws = []
        for h in range(2):
            a, c_new = _sb_weights(s_ref[slot, h], c_ref[h], ok, groups)
            c_ref[h] = c_new
            ws.append(a)
        for h in range(2):
            vt_h = v_ref[0, h * SB_HEAD_DIM:(h + 1) * SB_HEAD_DIM, pl.ds(off, tk)]
            acc_ref[h] += jnp.dot(vt_h, ws[h], preferred_element_type=jnp.float32)

    @pl.when(i == 0)
    def _():
        row = lax.broadcasted_iota(jnp.int32, (tk, tq), 0)
        col = lax.broadcasted_iota(jnp.int32, (tk, tq), 1)
        ok_ref[...] = jnp.where((row % SUBLANES) * groups + row // SUBLANES < col, 1.0, 0.0)

    acc_ref[...] = jnp.zeros_like(acc_ref)
    c_ref[...] = jnp.ones_like(c_ref)
    _sb_scores(k_ref, q_heads, s_ref, 0, i, tk)
    step(i, 0, True)

    def stick_left():
        return jnp.max(c_ref[...]) > 0.0

    def more(state):
        m, go = state
        return jnp.logical_and(m < i // 2, go)

    def pair(state):
        m, _ = state
        j = i - 1 - 2 * m
        step(j, 1, False)
        go_on = stick_left()

        @pl.when(go_on)
        def _():
            step(j - 1, 0, False)

        return m + 1, jnp.logical_and(go_on, stick_left())

    m_done, go = lax.while_loop(more, pair, (jnp.int32(0), stick_left()))

    @pl.when(jnp.logical_and(jnp.logical_and(i % 2 == 1, m_done == i // 2), go))
    def _():
        step(0, 1, False)

    ot = jnp.concatenate([acc_ref[0], acc_ref[1]], axis=0)
    o_ref[0] = ot.T.astype(jnp.bfloat16)


def _sb_attn(main, vt, tq, tk):
    bsz, seq, _ = main.shape
    assert tq == tk
    kern = functools.partial(_sb_attn_kernel, tq=tq, tk=tk)
    return pl.pallas_call(
        kern,
        grid=(bsz, SB_WIDTH // LANES, seq // tq),
        in_specs=[
            pl.BlockSpec((1, tq, LANES), lambda b, p, i: (b, i, p)),
            pl.BlockSpec((1, seq, LANES), lambda b, p, i: (b, 0, COLBLK_K_SB + p)),
            pl.BlockSpec((1, LANES, seq), lambda b, p, i: (b, p, 0)),
        ],
        out_specs=pl.BlockSpec((1, tq, LANES), lambda b, p, i: (b, i, p)),
        out_shape=jax.ShapeDtypeStruct((bsz, seq, SB_WIDTH), jnp.bfloat16),
        scratch_shapes=[
            pltpu.VMEM((2, SB_HEAD_DIM, tq), jnp.float32),
            pltpu.VMEM((2, SUBLANES, tq), jnp.float32),
            pltpu.VMEM((2, 2, tk, tq), jnp.float32),
            pltpu.VMEM((tk, tq), jnp.float32),
        ],
        compiler_params=_cparams(("arbitrary", "arbitrary", "arbitrary")),
        name="sb_attn",
    )(main, main, vt)


def _diff_attn_kernel(slopes_ref, inv_slopes_ref, q_ref, k_ref, v_ref, lam_ref, g_ref, o_ref,
                      acc_ref, m_ref, l_ref, s_ref, smax_ref, kn_ref, bias_ref, *, tq, tk):
    hd = pl.program_id(1)
    i = pl.program_id(2)
    slope = slopes_ref[hd]

    @pl.when(i == 0)
    def _():
        kf = k_ref[0].astype(jnp.float32)
        kn2 = jnp.max(jnp.sum(kf * kf, axis=-1, keepdims=True), axis=0, keepdims=True)
        kn_ref[...] = jnp.broadcast_to(kn2, kn_ref.shape)
        row0 = lax.broadcasted_iota(jnp.int32, (tk, tq), 0)
        col0 = lax.broadcasted_iota(jnp.int32, (tk, tq), 1)
        bias_ref[...] = slope * (row0 - col0).astype(jnp.float32)

    q2 = q_ref[0]
    lane = lax.broadcasted_iota(jnp.int32, q2.shape, 1)
    zero = jnp.zeros_like(q2)
    q_maps = (jnp.where(lane < DIFF_HEAD_DIM, q2, zero), jnp.where(lane < DIFF_HEAD_DIM, zero, q2))

    def scores(slot, j, masked):
        kb = k_ref[0, pl.ds(pl.multiple_of(j * tk, tk), tk), :]
        for m in range(2):
            s = lax.dot_general(kb, q_maps[m], (((1,), (1,)), ((), ())),
                                preferred_element_type=jnp.float32) + bias_ref[...]
            if masked:
                row = lax.broadcasted_iota(jnp.int32, (tk, tq), 0)
                col = lax.broadcasted_iota(jnp.int32, (tk, tq), 1)
                s = jnp.where(row <= col, s, NEG_BIG)
            s_ref[slot, m] = s
            smax_ref[slot, m] = jnp.max(s, axis=0, keepdims=True)

    def step(j, slot):
        scores(1 - slot, jnp.maximum(j - 1, 0), False)
        off = pl.multiple_of(j * tk, tk)
        vtb = v_ref[0, :, pl.ds(off, tk)]
        cb = slope * ((j - i) * tk).astype(jnp.float32)
        ps, alphas = [], []
        for m in range(2):
            s = s_ref[slot, m]
            m_old = m_ref[m]
            m_new = jnp.maximum(m_old, smax_ref[slot, m] + cb)
            alpha = jnp.exp(m_old - m_new)
            p = jnp.exp(s - (m_new - cb))
            l_ref[m] = alpha * l_ref[m] + jnp.sum(p, axis=0, keepdims=True)
            m_ref[m] = m_new
            ps.append(p.astype(jnp.bfloat16))
            alphas.append(alpha)
        for m in range(2):
            acc_ref[m] = alphas[m] * acc_ref[m] + jnp.dot(
                vtb, ps[m], preferred_element_type=jnp.float32)

    acc_ref[...] = jnp.zeros_like(acc_ref)
    m_ref[...] = jnp.full_like(m_ref, NEG_BIG)
    l_ref[...] = jnp.zeros_like(l_ref)
    scores(0, i, True)
    step(i, 0)

    qf = q2.astype(jnp.float32)
    qn2 = jnp.max(jnp.sum(qf * qf, axis=-1, keepdims=True), axis=0, keepdims=True)
    zabs = jnp.sqrt(qn2 * kn_ref[0:1, 0:1]) * NORM_SLACK
    m_lo = jnp.min(jnp.minimum(m_ref[0], m_ref[1]), axis=1, keepdims=True)
    reach = (EXP_ZERO_MARGIN + zabs - m_lo) * inv_slopes_ref[hd]
    n_need = jnp.floor(jnp.clip((reach - 1.0) * (1.0 / tk), -1.0, 1e6)) + 1.0
    n_back = jnp.minimum(i, jnp.max(n_need).astype(jnp.int32))

    def quad(n, carry):
        j = i - 1 - 4 * n
        step(j, 1)
        step(j - 1, 0)
        step(j - 2, 1)
        step(j - 3, 0)
        return carry

    n_quads = n_back // 4
    lax.fori_loop(0, n_quads, quad, 0)
    rest = n_back - 4 * n_quads

    @pl.when(rest >= 2)
    def _():
        j = i - 1 - 4 * n_quads
        step(j, 1)
        step(j - 1, 0)

    @pl.when(rest % 2 == 1)
    def _():
        step(i - n_back, 1)

    lam = lam_ref[0:1, 0:1]
    o = acc_ref[0] / l_ref[0] - lam * (acc_ref[1] / l_ref[1])
    ms = jnp.mean(o * o, axis=0, keepdims=True)
    y = o * lax.rsqrt(ms + RMS_EPS) * g_ref[...] * (1.0 - LAM_INIT)
    o_ref[0] = y.T.astype(jnp.bfloat16)


def _diff_attn(main, vt, slopes, lam, g_col, tq, tk):
    bsz, seq, _ = main.shape
    assert tq == tk
    kern = functools.partial(_diff_attn_kernel, tq=tq, tk=tk)
    vrow0 = SB_WIDTH // LANES
    return pl.pallas_call(
        kern,
        grid=(bsz, DIFF_HEADS, seq // tq),
        in_specs=[
            pl.BlockSpec(memory_space=pltpu.SMEM),
            pl.BlockSpec(memory_space=pltpu.SMEM),
            pl.BlockSpec((1, tq, LANES), lambda b, h, i: (b, i, COLBLK_Q_DF + h)),
            pl.BlockSpec((1, seq, LANES), lambda b, h, i: (b, 0, COLBLK_K_DF + h)),
            pl.BlockSpec((1, DIFF_V_DIM, seq), lambda b, h, i: (b, vrow0 + h, 0)),
            pl.BlockSpec((SUBLANES, LANES), lambda b, h, i: (0, 0)),
            pl.BlockSpec((DIFF_V_DIM, 1), lambda b, h, i: (0, 0)),
        ],
        out_specs=pl.BlockSpec((1, tq, DIFF_V_DIM), lambda b, h, i: (b, i, h)),
        out_shape=jax.ShapeDtypeStruct((bsz, seq, DIFF_V_WIDTH), jnp.bfloat16),
        scratch_shapes=[
            pltpu.VMEM((2, DIFF_V_DIM, tq), jnp.float32),
            pltpu.VMEM((2, 1, tq), jnp.float32),
            pltpu.VMEM((2, 1, tq), jnp.float32),
            pltpu.VMEM((2, 2, tk, tq), jnp.float32),
            pltpu.VMEM((2, 2, 1, tq), jnp.float32),
            pltpu.VMEM((SUBLANES, LANES), jnp.float32),
            pltpu.VMEM((tk, tq), jnp.float32),
        ],
        compiler_params=_cparams(("arbitrary", "arbitrary", "arbitrary")),
        name="diff_attn",
    )(slopes, 1.0 / slopes, main, main, vt, lam, g_col)


def _pack_bf16_pair(a, b):
    ab = pltpu.bitcast(a.astype(jnp.bfloat16).astype(jnp.float32), jnp.uint32)
    bb = pltpu.bitcast(b.astype(jnp.bfloat16).astype(jnp.float32), jnp.uint32)
    return ab | (bb >> 16)


def _unpack_bf16_pair(w):
    hi = pltpu.bitcast(w & jnp.uint32(0xFFFF0000), jnp.float32)
    lo = pltpu.bitcast(w << 16, jnp.float32)
    return jnp.concatenate([hi, lo], axis=1)


def _merge_router_kernel(ysb_ref, ydf_ref, gates_ref, x_ref, mod_ref, wsb_ref, wdf_ref, wout_ref,
                         gpost_ref, gpre_ref, wrh_ref, wrl_ref, br_ref,
                         x1_ref, h2_ref, idx_ref, wgt_ref, rank_ref, cnt_ref, base_ref):
    first = jnp.logical_and(pl.program_id(0) == 0, pl.program_id(1) == 0)

    @pl.when(first)
    def _():
        base_ref[...] = jnp.zeros_like(base_ref)

    mod = mod_ref[0]
    subs = [slice(s * MERGE_SUB, (s + 1) * MERGE_SUB) for s in range(x_ref.shape[1] // MERGE_SUB)]
    half = D_MODEL // 2
    branch = [(jnp.dot(ysb_ref[0, rows, :], wsb_ref[...], preferred_element_type=jnp.float32),
               jnp.dot(ydf_ref[0, rows, :], wdf_ref[...], preferred_element_type=jnp.float32))
              for rows in subs]
    merged = []
    for rows, (a, b) in zip(subs, branch):
        g = gates_ref[0, rows, :].astype(jnp.float32)
        merged.append((g[:, :D_MODEL] * a + g[:, D_MODEL:] * b).astype(jnp.bfloat16))
    mixes = [jnp.dot(m, wout_ref[...], preferred_element_type=jnp.float32) for m in merged]
    h2s = []
    for rows, mix in zip(subs, mixes):
        x1 = x_ref[0, rows, :] + mod[2:3] * (_rms(mix) * gpost_ref[...])
        x1_ref[0, rows, :] = x1
        h2 = _rms(x1) * gpre_ref[...]
        h2 = h2 * (1.0 + mod[4:5]) + mod[3:4]
        h2_ref[0, rows, :] = _pack_bf16_pair(h2[:, :half], h2[:, half:])
        h2s.append(h2)
    logit_list = []
    for h2 in h2s:
        hh = h2.astype(jnp.bfloat16)
        hl = (h2 - hh.astype(jnp.float32)).astype(jnp.bfloat16)
        logit_list.append(jnp.dot(hh, wrh_ref[...], preferred_element_type=jnp.float32)
                          + jnp.dot(hh, wrl_ref[...], preferred_element_type=jnp.float32)
                          + jnp.dot(hl, wrh_ref[...], preferred_element_type=jnp.float32)
                          + br_ref[...])
    for rows, logits in zip(subs, logit_list):
        _route_rows(rows, logits, idx_ref, wgt_ref, rank_ref, cnt_ref, base_ref)


def _route_rows(rows, logits, idx_ref, wgt_ref, rank_ref, cnt_ref, base_ref):
    lane = lax.broadcasted_iota(jnp.int32, logits.shape, 1)
    lanef = lane.astype(jnp.float32)
    vals, idxs = [], []
    cur = logits
    for _ in range(TOP_K):
        mx = jnp.max(cur, axis=-1, keepdims=True)
        ix = jnp.min(jnp.where(cur == mx, lanef, float(LANES)), axis=-1, keepdims=True)
        cur = jnp.where(lanef == ix, -jnp.inf, cur)
        vals.append(mx)
        idxs.append(ix)
    es = [jnp.exp(v - vals[0]) for v in vals]
    den = es[0] + es[1] + es[2] + es[3]
    oi = jnp.zeros(logits.shape, jnp.float32)
    ow = jnp.zeros(logits.shape, jnp.float32)
    for k in range(TOP_K):
        oi = jnp.where(lane == k, idxs[k], oi)
        ow = jnp.where(lane == k, es[k] / den, ow)
    idx_ref[0, rows, :] = oi.astype(jnp.int32)
    wgt_ref[0, rows, :] = ow

    ts = logits.shape[0]
    member = jnp.zeros(logits.shape, jnp.float32)
    for k in range(TOP_K):
        member = member + (lanef == idxs[k]).astype(jnp.float32)
    rr = lax.broadcasted_iota(jnp.int32, (ts, ts), 0)
    cc = lax.broadcasted_iota(jnp.int32, (ts, ts), 1)
    lower = jnp.where(cc < rr, 1.0, 0.0).astype(jnp.bfloat16)
    before = jnp.dot(lower, member.astype(jnp.bfloat16), preferred_element_type=jnp.float32)
    base = base_ref[0:1, :]
    rank_all = before + base
    orank = jnp.zeros(logits.shape, jnp.float32)
    for k in range(TOP_K):
        rk = jnp.sum(jnp.where(lanef == idxs[k], rank_all, 0.0), axis=-1, keepdims=True)
        orank = jnp.where(lane == k, rk, orank)
    rank_ref[0, rows, :] = orank.astype(jnp.int32)
    new_base = base + jnp.sum(member, axis=0, keepdims=True)
    base_ref[...] = jnp.broadcast_to(new_base, base_ref.shape)
    cnt_ref[...] = jnp.broadcast_to(new_base, cnt_ref.shape).astype(jnp.int32)


def _merge_router(ysb, ydf, main, x, mod3, wsb, wdf, wout, gpost, gpre, wrh, wrl, br, ts):
    bsz, seq, _ = x.shape
    const = lambda b, i: (0, 0)
    return pl.pallas_call(
        _merge_router_kernel,
        grid=(bsz, seq // ts),
        in_specs=[
            pl.BlockSpec((1, ts, SB_WIDTH), lambda b, i: (b, i, 0)),
            pl.BlockSpec((1, ts, DIFF_V_WIDTH), lambda b, i: (b, i, 0)),
            pl.BlockSpec((1, ts, 2 * D_MODEL), lambda b, i: (b, i, GATE_COL0 // (2 * D_MODEL))),
            pl.BlockSpec((1, ts, D_MODEL), lambda b, i: (b, i, 0)),
            pl.BlockSpec((1, N_MOD, D_MODEL), lambda b, i: (b, 0, 0)),
            pl.BlockSpec((SB_WIDTH, D_MODEL), const),
            pl.BlockSpec((DIFF_V_WIDTH, D_MODEL), const),
            pl.BlockSpec((D_MODEL, D_MODEL), const),
            pl.BlockSpec((1, D_MODEL), const),
            pl.BlockSpec((1, D_MODEL), const),
            pl.BlockSpec((D_MODEL, LANES), const),
            pl.BlockSpec((D_MODEL, LANES), const),
            pl.BlockSpec((1, LANES), const),
        ],
        out_specs=[
            pl.BlockSpec((1, ts, D_MODEL), lambda b, i: (b, i, 0)),
            pl.BlockSpec((1, ts, D_MODEL // 2), lambda b, i: (b, i, 0)),
            pl.BlockSpec((1, ts, LANES), lambda b, i: (b, i, 0)),
            pl.BlockSpec((1, ts, LANES), lambda b, i: (b, i, 0)),
            pl.BlockSpec((1, ts, LANES), lambda b, i: (b, i, 0)),
            pl.BlockSpec((SUBLANES, LANES), const),
        ],
        out_shape=[
            jax.ShapeDtypeStruct((bsz, seq, D_MODEL), jnp.float32),
            jax.ShapeDtypeStruct((bsz, seq, D_MODEL // 2), jnp.uint32),
            jax.ShapeDtypeStruct((bsz, seq, LANES), jnp.int32),
            jax.ShapeDtypeStruct((bsz, seq, LANES), jnp.float32),
            jax.ShapeDtypeStruct((bsz, seq, LANES), jnp.int32),
            jax.ShapeDtypeStruct((SUBLANES, LANES), jnp.int32),
        ],
        scratch_shapes=[pltpu.VMEM((SUBLANES, LANES), jnp.float32)],
        compiler_params=_cparams(("arbitrary", "arbitrary")),
        name="merge_router",
    )(ysb, ydf, main, x, mod3, wsb, wdf, wout, gpost, gpre, wrh, wrl, br)


def _sc_gather_rows(table, idx):
    n = idx.shape[0]
    width = table.shape[1]
    n_workers = SC_CORES * SC_SUBCORES
    per_worker = n // n_workers
    n_chunks = per_worker // SC_GATHER_ROWS
    assert n_chunks * SC_GATHER_ROWS * n_workers == n
    mesh = plsc.VectorSubcoreMesh(core_axis_name="c", subcore_axis_name="s",
                                  num_cores=SC_CORES, num_subcores=SC_SUBCORES)

    def body(table_hbm, idx_hbm, out_hbm, idx_v, rows_v, sem):
        wid = lax.axis_index("s") * SC_CORES + lax.axis_index("c")
        base = wid * per_worker

        @pl.loop(0, n_chunks)
        def _(ci):
            off = pl.multiple_of(base + ci * SC_GATHER_ROWS, SC_GATHER_ROWS)
            pltpu.sync_copy(idx_hbm.at[pl.ds(off, SC_GATHER_ROWS)], idx_v)
            pltpu.async_copy(table_hbm.at[idx_v], rows_v, sem).wait()
            pltpu.sync_copy(rows_v, out_hbm.at[pl.ds(off, SC_GATHER_ROWS)])

    return pl.kernel(
        body,
        out_type=jax.ShapeDtypeStruct((n, width), table.dtype),
        mesh=mesh,
        scratch_types=[
            pltpu.VMEM((SC_GATHER_ROWS,), jnp.int32),
            pltpu.VMEM((SC_GATHER_ROWS, width), table.dtype),
            pltpu.SemaphoreType.DMA,
        ],
        name="sc_gather_rows",
    )(table, idx)


def _sc_scatter_rows(src, pos_kmajor, n_rows):
    n_tok, width = src.shape
    n_workers = SC_CORES * SC_SUBCORES
    per_worker = n_tok // n_workers
    n_chunks = per_worker // SC_GATHER_ROWS
    assert n_chunks * SC_GATHER_ROWS * n_workers == n_tok
    mesh = plsc.VectorSubcoreMesh(core_axis_name="c", subcore_axis_name="s",
                                  num_cores=SC_CORES, num_subcores=SC_SUBCORES)

    def body(src_hbm, idx_hbm, out_hbm, idx_v, rows_v):
        wid = lax.axis_index("s") * SC_CORES + lax.axis_index("c")
        base = wid * per_worker

        @pl.loop(0, n_chunks)
        def _(ci):
            off = pl.multiple_of(base + ci * SC_GATHER_ROWS, SC_GATHER_ROWS)
            pltpu.sync_copy(src_hbm.at[pl.ds(off, SC_GATHER_ROWS)], rows_v)
            for k in range(TOP_K):
                koff = pl.multiple_of(k * n_tok + off, SC_GATHER_ROWS)
                pltpu.sync_copy(idx_hbm.at[pl.ds(koff, SC_GATHER_ROWS)], idx_v)
                pltpu.sync_copy(rows_v, out_hbm.at[idx_v])

    return pl.kernel(
        body,
        out_type=jax.ShapeDtypeStruct((n_rows, width), src.dtype),
        mesh=mesh,
        scratch_types=[
            pltpu.VMEM((SC_GATHER_ROWS,), jnp.int32),
            pltpu.VMEM((SC_GATHER_ROWS, width), src.dtype),
        ],
        name="sc_scatter_rows",
    )(src, pos_kmajor)


def _moe_ffn_kernel(te_ref, nt_ref, x_ref, wgu_ref, bgu_ref, wd_ref, bd_ref, o_ref,
                    wgu_bf, wd_bf):
    i = pl.program_id(0)
    n_valid = nt_ref[0]

    new_expert = jnp.logical_or(i == 0, te_ref[i] != te_ref[jnp.maximum(i - 1, 0)])

    @pl.when(jnp.logical_and(i < n_valid, new_expert))
    def _():
        wgu_bf[...] = wgu_ref[0].astype(jnp.bfloat16)
        wd_bf[...] = wd_ref[0].astype(jnp.bfloat16)

    @pl.when(i < n_valid)
    def _():
        xb = _unpack_bf16_pair(x_ref[...]).astype(jnp.bfloat16)
        gu = jnp.dot(xb, wgu_bf[...], preferred_element_type=jnp.float32) + bgu_ref[0]
        gate = jnp.minimum(gu[:, :D_EXPERT], SWIGLU_LIMIT)
        up = jnp.clip(gu[:, D_EXPERT:], -SWIGLU_LIMIT, SWIGLU_LIMIT)
        act = (up + 1.0) * (gate * jax.nn.sigmoid(SWIGLU_ALPHA * gate))
        out = jnp.dot(act.astype(jnp.bfloat16), wd_bf[...],
                      preferred_element_type=jnp.float32) + bd_ref[0]
        half = D_MODEL // 2
        o_ref[...] = _pack_bf16_pair(out[:, :half], out[:, half:])

    @pl.when(i >= n_valid)
    def _():
        o_ref[...] = jnp.zeros_like(o_ref)


def _moe_ffn(tile_expert, n_valid, xg, wgu, bgu, wd, bd, tm):
    n_tiles = xg.shape[0] // tm
    grid_spec = pltpu.PrefetchScalarGridSpec(
        num_scalar_prefetch=2,
        grid=(n_tiles,),
        in_specs=[
            pl.BlockSpec((tm, D_MODEL // 2), lambda i, te, nt: (jnp.minimum(i, nt[0] - 1), 0)),
            pl.BlockSpec((1, D_MODEL, 2 * D_EXPERT), lambda i, te, nt: (te[i], 0, 0)),
            pl.BlockSpec((1, 1, 2 * D_EXPERT), lambda i, te, nt: (te[i], 0, 0)),
            pl.BlockSpec((1, D_EXPERT, D_MODEL), lambda i, te, nt: (te[i], 0, 0)),
            pl.BlockSpec((1, 1, D_MODEL), lambda i, te, nt: (te[i], 0, 0)),
        ],
        out_specs=pl.BlockSpec((tm, D_MODEL // 2), lambda i, te, nt: (i, 0)),
        scratch_shapes=[
            pltpu.VMEM((D_MODEL, 2 * D_EXPERT), jnp.bfloat16),
            pltpu.VMEM((D_EXPERT, D_MODEL), jnp.bfloat16),
        ],
    )
    return pl.pallas_call(
        _moe_ffn_kernel,
        grid_spec=grid_spec,
        out_shape=jax.ShapeDtypeStruct((n_tiles * tm, D_MODEL // 2), jnp.uint32),
        compiler_params=_cparams(("arbitrary",)),
        name="moe_ffn",
    )(tile_expert, n_valid, xg, wgu, bgu.reshape(N_EXPERTS, 1, -1), wd,
      bd.reshape(N_EXPERTS, 1, -1))


def _moe_combine_kernel(rows_ref, wgt_ref, x1_ref, mod_ref, g_ref, o_ref):
    ts = x1_ref.shape[1]
    w = wgt_ref[0]
    y = jnp.zeros(x1_ref.shape[1:], jnp.float32)
    for k in range(TOP_K):
        y = y + w[:, k:k + 1] * _unpack_bf16_pair(rows_ref[k * ts:(k + 1) * ts, :])
    mod = mod_ref[0]
    o_ref[0] = x1_ref[0] + mod[5:6] * (_rms(y) * g_ref[...])


def _moe_combine(rows, wgt, x1, mod3, g_post, ts):
    bsz, seq, _ = x1.shape
    per_b = seq // ts
    return pl.pallas_call(
        _moe_combine_kernel,
        grid=(bsz, per_b),
        in_specs=[
            pl.BlockSpec((TOP_K * ts, D_MODEL // 2), lambda b, i: (b * per_b + i, 0)),
            pl.BlockSpec((1, ts, LANES), lambda b, i: (b, i, 0)),
            pl.BlockSpec((1, ts, D_MODEL), lambda b, i: (b, i, 0)),
            pl.BlockSpec((1, N_MOD, D_MODEL), lambda b, i: (b, 0, 0)),
            pl.BlockSpec((1, D_MODEL), lambda b, i: (0, 0)),
        ],
        out_specs=pl.BlockSpec((1, ts, D_MODEL), lambda b, i: (b, i, 0)),
        out_shape=jax.ShapeDtypeStruct((bsz, seq, D_MODEL), jnp.float32),
        compiler_params=_cparams(("arbitrary", "arbitrary")),
        name="moe_combine",
    )(rows, wgt, x1, mod3, g_post)


def _routing(top_idx, rank, counts, tm, n_tiles):
    padded = ((counts + tm - 1) // tm) * tm
    pend = jnp.cumsum(padded)
    pstart = pend - padded
    onehot = top_idx[:, :, None] == jnp.arange(N_EXPERTS, dtype=jnp.int32)[None, None, :]
    pos = rank + jnp.sum(jnp.where(onehot, pstart[None, None, :], 0), axis=-1)
    n_valid = (pend[-1] // tm).astype(jnp.int32)
    tile_row0 = jnp.arange(n_tiles, dtype=jnp.int32) * tm
    tile_expert = jnp.minimum(
        jnp.sum((tile_row0[:, None] >= pend[None, :]).astype(jnp.int32), axis=1), N_EXPERTS - 1)
    last_oh = jnp.arange(n_tiles, dtype=jnp.int32) == jnp.maximum(n_valid - 1, 0)
    last_expert = jnp.sum(jnp.where(last_oh, tile_expert, 0))
    tile_expert = jnp.where(jnp.arange(n_tiles, dtype=jnp.int32) < n_valid, tile_expert, last_expert)
    return pos.astype(jnp.int32), tile_expert.astype(jnp.int32), n_valid.reshape(1)


def _alibi_slopes(n_heads):
    return 2.0 ** (-8.0 * jnp.arange(1, n_heads + 1, dtype=jnp.float32) / n_heads)


def _layer(x, c, w_mod, b_mod, g_pre_mix, g_post_mix, w_in, lamv, g_subln, w_branch_sb,
           w_branch_diff, w_out, g_pre_ffn, g_post_ffn, w_router, b_router, w_gate_up,
           b_gate_up, w_down, b_down, *, ts_in, tq, ts_merge, tm, ts_comb):
    bsz, seq, d = x.shape
    n_tok = bsz * seq
    bf = jnp.bfloat16

    mod, lam = _mod_proj(c, w_mod, b_mod, lamv)
    mod3 = mod.reshape(bsz, N_MOD, d)

    o_vsb = 2 * SB_WIDTH
    o_qdf = 3 * SB_WIDTH
    o_vdf = o_qdf + 2 * DIFF_QK_WIDTH
    o_g = o_vdf + DIFF_V_WIDTH
    w_main = jnp.concatenate([w_in[:, :o_vsb], w_in[:, o_qdf:o_vdf], w_in[:, o_g:]], axis=1).astype(bf)
    w_vt = jnp.concatenate([w_in[:, o_vsb:o_qdf], w_in[:, o_vdf:o_g]], axis=1).T.astype(bf)

    main, vt = _in_proj(x, mod3, g_pre_mix.reshape(1, d), w_main, w_vt, ts_in, tq)
    y_sb = _sb_attn(main, vt, tq, tq)
    y_df = _diff_attn(main, vt, _alibi_slopes(DIFF_HEADS), lam,
                      g_subln.reshape(DIFF_V_DIM, 1), tq, tq)

    wr = jnp.zeros((d, LANES), jnp.float32).at[:, :N_EXPERTS].set(w_router)
    wrh = wr.astype(bf)
    wrl = (wr - wrh.astype(jnp.float32)).astype(bf)
    br = jnp.full((1, LANES), NEG_BIG, jnp.float32).at[0, :N_EXPERTS].set(b_router)
    x1, h2p, top_idx, top_w, rank, counts = _merge_router(
        y_sb, y_df, main, x, mod3, w_branch_sb.astype(bf), w_branch_diff.astype(bf),
        w_out.astype(bf), g_post_mix.reshape(1, d), g_pre_ffn.reshape(1, d), wrh, wrl, br, ts_merge)

    n_tiles = (n_tok * TOP_K) // tm + N_EXPERTS
    pos, tile_expert, n_valid = _routing(
        top_idx.reshape(n_tok, LANES)[:, :TOP_K], rank.reshape(n_tok, LANES)[:, :TOP_K],
        counts[0, :N_EXPERTS], tm, n_tiles)
    xg = _sc_scatter_rows(h2p.reshape(n_tok, d // 2), pos.T.reshape(TOP_K * n_tok), n_tiles * tm)
    rows = _moe_ffn(tile_expert, n_valid, xg, w_gate_up, b_gate_up, w_down, b_down, tm)
    pos_steps = pos.reshape(n_tok // ts_comb, ts_comb, TOP_K).swapaxes(1, 2).reshape(n_tok * TOP_K)
    tok_rows = _sc_gather_rows(rows, pos_steps)
    return _moe_combine(tok_rows, top_w, x1, mod3, g_post_ffn.reshape(1, d), ts_comb)


def kernel(x, c, w_mod, b_mod, g_pre_mix, g_post_mix, w_in, lambda_q1, lambda_k1, lambda_q2,
           lambda_k2, g_subln, w_branch_sb, w_branch_diff, w_out, g_pre_ffn, g_post_ffn,
           w_router, b_router, w_gate_up, b_gate_up, w_down, b_down):
    depth = w_mod.shape[0]
    for l in range(depth):
        lamv = jnp.stack([lambda_q1[l], lambda_k1[l], lambda_q2[l], lambda_k2[l]])
        x = _layer(x, c, w_mod[l], b_mod[l], g_pre_mix[l], g_post_mix[l], w_in[l], lamv,
                   g_subln[l], w_branch_sb[l], w_branch_diff[l], w_out[l], g_pre_ffn[l],
                   g_post_ffn[l], w_router[l], b_router[l], w_gate_up[l], b_gate_up[l],
                   w_down[l], b_down[l],
                   ts_in=512, tq=256, ts_merge=512, tm=512, ts_comb=256)
    return x
```

```python
import functools
import math
from typing import NamedTuple

import jax
import jax.numpy as jnp
from jax import lax
from jax.experimental import pallas as pl
from jax.experimental.pallas import tpu as pltpu
from jax.experimental.pallas import tpu_sc as plsc

D_MODEL = 1024
SB_HEADS = 8
SB_HEAD_DIM = 64
SB_WIDTH = SB_HEADS * SB_HEAD_DIM
DIFF_HEADS = 4
DIFF_HEAD_DIM = 64
DIFF_V_DIM = 2 * DIFF_HEAD_DIM
DIFF_QK_WIDTH = DIFF_HEADS * 2 * DIFF_HEAD_DIM
DIFF_V_WIDTH = DIFF_HEADS * DIFF_V_DIM
N_EXPERTS = 32
TOP_K = 4
D_EXPERT = D_MODEL
SWIGLU_LIMIT = 7.0
SWIGLU_ALPHA = 1.702
RMS_EPS = 1e-6
N_MOD = 6
LAM_INIT = 0.8 - 0.6 * math.exp(-0.3 * 0)

LANES = 128
SUBLANES = 8
NEG_BIG = -1e30
EXP_ZERO_MARGIN = 110.0
NORM_SLACK = 1.01
SC_CORES = 2
SC_SUBCORES = 16
SC_GATHER_ROWS = 128
MERGE_SUB = 256

MAIN_WIDTH = 2 * SB_WIDTH + 2 * DIFF_QK_WIDTH + 2 * D_MODEL
VT_ROWS = SB_WIDTH + DIFF_V_WIDTH
COLBLK_K_SB = SB_WIDTH // LANES
COLBLK_Q_DF = 2 * SB_WIDTH // LANES
COLBLK_K_DF = COLBLK_Q_DF + DIFF_QK_WIDTH // LANES
GATE_COL0 = 2 * SB_WIDTH + 2 * DIFF_QK_WIDTH

V7X_VMEM_BYTES = 64 * 1024 * 1024
VMEM_LIMIT = V7X_VMEM_BYTES - 8 * 1024 * 1024


class _Tiles(NamedTuple):
    in_proj: int = 512
    attn: int = 256
    merge: int = 512
    expert: int = 512
    combine: int = 256


def _cparams(sem, vmem=VMEM_LIMIT):
    return pltpu.CompilerParams(dimension_semantics=sem, vmem_limit_bytes=vmem)


def _rms(x):
    return x * lax.rsqrt(jnp.mean(x * x, axis=-1, keepdims=True) + RMS_EPS)


def _mod_kernel(c_ref, w_ref, b_ref, lamv_ref, mod_ref, lam_ref):
    c = c_ref[...]
    ca = c * jax.nn.sigmoid(c)
    mod_ref[...] = jnp.dot(ca, w_ref[...], preferred_element_type=jnp.float32,
                           precision=lax.Precision.HIGHEST) + b_ref[...]
    lv = lamv_ref[...]
    s1 = jnp.sum(lv[0:1] * lv[1:2], axis=-1, keepdims=True)
    s2 = jnp.sum(lv[2:3] * lv[3:4], axis=-1, keepdims=True)
    lam = jnp.exp(s1) - jnp.exp(s2) + LAM_INIT
    lam_ref[...] = jnp.broadcast_to(lam, lam_ref.shape)


def _mod_proj(c, w_mod, b_mod, lamv):
    bsz = c.shape[0]
    tn = 1536
    n = w_mod.shape[1]
    return pl.pallas_call(
        _mod_kernel,
        grid=(n // tn,),
        in_specs=[
            pl.BlockSpec((bsz, D_MODEL), lambda j: (0, 0)),
            pl.BlockSpec((D_MODEL, tn), lambda j: (0, j)),
            pl.BlockSpec((1, tn), lambda j: (0, j)),
            pl.BlockSpec((4, DIFF_HEAD_DIM), lambda j: (0, 0)),
        ],
        out_specs=[
            pl.BlockSpec((bsz, tn), lambda j: (0, j)),
            pl.BlockSpec((SUBLANES, LANES), lambda j: (0, 0)),
        ],
        out_shape=[
            jax.ShapeDtypeStruct((bsz, n), jnp.float32),
            jax.ShapeDtypeStruct((SUBLANES, LANES), jnp.float32),
        ],
        compiler_params=_cparams(("arbitrary",)),
        name="mod_proj",
    )(c, w_mod, b_mod.reshape(1, n), lamv)


IN_CHUNK = 1024


def _in_proj_kernel(x_ref, mod_ref, g_ref, wm_ref, wvt_ref, main_ref, vt_ref, h_scr, *, tk):
    x = x_ref[0]
    mod = mod_ref[0]
    h = _rms(x) * g_ref[...]
    h = h * (1.0 + mod[1:2]) + mod[0:1]
    hb = h.astype(jnp.bfloat16)
    groups = tk // SUBLANES
    cols = []
    for ct in range(D_MODEL // LANES):
        h_scr[ct] = h[:, ct * LANES:(ct + 1) * LANES]
        pieces = []
        for blk in range(h.shape[0] // tk):
            for g in range(groups):
                pieces.append(h_scr[ct, pl.ds(blk * tk + g, SUBLANES, stride=groups), :])
        cols.append(jnp.concatenate(pieces, axis=0))
    hpb = jnp.concatenate(cols, axis=1).astype(jnp.bfloat16)

    half = IN_CHUNK // 2
    for ci in range(MAIN_WIDTH // IN_CHUNK):
        c0 = ci * IN_CHUNK
        if c0 == 0:
            q = jnp.dot(hb, wm_ref[:, :half], preferred_element_type=jnp.float32)
            k = jnp.dot(hpb, wm_ref[:, half:IN_CHUNK], preferred_element_type=jnp.float32)
            main_ref[0, :, :half] = (q * 0.0625).astype(jnp.bfloat16)
            main_ref[0, :, half:IN_CHUNK] = k.astype(jnp.bfloat16)
            continue
        p = jnp.dot(hb, wm_ref[:, c0:c0 + IN_CHUNK], preferred_element_type=jnp.float32)
        if c0 < GATE_COL0:
            main_ref[0, :, c0:c0 + half] = (p[:, :half] * 0.125).astype(jnp.bfloat16)
            main_ref[0, :, c0 + half:c0 + IN_CHUNK] = p[:, half:].astype(jnp.bfloat16)
        else:
            main_ref[0, :, c0:c0 + IN_CHUNK] = jax.nn.sigmoid(p).astype(jnp.bfloat16)
    nt = (((1,), (1,)), ((), ()))
    vt_sb = lax.dot_general(wvt_ref[:SB_WIDTH, :], hpb, nt, preferred_element_type=jnp.float32)
    vt_df = lax.dot_general(wvt_ref[SB_WIDTH:, :], hb, nt, preferred_element_type=jnp.float32)
    vt_ref[0, :SB_WIDTH, :] = vt_sb.astype(jnp.bfloat16)
    vt_ref[0, SB_WIDTH:, :] = vt_df.astype(jnp.bfloat16)


def _in_proj(x, mod3, g_pre, w_main, w_vt, ts, tk):
    bsz, seq, _ = x.shape
    assert ts % tk == 0
    return pl.pallas_call(
        functools.partial(_in_proj_kernel, tk=tk),
        grid=(bsz, seq // ts),
        in_specs=[
            pl.BlockSpec((1, ts, D_MODEL), lambda b, i: (b, i, 0)),
            pl.BlockSpec((1, N_MOD, D_MODEL), lambda b, i: (b, 0, 0)),
            pl.BlockSpec((1, D_MODEL), lambda b, i: (0, 0)),
            pl.BlockSpec((D_MODEL, MAIN_WIDTH), lambda b, i: (0, 0)),
            pl.BlockSpec((VT_ROWS, D_MODEL), lambda b, i: (0, 0)),
        ],
        out_specs=[
            pl.BlockSpec((1, ts, MAIN_WIDTH), lambda b, i: (b, i, 0)),
            pl.BlockSpec((1, VT_ROWS, ts), lambda b, i: (b, 0, i)),
        ],
        out_shape=[
            jax.ShapeDtypeStruct((bsz, seq, MAIN_WIDTH), jnp.bfloat16),
            jax.ShapeDtypeStruct((bsz, VT_ROWS, seq), jnp.bfloat16),
        ],
        scratch_shapes=[pltpu.VMEM((D_MODEL // LANES, ts, LANES), jnp.float32)],
        compiler_params=_cparams(("arbitrary", "arbitrary")),
        name="in_proj",
    )(x, mod3, g_pre, w_main, w_vt)


def _suffix_excl_prod8(tot):
    sub = lax.broadcasted_iota(jnp.int32, tot.shape, 0)
    x = jnp.where(sub < SUBLANES - 1, pltpu.roll(tot, SUBLANES - 1, 0), 1.0)
    for sh in (1, 2, 4):
        x = x * jnp.where(sub + sh < SUBLANES, pltpu.roll(x, SUBLANES - sh, 0), 1.0)
    return x


def _sb_scores(k_ref, q_heads, s_ref, slot, j, tk):
    kb = k_ref[0, pl.ds(pl.multiple_of(j * tk, tk), tk), :]
    for h in range(2):
        s_ref[slot, h] = lax.dot_general(kb, q_heads[h], (((1,), (1,)), ((), ())),
                                         preferred_element_type=jnp.float32)


def _sb_weights(zt, c8, ok, groups):
    tq = zt.shape[1]
    r = 0.5 - 0.5 * jnp.tanh(zt)
    if ok is not None:
        r = jnp.where(ok, r, 1.0)
    rg = [r[g * SUBLANES:(g + 1) * SUBLANES, :] for g in range(groups)]
    tot = rg[0]
    for g in range(1, groups):
        tot = tot * rg[g]
    p = c8 * _suffix_excl_prod8(tot)
    pieces = [None] * groups
    for g in range(groups - 1, -1, -1):
        pn = p * rg[g]
        pieces[g] = p - pn
        p = pn
    a = jnp.concatenate(pieces, axis=0).astype(jnp.bfloat16)
    return a, jnp.broadcast_to(p[0:1, :], (SUBLANES, tq))


def _sb_attn_kernel(q_ref, k_ref, v_ref, o_ref, acc_ref, c_ref, s_ref, ok_ref, *, tq, tk):
    i = pl.program_id(2)
    groups = tk // SUBLANES
    q2 = q_ref[0]
    lane = lax.broadcasted_iota(jnp.int32, q2.shape, 1)
    zero = jnp.zeros_like(q2)
    q_heads = (jnp.where(lane < SB_HEAD_DIM, q2, zero), jnp.where(lane < SB_HEAD_DIM, zero, q2))

    def step(j, slot, masked):
        _sb_scores(k_ref, q_heads, s_ref, 1 - slot, jnp.maximum(j - 1, 0), tk)
        ok = (ok_ref[...] > 0.5) if masked else None
        off = pl.multiple_of(j * tk, tk)
        ws = []
        for h in range(2):
            a, c_new = _sb_weights(s_ref[slot, h], c_ref[h], ok, groups)
            c_ref[h] = c_new
            ws.append(a)
        for h in range(2):
            vt_h = v_ref[0, h * SB_HEAD_DIM:(h + 1) * SB_HEAD_DIM, pl.ds(off, tk)]
            acc_ref[h] += jnp.dot(vt_h, ws[h], preferred_element_type=jnp.float32)

    @pl.when(i == 0)
    def _():
        row = lax.broadcasted_iota(jnp.int32, (tk, tq), 0)
        col = lax.broadcasted_iota(jnp.int32, (tk, tq), 1)
        ok_ref[...] = jnp.where((row % SUBLANES) * groups + row // SUBLANES < col, 1.0, 0.0)

    acc_ref[...] = jnp.zeros_like(acc_ref)
    c_ref[...] = jnp.ones_like(c_ref)
    _sb_scores(k_ref, q_heads, s_ref, 0, i, tk)
    step(i, 0, True)

    def stick_left():
        return jnp.max(c_ref[...]) > 0.0

    def more(state):
        m, go = state
        return jnp.logical_and(m < i // 2, go)

    def pair(state):
        m, _ = state
        j = i - 1 - 2 * m
        step(j, 1, False)
        go_on = stick_left()

        @pl.when(go_on)
        def _():
            step(j - 1, 0, False)

        return m + 1, jnp.logical_and(go_on, stick_left())

    m_done, go = lax.while_loop(more, pair, (jnp.int32(0), i >= 0))

    @pl.when(jnp.logical_and(jnp.logical_and(i % 2 == 1, m_done == i // 2), go))
    def _():
        step(0, 1, False)

    ot = jnp.concatenate([acc_ref[0], acc_ref[1]], axis=0)
    o_ref[0] = ot.T.astype(jnp.bfloat16)


def _sb_attn(main, vt, tq, tk):
    bsz, seq, _ = main.shape
    assert tq == tk
    kern = functools.partial(_sb_attn_kernel, tq=tq, tk=tk)
    return pl.pallas_call(
        kern,
        grid=(bsz, SB_WIDTH // LANES, seq // tq),
        in_specs=[
            pl.BlockSpec((1, tq, LANES), lambda b, p, i: (b, i, p)),
            pl.BlockSpec((1, seq, LANES), lambda b, p, i: (b, 0, COLBLK_K_SB + p)),
            pl.BlockSpec((1, LANES, seq), lambda b, p, i: (b, p, 0)),
        ],
        out_specs=pl.BlockSpec((1, tq, LANES), lambda b, p, i: (b, i, p)),
        out_shape=jax.ShapeDtypeStruct((bsz, seq, SB_WIDTH), jnp.bfloat16),
        scratch_shapes=[
            pltpu.VMEM((2, SB_HEAD_DIM, tq), jnp.float32),
            pltpu.VMEM((2, SUBLANES, tq), jnp.float32),
            pltpu.VMEM((2, 2, tk, tq), jnp.float32),
            pltpu.VMEM((tk, tq), jnp.float32),
        ],
        compiler_params=_cparams(("arbitrary", "arbitrary", "arbitrary")),
        name="sb_attn",
    )(main, main, vt)


def _diff_attn_kernel(slopes_ref, inv_slopes_ref, q_ref, k_ref, v_ref, lam_ref, g_ref, o_ref,
                      acc_ref, m_ref, l_ref, s_ref, smax_ref, kn_ref, bias_ref, *, tq, tk):
    hd = pl.program_id(1)
    i = pl.program_id(2)
    slope = slopes_ref[hd]

    @pl.when(i == 0)
    def _():
        kf = k_ref[0].astype(jnp.float32)
        kn2 = jnp.max(jnp.sum(kf * kf, axis=-1, keepdims=True), axis=0, keepdims=True)
        kn_ref[...] = jnp.broadcast_to(kn2, kn_ref.shape)
        row0 = lax.broadcasted_iota(jnp.int32, (tk, tq), 0)
        col0 = lax.broadcasted_iota(jnp.int32, (tk, tq), 1)
        bias_ref[...] = slope * (row0 - col0).astype(jnp.float32)

    q2 = q_ref[0]
    lane = lax.broadcasted_iota(jnp.int32, q2.shape, 1)
    zero = jnp.zeros_like(q2)
    q_maps = (jnp.where(lane < DIFF_HEAD_DIM, q2, zero), jnp.where(lane < DIFF_HEAD_DIM, zero, q2))

    def scores(slot, j, masked):
        kb = k_ref[0, pl.ds(pl.multiple_of(j * tk, tk), tk), :]
        for m in range(2):
            s = lax.dot_general(kb, q_maps[m], (((1,), (1,)), ((), ())),
                                preferred_element_type=jnp.float32) + bias_ref[...]
            if masked:
                row = lax.broadcasted_iota(jnp.int32, (tk, tq), 0)
                col = lax.broadcasted_iota(jnp.int32, (tk, tq), 1)
                s = jnp.where(row <= col, s, NEG_BIG)
            s_ref[slot, m] = s
            smax_ref[slot, m] = jnp.max(s, axis=0, keepdims=True)

    def step(j, slot):
        scores(1 - slot, jnp.maximum(j - 1, 0), False)
        off = pl.multiple_of(j * tk, tk)
        vtb = v_ref[0, :, pl.ds(off, tk)]
        cb = slope * ((j - i) * tk).astype(jnp.float32)
        ps, alphas = [], []
        for m in range(2):
            s = s_ref[slot, m]
            m_old = m_ref[m]
            m_new = jnp.maximum(m_old, smax_ref[slot, m] + cb)
            alpha = jnp.exp(m_old - m_new)
            p = jnp.exp(s - (m_new - cb))
            l_ref[m] = alpha * l_ref[m] + jnp.sum(p, axis=0, keepdims=True)
            m_ref[m] = m_new
            ps.append(p.astype(jnp.bfloat16))
            alphas.append(alpha)
        for m in range(2):
            acc_ref[m] = alphas[m] * acc_ref[m] + jnp.dot(
                vtb, ps[m], preferred_element_type=jnp.float32)

    acc_ref[...] = jnp.zeros_like(acc_ref)
    m_ref[...] = jnp.full_like(m_ref, NEG_BIG)
    l_ref[...] = jnp.zeros_like(l_ref)
    scores(0, i, True)
    step(i, 0)

    qf = q2.astype(jnp.float32)
    qn2 = jnp.max(jnp.sum(qf * qf, axis=-1, keepdims=True), axis=0, keepdims=True)
    zabs = jnp.sqrt(qn2 * kn_ref[0:1, 0:1]) * NORM_SLACK
    m_lo = jnp.min(jnp.minimum(m_ref[0], m_ref[1]), axis=1, keepdims=True)
    reach = (EXP_ZERO_MARGIN + zabs - m_lo) * inv_slopes_ref[hd]
    n_need = jnp.floor(jnp.clip((reach - 1.0) * (1.0 / tk), -1.0, 1e6)) + 1.0
    n_back = jnp.minimum(i, jnp.max(n_need).astype(jnp.int32))

    def quad(n, carry):
        j = i - 1 - 4 * n
        step(j, 1)
        step(j - 1, 0)
        step(j - 2, 1)
        step(j - 3, 0)
        return carry

    n_quads = n_back // 4
    lax.fori_loop(0, n_quads, quad, 0)
    rest = n_back - 4 * n_quads

    @pl.when(rest >= 2)
    def _():
        j = i - 1 - 4 * n_quads
        step(j, 1)
        step(j - 1, 0)

    @pl.when(rest % 2 == 1)
    def _():
        step(i - n_back, 1)

    lam = lam_ref[0:1, 0:1]
    o = acc_ref[0] / l_ref[0] - lam * (acc_ref[1] / l_ref[1])
    ms = jnp.mean(o * o, axis=0, keepdims=True)
    y = o * lax.rsqrt(ms + RMS_EPS) * g_ref[...] * (1.0 - LAM_INIT)
    o_ref[0] = y.T.astype(jnp.bfloat16)


def _diff_attn(main, vt, slopes, lam, g_col, tq, tk):
    bsz, seq, _ = main.shape
    assert tq == tk
    kern = functools.partial(_diff_attn_kernel, tq=tq, tk=tk)
    vrow0 = SB_WIDTH // LANES
    return pl.pallas_call(
        kern,
        grid=(bsz, DIFF_HEADS, seq // tq),
        in_specs=[
            pl.BlockSpec(memory_space=pltpu.SMEM),
            pl.BlockSpec(memory_space=pltpu.SMEM),
            pl.BlockSpec((1, tq, LANES), lambda b, h, i: (b, i, COLBLK_Q_DF + h)),
            pl.BlockSpec((1, seq, LANES), lambda b, h, i: (b, 0, COLBLK_K_DF + h)),
            pl.BlockSpec((1, DIFF_V_DIM, seq), lambda b, h, i: (b, vrow0 + h, 0)),
            pl.BlockSpec((SUBLANES, LANES), lambda b, h, i: (0, 0)),
            pl.BlockSpec((DIFF_V_DIM, 1), lambda b, h, i: (0, 0)),
        ],
        out_specs=pl.BlockSpec((1, tq, DIFF_V_DIM), lambda b, h, i: (b, i, h)),
        out_shape=jax.ShapeDtypeStruct((bsz, seq, DIFF_V_WIDTH), jnp.bfloat16),
        scratch_shapes=[
            pltpu.VMEM((2, DIFF_V_DIM, tq), jnp.float32),
            pltpu.VMEM((2, 1, tq), jnp.float32),
            pltpu.VMEM((2, 1, tq), jnp.float32),
            pltpu.VMEM((2, 2, tk, tq), jnp.float32),
            pltpu.VMEM((2, 2, 1, tq), jnp.float32),
            pltpu.VMEM((SUBLANES, LANES), jnp.float32),
            pltpu.VMEM((tk, tq), jnp.float32),
        ],
        compiler_params=_cparams(("arbitrary", "arbitrary", "arbitrary")),
        name="diff_attn",
    )(slopes, 1.0 / slopes, main, main, vt, lam, g_col)


def _pack_bf16_pair(a, b):
    ab = pltpu.bitcast(a.astype(jnp.bfloat16).astype(jnp.float32), jnp.uint32)
    bb = pltpu.bitcast(b.astype(jnp.bfloat16).astype(jnp.float32), jnp.uint32)
    return ab | (bb >> 16)


def _unpack_bf16_pair(w):
    hi = pltpu.bitcast(w & jnp.uint32(0xFFFF0000), jnp.float32)
    lo = pltpu.bitcast(w << 16, jnp.float32)
    return jnp.concatenate([hi, lo], axis=1)


def _merge_router_kernel(ysb_ref, ydf_ref, gates_ref, x_ref, mod_ref, wsb_ref, wdf_ref, wout_ref,
                         gpost_ref, gpre_ref, wrh_ref, wrl_ref, br_ref,
                         x1_ref, h2_ref, idx_ref, wgt_ref, rank_ref, cnt_ref, base_ref):
    first = jnp.logical_and(pl.program_id(0) == 0, pl.program_id(1) == 0)

    @pl.when(first)
    def _():
        base_ref[...] = jnp.zeros_like(base_ref)

    mod = mod_ref[0]
    subs = [slice(s * MERGE_SUB, (s + 1) * MERGE_SUB) for s in range(x_ref.shape[1] // MERGE_SUB)]
    half = D_MODEL // 2
    branch = [(jnp.dot(ysb_ref[0, rows, :], wsb_ref[...], preferred_element_type=jnp.float32),
               jnp.dot(ydf_ref[0, rows, :], wdf_ref[...], preferred_element_type=jnp.float32))
              for rows in subs]
    merged = []
    for rows, (a, b) in zip(subs, branch):
        g = gates_ref[0, rows, :].astype(jnp.float32)
        merged.append((g[:, :D_MODEL] * a + g[:, D_MODEL:] * b).astype(jnp.bfloat16))
    mixes = [jnp.dot(m, wout_ref[...], preferred_element_type=jnp.float32) for m in merged]
    h2s = []
    for rows, mix in zip(subs, mixes):
        x1 = x_ref[0, rows, :] + mod[2:3] * (_rms(mix) * gpost_ref[...])
        x1_ref[0, rows, :] = x1
        h2 = _rms(x1) * gpre_ref[...]
        h2 = h2 * (1.0 + mod[4:5]) + mod[3:4]
        h2_ref[0, rows, :] = _pack_bf16_pair(h2[:, :half], h2[:, half:])
        h2s.append(h2)
    logit_list = []
    for h2 in h2s:
        hh = h2.astype(jnp.bfloat16)
        hl = (h2 - hh.astype(jnp.float32)).astype(jnp.bfloat16)
        logit_list.append(jnp.dot(hh, wrh_ref[...], preferred_element_type=jnp.float32)
                          + jnp.dot(hh, wrl_ref[...], preferred_element_type=jnp.float32)
                          + jnp.dot(hl, wrh_ref[...], preferred_element_type=jnp.float32)
                          + br_ref[...])
    for rows, logits in zip(subs, logit_list):
        _route_rows(rows, logits, idx_ref, wgt_ref, rank_ref, cnt_ref, base_ref)


def _route_rows(rows, logits, idx_ref, wgt_ref, rank_ref, cnt_ref, base_ref):
    lane = lax.broadcasted_iota(jnp.int32, logits.shape, 1)
    lanef = lane.astype(jnp.float32)
    vals, idxs = [], []
    cur = logits
    for _ in range(TOP_K):
        mx = jnp.max(cur, axis=-1, keepdims=True)
        ix = jnp.min(jnp.where(cur == mx, lanef, float(LANES)), axis=-1, keepdims=True)
        cur = jnp.where(lanef == ix, -jnp.inf, cur)
        vals.append(mx)
        idxs.append(ix)
    es = [jnp.exp(v - vals[0]) for v in vals]
    den = es[0] + es[1] + es[2] + es[3]
    oi = jnp.zeros(logits.shape, jnp.float32)
    ow = jnp.zeros(logits.shape, jnp.float32)
    for k in range(TOP_K):
        oi = jnp.where(lane == k, idxs[k], oi)
        ow = jnp.where(lane == k, es[k] / den, ow)
    idx_ref[0, rows, :] = oi.astype(jnp.int32)
    wgt_ref[0, rows, :] = ow

    ts = logits.shape[0]
    member = jnp.zeros(logits.shape, jnp.float32)
    for k in range(TOP_K):
        member = member + (lanef == idxs[k]).astype(jnp.float32)
    rr = lax.broadcasted_iota(jnp.int32, (ts, ts), 0)
    cc = lax.broadcasted_iota(jnp.int32, (ts, ts), 1)
    lower = jnp.where(cc < rr, 1.0, 0.0).astype(jnp.bfloat16)
    before = jnp.dot(lower, member.astype(jnp.bfloat16), preferred_element_type=jnp.float32)
    base = base_ref[0:1, :]
    rank_all = before + base
    orank = jnp.zeros(logits.shape, jnp.float32)
    for k in range(TOP_K):
        rk = jnp.sum(jnp.where(lanef == idxs[k], rank_all, 0.0), axis=-1, keepdims=True)
        orank = jnp.where(lane == k, rk, orank)
    rank_ref[0, rows, :] = orank.astype(jnp.int32)
    new_base = base + jnp.sum(member, axis=0, keepdims=True)
    base_ref[...] = jnp.broadcast_to(new_base, base_ref.shape)
    cnt_ref[...] = jnp.broadcast_to(new_base, cnt_ref.shape).astype(jnp.int32)


def _merge_router(ysb, ydf, main, x, mod3, wsb, wdf, wout, gpost, gpre, wrh, wrl, br, ts):
    bsz, seq, _ = x.shape
    const = lambda b, i: (0, 0)
    return pl.pallas_call(
        _merge_router_kernel,
        grid=(bsz, seq // ts),
        in_specs=[
            pl.BlockSpec((1, ts, SB_WIDTH), lambda b, i: (b, i, 0)),
            pl.BlockSpec((1, ts, DIFF_V_WIDTH), lambda b, i: (b, i, 0)),
            pl.BlockSpec((1, ts, 2 * D_MODEL), lambda b, i: (b, i, GATE_COL0 // (2 * D_MODEL))),
            pl.BlockSpec((1, ts, D_MODEL), lambda b, i: (b, i, 0)),
            pl.BlockSpec((1, N_MOD, D_MODEL), lambda b, i: (b, 0, 0)),
            pl.BlockSpec((SB_WIDTH, D_MODEL), const),
            pl.BlockSpec((DIFF_V_WIDTH, D_MODEL), const),
            pl.BlockSpec((D_MODEL, D_MODEL), const),
            pl.BlockSpec((1, D_MODEL), const),
            pl.BlockSpec((1, D_MODEL), const),
            pl.BlockSpec((D_MODEL, LANES), const),
            pl.BlockSpec((D_MODEL, LANES), const),
            pl.BlockSpec((1, LANES), const),
        ],
        out_specs=[
            pl.BlockSpec((1, ts, D_MODEL), lambda b, i: (b, i, 0)),
            pl.BlockSpec((1, ts, D_MODEL // 2), lambda b, i: (b, i, 0)),
            pl.BlockSpec((1, ts, LANES), lambda b, i: (b, i, 0)),
            pl.BlockSpec((1, ts, LANES), lambda b, i: (b, i, 0)),
            pl.BlockSpec((1, ts, LANES), lambda b, i: (b, i, 0)),
            pl.BlockSpec((SUBLANES, LANES), const),
        ],
        out_shape=[
            jax.ShapeDtypeStruct((bsz, seq, D_MODEL), jnp.float32),
            jax.ShapeDtypeStruct((bsz, seq, D_MODEL // 2), jnp.uint32),
            jax.ShapeDtypeStruct((bsz, seq, LANES), jnp.int32),
            jax.ShapeDtypeStruct((bsz, seq, LANES), jnp.float32),
            jax.ShapeDtypeStruct((bsz, seq, LANES), jnp.int32),
            jax.ShapeDtypeStruct((SUBLANES, LANES), jnp.int32),
        ],
        scratch_shapes=[pltpu.VMEM((SUBLANES, LANES), jnp.float32)],
        compiler_params=_cparams(("arbitrary", "arbitrary")),
        name="merge_router",
    )(ysb, ydf, main, x, mod3, wsb, wdf, wout, gpost, gpre, wrh, wrl, br)


def _sc_gather_rows(table, idx):
    n = idx.shape[0]
    width = table.shape[1]
    n_workers = SC_CORES * SC_SUBCORES
    per_worker = n // n_workers
    n_chunks = per_worker // SC_GATHER_ROWS
    assert n_chunks * SC_GATHER_ROWS * n_workers == n
    mesh = plsc.VectorSubcoreMesh(core_axis_name="c", subcore_axis_name="s",
                                  num_cores=SC_CORES, num_subcores=SC_SUBCORES)

    def body(table_hbm, idx_hbm, out_hbm, idx_v, rows_v, sem):
        wid = lax.axis_index("s") * SC_CORES + lax.axis_index("c")
        base = wid * per_worker

        @pl.loop(0, n_chunks)
        def _(ci):
            off = pl.multiple_of(base + ci * SC_GATHER_ROWS, SC_GATHER_ROWS)
            pltpu.sync_copy(idx_hbm.at[pl.ds(off, SC_GATHER_ROWS)], idx_v)
            pltpu.async_copy(table_hbm.at[idx_v], rows_v, sem).wait()
            pltpu.sync_copy(rows_v, out_hbm.at[pl.ds(off, SC_GATHER_ROWS)])

    return pl.kernel(
        body,
        out_type=jax.ShapeDtypeStruct((n, width), table.dtype),
        mesh=mesh,
        scratch_types=[
            pltpu.VMEM((SC_GATHER_ROWS,), jnp.int32),
            pltpu.VMEM((SC_GATHER_ROWS, width), table.dtype),
            pltpu.SemaphoreType.DMA,
        ],
        name="sc_gather_rows",
    )(table, idx)


def _sc_scatter_rows(src, pos_kmajor, n_rows):
    n_tok, width = src.shape
    n_workers = SC_CORES * SC_SUBCORES
    per_worker = n_tok // n_workers
    n_chunks = per_worker // SC_GATHER_ROWS
    assert n_chunks * SC_GATHER_ROWS * n_workers == n_tok
    mesh = plsc.VectorSubcoreMesh(core_axis_name="c", subcore_axis_name="s",
                                  num_cores=SC_CORES, num_subcores=SC_SUBCORES)

    def body(src_hbm, idx_hbm, out_hbm, idx_v, rows_v):
        wid = lax.axis_index("s") * SC_CORES + lax.axis_index("c")
        base = wid * per_worker

        @pl.loop(0, n_chunks)
        def _(ci):
            off = pl.multiple_of(base + ci * SC_GATHER_ROWS, SC_GATHER_ROWS)
            pltpu.sync_copy(src_hbm.at[pl.ds(off, SC_GATHER_ROWS)], rows_v)
            for k in range(TOP_K):
                koff = pl.multiple_of(k * n_tok + off, SC_GATHER_ROWS)
                pltpu.sync_copy(idx_hbm.at[pl.ds(koff, SC_GATHER_ROWS)], idx_v)
                pltpu.sync_copy(rows_v, out_hbm.at[idx_v])

    return pl.kernel(
        body,
        out_type=jax.ShapeDtypeStruct((n_rows, width), src.dtype),
        mesh=mesh,
        scratch_types=[
            pltpu.VMEM((SC_GATHER_ROWS,), jnp.int32),
            pltpu.VMEM((SC_GATHER_ROWS, width), src.dtype),
        ],
        name="sc_scatter_rows",
    )(src, pos_kmajor)


def _moe_ffn_kernel(te_ref, nt_ref, x_ref, wgu_ref, bgu_ref, wd_ref, bd_ref, o_ref,
                    wgu_bf, wd_bf):
    i = pl.program_id(0)
    n_valid = nt_ref[0]

    new_expert = jnp.logical_or(i == 0, te_ref[i] != te_ref[jnp.maximum(i - 1, 0)])

    @pl.when(jnp.logical_and(i < n_valid, new_expert))
    def _():
        wgu_bf[...] = wgu_ref[0].astype(jnp.bfloat16)
        wd_bf[...] = wd_ref[0].astype(jnp.bfloat16)

    @pl.when(i < n_valid)
    def _():
        xb = _unpack_bf16_pair(x_ref[...]).astype(jnp.bfloat16)
        gu = jnp.dot(xb, wgu_bf[...], preferred_element_type=jnp.float32) + bgu_ref[0]
        gate = jnp.minimum(gu[:, :D_EXPERT], SWIGLU_LIMIT)
        up = jnp.clip(gu[:, D_EXPERT:], -SWIGLU_LIMIT, SWIGLU_LIMIT)
        act = (up + 1.0) * (gate * jax.nn.sigmoid(SWIGLU_ALPHA * gate))
        out = jnp.dot(act.astype(jnp.bfloat16), wd_bf[...],
                      preferred_element_type=jnp.float32) + bd_ref[0]
        half = D_MODEL // 2
        o_ref[...] = _pack_bf16_pair(out[:, :half], out[:, half:])

    @pl.when(i >= n_valid)
    def _():
        o_ref[...] = jnp.zeros_like(o_ref)


def _moe_ffn(tile_expert, n_valid, xg, wgu, bgu, wd, bd, tm):
    n_tiles = xg.shape[0] // tm
    grid_spec = pltpu.PrefetchScalarGridSpec(
        num_scalar_prefetch=2,
        grid=(n_tiles,),
        in_specs=[
            pl.BlockSpec((tm, D_MODEL // 2), lambda i, te, nt: (jnp.minimum(i, nt[0] - 1), 0)),
            pl.BlockSpec((1, D_MODEL, 2 * D_EXPERT), lambda i, te, nt: (te[i], 0, 0)),
            pl.BlockSpec((1, 1, 2 * D_EXPERT), lambda i, te, nt: (te[i], 0, 0)),
            pl.BlockSpec((1, D_EXPERT, D_MODEL), lambda i, te, nt: (te[i], 0, 0)),
            pl.BlockSpec((1, 1, D_MODEL), lambda i, te, nt: (te[i], 0, 0)),
        ],
        out_specs=pl.BlockSpec((tm, D_MODEL // 2), lambda i, te, nt: (i, 0)),
        scratch_shapes=[
            pltpu.VMEM((D_MODEL, 2 * D_EXPERT), jnp.bfloat16),
            pltpu.VMEM((D_EXPERT, D_MODEL), jnp.bfloat16),
        ],
    )
    return pl.pallas_call(
        _moe_ffn_kernel,
        grid_spec=grid_spec,
        out_shape=jax.ShapeDtypeStruct((n_tiles * tm, D_MODEL // 2), jnp.uint32),
        compiler_params=_cparams(("arbitrary",)),
        name="moe_ffn",
    )(tile_expert, n_valid, xg, wgu, bgu.reshape(N_EXPERTS, 1, -1), wd,
      bd.reshape(N_EXPERTS, 1, -1))


def _moe_combine_kernel(rows_ref, wgt_ref, x1_ref, mod_ref, g_ref, o_ref):
    ts = x1_ref.shape[1]
    w = wgt_ref[0]
    y = jnp.zeros(x1_ref.shape[1:], jnp.float32)
    for k in range(TOP_K):
        y = y + w[:, k:k + 1] * _unpack_bf16_pair(rows_ref[k * ts:(k + 1) * ts, :])
    mod = mod_ref[0]
    o_ref[0] = x1_ref[0] + mod[5:6] * (_rms(y) * g_ref[...])


def _moe_combine(rows, wgt, x1, mod3, g_post, ts):
    bsz, seq, _ = x1.shape
    per_b = seq // ts
    return pl.pallas_call(
        _moe_combine_kernel,
        grid=(bsz, per_b),
        in_specs=[
            pl.BlockSpec((TOP_K * ts, D_MODEL // 2), lambda b, i: (b * per_b + i, 0)),
            pl.BlockSpec((1, ts, LANES), lambda b, i: (b, i, 0)),
            pl.BlockSpec((1, ts, D_MODEL), lambda b, i: (b, i, 0)),
            pl.BlockSpec((1, N_MOD, D_MODEL), lambda b, i: (b, 0, 0)),
            pl.BlockSpec((1, D_MODEL), lambda b, i: (0, 0)),
        ],
        out_specs=pl.BlockSpec((1, ts, D_MODEL), lambda b, i: (b, i, 0)),
        out_shape=jax.ShapeDtypeStruct((bsz, seq, D_MODEL), jnp.float32),
        compiler_params=_cparams(("arbitrary", "arbitrary")),
        name="moe_combine",
    )(rows, wgt, x1, mod3, g_post)


def _routing(top_idx, rank, counts, tm, n_tiles):
    padded = ((counts + tm - 1) // tm) * tm
    pend = jnp.cumsum(padded)
    pstart = pend - padded
    onehot = top_idx[:, :, None] == jnp.arange(N_EXPERTS, dtype=jnp.int32)[None, None, :]
    pos = rank + jnp.sum(jnp.where(onehot, pstart[None, None, :], 0), axis=-1)
    n_valid = (pend[-1] // tm).astype(jnp.int32)
    tile_row0 = jnp.arange(n_tiles, dtype=jnp.int32) * tm
    tile_expert = jnp.minimum(
        jnp.sum((tile_row0[:, None] >= pend[None, :]).astype(jnp.int32), axis=1), N_EXPERTS - 1)
    last_oh = jnp.arange(n_tiles, dtype=jnp.int32) == jnp.maximum(n_valid - 1, 0)
    last_expert = jnp.sum(jnp.where(last_oh, tile_expert, 0))
    tile_expert = jnp.where(jnp.arange(n_tiles, dtype=jnp.int32) < n_valid, tile_expert, last_expert)
    return pos.astype(jnp.int32), tile_expert.astype(jnp.int32), n_valid.reshape(1)


def _alibi_slopes(n_heads):
    return 2.0 ** (-8.0 * jnp.arange(1, n_heads + 1, dtype=jnp.float32) / n_heads)


def _layer(x, c, w_mod, b_mod, g_pre_mix, g_post_mix, w_in, lamv, g_subln, w_branch_sb,
           w_branch_diff, w_out, g_pre_ffn, g_post_ffn, w_router, b_router, w_gate_up,
           b_gate_up, w_down, b_down, tiles):
    ts_in, tq, ts_merge, tm, ts_comb = tiles
    bsz, seq, d = x.shape
    n_tok = bsz * seq
    bf = jnp.bfloat16

    mod, lam = _mod_proj(c, w_mod, b_mod, lamv)
    mod3 = mod.reshape(bsz, N_MOD, d)

    o_vsb = 2 * SB_WIDTH
    o_qdf = 3 * SB_WIDTH
    o_vdf = o_qdf + 2 * DIFF_QK_WIDTH
    o_g = o_vdf + DIFF_V_WIDTH
    w_main = jnp.concatenate([w_in[:, :o_vsb], w_in[:, o_qdf:o_vdf], w_in[:, o_g:]], axis=1).astype(bf)
    w_vt = jnp.concatenate([w_in[:, o_vsb:o_qdf], w_in[:, o_vdf:o_g]], axis=1).T.astype(bf)

    main, vt = _in_proj(x, mod3, g_pre_mix.reshape(1, d), w_main, w_vt, ts_in, tq)
    y_sb = _sb_attn(main, vt, tq, tq)
    y_df = _diff_attn(main, vt, _alibi_slopes(DIFF_HEADS), lam,
                      g_subln.reshape(DIFF_V_DIM, 1), tq, tq)

    wr = jnp.zeros((d, LANES), jnp.float32).at[:, :N_EXPERTS].set(w_router)
    wrh = wr.astype(bf)
    wrl = (wr - wrh.astype(jnp.float32)).astype(bf)
    br = jnp.full((1, LANES), NEG_BIG, jnp.float32).at[0, :N_EXPERTS].set(b_router)
    x1, h2p, top_idx, top_w, rank, counts = _merge_router(
        y_sb, y_df, main, x, mod3, w_branch_sb.astype(bf), w_branch_diff.astype(bf),
        w_out.astype(bf), g_post_mix.reshape(1, d), g_pre_ffn.reshape(1, d), wrh, wrl, br, ts_merge)

    n_tiles = (n_tok * TOP_K) // tm + N_EXPERTS
    pos, tile_expert, n_valid = _routing(
        top_idx.reshape(n_tok, LANES)[:, :TOP_K], rank.reshape(n_tok, LANES)[:, :TOP_K],
        counts[0, :N_EXPERTS], tm, n_tiles)
    xg = _sc_scatter_rows(h2p.reshape(n_tok, d // 2), pos.T.reshape(TOP_K * n_tok), n_tiles * tm)
    rows = _moe_ffn(tile_expert, n_valid, xg, w_gate_up, b_gate_up, w_down, b_down, tm)
    pos_steps = pos.reshape(n_tok // ts_comb, ts_comb, TOP_K).swapaxes(1, 2).reshape(n_tok * TOP_K)
    tok_rows = _sc_gather_rows(rows, pos_steps)
    return _moe_combine(tok_rows, top_w, x1, mod3, g_post_ffn.reshape(1, d), ts_comb)


def kernel(x, c, w_mod, b_mod, g_pre_mix, g_post_mix, w_in, lambda_q1, lambda_k1, lambda_q2,
           lambda_k2, g_subln, w_branch_sb, w_branch_diff, w_out, g_pre_ffn, g_post_ffn,
           w_router, b_router, w_gate_up, b_gate_up, w_down, b_down):
    depth = w_mod.shape[0]
    assert depth == 1, "LAM_INIT is the layer-0 value"
    for l in range(depth):
        lamv = jnp.stack([lambda_q1[l], lambda_k1[l], lambda_q2[l], lambda_k2[l]])
        x = _layer(x, c, w_mod[l], b_mod[l], g_pre_mix[l], g_post_mix[l], w_in[l], lamv,
                   g_subln[l], w_branch_sb[l], w_branch_diff[l], w_out[l], g_pre_ffn[l],
                   g_post_ffn[l], w_router[l], b_router[l], w_gate_up[l], b_gate_up[l],
                   w_down[l], b_down[l], _Tiles())
    return x
```

```python
import functools
import math
from typing import NamedTuple

import jax
import jax.numpy as jnp
from jax import lax
from jax.experimental import pallas as pl
from jax.experimental.pallas import tpu as pltpu
from jax.experimental.pallas import tpu_sc as plsc

D_MODEL = 1024
SB_HEADS = 8
SB_HEAD_DIM = 64
SB_WIDTH = SB_HEADS * SB_HEAD_DIM
DIFF_HEADS = 4
DIFF_HEAD_DIM = 64
DIFF_V_DIM = 2 * DIFF_HEAD_DIM
DIFF_QK_WIDTH = DIFF_HEADS * 2 * DIFF_HEAD_DIM
DIFF_V_WIDTH = DIFF_HEADS * DIFF_V_DIM
N_EXPERTS = 32
TOP_K = 4
D_EXPERT = D_MODEL
SWIGLU_LIMIT = 7.0
SWIGLU_ALPHA = 1.702
RMS_EPS = 1e-6
N_MOD = 6
LAM_INIT = 0.8 - 0.6 * math.exp(-0.3 * 0)

LANES = 128
SUBLANES = 8
NEG_BIG = -1e30
EXP_ZERO_MARGIN = 110.0
NORM_SLACK = 1.01
SC_CORES = 2
SC_SUBCORES = 16
SC_GATHER_ROWS = 128
MERGE_SUB = 256

MAIN_WIDTH = 2 * SB_WIDTH + 2 * DIFF_QK_WIDTH + 2 * D_MODEL
VT_ROWS = SB_WIDTH + DIFF_V_WIDTH
COLBLK_K_SB = SB_WIDTH // LANES
COLBLK_Q_DF = 2 * SB_WIDTH // LANES
COLBLK_K_DF = COLBLK_Q_DF + DIFF_QK_WIDTH // LANES
GATE_COL0 = 2 * SB_WIDTH + 2 * DIFF_QK_WIDTH

V7X_VMEM_BYTES = 64 * 1024 * 1024
VMEM_LIMIT = V7X_VMEM_BYTES - 8 * 1024 * 1024


class _Tiles(NamedTuple):
    in_proj: int = 512
    attn: int = 256
    merge: int = 512
    expert: int = 512
    combine: int = 256


def _cparams(sem, vmem=VMEM_LIMIT):
    return pltpu.CompilerParams(dimension_semantics=sem, vmem_limit_bytes=vmem)


def _rms(x):
    return x * lax.rsqrt(jnp.mean(x * x, axis=-1, keepdims=True) + RMS_EPS)


def _mod_kernel(c_ref, w_ref, b_ref, lamv_ref, mod_ref, lam_ref):
    c = c_ref[...]
    ca = c * jax.nn.sigmoid(c)
    mod_ref[...] = jnp.dot(ca, w_ref[...], preferred_element_type=jnp.float32,
                           precision=lax.Precision.HIGHEST) + b_ref[...]
    lv = lamv_ref[...]
    s1 = jnp.sum(lv[0:1] * lv[1:2], axis=-1, keepdims=True)
    s2 = jnp.sum(lv[2:3] * lv[3:4], axis=-1, keepdims=True)
    lam = jnp.exp(s1) - jnp.exp(s2) + LAM_INIT
    lam_ref[...] = jnp.broadcast_to(lam, lam_ref.shape)


def _mod_proj(c, w_mod, b_mod, lamv):
    bsz = c.shape[0]
    tn = 1536
    n = w_mod.shape[1]
    return pl.pallas_call(
        _mod_kernel,
        grid=(n // tn,),
        in_specs=[
            pl.BlockSpec((bsz, D_MODEL), lambda j: (0, 0)),
            pl.BlockSpec((D_MODEL, tn), lambda j: (0, j)),
            pl.BlockSpec((1, tn), lambda j: (0, j)),
            pl.BlockSpec((4, DIFF_HEAD_DIM), lambda j: (0, 0)),
        ],
        out_specs=[
            pl.BlockSpec((bsz, tn), lambda j: (0, j)),
            pl.BlockSpec((SUBLANES, LANES), lambda j: (0, 0)),
        ],
        out_shape=[
            jax.ShapeDtypeStruct((bsz, n), jnp.float32),
            jax.ShapeDtypeStruct((SUBLANES, LANES), jnp.float32),
        ],
        compiler_params=_cparams(("arbitrary",)),
        name="mod_proj",
    )(c, w_mod, b_mod.reshape(1, n), lamv)


IN_CHUNK = 1024


def _in_proj_kernel(x_ref, mod_ref, g_ref, wm_ref, wvt_ref, main_ref, vt_ref, h_scr, *, tk):
    x = x_ref[0]
    mod = mod_ref[0]
    h = _rms(x) * g_ref[...]
    h = h * (1.0 + mod[1:2]) + mod[0:1]
    hb = h.astype(jnp.bfloat16)
    groups = tk // SUBLANES
    cols = []
    for ct in range(D_MODEL // LANES):
        h_scr[ct] = h[:, ct * LANES:(ct + 1) * LANES]
        pieces = []
        for blk in range(h.shape[0] // tk):
            for g in range(groups):
                pieces.append(h_scr[ct, pl.ds(blk * tk + g, SUBLANES, stride=groups), :])
        cols.append(jnp.concatenate(pieces, axis=0))
    hpb = jnp.concatenate(cols, axis=1).astype(jnp.bfloat16)

    half = IN_CHUNK // 2
    for ci in range(MAIN_WIDTH // IN_CHUNK):
        c0 = ci * IN_CHUNK
        if c0 == 0:
            q = jnp.dot(hb, wm_ref[:, :half], preferred_element_type=jnp.float32)
            k = jnp.dot(hpb, wm_ref[:, half:IN_CHUNK], preferred_element_type=jnp.float32)
            main_ref[0, :, :half] = (q * 0.0625).astype(jnp.bfloat16)
            main_ref[0, :, half:IN_CHUNK] = k.astype(jnp.bfloat16)
            continue
        p = jnp.dot(hb, wm_ref[:, c0:c0 + IN_CHUNK], preferred_element_type=jnp.float32)
        if c0 < GATE_COL0:
            main_ref[0, :, c0:c0 + half] = (p[:, :half] * 0.125).astype(jnp.bfloat16)
            main_ref[0, :, c0 + half:c0 + IN_CHUNK] = p[:, half:].astype(jnp.bfloat16)
        else:
            main_ref[0, :, c0:c0 + IN_CHUNK] = jax.nn.sigmoid(p).astype(jnp.bfloat16)
    nt = (((1,), (1,)), ((), ()))
    vt_sb = lax.dot_general(wvt_ref[:SB_WIDTH, :], hpb, nt, preferred_element_type=jnp.float32)
    vt_df = lax.dot_general(wvt_ref[SB_WIDTH:, :], hb, nt, preferred_element_type=jnp.float32)
    vt_ref[0, :SB_WIDTH, :] = vt_sb.astype(jnp.bfloat16)
    vt_ref[0, SB_WIDTH:, :] = vt_df.astype(jnp.bfloat16)


def _in_proj(x, mod3, g_pre, w_main, w_vt, ts, tk):
    bsz, seq, _ = x.shape
    assert ts % tk == 0
    return pl.pallas_call(
        functools.partial(_in_proj_kernel, tk=tk),
        grid=(bsz, seq // ts),
        in_specs=[
            pl.BlockSpec((1, ts, D_MODEL), lambda b, i: (b, i, 0)),
            pl.BlockSpec((1, N_MOD, D_MODEL), lambda b, i: (b, 0, 0)),
            pl.BlockSpec((1, D_MODEL), lambda b, i: (0, 0)),
            pl.BlockSpec((D_MODEL, MAIN_WIDTH), lambda b, i: (0, 0)),
            pl.BlockSpec((VT_ROWS, D_MODEL), lambda b, i: (0, 0)),
        ],
        out_specs=[
            pl.BlockSpec((1, ts, MAIN_WIDTH), lambda b, i: (b, i, 0)),
            pl.BlockSpec((1, VT_ROWS, ts), lambda b, i: (b, 0, i)),
        ],
        out_shape=[
            jax.ShapeDtypeStruct((bsz, seq, MAIN_WIDTH), jnp.bfloat16),
            jax.ShapeDtypeStruct((bsz, VT_ROWS, seq), jnp.bfloat16),
        ],
        scratch_shapes=[pltpu.VMEM((D_MODEL // LANES, ts, LANES), jnp.float32)],
        compiler_params=_cparams(("arbitrary", "arbitrary")),
        name="in_proj",
    )(x, mod3, g_pre, w_main, w_vt)


def _suffix_excl_prod8(tot):
    sub = lax.broadcasted_iota(jnp.int32, tot.shape, 0)
    x = jnp.where(sub < SUBLANES - 1, pltpu.roll(tot, SUBLANES - 1, 0), 1.0)
    for sh in (1, 2, 4):
        x = x * jnp.where(sub + sh < SUBLANES, pltpu.roll(x, SUBLANES - sh, 0), 1.0)
    return x


def _sb_scores(k_ref, q_heads, s_ref, slot, j, tk):
    kb = k_ref[0, pl.ds(pl.multiple_of(j * tk, tk), tk), :]
    for h in range(2):
        s_ref[slot, h] = lax.dot_general(kb, q_heads[h], (((1,), (1,)), ((), ())),
                                         preferred_element_type=jnp.float32)


def _sb_weights(zt, c8, ok, groups):
    tq = zt.shape[1]
    r = 0.5 - 0.5 * jnp.tanh(zt)
    if ok is not None:
        r = jnp.where(ok, r, 1.0)
    rg = [r[g * SUBLANES:(g + 1) * SUBLANES, :] for g in range(groups)]
    tot = rg[0]
    for g in range(1, groups):
        tot = tot * rg[g]
    p = c8 * _suffix_excl_prod8(tot)
    pieces = [None] * groups
    for g in range(groups - 1, -1, -1):
        pn = p * rg[g]
        pieces[g] = p - pn
        p = pn
    a = jnp.concatenate(pieces, axis=0).astype(jnp.bfloat16)
    return a, jnp.broadcast_to(p[0:1, :], (SUBLANES, tq))


def _sb_attn_kernel(q_ref, k_ref, v_ref, o_ref, acc_ref, c_ref, s_ref, ok_ref, *, tq, tk):
    i = pl.program_id(2)
    groups = tk // SUBLANES
    q2 = q_ref[0]
    lane = lax.broadcasted_iota(jnp.int32, q2.shape, 1)
    zero = jnp.zeros_like(q2)
    q_heads = (jnp.where(lane < SB_HEAD_DIM, q2, zero), jnp.where(lane < SB_HEAD_DIM, zero, q2))

    def step(j, slot, masked):
        _sb_scores(k_ref, q_heads, s_ref, 1 - slot, jnp.maximum(j - 1, 0), tk)
        ok = (ok_ref[...] > 0.5) if masked else None
        off = pl.multiple_of(j * tk, tk)
        ws = []
        for h in range(2):
            a, c_new = _sb_weights(s_ref[slot, h], c_ref[h], ok, groups)
            c_ref[h] = c_new
            ws.append(a)
        for h in range(2):
            vt_h = v_ref[0, h * SB_HEAD_DIM:(h + 1) * SB_HEAD_DIM, pl.ds(off, tk)]
            acc_ref[h] += jnp.dot(vt_h, ws[h], preferred_element_type=jnp.float32)

    @pl.when(i == 0)
    def _():
        row = lax.broadcasted_iota(jnp.int32, (tk, tq), 0)
        col = lax.broadcasted_iota(jnp.int32, (tk, tq), 1)
        ok_ref[...] = jnp.where((row % SUBLANES) * groups + row // SUBLANES < col, 1.0, 0.0)

    acc_ref[...] = jnp.zeros_like(acc_ref)
    c_ref[...] = jnp.ones_like(c_ref)
    _sb_scores(k_ref, q_heads, s_ref, 0, i, tk)
    step(i, 0, True)
    c_ref[...] = c_ref[...] * jnp.where(i >= 1, 1.0, 0.0)
    step(jnp.maximum(i - 1, 0), 1, False)

    def stick_left():
        return jnp.max(c_ref[...]) > 0.0

    n_rest = jnp.maximum(i - 1, 0)

    def more(state):
        m, go = state
        return jnp.logical_and(m < n_rest // 2, go)

    def pair(state):
        m, _ = state
        j = i - 2 - 2 * m
        step(j, 0, False)
        go_on = stick_left()

        @pl.when(go_on)
        def _():
            step(j - 1, 1, False)

        return m + 1, jnp.logical_and(go_on, stick_left())

    m_done, go = lax.while_loop(more, pair, (jnp.int32(0), stick_left()))

    @pl.when(jnp.logical_and(jnp.logical_and(n_rest % 2 == 1, m_done == n_rest // 2), go))
    def _():
        step(0, 0, False)

    ot = jnp.concatenate([acc_ref[0], acc_ref[1]], axis=0)
    o_ref[0] = ot.T.astype(jnp.bfloat16)


def _sb_attn(main, vt, tq, tk):
    bsz, seq, _ = main.shape
    assert tq == tk
    kern = functools.partial(_sb_attn_kernel, tq=tq, tk=tk)
    return pl.pallas_call(
        kern,
        grid=(bsz, SB_WIDTH // LANES, seq // tq),
        in_specs=[
            pl.BlockSpec((1, tq, LANES), lambda b, p, i: (b, i, p)),
            pl.BlockSpec((1, seq, LANES), lambda b, p, i: (b, 0, COLBLK_K_SB + p)),
            pl.BlockSpec((1, LANES, seq), lambda b, p, i: (b, p, 0)),
        ],
        out_specs=pl.BlockSpec((1, tq, LANES), lambda b, p, i: (b, i, p)),
        out_shape=jax.ShapeDtypeStruct((bsz, seq, SB_WIDTH), jnp.bfloat16),
        scratch_shapes=[
            pltpu.VMEM((2, SB_HEAD_DIM, tq), jnp.float32),
            pltpu.VMEM((2, SUBLANES, tq), jnp.float32),
            pltpu.VMEM((2, 2, tk, tq), jnp.float32),
            pltpu.VMEM((tk, tq), jnp.float32),
        ],
        compiler_params=_cparams(("arbitrary", "arbitrary", "arbitrary")),
        name="sb_attn",
    )(main, main, vt)


def _diff_attn_kernel(slopes_ref, inv_slopes_ref, q_ref, k_ref, v_ref, lam_ref, g_ref, o_ref,
                      acc_ref, m_ref, l_ref, s_ref, smax_ref, kn_ref, bias_ref, *, tq, tk):
    hd = pl.program_id(1)
    i = pl.program_id(2)
    slope = slopes_ref[hd]

    @pl.when(i == 0)
    def _():
        kf = k_ref[0].astype(jnp.float32)
        kn2 = jnp.max(jnp.sum(kf * kf, axis=-1, keepdims=True), axis=0, keepdims=True)
        kn_ref[...] = jnp.broadcast_to(kn2, kn_ref.shape)
        row0 = lax.broadcasted_iota(jnp.int32, (tk, tq), 0)
        col0 = lax.broadcasted_iota(jnp.int32, (tk, tq), 1)
        bias_ref[...] = slope * (row0 - col0).astype(jnp.float32)

    q2 = q_ref[0]
    lane = lax.broadcasted_iota(jnp.int32, q2.shape, 1)
    zero = jnp.zeros_like(q2)
    q_maps = (jnp.where(lane < DIFF_HEAD_DIM, q2, zero), jnp.where(lane < DIFF_HEAD_DIM, zero, q2))

    def scores(slot, j, masked, dead=None):
        kb = k_ref[0, pl.ds(pl.multiple_of(j * tk, tk), tk), :]
        for m in range(2):
            s = lax.dot_general(kb, q_maps[m], (((1,), (1,)), ((), ())),
                                preferred_element_type=jnp.float32) + bias_ref[...]
            if masked:
                row = lax.broadcasted_iota(jnp.int32, (tk, tq), 0)
                col = lax.broadcasted_iota(jnp.int32, (tk, tq), 1)
                s = jnp.where(row <= col, s, NEG_BIG)
            if dead is not None:
                s = jnp.where(dead, NEG_BIG, s)
            s_ref[slot, m] = s
            smax_ref[slot, m] = jnp.max(s, axis=0, keepdims=True)

    def step(j, slot, next_dead=None):
        scores(1 - slot, jnp.maximum(j - 1, 0), False, next_dead)
        off = pl.multiple_of(j * tk, tk)
        vtb = v_ref[0, :, pl.ds(off, tk)]
        cb = slope * ((j - i) * tk).astype(jnp.float32)
        ps, alphas = [], []
        for m in range(2):
            s = s_ref[slot, m]
            m_old = m_ref[m]
            m_new = jnp.maximum(m_old, smax_ref[slot, m] + cb)
            alpha = jnp.exp(m_old - m_new)
            p = jnp.exp(s - (m_new - cb))
            l_ref[m] = alpha * l_ref[m] + jnp.sum(p, axis=0, keepdims=True)
            m_ref[m] = m_new
            ps.append(p.astype(jnp.bfloat16))
            alphas.append(alpha)
        for m in range(2):
            acc_ref[m] = alphas[m] * acc_ref[m] + jnp.dot(
                vtb, ps[m], preferred_element_type=jnp.float32)

    acc_ref[...] = jnp.zeros_like(acc_ref)
    m_ref[...] = jnp.full_like(m_ref, NEG_BIG)
    l_ref[...] = jnp.zeros_like(l_ref)
    scores(0, i, True)
    step(i, 0, next_dead=(i == 0))
    step(jnp.maximum(i - 1, 0), 1)

    qf = q2.astype(jnp.float32)
    qn2 = jnp.max(jnp.sum(qf * qf, axis=-1, keepdims=True), axis=0, keepdims=True)
    zabs = jnp.sqrt(qn2 * kn_ref[0:1, 0:1]) * NORM_SLACK
    m_lo = jnp.min(jnp.minimum(m_ref[0], m_ref[1]), axis=1, keepdims=True)
    reach = (EXP_ZERO_MARGIN + zabs - m_lo) * inv_slopes_ref[hd]
    n_need = jnp.floor(jnp.clip((reach - 1.0) * (1.0 / tk), -1.0, 1e6)) + 1.0
    n_back = jnp.maximum(jnp.minimum(i, jnp.max(n_need).astype(jnp.int32)) - 1, 0)

    def quad(n, carry):
        j = i - 2 - 4 * n
        step(j, 0)
        step(j - 1, 1)
        step(j - 2, 0)
        step(j - 3, 1)
        return carry

    n_quads = n_back // 4
    lax.fori_loop(0, n_quads, quad, 0)
    rest = n_back - 4 * n_quads

    @pl.when(rest >= 2)
    def _():
        j = i - 2 - 4 * n_quads
        step(j, 0)
        step(j - 1, 1)

    @pl.when(rest % 2 == 1)
    def _():
        step(i - 1 - n_back, 0)

    lam = lam_ref[0:1, 0:1]
    o = acc_ref[0] / l_ref[0] - lam * (acc_ref[1] / l_ref[1])
    ms = jnp.mean(o * o, axis=0, keepdims=True)
    y = o * lax.rsqrt(ms + RMS_EPS) * g_ref[...] * (1.0 - LAM_INIT)
    o_ref[0] = y.T.astype(jnp.bfloat16)


def _diff_attn(main, vt, slopes, lam, g_col, tq, tk):
    bsz, seq, _ = main.shape
    assert tq == tk
    kern = functools.partial(_diff_attn_kernel, tq=tq, tk=tk)
    vrow0 = SB_WIDTH // LANES
    return pl.pallas_call(
        kern,
        grid=(bsz, DIFF_HEADS, seq // tq),
        in_specs=[
            pl.BlockSpec(memory_space=pltpu.SMEM),
            pl.BlockSpec(memory_space=pltpu.SMEM),
            pl.BlockSpec((1, tq, LANES), lambda b, h, i: (b, i, COLBLK_Q_DF + h)),
            pl.BlockSpec((1, seq, LANES), lambda b, h, i: (b, 0, COLBLK_K_DF + h)),
            pl.BlockSpec((1, DIFF_V_DIM, seq), lambda b, h, i: (b, vrow0 + h, 0)),
            pl.BlockSpec((SUBLANES, LANES), lambda b, h, i: (0, 0)),
            pl.BlockSpec((DIFF_V_DIM, 1), lambda b, h, i: (0, 0)),
        ],
        out_specs=pl.BlockSpec((1, tq, DIFF_V_DIM), lambda b, h, i: (b, i, h)),
        out_shape=jax.ShapeDtypeStruct((bsz, seq, DIFF_V_WIDTH), jnp.bfloat16),
        scratch_shapes=[
            pltpu.VMEM((2, DIFF_V_DIM, tq), jnp.float32),
            pltpu.VMEM((2, 1, tq), jnp.float32),
            pltpu.VMEM((2, 1, tq), jnp.float32),
            pltpu.VMEM((2, 2, tk, tq), jnp.float32),
            pltpu.VMEM((2, 2, 1, tq), jnp.float32),
            pltpu.VMEM((SUBLANES, LANES), jnp.float32),
            pltpu.VMEM((tk, tq), jnp.float32),
        ],
        compiler_params=_cparams(("arbitrary", "arbitrary", "arbitrary")),
        name="diff_attn",
    )(slopes, 1.0 / slopes, main, main, vt, lam, g_col)


def _pack_bf16_pair(a, b):
    ab = pltpu.bitcast(a.astype(jnp.bfloat16).astype(jnp.float32), jnp.uint32)
    bb = pltpu.bitcast(b.astype(jnp.bfloat16).astype(jnp.float32), jnp.uint32)
    return ab | (bb >> 16)


def _unpack_bf16_pair(w):
    hi = pltpu.bitcast(w & jnp.uint32(0xFFFF0000), jnp.float32)
    lo = pltpu.bitcast(w << 16, jnp.float32)
    return jnp.concatenate([hi, lo], axis=1)


def _merge_router_kernel(ysb_ref, ydf_ref, gates_ref, x_ref, mod_ref, wsb_ref, wdf_ref, wout_ref,
                         gpost_ref, gpre_ref, wrh_ref, wrl_ref, br_ref,
                         x1_ref, h2_ref, idx_ref, wgt_ref, rank_ref, cnt_ref, base_ref):
    first = jnp.logical_and(pl.program_id(0) == 0, pl.program_id(1) == 0)

    @pl.when(first)
    def _():
        base_ref[...] = jnp.zeros_like(base_ref)

    mod = mod_ref[0]
    subs = [slice(s * MERGE_SUB, (s + 1) * MERGE_SUB) for s in range(x_ref.shape[1] // MERGE_SUB)]
    half = D_MODEL // 2
    branch = [(jnp.dot(ysb_ref[0, rows, :], wsb_ref[...], preferred_element_type=jnp.float32),
               jnp.dot(ydf_ref[0, rows, :], wdf_ref[...], preferred_element_type=jnp.float32))
              for rows in subs]
    merged = []
    for rows, (a, b) in zip(subs, branch):
        g = gates_ref[0, rows, :].astype(jnp.float32)
        merged.append((g[:, :D_MODEL] * a + g[:, D_MODEL:] * b).astype(jnp.bfloat16))
    mixes = [jnp.dot(m, wout_ref[...], preferred_element_type=jnp.float32) for m in merged]
    h2s = []
    for rows, mix in zip(subs, mixes):
        x1 = x_ref[0, rows, :] + mod[2:3] * (_rms(mix) * gpost_ref[...])
        x1_ref[0, rows, :] = x1
        h2 = _rms(x1) * gpre_ref[...]
        h2 = h2 * (1.0 + mod[4:5]) + mod[3:4]
        h2_ref[0, rows, :] = _pack_bf16_pair(h2[:, :half], h2[:, half:])
        h2s.append(h2)
    logit_list = []
    for h2 in h2s:
        hh = h2.astype(jnp.bfloat16)
        hl = (h2 - hh.astype(jnp.float32)).astype(jnp.bfloat16)
        logit_list.append(jnp.dot(hh, wrh_ref[...], preferred_element_type=jnp.float32)
                          + jnp.dot(hh, wrl_ref[...], preferred_element_type=jnp.float32)
                          + jnp.dot(hl, wrh_ref[...], preferred_element_type=jnp.float32)
                          + br_ref[...])
    for rows, logits in zip(subs, logit_list):
        _route_rows(rows, logits, idx_ref, wgt_ref, rank_ref, cnt_ref, base_ref)


def _route_rows(rows, logits, idx_ref, wgt_ref, rank_ref, cnt_ref, base_ref):
    lane = lax.broadcasted_iota(jnp.int32, logits.shape, 1)
    lanef = lane.astype(jnp.float32)
    vals, idxs = [], []
    cur = logits
    for _ in range(TOP_K):
        mx = jnp.max(cur, axis=-1, keepdims=True)
        ix = jnp.min(jnp.where(cur == mx, lanef, float(LANES)), axis=-1, keepdims=True)
        cur = jnp.where(lanef == ix, -jnp.inf, cur)
        vals.append(mx)
        idxs.append(ix)
    es = [jnp.exp(v - vals[0]) for v in vals]
    den = es[0] + es[1] + es[2] + es[3]
    oi = jnp.zeros(logits.shape, jnp.float32)
    ow = jnp.zeros(logits.shape, jnp.float32)
    for k in range(TOP_K):
        oi = jnp.where(lane == k, idxs[k], oi)
        ow = jnp.where(lane == k, es[k] / den, ow)
    idx_ref[0, rows, :] = oi.astype(jnp.int32)
    wgt_ref[0, rows, :] = ow

    ts = logits.shape[0]
    member = jnp.zeros(logits.shape, jnp.float32)
    for k in range(TOP_K):
        member = member + (lanef == idxs[k]).astype(jnp.float32)
    rr = lax.broadcasted_iota(jnp.int32, (ts, ts), 0)
    cc = lax.broadcasted_iota(jnp.int32, (ts, ts), 1)
    lower = jnp.where(cc < rr, 1.0, 0.0).astype(jnp.bfloat16)
    before = jnp.dot(lower, member.astype(jnp.bfloat16), preferred_element_type=jnp.float32)
    base = base_ref[0:1, :]
    rank_all = before + base
    orank = jnp.zeros(logits.shape, jnp.float32)
    for k in range(TOP_K):
        rk = jnp.sum(jnp.where(lanef == idxs[k], rank_all, 0.0), axis=-1, keepdims=True)
        orank = jnp.where(lane == k, rk, orank)
    rank_ref[0, rows, :] = orank.astype(jnp.int32)
    new_base = base + jnp.sum(member, axis=0, keepdims=True)
    base_ref[...] = jnp.broadcast_to(new_base, base_ref.shape)
    cnt_ref[...] = jnp.broadcast_to(new_base, cnt_ref.shape).astype(jnp.int32)


def _merge_router(ysb, ydf, main, x, mod3, wsb, wdf, wout, gpost, gpre, wrh, wrl, br, ts):
    bsz, seq, _ = x.shape
    const = lambda b, i: (0, 0)
    return pl.pallas_call(
        _merge_router_kernel,
        grid=(bsz, seq // ts),
        in_specs=[
            pl.BlockSpec((1, ts, SB_WIDTH), lambda b, i: (b, i, 0)),
            pl.BlockSpec((1, ts, DIFF_V_WIDTH), lambda b, i: (b, i, 0)),
            pl.BlockSpec((1, ts, 2 * D_MODEL), lambda b, i: (b, i, GATE_COL0 // (2 * D_MODEL))),
            pl.BlockSpec((1, ts, D_MODEL), lambda b, i: (b, i, 0)),
            pl.BlockSpec((1, N_MOD, D_MODEL), lambda b, i: (b, 0, 0)),
            pl.BlockSpec((SB_WIDTH, D_MODEL), const),
            pl.BlockSpec((DIFF_V_WIDTH, D_MODEL), const),
            pl.BlockSpec((D_MODEL, D_MODEL), const),
            pl.BlockSpec((1, D_MODEL), const),
            pl.BlockSpec((1, D_MODEL), const),
            pl.BlockSpec((D_MODEL, LANES), const),
            pl.BlockSpec((D_MODEL, LANES), const),
            pl.BlockSpec((1, LANES), const),
        ],
        out_specs=[
            pl.BlockSpec((1, ts, D_MODEL), lambda b, i: (b, i, 0)),
            pl.BlockSpec((1, ts, D_MODEL // 2), lambda b, i: (b, i, 0)),
            pl.BlockSpec((1, ts, LANES), lambda b, i: (b, i, 0)),
            pl.BlockSpec((1, ts, LANES), lambda b, i: (b, i, 0)),
            pl.BlockSpec((1, ts, LANES), lambda b, i: (b, i, 0)),
            pl.BlockSpec((SUBLANES, LANES), const),
        ],
        out_shape=[
            jax.ShapeDtypeStruct((bsz, seq, D_MODEL), jnp.float32),
            jax.ShapeDtypeStruct((bsz, seq, D_MODEL // 2), jnp.uint32),
            jax.ShapeDtypeStruct((bsz, seq, LANES), jnp.int32),
            jax.ShapeDtypeStruct((bsz, seq, LANES), jnp.float32),
            jax.ShapeDtypeStruct((bsz, seq, LANES), jnp.int32),
            jax.ShapeDtypeStruct((SUBLANES, LANES), jnp.int32),
        ],
        scratch_shapes=[pltpu.VMEM((SUBLANES, LANES), jnp.float32)],
        compiler_params=_cparams(("arbitrary", "arbitrary")),
        name="merge_router",
    )(ysb, ydf, main, x, mod3, wsb, wdf, wout, gpost, gpre, wrh, wrl, br)


def _sc_gather_rows(table, idx):
    n = idx.shape[0]
    width = table.shape[1]
    n_workers = SC_CORES * SC_SUBCORES
    per_worker = n // n_workers
    n_chunks = per_worker // SC_GATHER_ROWS
    assert n_chunks * SC_GATHER_ROWS * n_workers == n
    mesh = plsc.VectorSubcoreMesh(core_axis_name="c", subcore_axis_name="s",
                                  num_cores=SC_CORES, num_subcores=SC_SUBCORES)

    def body(table_hbm, idx_hbm, out_hbm, idx_v, rows_v, sem):
        wid = lax.axis_index("s") * SC_CORES + lax.axis_index("c")
        base = wid * per_worker

        @pl.loop(0, n_chunks)
        def _(ci):
            off = pl.multiple_of(base + ci * SC_GATHER_ROWS, SC_GATHER_ROWS)
            pltpu.sync_copy(idx_hbm.at[pl.ds(off, SC_GATHER_ROWS)], idx_v)
            pltpu.async_copy(table_hbm.at[idx_v], rows_v, sem).wait()
            pltpu.sync_copy(rows_v, out_hbm.at[pl.ds(off, SC_GATHER_ROWS)])

    return pl.kernel(
        body,
        out_type=jax.ShapeDtypeStruct((n, width), table.dtype),
        mesh=mesh,
        scratch_types=[
            pltpu.VMEM((SC_GATHER_ROWS,), jnp.int32),
            pltpu.VMEM((SC_GATHER_ROWS, width), table.dtype),
            pltpu.SemaphoreType.DMA,
        ],
        name="sc_gather_rows",
    )(table, idx)


def _sc_scatter_rows(src, pos_kmajor, n_rows):
    n_tok, width = src.shape
    n_workers = SC_CORES * SC_SUBCORES
    per_worker = n_tok // n_workers
    n_chunks = per_worker // SC_GATHER_ROWS
    assert n_chunks * SC_GATHER_ROWS * n_workers == n_tok
    mesh = plsc.VectorSubcoreMesh(core_axis_name="c", subcore_axis_name="s",
                                  num_cores=SC_CORES, num_subcores=SC_SUBCORES)

    def body(src_hbm, idx_hbm, out_hbm, idx_v, rows_v):
        wid = lax.axis_index("s") * SC_CORES + lax.axis_index("c")
        base = wid * per_worker

        @pl.loop(0, n_chunks)
        def _(ci):
            off = pl.multiple_of(base + ci * SC_GATHER_ROWS, SC_GATHER_ROWS)
            pltpu.sync_copy(src_hbm.at[pl.ds(off, SC_GATHER_ROWS)], rows_v)
            for k in range(TOP_K):
                koff = pl.multiple_of(k * n_tok + off, SC_GATHER_ROWS)
                pltpu.sync_copy(idx_hbm.at[pl.ds(koff, SC_GATHER_ROWS)], idx_v)
                pltpu.sync_copy(rows_v, out_hbm.at[idx_v])

    return pl.kernel(
        body,
        out_type=jax.ShapeDtypeStruct((n_rows, width), src.dtype),
        mesh=mesh,
        scratch_types=[
            pltpu.VMEM((SC_GATHER_ROWS,), jnp.int32),
            pltpu.VMEM((SC_GATHER_ROWS, width), src.dtype),
        ],
        name="sc_scatter_rows",
    )(src, pos_kmajor)


def _moe_ffn_kernel(te_ref, nt_ref, x_ref, wgu_ref, bgu_ref, wd_ref, bd_ref, o_ref,
                    wgu_bf, wd_bf):
    i = pl.program_id(0)
    n_valid = nt_ref[0]

    new_expert = jnp.logical_or(i == 0, te_ref[i] != te_ref[jnp.maximum(i - 1, 0)])

    @pl.when(jnp.logical_and(i < n_valid, new_expert))
    def _():
        wgu_bf[...] = wgu_ref[0].astype(jnp.bfloat16)
        wd_bf[...] = wd_ref[0].astype(jnp.bfloat16)

    @pl.when(i < n_valid)
    def _():
        xb = _unpack_bf16_pair(x_ref[...]).astype(jnp.bfloat16)
        gu = jnp.dot(xb, wgu_bf[...], preferred_element_type=jnp.float32) + bgu_ref[0]
        gate = jnp.minimum(gu[:, :D_EXPERT], SWIGLU_LIMIT)
        up = jnp.clip(gu[:, D_EXPERT:], -SWIGLU_LIMIT, SWIGLU_LIMIT)
        act = (up + 1.0) * (gate * jax.nn.sigmoid(SWIGLU_ALPHA * gate))
        out = jnp.dot(act.astype(jnp.bfloat16), wd_bf[...],
                      preferred_element_type=jnp.float32) + bd_ref[0]
        half = D_MODEL // 2
        o_ref[...] = _pack_bf16_pair(out[:, :half], out[:, half:])

    @pl.when(i >= n_valid)
    def _():
        o_ref[...] = jnp.zeros_like(o_ref)


def _moe_ffn(tile_expert, n_valid, xg, wgu, bgu, wd, bd, tm):
    n_tiles = xg.shape[0] // tm
    grid_spec = pltpu.PrefetchScalarGridSpec(
        num_scalar_prefetch=2,
        grid=(n_tiles,),
        in_specs=[
            pl.BlockSpec((tm, D_MODEL // 2), lambda i, te, nt: (jnp.minimum(i, nt[0] - 1), 0)),
            pl.BlockSpec((1, D_MODEL, 2 * D_EXPERT), lambda i, te, nt: (te[i], 0, 0)),
            pl.BlockSpec((1, 1, 2 * D_EXPERT), lambda i, te, nt: (te[i], 0, 0)),
            pl.BlockSpec((1, D_EXPERT, D_MODEL), lambda i, te, nt: (te[i], 0, 0)),
            pl.BlockSpec((1, 1, D_MODEL), lambda i, te, nt: (te[i], 0, 0)),
        ],
        out_specs=pl.BlockSpec((tm, D_MODEL // 2), lambda i, te, nt: (i, 0)),
        scratch_shapes=[
            pltpu.VMEM((D_MODEL, 2 * D_EXPERT), jnp.bfloat16),
            pltpu.VMEM((D_EXPERT, D_MODEL), jnp.bfloat16),
        ],
    )
    return pl.pallas_call(
        _moe_ffn_kernel,
        grid_spec=grid_spec,
        out_shape=jax.ShapeDtypeStruct((n_tiles * tm, D_MODEL // 2), jnp.uint32),
        compiler_params=_cparams(("arbitrary",)),
        name="moe_ffn",
    )(tile_expert, n_valid, xg, wgu, bgu.reshape(N_EXPERTS, 1, -1), wd,
      bd.reshape(N_EXPERTS, 1, -1))


def _moe_combine_kernel(rows_ref, wgt_ref, x1_ref, mod_ref, g_ref, o_ref):
    ts = x1_ref.shape[1]
    w = wgt_ref[0]
    y = jnp.zeros(x1_ref.shape[1:], jnp.float32)
    for k in range(TOP_K):
        y = y + w[:, k:k + 1] * _unpack_bf16_pair(rows_ref[k * ts:(k + 1) * ts, :])
    mod = mod_ref[0]
    o_ref[0] = x1_ref[0] + mod[5:6] * (_rms(y) * g_ref[...])


def _moe_combine(rows, wgt, x1, mod3, g_post, ts):
    bsz, seq, _ = x1.shape
    per_b = seq // ts
    return pl.pallas_call(
        _moe_combine_kernel,
        grid=(bsz, per_b),
        in_specs=[
            pl.BlockSpec((TOP_K * ts, D_MODEL // 2), lambda b, i: (b * per_b + i, 0)),
            pl.BlockSpec((1, ts, LANES), lambda b, i: (b, i, 0)),
            pl.BlockSpec((1, ts, D_MODEL), lambda b, i: (b, i, 0)),
            pl.BlockSpec((1, N_MOD, D_MODEL), lambda b, i: (b, 0, 0)),
            pl.BlockSpec((1, D_MODEL), lambda b, i: (0, 0)),
        ],
        out_specs=pl.BlockSpec((1, ts, D_MODEL), lambda b, i: (b, i, 0)),
        out_shape=jax.ShapeDtypeStruct((bsz, seq, D_MODEL), jnp.float32),
        compiler_params=_cparams(("arbitrary", "arbitrary")),
        name="moe_combine",
    )(rows, wgt, x1, mod3, g_post)


def _routing(top_idx, rank, counts, tm, n_tiles):
    padded = ((counts + tm - 1) // tm) * tm
    pend = jnp.cumsum(padded)
    pstart = pend - padded
    onehot = top_idx[:, :, None] == jnp.arange(N_EXPERTS, dtype=jnp.int32)[None, None, :]
    pos = rank + jnp.sum(jnp.where(onehot, pstart[None, None, :], 0), axis=-1)
    n_valid = (pend[-1] // tm).astype(jnp.int32)
    tile_row0 = jnp.arange(n_tiles, dtype=jnp.int32) * tm
    tile_expert = jnp.minimum(
        jnp.sum((tile_row0[:, None] >= pend[None, :]).astype(jnp.int32), axis=1), N_EXPERTS - 1)
    last_oh = jnp.arange(n_tiles, dtype=jnp.int32) == jnp.maximum(n_valid - 1, 0)
    last_expert = jnp.sum(jnp.where(last_oh, tile_expert, 0))
    tile_expert = jnp.where(jnp.arange(n_tiles, dtype=jnp.int32) < n_valid, tile_expert, last_expert)
    return pos.astype(jnp.int32), tile_expert.astype(jnp.int32), n_valid.reshape(1)


def _alibi_slopes(n_heads):
    return 2.0 ** (-8.0 * jnp.arange(1, n_heads + 1, dtype=jnp.float32) / n_heads)


def _layer(x, c, w_mod, b_mod, g_pre_mix, g_post_mix, w_in, lamv, g_subln, w_branch_sb,
           w_branch_diff, w_out, g_pre_ffn, g_post_ffn, w_router, b_router, w_gate_up,
           b_gate_up, w_down, b_down, tiles):
    ts_in, tq, ts_merge, tm, ts_comb = tiles
    bsz, seq, d = x.shape
    n_tok = bsz * seq
    bf = jnp.bfloat16

    mod, lam = _mod_proj(c, w_mod, b_mod, lamv)
    mod3 = mod.reshape(bsz, N_MOD, d)

    o_vsb = 2 * SB_WIDTH
    o_qdf = 3 * SB_WIDTH
    o_vdf = o_qdf + 2 * DIFF_QK_WIDTH
    o_g = o_vdf + DIFF_V_WIDTH
    w_main = jnp.concatenate([w_in[:, :o_vsb], w_in[:, o_qdf:o_vdf], w_in[:, o_g:]], axis=1).astype(bf)
    w_vt = jnp.concatenate([w_in[:, o_vsb:o_qdf], w_in[:, o_vdf:o_g]], axis=1).T.astype(bf)

    main, vt = _in_proj(x, mod3, g_pre_mix.reshape(1, d), w_main, w_vt, ts_in, tq)
    y_sb = _sb_attn(main, vt, tq, tq)
    y_df = _diff_attn(main, vt, _alibi_slopes(DIFF_HEADS), lam,
                      g_subln.reshape(DIFF_V_DIM, 1), tq, tq)

    wr = jnp.zeros((d, LANES), jnp.float32).at[:, :N_EXPERTS].set(w_router)
    wrh = wr.astype(bf)
    wrl = (wr - wrh.astype(jnp.float32)).astype(bf)
    br = jnp.full((1, LANES), NEG_BIG, jnp.float32).at[0, :N_EXPERTS].set(b_router)
    x1, h2p, top_idx, top_w, rank, counts = _merge_router(
        y_sb, y_df, main, x, mod3, w_branch_sb.astype(bf), w_branch_diff.astype(bf),
        w_out.astype(bf), g_post_mix.reshape(1, d), g_pre_ffn.reshape(1, d), wrh, wrl, br, ts_merge)

    n_tiles = (n_tok * TOP_K) // tm + N_EXPERTS
    pos, tile_expert, n_valid = _routing(
        top_idx.reshape(n_tok, LANES)[:, :TOP_K], rank.reshape(n_tok, LANES)[:, :TOP_K],
        counts[0, :N_EXPERTS], tm, n_tiles)
    xg = _sc_scatter_rows(h2p.reshape(n_tok, d // 2), pos.T.reshape(TOP_K * n_tok), n_tiles * tm)
    rows = _moe_ffn(tile_expert, n_valid, xg, w_gate_up, b_gate_up, w_down, b_down, tm)
    pos_steps = pos.reshape(n_tok // ts_comb, ts_comb, TOP_K).swapaxes(1, 2).reshape(n_tok * TOP_K)
    tok_rows = _sc_gather_rows(rows, pos_steps)
    return _moe_combine(tok_rows, top_w, x1, mod3, g_post_ffn.reshape(1, d), ts_comb)


def kernel(x, c, w_mod, b_mod, g_pre_mix, g_post_mix, w_in, lambda_q1, lambda_k1, lambda_q2,
           lambda_k2, g_subln, w_branch_sb, w_branch_diff, w_out, g_pre_ffn, g_post_ffn,
           w_router, b_router, w_gate_up, b_gate_up, w_down, b_down):
    depth = w_mod.shape[0]
    assert depth == 1, "LAM_INIT is the layer-0 value"
    for l in range(depth):
        lamv = jnp.stack([lambda_q1[l], lambda_k1[l], lambda_q2[l], lambda_k2[l]])
        x = _layer(x, c, w_mod[l], b_mod[l], g_pre_mix[l], g_post_mix[l], w_in[l], lamv,
                   g_subln[l], w_branch_sb[l], w_branch_diff[l], w_out[l], g_pre_ffn[l],
                   g_post_ffn[l], w_router[l], b_router[l], w_gate_up[l], b_gate_up[l],
                   w_down[l], b_down[l], _Tiles())
    return x
```

```python
import functools
import math
from typing import NamedTuple

import jax
import jax.numpy as jnp
from jax import lax
from jax.experimental import pallas as pl
from jax.experimental.pallas import tpu as pltpu
from jax.experimental.pallas import tpu_sc as plsc

D_MODEL = 1024
SB_HEADS = 8
SB_HEAD_DIM = 64
SB_WIDTH = SB_HEADS * SB_HEAD_DIM
DIFF_HEADS = 4
DIFF_HEAD_DIM = 64
DIFF_V_DIM = 2 * DIFF_HEAD_DIM
DIFF_QK_WIDTH = DIFF_HEADS * 2 * DIFF_HEAD_DIM
DIFF_V_WIDTH = DIFF_HEADS * DIFF_V_DIM
N_EXPERTS = 32
TOP_K = 4
D_EXPERT = D_MODEL
SWIGLU_LIMIT = 7.0
SWIGLU_ALPHA = 1.702
RMS_EPS = 1e-6
N_MOD = 6
LAM_INIT = 0.8 - 0.6 * math.exp(-0.3 * 0)

LANES = 128
SUBLANES = 8
NEG_BIG = -1e30
EXP_ZERO_MARGIN = 110.0
NORM_SLACK = 1.01
SC_CORES = 2
SC_SUBCORES = 16
SC_GATHER_ROWS = 128
MERGE_SUB = 256

MAIN_WIDTH = 2 * SB_WIDTH + 2 * DIFF_QK_WIDTH + 2 * D_MODEL
VT_ROWS = SB_WIDTH + DIFF_V_WIDTH
COLBLK_K_SB = SB_WIDTH // LANES
COLBLK_Q_DF = 2 * SB_WIDTH // LANES
COLBLK_K_DF = COLBLK_Q_DF + DIFF_QK_WIDTH // LANES
GATE_COL0 = 2 * SB_WIDTH + 2 * DIFF_QK_WIDTH

V7X_VMEM_BYTES = 64 * 1024 * 1024
VMEM_LIMIT = V7X_VMEM_BYTES - 8 * 1024 * 1024


class _Tiles(NamedTuple):
    in_proj: int = 512
    attn: int = 256
    merge: int = 512
    expert: int = 512
    combine: int = 256


def _cparams(sem, vmem=VMEM_LIMIT):
    return pltpu.CompilerParams(dimension_semantics=sem, vmem_limit_bytes=vmem)


def _rms(x):
    return x * lax.rsqrt(jnp.mean(x * x, axis=-1, keepdims=True) + RMS_EPS)


def _mod_kernel(c_ref, w_ref, b_ref, lamv_ref, mod_ref, lam_ref):
    c = c_ref[...]
    ca = c * jax.nn.sigmoid(c)
    mod_ref[...] = jnp.dot(ca, w_ref[...], preferred_element_type=jnp.float32,
                           precision=lax.Precision.HIGHEST) + b_ref[...]
    lv = lamv_ref[...]
    s1 = jnp.sum(lv[0:1] * lv[1:2], axis=-1, keepdims=True)
    s2 = jnp.sum(lv[2:3] * lv[3:4], axis=-1, keepdims=True)
    lam = jnp.exp(s1) - jnp.exp(s2) + LAM_INIT
    lam_ref[...] = jnp.broadcast_to(lam, lam_ref.shape)


def _mod_proj(c, w_mod, b_mod, lamv):
    bsz = c.shape[0]
    tn = 1536
    n = w_mod.shape[1]
    return pl.pallas_call(
        _mod_kernel,
        grid=(n // tn,),
        in_specs=[
            pl.BlockSpec((bsz, D_MODEL), lambda j: (0, 0)),
            pl.BlockSpec((D_MODEL, tn), lambda j: (0, j)),
            pl.BlockSpec((1, tn), lambda j: (0, j)),
            pl.BlockSpec((4, DIFF_HEAD_DIM), lambda j: (0, 0)),
        ],
        out_specs=[
            pl.BlockSpec((bsz, tn), lambda j: (0, j)),
            pl.BlockSpec((SUBLANES, LANES), lambda j: (0, 0)),
        ],
        out_shape=[
            jax.ShapeDtypeStruct((bsz, n), jnp.float32),
            jax.ShapeDtypeStruct((SUBLANES, LANES), jnp.float32),
        ],
        compiler_params=_cparams(("arbitrary",)),
        name="mod_proj",
    )(c, w_mod, b_mod.reshape(1, n), lamv)


IN_CHUNK = 1024


def _in_proj_kernel(x_ref, mod_ref, g_ref, wm_ref, wvt_ref, main_ref, vt_ref, h_scr, *, tk):
    x = x_ref[0]
    mod = mod_ref[0]
    h = _rms(x) * g_ref[...]
    h = h * (1.0 + mod[1:2]) + mod[0:1]
    hb = h.astype(jnp.bfloat16)
    groups = tk // SUBLANES
    cols = []
    for ct in range(D_MODEL // LANES):
        h_scr[ct] = h[:, ct * LANES:(ct + 1) * LANES]
        pieces = []
        for blk in range(h.shape[0] // tk):
            for g in range(groups):
                pieces.append(h_scr[ct, pl.ds(blk * tk + g, SUBLANES, stride=groups), :])
        cols.append(jnp.concatenate(pieces, axis=0))
    hpb = jnp.concatenate(cols, axis=1).astype(jnp.bfloat16)

    half = IN_CHUNK // 2
    for ci in range(MAIN_WIDTH // IN_CHUNK):
        c0 = ci * IN_CHUNK
        if c0 == 0:
            q = jnp.dot(hb, wm_ref[:, :half], preferred_element_type=jnp.float32)
            k = jnp.dot(hpb, wm_ref[:, half:IN_CHUNK], preferred_element_type=jnp.float32)
            main_ref[0, :, :half] = (q * 0.0625).astype(jnp.bfloat16)
            main_ref[0, :, half:IN_CHUNK] = k.astype(jnp.bfloat16)
            continue
        p = jnp.dot(hb, wm_ref[:, c0:c0 + IN_CHUNK], preferred_element_type=jnp.float32)
        if c0 < GATE_COL0:
            main_ref[0, :, c0:c0 + half] = (p[:, :half] * 0.125).astype(jnp.bfloat16)
            main_ref[0, :, c0 + half:c0 + IN_CHUNK] = p[:, half:].astype(jnp.bfloat16)
        else:
            main_ref[0, :, c0:c0 + IN_CHUNK] = jax.nn.sigmoid(p).astype(jnp.bfloat16)
    nt = (((1,), (1,)), ((), ()))
    vt_sb = lax.dot_general(wvt_ref[:SB_WIDTH, :], hpb, nt, preferred_element_type=jnp.float32)
    vt_df = lax.dot_general(wvt_ref[SB_WIDTH:, :], hb, nt, preferred_element_type=jnp.float32)
    vt_ref[0, :SB_WIDTH, :] = vt_sb.astype(jnp.bfloat16)
    vt_ref[0, SB_WIDTH:, :] = vt_df.astype(jnp.bfloat16)


def _in_proj(x, mod3, g_pre, w_main, w_vt, ts, tk):
    bsz, seq, _ = x.shape
    assert ts % tk == 0
    return pl.pallas_call(
        functools.partial(_in_proj_kernel, tk=tk),
        grid=(bsz, seq // ts),
        in_specs=[
            pl.BlockSpec((1, ts, D_MODEL), lambda b, i: (b, i, 0)),
            pl.BlockSpec((1, N_MOD, D_MODEL), lambda b, i: (b, 0, 0)),
            pl.BlockSpec((1, D_MODEL), lambda b, i: (0, 0)),
            pl.BlockSpec((D_MODEL, MAIN_WIDTH), lambda b, i: (0, 0)),
            pl.BlockSpec((VT_ROWS, D_MODEL), lambda b, i: (0, 0)),
        ],
        out_specs=[
            pl.BlockSpec((1, ts, MAIN_WIDTH), lambda b, i: (b, i, 0)),
            pl.BlockSpec((1, VT_ROWS, ts), lambda b, i: (b, 0, i)),
        ],
        out_shape=[
            jax.ShapeDtypeStruct((bsz, seq, MAIN_WIDTH), jnp.bfloat16),
            jax.ShapeDtypeStruct((bsz, VT_ROWS, seq), jnp.bfloat16),
        ],
        scratch_shapes=[pltpu.VMEM((D_MODEL // LANES, ts, LANES), jnp.float32)],
        compiler_params=_cparams(("arbitrary", "arbitrary")),
        name="in_proj",
    )(x, mod3, g_pre, w_main, w_vt)


def _suffix_excl_prod8(tot):
    sub = lax.broadcasted_iota(jnp.int32, tot.shape, 0)
    x = jnp.where(sub < SUBLANES - 1, pltpu.roll(tot, SUBLANES - 1, 0), 1.0)
    for sh in (1, 2, 4):
        x = x * jnp.where(sub + sh < SUBLANES, pltpu.roll(x, SUBLANES - sh, 0), 1.0)
    return x


def _sb_scores(k_ref, q_heads, s_ref, slot, j, tk):
    kb = k_ref[0, pl.ds(pl.multiple_of(j * tk, tk), tk), :]
    for h in range(2):
        s_ref[slot, h] = lax.dot_general(kb, q_heads[h], (((1,), (1,)), ((), ())),
                                         preferred_element_type=jnp.float32)


def _sb_weights(zt, c8, ok, groups):
    tq = zt.shape[1]
    r = 0.5 - 0.5 * jnp.tanh(zt)
    if ok is not None:
        r = jnp.where(ok, r, 1.0)
    rg = [r[g * SUBLANES:(g + 1) * SUBLANES, :] for g in range(groups)]
    tot = rg[0]
    for g in range(1, groups):
        tot = tot * rg[g]
    p = c8 * _suffix_excl_prod8(tot)
    pieces = [None] * groups
    for g in range(groups - 1, -1, -1):
        pn = p * rg[g]
        pieces[g] = p - pn
        p = pn
    a = jnp.concatenate(pieces, axis=0).astype(jnp.bfloat16)
    return a, jnp.broadcast_to(p[0:1, :], (SUBLANES, tq))


def _sb_attn_kernel(q_ref, k_ref, v_ref, o_ref, acc_ref, c_ref, s_ref, ok_ref, *, tq, tk):
    i = pl.program_id(2)
    groups = tk // SUBLANES
    q2 = q_ref[0]
    lane = lax.broadcasted_iota(jnp.int32, q2.shape, 1)
    zero = jnp.zeros_like(q2)
    q_heads = (jnp.where(lane < SB_HEAD_DIM, q2, zero), jnp.where(lane < SB_HEAD_DIM, zero, q2))

    def step(j, slot, masked):
        _sb_scores(k_ref, q_heads, s_ref, 1 - slot, jnp.maximum(j - 1, 0), tk)
        ok = (ok_ref[...] > 0.5) if masked else None
        off = pl.multiple_of(j * tk, tk)
        ws = []
        for h in range(2):
            a, c_new = _sb_weights(s_ref[slot, h], c_ref[h], ok, groups)
            c_ref[h] = c_new
            ws.append(a)
        for h in range(2):
            vt_h = v_ref[0, h * SB_HEAD_DIM:(h + 1) * SB_HEAD_DIM, pl.ds(off, tk)]
            acc_ref[h] += jnp.dot(vt_h, ws[h], preferred_element_type=jnp.float32)

    @pl.when(i == 0)
    def _():
        row = lax.broadcasted_iota(jnp.int32, (tk, tq), 0)
        col = lax.broadcasted_iota(jnp.int32, (tk, tq), 1)
        ok_ref[...] = jnp.where((row % SUBLANES) * groups + row // SUBLANES < col, 1.0, 0.0)

    acc_ref[...] = jnp.zeros_like(acc_ref)
    c_ref[...] = jnp.ones_like(c_ref)
    _sb_scores(k_ref, q_heads, s_ref, 0, i, tk)
    step(i, 0, True)
    c_ref[...] = c_ref[...] * jnp.where(i >= 1, 1.0, 0.0)
    step(jnp.maximum(i - 1, 0), 1, False)

    def stick_left():
        return jnp.max(c_ref[...]) > 0.0

    n_rest = jnp.maximum(i - 1, 0)

    def more(state):
        m, go = state
        return jnp.logical_and(m < n_rest // 2, go)

    def pair(state):
        m, _ = state
        j = i - 2 - 2 * m
        step(j, 0, False)
        go_on = stick_left()

        @pl.when(go_on)
        def _():
            step(j - 1, 1, False)

        return m + 1, jnp.logical_and(go_on, stick_left())

    m_done, go = lax.while_loop(more, pair, (jnp.int32(0), stick_left()))

    @pl.when(jnp.logical_and(jnp.logical_and(n_rest % 2 == 1, m_done == n_rest // 2), go))
    def _():
        step(0, 0, False)

    ot = jnp.concatenate([acc_ref[0], acc_ref[1]], axis=0)
    o_ref[0] = ot.astype(jnp.bfloat16)


def _sb_attn(main, vt, tq, tk):
    bsz, seq, _ = main.shape
    assert tq == tk
    kern = functools.partial(_sb_attn_kernel, tq=tq, tk=tk)
    return pl.pallas_call(
        kern,
        grid=(bsz, SB_WIDTH // LANES, seq // tq),
        in_specs=[
            pl.BlockSpec((1, tq, LANES), lambda b, p, i: (b, i, p)),
            pl.BlockSpec((1, seq, LANES), lambda b, p, i: (b, 0, COLBLK_K_SB + p)),
            pl.BlockSpec((1, LANES, seq), lambda b, p, i: (b, p, 0)),
        ],
        out_specs=pl.BlockSpec((1, LANES, tq), lambda b, p, i: (b, p, i)),
        out_shape=jax.ShapeDtypeStruct((bsz, SB_WIDTH, seq), jnp.bfloat16),
        scratch_shapes=[
            pltpu.VMEM((2, SB_HEAD_DIM, tq), jnp.float32),
            pltpu.VMEM((2, SUBLANES, tq), jnp.float32),
            pltpu.VMEM((2, 2, tk, tq), jnp.float32),
            pltpu.VMEM((tk, tq), jnp.float32),
        ],
        compiler_params=_cparams(("arbitrary", "arbitrary", "arbitrary")),
        name="sb_attn",
    )(main, main, vt)


def _diff_attn_kernel(slopes_ref, inv_slopes_ref, q_ref, k_ref, v_ref, lam_ref, g_ref, o_ref,
                      acc_ref, m_ref, l_ref, s_ref, smax_ref, kn_ref, bias_ref, *, tq, tk):
    hd = pl.program_id(1)
    i = pl.program_id(2)
    slope = slopes_ref[hd]

    @pl.when(i == 0)
    def _():
        kf = k_ref[0].astype(jnp.float32)
        kn2 = jnp.max(jnp.sum(kf * kf, axis=-1, keepdims=True), axis=0, keepdims=True)
        kn_ref[...] = jnp.broadcast_to(kn2, kn_ref.shape)
        row0 = lax.broadcasted_iota(jnp.int32, (tk, tq), 0)
        col0 = lax.broadcasted_iota(jnp.int32, (tk, tq), 1)
        bias_ref[...] = slope * (row0 - col0).astype(jnp.float32)

    q2 = q_ref[0]
    lane = lax.broadcasted_iota(jnp.int32, q2.shape, 1)
    zero = jnp.zeros_like(q2)
    q_maps = (jnp.where(lane < DIFF_HEAD_DIM, q2, zero), jnp.where(lane < DIFF_HEAD_DIM, zero, q2))

    def scores(slot, j, masked, dead=None):
        kb = k_ref[0, pl.ds(pl.multiple_of(j * tk, tk), tk), :]
        for m in range(2):
            s = lax.dot_general(kb, q_maps[m], (((1,), (1,)), ((), ())),
                                preferred_element_type=jnp.float32) + bias_ref[...]
            if masked:
                row = lax.broadcasted_iota(jnp.int32, (tk, tq), 0)
                col = lax.broadcasted_iota(jnp.int32, (tk, tq), 1)
                s = jnp.where(row <= col, s, NEG_BIG)
            if dead is not None:
                s = jnp.where(dead, NEG_BIG, s)
            s_ref[slot, m] = s
            smax_ref[slot, m] = jnp.max(s, axis=0, keepdims=True)

    def step(j, slot, next_dead=None):
        scores(1 - slot, jnp.maximum(j - 1, 0), False, next_dead)
        off = pl.multiple_of(j * tk, tk)
        vtb = v_ref[0, :, pl.ds(off, tk)]
        cb = slope * ((j - i) * tk).astype(jnp.float32)
        ps, alphas = [], []
        for m in range(2):
            s = s_ref[slot, m]
            m_old = m_ref[m]
            m_new = jnp.maximum(m_old, smax_ref[slot, m] + cb)
            alpha = jnp.exp(m_old - m_new)
            p = jnp.exp(s - (m_new - cb))
            l_ref[m] = alpha * l_ref[m] + jnp.sum(p, axis=0, keepdims=True)
            m_ref[m] = m_new
            ps.append(p.astype(jnp.bfloat16))
            alphas.append(alpha)
        for m in range(2):
            acc_ref[m] = alphas[m] * acc_ref[m] + jnp.dot(
                vtb, ps[m], preferred_element_type=jnp.float32)

    acc_ref[...] = jnp.zeros_like(acc_ref)
    m_ref[...] = jnp.full_like(m_ref, NEG_BIG)
    l_ref[...] = jnp.zeros_like(l_ref)
    scores(0, i, True)
    step(i, 0, next_dead=(i == 0))
    step(jnp.maximum(i - 1, 0), 1)

    qf = q2.astype(jnp.float32)
    qn2 = jnp.max(jnp.sum(qf * qf, axis=-1, keepdims=True), axis=0, keepdims=True)
    zabs = jnp.sqrt(qn2 * kn_ref[0:1, 0:1]) * NORM_SLACK
    m_lo = jnp.min(jnp.minimum(m_ref[0], m_ref[1]), axis=1, keepdims=True)
    reach = (EXP_ZERO_MARGIN + zabs - m_lo) * inv_slopes_ref[hd]
    n_need = jnp.floor(jnp.clip((reach - 1.0) * (1.0 / tk), -1.0, 1e6)) + 1.0
    n_back = jnp.maximum(jnp.minimum(i, jnp.max(n_need).astype(jnp.int32)) - 1, 0)

    def quad(n, carry):
        j = i - 2 - 4 * n
        step(j, 0)
        step(j - 1, 1)
        step(j - 2, 0)
        step(j - 3, 1)
        return carry

    n_quads = n_back // 4
    lax.fori_loop(0, n_quads, quad, 0)
    rest = n_back - 4 * n_quads

    @pl.when(rest >= 2)
    def _():
        j = i - 2 - 4 * n_quads
        step(j, 0)
        step(j - 1, 1)

    @pl.when(rest % 2 == 1)
    def _():
        step(i - 1 - n_back, 0)

    lam = lam_ref[0:1, 0:1]
    o = acc_ref[0] / l_ref[0] - lam * (acc_ref[1] / l_ref[1])
    ms = jnp.mean(o * o, axis=0, keepdims=True)
    y = o * lax.rsqrt(ms + RMS_EPS) * g_ref[...] * (1.0 - LAM_INIT)
    o_ref[0] = y.astype(jnp.bfloat16)


def _diff_attn(main, vt, slopes, lam, g_col, tq, tk):
    bsz, seq, _ = main.shape
    assert tq == tk
    kern = functools.partial(_diff_attn_kernel, tq=tq, tk=tk)
    vrow0 = SB_WIDTH // LANES
    return pl.pallas_call(
        kern,
        grid=(bsz, DIFF_HEADS, seq // tq),
        in_specs=[
            pl.BlockSpec(memory_space=pltpu.SMEM),
            pl.BlockSpec(memory_space=pltpu.SMEM),
            pl.BlockSpec((1, tq, LANES), lambda b, h, i: (b, i, COLBLK_Q_DF + h)),
            pl.BlockSpec((1, seq, LANES), lambda b, h, i: (b, 0, COLBLK_K_DF + h)),
            pl.BlockSpec((1, DIFF_V_DIM, seq), lambda b, h, i: (b, vrow0 + h, 0)),
            pl.BlockSpec((SUBLANES, LANES), lambda b, h, i: (0, 0)),
            pl.BlockSpec((DIFF_V_DIM, 1), lambda b, h, i: (0, 0)),
        ],
        out_specs=pl.BlockSpec((1, DIFF_V_DIM, tq), lambda b, h, i: (b, h, i)),
        out_shape=jax.ShapeDtypeStruct((bsz, DIFF_V_WIDTH, seq), jnp.bfloat16),
        scratch_shapes=[
            pltpu.VMEM((2, DIFF_V_DIM, tq), jnp.float32),
            pltpu.VMEM((2, 1, tq), jnp.float32),
            pltpu.VMEM((2, 1, tq), jnp.float32),
            pltpu.VMEM((2, 2, tk, tq), jnp.float32),
            pltpu.VMEM((2, 2, 1, tq), jnp.float32),
            pltpu.VMEM((SUBLANES, LANES), jnp.float32),
            pltpu.VMEM((tk, tq), jnp.float32),
        ],
        compiler_params=_cparams(("arbitrary", "arbitrary", "arbitrary")),
        name="diff_attn",
    )(slopes, 1.0 / slopes, main, main, vt, lam, g_col)


def _pack_bf16_pair(a, b):
    ab = pltpu.bitcast(a.astype(jnp.bfloat16).astype(jnp.float32), jnp.uint32)
    bb = pltpu.bitcast(b.astype(jnp.bfloat16).astype(jnp.float32), jnp.uint32)
    return ab | (bb >> 16)


def _unpack_bf16_pair(w):
    hi = pltpu.bitcast(w & jnp.uint32(0xFFFF0000), jnp.float32)
    lo = pltpu.bitcast(w << 16, jnp.float32)
    return jnp.concatenate([hi, lo], axis=1)


def _merge_router_kernel(ysb_ref, ydf_ref, gates_ref, x_ref, mod_ref, wsb_ref, wdf_ref, wout_ref,
                         gpost_ref, gpre_ref, wrh_ref, wrl_ref, br_ref,
                         x1_ref, h2_ref, idx_ref, wgt_ref, rank_ref, cnt_ref, base_ref):
    first = jnp.logical_and(pl.program_id(0) == 0, pl.program_id(1) == 0)

    @pl.when(first)
    def _():
        base_ref[...] = jnp.zeros_like(base_ref)

    mod = mod_ref[0]
    subs = [slice(s * MERGE_SUB, (s + 1) * MERGE_SUB) for s in range(x_ref.shape[1] // MERGE_SUB)]
    half = D_MODEL // 2
    branch = [(jnp.dot(ysb_ref[0, :, rows].T, wsb_ref[...], preferred_element_type=jnp.float32),
               jnp.dot(ydf_ref[0, :, rows].T, wdf_ref[...], preferred_element_type=jnp.float32))
              for rows in subs]
    merged = []
    for rows, (a, b) in zip(subs, branch):
        g = gates_ref[0, rows, :].astype(jnp.float32)
        merged.append((g[:, :D_MODEL] * a + g[:, D_MODEL:] * b).astype(jnp.bfloat16))
    mixes = [jnp.dot(m, wout_ref[...], preferred_element_type=jnp.float32) for m in merged]
    h2s = []
    for rows, mix in zip(subs, mixes):
        x1 = x_ref[0, rows, :] + mod[2:3] * (_rms(mix) * gpost_ref[...])
        x1_ref[0, rows, :] = x1
        h2 = _rms(x1) * gpre_ref[...]
        h2 = h2 * (1.0 + mod[4:5]) + mod[3:4]
        h2_ref[0, rows, :] = _pack_bf16_pair(h2[:, :half], h2[:, half:])
        h2s.append(h2)
    logit_list = []
    for h2 in h2s:
        hh = h2.astype(jnp.bfloat16)
        hl = (h2 - hh.astype(jnp.float32)).astype(jnp.bfloat16)
        logit_list.append(jnp.dot(hh, wrh_ref[...], preferred_element_type=jnp.float32)
                          + jnp.dot(hh, wrl_ref[...], preferred_element_type=jnp.float32)
                          + jnp.dot(hl, wrh_ref[...], preferred_element_type=jnp.float32)
                          + br_ref[...])
    for rows, logits in zip(subs, logit_list):
        _route_rows(rows, logits, idx_ref, wgt_ref, rank_ref, cnt_ref, base_ref)


def _route_rows(rows, logits, idx_ref, wgt_ref, rank_ref, cnt_ref, base_ref):
    lane = lax.broadcasted_iota(jnp.int32, logits.shape, 1)
    lanef = lane.astype(jnp.float32)
    vals, idxs = [], []
    cur = logits
    for _ in range(TOP_K):
        mx = jnp.max(cur, axis=-1, keepdims=True)
        ix = jnp.min(jnp.where(cur == mx, lanef, float(LANES)), axis=-1, keepdims=True)
        cur = jnp.where(lanef == ix, -jnp.inf, cur)
        vals.append(mx)
        idxs.append(ix)
    es = [jnp.exp(v - vals[0]) for v in vals]
    den = es[0] + es[1] + es[2] + es[3]
    oi = jnp.zeros(logits.shape, jnp.float32)
    ow = jnp.zeros(logits.shape, jnp.float32)
    for k in range(TOP_K):
        oi = jnp.where(lane == k, idxs[k], oi)
        ow = jnp.where(lane == k, es[k] / den, ow)
    idx_ref[0, rows, :] = oi.astype(jnp.int32)
    wgt_ref[0, rows, :] = ow

    ts = logits.shape[0]
    member = jnp.zeros(logits.shape, jnp.float32)
    for k in range(TOP_K):
        member = member + (lanef == idxs[k]).astype(jnp.float32)
    rr = lax.broadcasted_iota(jnp.int32, (ts, ts), 0)
    cc = lax.broadcasted_iota(jnp.int32, (ts, ts), 1)
    lower = jnp.where(cc < rr, 1.0, 0.0).astype(jnp.bfloat16)
    before = jnp.dot(lower, member.astype(jnp.bfloat16), preferred_element_type=jnp.float32)
    base = base_ref[0:1, :]
    rank_all = before + base
    orank = jnp.zeros(logits.shape, jnp.float32)
    for k in range(TOP_K):
        rk = jnp.sum(jnp.where(lanef == idxs[k], rank_all, 0.0), axis=-1, keepdims=True)
        orank = jnp.where(lane == k, rk, orank)
    rank_ref[0, rows, :] = orank.astype(jnp.int32)
    new_base = base + jnp.sum(member, axis=0, keepdims=True)
    base_ref[...] = jnp.broadcast_to(new_base, base_ref.shape)
    cnt_ref[...] = jnp.broadcast_to(new_base, cnt_ref.shape).astype(jnp.int32)


def _merge_router(ysb, ydf, main, x, mod3, wsb, wdf, wout, gpost, gpre, wrh, wrl, br, ts):
    bsz, seq, _ = x.shape
    const = lambda b, i: (0, 0)
    return pl.pallas_call(
        _merge_router_kernel,
        grid=(bsz, seq // ts),
        in_specs=[
            pl.BlockSpec((1, SB_WIDTH, ts), lambda b, i: (b, 0, i)),
            pl.BlockSpec((1, DIFF_V_WIDTH, ts), lambda b, i: (b, 0, i)),
            pl.BlockSpec((1, ts, 2 * D_MODEL), lambda b, i: (b, i, GATE_COL0 // (2 * D_MODEL))),
            pl.BlockSpec((1, ts, D_MODEL), lambda b, i: (b, i, 0)),
            pl.BlockSpec((1, N_MOD, D_MODEL), lambda b, i: (b, 0, 0)),
            pl.BlockSpec((SB_WIDTH, D_MODEL), const),
            pl.BlockSpec((DIFF_V_WIDTH, D_MODEL), const),
            pl.BlockSpec((D_MODEL, D_MODEL), const),
            pl.BlockSpec((1, D_MODEL), const),
            pl.BlockSpec((1, D_MODEL), const),
            pl.BlockSpec((D_MODEL, LANES), const),
            pl.BlockSpec((D_MODEL, LANES), const),
            pl.BlockSpec((1, LANES), const),
        ],
        out_specs=[
            pl.BlockSpec((1, ts, D_MODEL), lambda b, i: (b, i, 0)),
            pl.BlockSpec((1, ts, D_MODEL // 2), lambda b, i: (b, i, 0)),
            pl.BlockSpec((1, ts, LANES), lambda b, i: (b, i, 0)),
            pl.BlockSpec((1, ts, LANES), lambda b, i: (b, i, 0)),
            pl.BlockSpec((1, ts, LANES), lambda b, i: (b, i, 0)),
            pl.BlockSpec((SUBLANES, LANES), const),
        ],
        out_shape=[
            jax.ShapeDtypeStruct((bsz, seq, D_MODEL), jnp.float32),
            jax.ShapeDtypeStruct((bsz, seq, D_MODEL // 2), jnp.uint32),
            jax.ShapeDtypeStruct((bsz, seq, LANES), jnp.int32),
            jax.ShapeDtypeStruct((bsz, seq, LANES), jnp.float32),
            jax.ShapeDtypeStruct((bsz, seq, LANES), jnp.int32),
            jax.ShapeDtypeStruct((SUBLANES, LANES), jnp.int32),
        ],
        scratch_shapes=[pltpu.VMEM((SUBLANES, LANES), jnp.float32)],
        compiler_params=_cparams(("arbitrary", "arbitrary")),
        name="merge_router",
    )(ysb, ydf, main, x, mod3, wsb, wdf, wout, gpost, gpre, wrh, wrl, br)


def _sc_gather_rows(table, idx):
    n = idx.shape[0]
    width = table.shape[1]
    n_workers = SC_CORES * SC_SUBCORES
    per_worker = n // n_workers
    n_chunks = per_worker // SC_GATHER_ROWS
    assert n_chunks * SC_GATHER_ROWS * n_workers == n
    mesh = plsc.VectorSubcoreMesh(core_axis_name="c", subcore_axis_name="s",
                                  num_cores=SC_CORES, num_subcores=SC_SUBCORES)

    def body(table_hbm, idx_hbm, out_hbm, idx_v, rows_v, sem):
        wid = lax.axis_index("s") * SC_CORES + lax.axis_index("c")
        base = wid * per_worker

        @pl.loop(0, n_chunks)
        def _(ci):
            off = pl.multiple_of(base + ci * SC_GATHER_ROWS, SC_GATHER_ROWS)
            pltpu.sync_copy(idx_hbm.at[pl.ds(off, SC_GATHER_ROWS)], idx_v)
            pltpu.async_copy(table_hbm.at[idx_v], rows_v, sem).wait()
            pltpu.sync_copy(rows_v, out_hbm.at[pl.ds(off, SC_GATHER_ROWS)])

    return pl.kernel(
        body,
        out_type=jax.ShapeDtypeStruct((n, width), table.dtype),
        mesh=mesh,
        scratch_types=[
            pltpu.VMEM((SC_GATHER_ROWS,), jnp.int32),
            pltpu.VMEM((SC_GATHER_ROWS, width), table.dtype),
            pltpu.SemaphoreType.DMA,
        ],
        name="sc_gather_rows",
    )(table, idx)


def _sc_scatter_rows(src, pos_kmajor, n_rows):
    n_tok, width = src.shape
    n_workers = SC_CORES * SC_SUBCORES
    per_worker = n_tok // n_workers
    n_chunks = per_worker // SC_GATHER_ROWS
    assert n_chunks * SC_GATHER_ROWS * n_workers == n_tok
    mesh = plsc.VectorSubcoreMesh(core_axis_name="c", subcore_axis_name="s",
                                  num_cores=SC_CORES, num_subcores=SC_SUBCORES)

    def body(src_hbm, idx_hbm, out_hbm, idx_v, rows_v):
        wid = lax.axis_index("s") * SC_CORES + lax.axis_index("c")
        base = wid * per_worker

        @pl.loop(0, n_chunks)
        def _(ci):
            off = pl.multiple_of(base + ci * SC_GATHER_ROWS, SC_GATHER_ROWS)
            pltpu.sync_copy(src_hbm.at[pl.ds(off, SC_GATHER_ROWS)], rows_v)
            for k in range(TOP_K):
                koff = pl.multiple_of(k * n_tok + off, SC_GATHER_ROWS)
                pltpu.sync_copy(idx_hbm.at[pl.ds(koff, SC_GATHER_ROWS)], idx_v)
                pltpu.sync_copy(rows_v, out_hbm.at[idx_v])

    return pl.kernel(
        body,
        out_type=jax.ShapeDtypeStruct((n_rows, width), src.dtype),
        mesh=mesh,
        scratch_types=[
            pltpu.VMEM((SC_GATHER_ROWS,), jnp.int32),
            pltpu.VMEM((SC_GATHER_ROWS, width), src.dtype),
        ],
        name="sc_scatter_rows",
    )(src, pos_kmajor)


def _moe_ffn_kernel(te_ref, nt_ref, x_ref, wgu_ref, bgu_ref, wd_ref, bd_ref, o_ref,
                    wgu_bf, wd_bf):
    i = pl.program_id(0)
    n_valid = nt_ref[0]

    new_expert = jnp.logical_or(i == 0, te_ref[i] != te_ref[jnp.maximum(i - 1, 0)])

    @pl.when(jnp.logical_and(i < n_valid, new_expert))
    def _():
        wgu_bf[...] = wgu_ref[0].astype(jnp.bfloat16)
        wd_bf[...] = wd_ref[0].astype(jnp.bfloat16)

    @pl.when(i < n_valid)
    def _():
        xb = _unpack_bf16_pair(x_ref[...]).astype(jnp.bfloat16)
        gu = jnp.dot(xb, wgu_bf[...], preferred_element_type=jnp.float32) + bgu_ref[0]
        gate = jnp.minimum(gu[:, :D_EXPERT], SWIGLU_LIMIT)
        up = jnp.clip(gu[:, D_EXPERT:], -SWIGLU_LIMIT, SWIGLU_LIMIT)
        act = (up + 1.0) * (gate * jax.nn.sigmoid(SWIGLU_ALPHA * gate))
        out = jnp.dot(act.astype(jnp.bfloat16), wd_bf[...],
                      preferred_element_type=jnp.float32) + bd_ref[0]
        half = D_MODEL // 2
        o_ref[...] = _pack_bf16_pair(out[:, :half], out[:, half:])

    @pl.when(i >= n_valid)
    def _():
        o_ref[...] = jnp.zeros_like(o_ref)


def _moe_ffn(tile_expert, n_valid, xg, wgu, bgu, wd, bd, tm):
    n_tiles = xg.shape[0] // tm
    grid_spec = pltpu.PrefetchScalarGridSpec(
        num_scalar_prefetch=2,
        grid=(n_tiles,),
        in_specs=[
            pl.BlockSpec((tm, D_MODEL // 2), lambda i, te, nt: (jnp.minimum(i, nt[0] - 1), 0)),
            pl.BlockSpec((1, D_MODEL, 2 * D_EXPERT), lambda i, te, nt: (te[i], 0, 0)),
            pl.BlockSpec((1, 1, 2 * D_EXPERT), lambda i, te, nt: (te[i], 0, 0)),
            pl.BlockSpec((1, D_EXPERT, D_MODEL), lambda i, te, nt: (te[i], 0, 0)),
            pl.BlockSpec((1, 1, D_MODEL), lambda i, te, nt: (te[i], 0, 0)),
        ],
        out_specs=pl.BlockSpec((tm, D_MODEL // 2), lambda i, te, nt: (i, 0)),
        scratch_shapes=[
            pltpu.VMEM((D_MODEL, 2 * D_EXPERT), jnp.bfloat16),
            pltpu.VMEM((D_EXPERT, D_MODEL), jnp.bfloat16),
        ],
    )
    return pl.pallas_call(
        _moe_ffn_kernel,
        grid_spec=grid_spec,
        out_shape=jax.ShapeDtypeStruct((n_tiles * tm, D_MODEL // 2), jnp.uint32),
        compiler_params=_cparams(("arbitrary",)),
        name="moe_ffn",
    )(tile_expert, n_valid, xg, wgu, bgu.reshape(N_EXPERTS, 1, -1), wd,
      bd.reshape(N_EXPERTS, 1, -1))


def _moe_combine_kernel(rows_ref, wgt_ref, x1_ref, mod_ref, g_ref, o_ref):
    ts = x1_ref.shape[1]
    w = wgt_ref[0]
    y = jnp.zeros(x1_ref.shape[1:], jnp.float32)
    for k in range(TOP_K):
        y = y + w[:, k:k + 1] * _unpack_bf16_pair(rows_ref[k * ts:(k + 1) * ts, :])
    mod = mod_ref[0]
    o_ref[0] = x1_ref[0] + mod[5:6] * (_rms(y) * g_ref[...])


def _moe_combine(rows, wgt, x1, mod3, g_post, ts):
    bsz, seq, _ = x1.shape
    per_b = seq // ts
    return pl.pallas_call(
        _moe_combine_kernel,
        grid=(bsz, per_b),
        in_specs=[
            pl.BlockSpec((TOP_K * ts, D_MODEL // 2), lambda b, i: (b * per_b + i, 0)),
            pl.BlockSpec((1, ts, LANES), lambda b, i: (b, i, 0)),
            pl.BlockSpec((1, ts, D_MODEL), lambda b, i: (b, i, 0)),
            pl.BlockSpec((1, N_MOD, D_MODEL), lambda b, i: (b, 0, 0)),
            pl.BlockSpec((1, D_MODEL), lambda b, i: (0, 0)),
        ],
        out_specs=pl.BlockSpec((1, ts, D_MODEL), lambda b, i: (b, i, 0)),
        out_shape=jax.ShapeDtypeStruct((bsz, seq, D_MODEL), jnp.float32),
        compiler_params=_cparams(("arbitrary", "arbitrary")),
        name="moe_combine",
    )(rows, wgt, x1, mod3, g_post)


def _routing(top_idx, rank, counts, tm, n_tiles):
    padded = ((counts + tm - 1) // tm) * tm
    pend = jnp.cumsum(padded)
    pstart = pend - padded
    onehot = top_idx[:, :, None] == jnp.arange(N_EXPERTS, dtype=jnp.int32)[None, None, :]
    pos = rank + jnp.sum(jnp.where(onehot, pstart[None, None, :], 0), axis=-1)
    n_valid = (pend[-1] // tm).astype(jnp.int32)
    tile_row0 = jnp.arange(n_tiles, dtype=jnp.int32) * tm
    tile_expert = jnp.minimum(
        jnp.sum((tile_row0[:, None] >= pend[None, :]).astype(jnp.int32), axis=1), N_EXPERTS - 1)
    last_oh = jnp.arange(n_tiles, dtype=jnp.int32) == jnp.maximum(n_valid - 1, 0)
    last_expert = jnp.sum(jnp.where(last_oh, tile_expert, 0))
    tile_expert = jnp.where(jnp.arange(n_tiles, dtype=jnp.int32) < n_valid, tile_expert, last_expert)
    return pos.astype(jnp.int32), tile_expert.astype(jnp.int32), n_valid.reshape(1)


def _alibi_slopes(n_heads):
    return 2.0 ** (-8.0 * jnp.arange(1, n_heads + 1, dtype=jnp.float32) / n_heads)


def _layer(x, c, w_mod, b_mod, g_pre_mix, g_post_mix, w_in, lamv, g_subln, w_branch_sb,
           w_branch_diff, w_out, g_pre_ffn, g_post_ffn, w_router, b_router, w_gate_up,
           b_gate_up, w_down, b_down, tiles):
    ts_in, tq, ts_merge, tm, ts_comb = tiles
    bsz, seq, d = x.shape
    n_tok = bsz * seq
    bf = jnp.bfloat16

    mod, lam = _mod_proj(c, w_mod, b_mod, lamv)
    mod3 = mod.reshape(bsz, N_MOD, d)

    o_vsb = 2 * SB_WIDTH
    o_qdf = 3 * SB_WIDTH
    o_vdf = o_qdf + 2 * DIFF_QK_WIDTH
    o_g = o_vdf + DIFF_V_WIDTH
    w_main = jnp.concatenate([w_in[:, :o_vsb], w_in[:, o_qdf:o_vdf], w_in[:, o_g:]], axis=1).astype(bf)
    w_vt = jnp.concatenate([w_in[:, o_vsb:o_qdf], w_in[:, o_vdf:o_g]], axis=1).T.astype(bf)

    main, vt = _in_proj(x, mod3, g_pre_mix.reshape(1, d), w_main, w_vt, ts_in, tq)
    y_sb = _sb_attn(main, vt, tq, tq)
    y_df = _diff_attn(main, vt, _alibi_slopes(DIFF_HEADS), lam,
                      g_subln.reshape(DIFF_V_DIM, 1), tq, tq)

    wr = jnp.zeros((d, LANES), jnp.float32).at[:, :N_EXPERTS].set(w_router)
    wrh = wr.astype(bf)
    wrl = (wr - wrh.astype(jnp.float32)).astype(bf)
    br = jnp.full((1, LANES), NEG_BIG, jnp.float32).at[0, :N_EXPERTS].set(b_router)
    x1, h2p, top_idx, top_w, rank, counts = _merge_router(
        y_sb, y_df, main, x, mod3, w_branch_sb.astype(bf), w_branch_diff.astype(bf),
        w_out.astype(bf), g_post_mix.reshape(1, d), g_pre_ffn.reshape(1, d), wrh, wrl, br, ts_merge)

    n_tiles = (n_tok * TOP_K) // tm + N_EXPERTS
    pos, tile_expert, n_valid = _routing(
        top_idx.reshape(n_tok, LANES)[:, :TOP_K], rank.reshape(n_tok, LANES)[:, :TOP_K],
        counts[0, :N_EXPERTS], tm, n_tiles)
    xg = _sc_scatter_rows(h2p.reshape(n_tok, d // 2), pos.T.reshape(TOP_K * n_tok), n_tiles * tm)
    rows = _moe_ffn(tile_expert, n_valid, xg, w_gate_up, b_gate_up, w_down, b_down, tm)
    pos_steps = pos.reshape(n_tok // ts_comb, ts_comb, TOP_K).swapaxes(1, 2).reshape(n_tok * TOP_K)
    tok_rows = _sc_gather_rows(rows, pos_steps)
    return _moe_combine(tok_rows, top_w, x1, mod3, g_post_ffn.reshape(1, d), ts_comb)


def kernel(x, c, w_mod, b_mod, g_pre_mix, g_post_mix, w_in, lambda_q1, lambda_k1, lambda_q2,
           lambda_k2, g_subln, w_branch_sb, w_branch_diff, w_out, g_pre_ffn, g_post_ffn,
           w_router, b_router, w_gate_up, b_gate_up, w_down, b_down):
    depth = w_mod.shape[0]
    assert depth == 1, "LAM_INIT is the layer-0 value"
    for l in range(depth):
        lamv = jnp.stack([lambda_q1[l], lambda_k1[l], lambda_q2[l], lambda_k2[l]])
        x = _layer(x, c, w_mod[l], b_mod[l], g_pre_mix[l], g_post_mix[l], w_in[l], lamv,
                   g_subln[l], w_branch_sb[l], w_branch_diff[l], w_out[l], g_pre_ffn[l],
                   g_post_ffn[l], w_router[l], b_router[l], w_gate_up[l], b_gate_up[l],
                   w_down[l], b_down[l], _Tiles())
    return x
```

```python
import functools
import math
from typing import NamedTuple

import jax
import jax.numpy as jnp
from jax import lax
from jax.experimental import pallas as pl
from jax.experimental.pallas import tpu as pltpu
from jax.experimental.pallas import tpu_sc as plsc

D_MODEL = 1024
SB_HEADS = 8
SB_HEAD_DIM = 64
SB_WIDTH = SB_HEADS * SB_HEAD_DIM
DIFF_HEADS = 4
DIFF_HEAD_DIM = 64
DIFF_V_DIM = 2 * DIFF_HEAD_DIM
DIFF_QK_WIDTH = DIFF_HEADS * 2 * DIFF_HEAD_DIM
DIFF_V_WIDTH = DIFF_HEADS * DIFF_V_DIM
N_EXPERTS = 32
TOP_K = 4
D_EXPERT = D_MODEL
SWIGLU_LIMIT = 7.0
SWIGLU_ALPHA = 1.702
RMS_EPS = 1e-6
N_MOD = 6
LAM_INIT = 0.8 - 0.6 * math.exp(-0.3 * 0)

LANES = 128
SUBLANES = 8
NEG_BIG = -1e30
EXP_ZERO_MARGIN = 110.0
NORM_SLACK = 1.01
SC_CORES = 2
SC_SUBCORES = 16
SC_GATHER_ROWS = 128
MERGE_SUB = 256
SB_HEADS_PER_STEP = 4

MAIN_WIDTH = 2 * SB_WIDTH + 2 * DIFF_QK_WIDTH + 2 * D_MODEL
VT_ROWS = SB_WIDTH + DIFF_V_WIDTH
COLBLK_K_SB = SB_WIDTH // LANES
COLBLK_Q_DF = 2 * SB_WIDTH // LANES
COLBLK_K_DF = COLBLK_Q_DF + DIFF_QK_WIDTH // LANES
GATE_COL0 = 2 * SB_WIDTH + 2 * DIFF_QK_WIDTH

V7X_VMEM_BYTES = 64 * 1024 * 1024
VMEM_LIMIT = V7X_VMEM_BYTES - 8 * 1024 * 1024


class _Tiles(NamedTuple):
    in_proj: int = 512
    attn: int = 256
    merge: int = 512
    expert: int = 512
    combine: int = 256


def _cparams(sem, vmem=VMEM_LIMIT):
    return pltpu.CompilerParams(dimension_semantics=sem, vmem_limit_bytes=vmem)


def _rms(x):
    return x * lax.rsqrt(jnp.mean(x * x, axis=-1, keepdims=True) + RMS_EPS)


def _mod_kernel(c_ref, w_ref, b_ref, lamv_ref, mod_ref, lam_ref):
    c = c_ref[...]
    ca = c * jax.nn.sigmoid(c)
    mod_ref[...] = jnp.dot(ca, w_ref[...], preferred_element_type=jnp.float32,
                           precision=lax.Precision.HIGHEST) + b_ref[...]
    lv = lamv_ref[...]
    s1 = jnp.sum(lv[0:1] * lv[1:2], axis=-1, keepdims=True)
    s2 = jnp.sum(lv[2:3] * lv[3:4], axis=-1, keepdims=True)
    lam = jnp.exp(s1) - jnp.exp(s2) + LAM_INIT
    lam_ref[...] = jnp.broadcast_to(lam, lam_ref.shape)


def _mod_proj(c, w_mod, b_mod, lamv):
    bsz = c.shape[0]
    tn = 1536
    n = w_mod.shape[1]
    return pl.pallas_call(
        _mod_kernel,
        grid=(n // tn,),
        in_specs=[
            pl.BlockSpec((bsz, D_MODEL), lambda j: (0, 0)),
            pl.BlockSpec((D_MODEL, tn), lambda j: (0, j)),
            pl.BlockSpec((1, tn), lambda j: (0, j)),
            pl.BlockSpec((4, DIFF_HEAD_DIM), lambda j: (0, 0)),
        ],
        out_specs=[
            pl.BlockSpec((bsz, tn), lambda j: (0, j)),
            pl.BlockSpec((SUBLANES, LANES), lambda j: (0, 0)),
        ],
        out_shape=[
            jax.ShapeDtypeStruct((bsz, n), jnp.float32),
            jax.ShapeDtypeStruct((SUBLANES, LANES), jnp.float32),
        ],
        compiler_params=_cparams(("arbitrary",)),
        name="mod_proj",
    )(c, w_mod, b_mod.reshape(1, n), lamv)


IN_CHUNK = 1024


def _in_proj_kernel(x_ref, mod_ref, g_ref, wm_ref, wvt_ref, main_ref, vt_ref, h_scr, *, tk):
    x = x_ref[0]
    mod = mod_ref[0]
    h = _rms(x) * g_ref[...]
    h = h * (1.0 + mod[1:2]) + mod[0:1]
    hb = h.astype(jnp.bfloat16)
    groups = tk // SUBLANES
    cols = []
    for ct in range(D_MODEL // LANES):
        h_scr[ct] = h[:, ct * LANES:(ct + 1) * LANES]
        pieces = []
        for blk in range(h.shape[0] // tk):
            for g in range(groups):
                pieces.append(h_scr[ct, pl.ds(blk * tk + g, SUBLANES, stride=groups), :])
        cols.append(jnp.concatenate(pieces, axis=0))
    hpb = jnp.concatenate(cols, axis=1).astype(jnp.bfloat16)

    half = IN_CHUNK // 2
    for ci in range(MAIN_WIDTH // IN_CHUNK):
        c0 = ci * IN_CHUNK
        if c0 == 0:
            q = jnp.dot(hb, wm_ref[:, :half], preferred_element_type=jnp.float32)
            k = jnp.dot(hpb, wm_ref[:, half:IN_CHUNK], preferred_element_type=jnp.float32)
            main_ref[0, :, :half] = (q * 0.0625).astype(jnp.bfloat16)
            main_ref[0, :, half:IN_CHUNK] = k.astype(jnp.bfloat16)
            continue
        p = jnp.dot(hb, wm_ref[:, c0:c0 + IN_CHUNK], preferred_element_type=jnp.float32)
        if c0 < GATE_COL0:
            main_ref[0, :, c0:c0 + half] = (p[:, :half] * 0.125).astype(jnp.bfloat16)
            main_ref[0, :, c0 + half:c0 + IN_CHUNK] = p[:, half:].astype(jnp.bfloat16)
        else:
            main_ref[0, :, c0:c0 + IN_CHUNK] = jax.nn.sigmoid(p).astype(jnp.bfloat16)
    nt = (((1,), (1,)), ((), ()))
    vt_sb = lax.dot_general(wvt_ref[:SB_WIDTH, :], hpb, nt, preferred_element_type=jnp.float32)
    vt_df = lax.dot_general(wvt_ref[SB_WIDTH:, :], hb, nt, preferred_element_type=jnp.float32)
    vt_ref[0, :SB_WIDTH, :] = vt_sb.astype(jnp.bfloat16)
    vt_ref[0, SB_WIDTH:, :] = vt_df.astype(jnp.bfloat16)


def _in_proj(x, mod3, g_pre, w_main, w_vt, ts, tk):
    bsz, seq, _ = x.shape
    assert ts % tk == 0
    return pl.pallas_call(
        functools.partial(_in_proj_kernel, tk=tk),
        grid=(bsz, seq // ts),
        in_specs=[
            pl.BlockSpec((1, ts, D_MODEL), lambda b, i: (b, i, 0)),
            pl.BlockSpec((1, N_MOD, D_MODEL), lambda b, i: (b, 0, 0)),
            pl.BlockSpec((1, D_MODEL), lambda b, i: (0, 0)),
            pl.BlockSpec((D_MODEL, MAIN_WIDTH), lambda b, i: (0, 0)),
            pl.BlockSpec((VT_ROWS, D_MODEL), lambda b, i: (0, 0)),
        ],
        out_specs=[
            pl.BlockSpec((1, ts, MAIN_WIDTH), lambda b, i: (b, i, 0)),
            pl.BlockSpec((1, VT_ROWS, ts), lambda b, i: (b, 0, i)),
        ],
        out_shape=[
            jax.ShapeDtypeStruct((bsz, seq, MAIN_WIDTH), jnp.bfloat16),
            jax.ShapeDtypeStruct((bsz, VT_ROWS, seq), jnp.bfloat16),
        ],
        scratch_shapes=[pltpu.VMEM((D_MODEL // LANES, ts, LANES), jnp.float32)],
        compiler_params=_cparams(("arbitrary", "arbitrary")),
        name="in_proj",
    )(x, mod3, g_pre, w_main, w_vt)


def _suffix_excl_prod8(tot):
    sub = lax.broadcasted_iota(jnp.int32, tot.shape, 0)
    x = jnp.where(sub < SUBLANES - 1, pltpu.roll(tot, SUBLANES - 1, 0), 1.0)
    for sh in (1, 2, 4):
        x = x * jnp.where(sub + sh < SUBLANES, pltpu.roll(x, SUBLANES - sh, 0), 1.0)
    return x


def _sb_scores(k_ref, q_heads, s_ref, slot, j, tk):
    rows = pl.ds(pl.multiple_of(j * tk, tk), tk)
    for h, q_h in enumerate(q_heads):
        kb = k_ref[0, rows, (h // 2) * LANES:(h // 2 + 1) * LANES]
        s_ref[slot, h] = lax.dot_general(kb, q_h, (((1,), (1,)), ((), ())),
                                         preferred_element_type=jnp.float32)


def _sb_weights(zt, c8, ok, groups):
    tq = zt.shape[1]
    r = 0.5 - 0.5 * jnp.tanh(zt)
    if ok is not None:
        r = jnp.where(ok, r, 1.0)
    rg = [r[g * SUBLANES:(g + 1) * SUBLANES, :] for g in range(groups)]
    tot = rg[0]
    for g in range(1, groups):
        tot = tot * rg[g]
    p = c8 * _suffix_excl_prod8(tot)
    pieces = [None] * groups
    for g in range(groups - 1, -1, -1):
        pn = p * rg[g]
        pieces[g] = p - pn
        p = pn
    a = jnp.concatenate(pieces, axis=0).astype(jnp.bfloat16)
    return a, jnp.broadcast_to(p[0:1, :], (SUBLANES, tq))


def _sb_attn_kernel(q_ref, k_ref, v_ref, o_ref, acc_ref, c_ref, s_ref, ok_ref, *, tq, tk):
    i = pl.program_id(2)
    groups = tk // SUBLANES
    q_heads = []
    for pair_idx in range(SB_HEADS_PER_STEP // 2):
        q2 = q_ref[0, :, pair_idx * LANES:(pair_idx + 1) * LANES]
        lane = lax.broadcasted_iota(jnp.int32, q2.shape, 1)
        zero = jnp.zeros_like(q2)
        q_heads += [jnp.where(lane < SB_HEAD_DIM, q2, zero), jnp.where(lane < SB_HEAD_DIM, zero, q2)]

    def step(j, slot, masked):
        _sb_scores(k_ref, q_heads, s_ref, 1 - slot, jnp.maximum(j - 1, 0), tk)
        ok = (ok_ref[...] > 0.5) if masked else None
        off = pl.multiple_of(j * tk, tk)
        ws = []
        for h in range(SB_HEADS_PER_STEP):
            a, c_new = _sb_weights(s_ref[slot, h], c_ref[h], ok, groups)
            c_ref[h] = c_new
            ws.append(a)
        for h in range(SB_HEADS_PER_STEP):
            vt_h = v_ref[0, h * SB_HEAD_DIM:(h + 1) * SB_HEAD_DIM, pl.ds(off, tk)]
            acc_ref[h] += jnp.dot(vt_h, ws[h], preferred_element_type=jnp.float32)

    @pl.when(i == 0)
    def _():
        row = lax.broadcasted_iota(jnp.int32, (tk, tq), 0)
        col = lax.broadcasted_iota(jnp.int32, (tk, tq), 1)
        ok_ref[...] = jnp.where((row % SUBLANES) * groups + row // SUBLANES < col, 1.0, 0.0)

    acc_ref[...] = jnp.zeros_like(acc_ref)
    c_ref[...] = jnp.ones_like(c_ref)
    _sb_scores(k_ref, q_heads, s_ref, 0, i, tk)
    step(i, 0, True)
    c_ref[...] = c_ref[...] * jnp.where(i >= 1, 1.0, 0.0)
    step(jnp.maximum(i - 1, 0), 1, False)

    def stick_left():
        return jnp.max(c_ref[...]) > 0.0

    n_rest = jnp.maximum(i - 1, 0)

    def more(state):
        m, go = state
        return jnp.logical_and(m < n_rest // 2, go)

    def pair(state):
        m, _ = state
        j = i - 2 - 2 * m
        step(j, 0, False)
        go_on = stick_left()

        @pl.when(go_on)
        def _():
            step(j - 1, 1, False)

        return m + 1, jnp.logical_and(go_on, stick_left())

    m_done, go = lax.while_loop(more, pair, (jnp.int32(0), stick_left()))

    @pl.when(jnp.logical_and(jnp.logical_and(n_rest % 2 == 1, m_done == n_rest // 2), go))
    def _():
        step(0, 0, False)

    ot = jnp.concatenate([acc_ref[h] for h in range(SB_HEADS_PER_STEP)], axis=0)
    o_ref[0] = ot.astype(jnp.bfloat16)


def _sb_attn(main, vt, tq, tk):
    bsz, seq, _ = main.shape
    assert tq == tk
    width = SB_HEADS_PER_STEP * SB_HEAD_DIM
    k_blk0 = SB_WIDTH // width
    kern = functools.partial(_sb_attn_kernel, tq=tq, tk=tk)
    return pl.pallas_call(
        kern,
        grid=(bsz, SB_WIDTH // width, seq // tq),
        in_specs=[
            pl.BlockSpec((1, tq, width), lambda b, p, i: (b, i, p)),
            pl.BlockSpec((1, seq, width), lambda b, p, i: (b, 0, k_blk0 + p)),
            pl.BlockSpec((1, width, seq), lambda b, p, i: (b, p, 0)),
        ],
        out_specs=pl.BlockSpec((1, width, tq), lambda b, p, i: (b, p, i)),
        out_shape=jax.ShapeDtypeStruct((bsz, SB_WIDTH, seq), jnp.bfloat16),
        scratch_shapes=[
            pltpu.VMEM((SB_HEADS_PER_STEP, SB_HEAD_DIM, tq), jnp.float32),
            pltpu.VMEM((SB_HEADS_PER_STEP, SUBLANES, tq), jnp.float32),
            pltpu.VMEM((2, SB_HEADS_PER_STEP, tk, tq), jnp.float32),
            pltpu.VMEM((tk, tq), jnp.float32),
        ],
        compiler_params=_cparams(("arbitrary", "arbitrary", "arbitrary")),
        name="sb_attn",
    )(main, main, vt)


def _diff_attn_kernel(slopes_ref, inv_slopes_ref, q_ref, k_ref, v_ref, lam_ref, g_ref, o_ref,
                      acc_ref, m_ref, l_ref, s_ref, smax_ref, kn_ref, bias_ref, *, tq, tk):
    hd = pl.program_id(1)
    i = pl.program_id(2)
    slope = slopes_ref[hd]

    @pl.when(i == 0)
    def _():
        kf = k_ref[0].astype(jnp.float32)
        kn2 = jnp.max(jnp.sum(kf * kf, axis=-1, keepdims=True), axis=0, keepdims=True)
        kn_ref[...] = jnp.broadcast_to(kn2, kn_ref.shape)
        row0 = lax.broadcasted_iota(jnp.int32, (tk, tq), 0)
        col0 = lax.broadcasted_iota(jnp.int32, (tk, tq), 1)
        bias_ref[...] = slope * (row0 - col0).astype(jnp.float32)

    q2 = q_ref[0]
    lane = lax.broadcasted_iota(jnp.int32, q2.shape, 1)
    zero = jnp.zeros_like(q2)
    q_maps = (jnp.where(lane < DIFF_HEAD_DIM, q2, zero), jnp.where(lane < DIFF_HEAD_DIM, zero, q2))

    def scores(slot, j, masked, dead=None):
        kb = k_ref[0, pl.ds(pl.multiple_of(j * tk, tk), tk), :]
        for m in range(2):
            s = lax.dot_general(kb, q_maps[m], (((1,), (1,)), ((), ())),
                                preferred_element_type=jnp.float32) + bias_ref[...]
            if masked:
                row = lax.broadcasted_iota(jnp.int32, (tk, tq), 0)
                col = lax.broadcasted_iota(jnp.int32, (tk, tq), 1)
                s = jnp.where(row <= col, s, NEG_BIG)
            if dead is not None:
                s = jnp.where(dead, NEG_BIG, s)
            s_ref[slot, m] = s
            smax_ref[slot, m] = jnp.max(s, axis=0, keepdims=True)

    def step(j, slot, next_dead=None):
        scores(1 - slot, jnp.maximum(j - 1, 0), False, next_dead)
        off = pl.multiple_of(j * tk, tk)
        vtb = v_ref[0, :, pl.ds(off, tk)]
        cb = slope * ((j - i) * tk).astype(jnp.float32)
        ps, alphas = [], []
        for m in range(2):
            s = s_ref[slot, m]
            m_old = m_ref[m]
            m_new = jnp.maximum(m_old, smax_ref[slot, m] + cb)
            alpha = jnp.exp(m_old - m_new)
            p = jnp.exp(s - (m_new - cb))
            l_ref[m] = alpha * l_ref[m] + jnp.sum(p, axis=0, keepdims=True)
            m_ref[m] = m_new
            ps.append(p.astype(jnp.bfloat16))
            alphas.append(alpha)
        for m in range(2):
            acc_ref[m] = alphas[m] * acc_ref[m] + jnp.dot(
                vtb, ps[m], preferred_element_type=jnp.float32)

    acc_ref[...] = jnp.zeros_like(acc_ref)
    m_ref[...] = jnp.full_like(m_ref, NEG_BIG)
    l_ref[...] = jnp.zeros_like(l_ref)
    scores(0, i, True)
    step(i, 0, next_dead=(i == 0))
    step(jnp.maximum(i - 1, 0), 1)

    qf = q2.astype(jnp.float32)
    qn2 = jnp.max(jnp.sum(qf * qf, axis=-1, keepdims=True), axis=0, keepdims=True)
    zabs = jnp.sqrt(qn2 * kn_ref[0:1, 0:1]) * NORM_SLACK
    m_lo = jnp.min(jnp.minimum(m_ref[0], m_ref[1]), axis=1, keepdims=True)
    reach = (EXP_ZERO_MARGIN + zabs - m_lo) * inv_slopes_ref[hd]
    n_need = jnp.floor(jnp.clip((reach - 1.0) * (1.0 / tk), -1.0, 1e6)) + 1.0
    n_back = jnp.maximum(jnp.minimum(i, jnp.max(n_need).astype(jnp.int32)) - 1, 0)

    def quad(n, carry):
        j = i - 2 - 4 * n
        step(j, 0)
        step(j - 1, 1)
        step(j - 2, 0)
        step(j - 3, 1)
        return carry

    n_quads = n_back // 4
    lax.fori_loop(0, n_quads, quad, 0)
    rest = n_back - 4 * n_quads

    @pl.when(rest >= 2)
    def _():
        j = i - 2 - 4 * n_quads
        step(j, 0)
        step(j - 1, 1)

    @pl.when(rest % 2 == 1)
    def _():
        step(i - 1 - n_back, 0)

    lam = lam_ref[0:1, 0:1]
    o = acc_ref[0] / l_ref[0] - lam * (acc_ref[1] / l_ref[1])
    ms = jnp.mean(o * o, axis=0, keepdims=True)
    y = o * lax.rsqrt(ms + RMS_EPS) * g_ref[...] * (1.0 - LAM_INIT)
    o_ref[0] = y.astype(jnp.bfloat16)


def _diff_attn(main, vt, slopes, lam, g_col, tq, tk):
    bsz, seq, _ = main.shape
    assert tq == tk
    kern = functools.partial(_diff_attn_kernel, tq=tq, tk=tk)
    vrow0 = SB_WIDTH // LANES
    return pl.pallas_call(
        kern,
        grid=(bsz, DIFF_HEADS, seq // tq),
        in_specs=[
            pl.BlockSpec(memory_space=pltpu.SMEM),
            pl.BlockSpec(memory_space=pltpu.SMEM),
            pl.BlockSpec((1, tq, LANES), lambda b, h, i: (b, i, COLBLK_Q_DF + h)),
            pl.BlockSpec((1, seq, LANES), lambda b, h, i: (b, 0, COLBLK_K_DF + h)),
            pl.BlockSpec((1, DIFF_V_DIM, seq), lambda b, h, i: (b, vrow0 + h, 0)),
            pl.BlockSpec((SUBLANES, LANES), lambda b, h, i: (0, 0)),
            pl.BlockSpec((DIFF_V_DIM, 1), lambda b, h, i: (0, 0)),
        ],
        out_specs=pl.BlockSpec((1, DIFF_V_DIM, tq), lambda b, h, i: (b, h, i)),
        out_shape=jax.ShapeDtypeStruct((bsz, DIFF_V_WIDTH, seq), jnp.bfloat16),
        scratch_shapes=[
            pltpu.VMEM((2, DIFF_V_DIM, tq), jnp.float32),
            pltpu.VMEM((2, 1, tq), jnp.float32),
            pltpu.VMEM((2, 1, tq), jnp.float32),
            pltpu.VMEM((2, 2, tk, tq), jnp.float32),
            pltpu.VMEM((2, 2, 1, tq), jnp.float32),
            pltpu.VMEM((SUBLANES, LANES), jnp.float32),
            pltpu.VMEM((tk, tq), jnp.float32),
        ],
        compiler_params=_cparams(("arbitrary", "arbitrary", "arbitrary")),
        name="diff_attn",
    )(slopes, 1.0 / slopes, main, main, vt, lam, g_col)


def _pack_bf16_pair(a, b):
    ab = pltpu.bitcast(a.astype(jnp.bfloat16).astype(jnp.float32), jnp.uint32)
    bb = pltpu.bitcast(b.astype(jnp.bfloat16).astype(jnp.float32), jnp.uint32)
    return ab | (bb >> 16)


def _unpack_bf16_pair(w):
    hi = pltpu.bitcast(w & jnp.uint32(0xFFFF0000), jnp.float32)
    lo = pltpu.bitcast(w << 16, jnp.float32)
    return jnp.concatenate([hi, lo], axis=1)


def _merge_router_kernel(ysb_ref, ydf_ref, gates_ref, x_ref, mod_ref, wsb_ref, wdf_ref, wout_ref,
                         gpost_ref, gpre_ref, wrh_ref, wrl_ref, br_ref,
                         x1_ref, h2_ref, idx_ref, wgt_ref, rank_ref, cnt_ref, base_ref):
    first = jnp.logical_and(pl.program_id(0) == 0, pl.program_id(1) == 0)

    @pl.when(first)
    def _():
        base_ref[...] = jnp.zeros_like(base_ref)

    mod = mod_ref[0]
    subs = [slice(s * MERGE_SUB, (s + 1) * MERGE_SUB) for s in range(x_ref.shape[1] // MERGE_SUB)]
    half = D_MODEL // 2
    branch = [(jnp.dot(ysb_ref[0, :, rows].T, wsb_ref[...], preferred_element_type=jnp.float32),
               jnp.dot(ydf_ref[0, :, rows].T, wdf_ref[...], preferred_element_type=jnp.float32))
              for rows in subs]
    merged = []
    for rows, (a, b) in zip(subs, branch):
        g = gates_ref[0, rows, :].astype(jnp.float32)
        merged.append((g[:, :D_MODEL] * a + g[:, D_MODEL:] * b).astype(jnp.bfloat16))
    mixes = [jnp.dot(m, wout_ref[...], preferred_element_type=jnp.float32) for m in merged]
    h2s = []
    for rows, mix in zip(subs, mixes):
        x1 = x_ref[0, rows, :] + mod[2:3] * (_rms(mix) * gpost_ref[...])
        x1_ref[0, rows, :] = x1
        h2 = _rms(x1) * gpre_ref[...]
        h2 = h2 * (1.0 + mod[4:5]) + mod[3:4]
        h2_ref[0, rows, :] = _pack_bf16_pair(h2[:, :half], h2[:, half:])
        h2s.append(h2)
    logit_list = []
    for h2 in h2s:
        hh = h2.astype(jnp.bfloat16)
        hl = (h2 - hh.astype(jnp.float32)).astype(jnp.bfloat16)
        logit_list.append(jnp.dot(hh, wrh_ref[...], preferred_element_type=jnp.float32)
                          + jnp.dot(hh, wrl_ref[...], preferred_element_type=jnp.float32)
                          + jnp.dot(hl, wrh_ref[...], preferred_element_type=jnp.float32)
                          + br_ref[...])
    for rows, logits in zip(subs, logit_list):
        _route_rows(rows, logits, idx_ref, wgt_ref, rank_ref, cnt_ref, base_ref)


def _route_rows(rows, logits, idx_ref, wgt_ref, rank_ref, cnt_ref, base_ref):
    lane = lax.broadcasted_iota(jnp.int32, logits.shape, 1)
    lanef = lane.astype(jnp.float32)
    vals, idxs = [], []
    cur = logits
    for _ in range(TOP_K):
        mx = jnp.max(cur, axis=-1, keepdims=True)
        ix = jnp.min(jnp.where(cur == mx, lanef, float(LANES)), axis=-1, keepdims=True)
        cur = jnp.where(lanef == ix, -jnp.inf, cur)
        vals.append(mx)
        idxs.append(ix)
    es = [jnp.exp(v - vals[0]) for v in vals]
    den = es[0] + es[1] + es[2] + es[3]
    oi = jnp.zeros(logits.shape, jnp.float32)
    ow = jnp.zeros(logits.shape, jnp.float32)
    for k in range(TOP_K):
        oi = jnp.where(lane == k, idxs[k], oi)
        ow = jnp.where(lane == k, es[k] / den, ow)
    idx_ref[0, rows, :] = oi.astype(jnp.int32)
    wgt_ref[0, rows, :] = ow

    ts = logits.shape[0]
    member = jnp.zeros(logits.shape, jnp.float32)
    for k in range(TOP_K):
        member = member + (lanef == idxs[k]).astype(jnp.float32)
    rr = lax.broadcasted_iota(jnp.int32, (ts, ts), 0)
    cc = lax.broadcasted_iota(jnp.int32, (ts, ts), 1)
    lower = jnp.where(cc < rr, 1.0, 0.0).astype(jnp.bfloat16)
    before = jnp.dot(lower, member.astype(jnp.bfloat16), preferred_element_type=jnp.float32)
    base = base_ref[0:1, :]
    rank_all = before + base
    orank = jnp.zeros(logits.shape, jnp.float32)
    for k in range(TOP_K):
        rk = jnp.sum(jnp.where(lanef == idxs[k], rank_all, 0.0), axis=-1, keepdims=True)
        orank = jnp.where(lane == k, rk, orank)
    rank_ref[0, rows, :] = orank.astype(jnp.int32)
    new_base = base + jnp.sum(member, axis=0, keepdims=True)
    base_ref[...] = jnp.broadcast_to(new_base, base_ref.shape)
    cnt_ref[...] = jnp.broadcast_to(new_base, cnt_ref.shape).astype(jnp.int32)


def _merge_router(ysb, ydf, main, x, mod3, wsb, wdf, wout, gpost, gpre, wrh, wrl, br, ts):
    bsz, seq, _ = x.shape
    const = lambda b, i: (0, 0)
    return pl.pallas_call(
        _merge_router_kernel,
        grid=(bsz, seq // ts),
        in_specs=[
            pl.BlockSpec((1, SB_WIDTH, ts), lambda b, i: (b, 0, i)),
            pl.BlockSpec((1, DIFF_V_WIDTH, ts), lambda b, i: (b, 0, i)),
            pl.BlockSpec((1, ts, 2 * D_MODEL), lambda b, i: (b, i, GATE_COL0 // (2 * D_MODEL))),
            pl.BlockSpec((1, ts, D_MODEL), lambda b, i: (b, i, 0)),
            pl.BlockSpec((1, N_MOD, D_MODEL), lambda b, i: (b, 0, 0)),
            pl.BlockSpec((SB_WIDTH, D_MODEL), const),
            pl.BlockSpec((DIFF_V_WIDTH, D_MODEL), const),
            pl.BlockSpec((D_MODEL, D_MODEL), const),
            pl.BlockSpec((1, D_MODEL), const),
            pl.BlockSpec((1, D_MODEL), const),
            pl.BlockSpec((D_MODEL, LANES), const),
            pl.BlockSpec((D_MODEL, LANES), const),
            pl.BlockSpec((1, LANES), const),
        ],
        out_specs=[
            pl.BlockSpec((1, ts, D_MODEL), lambda b, i: (b, i, 0)),
            pl.BlockSpec((1, ts, D_MODEL // 2), lambda b, i: (b, i, 0)),
            pl.BlockSpec((1, ts, LANES), lambda b, i: (b, i, 0)),
            pl.BlockSpec((1, ts, LANES), lambda b, i: (b, i, 0)),
            pl.BlockSpec((1, ts, LANES), lambda b, i: (b, i, 0)),
            pl.BlockSpec((SUBLANES, LANES), const),
        ],
        out_shape=[
            jax.ShapeDtypeStruct((bsz, seq, D_MODEL), jnp.float32),
            jax.ShapeDtypeStruct((bsz, seq, D_MODEL // 2), jnp.uint32),
            jax.ShapeDtypeStruct((bsz, seq, LANES), jnp.int32),
            jax.ShapeDtypeStruct((bsz, seq, LANES), jnp.float32),
            jax.ShapeDtypeStruct((bsz, seq, LANES), jnp.int32),
            jax.ShapeDtypeStruct((SUBLANES, LANES), jnp.int32),
        ],
        scratch_shapes=[pltpu.VMEM((SUBLANES, LANES), jnp.float32)],
        compiler_params=_cparams(("arbitrary", "arbitrary")),
        name="merge_router",
    )(ysb, ydf, main, x, mod3, wsb, wdf, wout, gpost, gpre, wrh, wrl, br)


def _sc_gather_rows(table, idx):
    n = idx.shape[0]
    width = table.shape[1]
    n_workers = SC_CORES * SC_SUBCORES
    per_worker = n // n_workers
    n_chunks = per_worker // SC_GATHER_ROWS
    assert n_chunks * SC_GATHER_ROWS * n_workers == n
    mesh = plsc.VectorSubcoreMesh(core_axis_name="c", subcore_axis_name="s",
                                  num_cores=SC_CORES, num_subcores=SC_SUBCORES)

    def body(table_hbm, idx_hbm, out_hbm, idx_v, rows_v, sem):
        wid = lax.axis_index("s") * SC_CORES + lax.axis_index("c")
        base = wid * per_worker

        @pl.loop(0, n_chunks)
        def _(ci):
            off = pl.multiple_of(base + ci * SC_GATHER_ROWS, SC_GATHER_ROWS)
            pltpu.sync_copy(idx_hbm.at[pl.ds(off, SC_GATHER_ROWS)], idx_v)
            pltpu.async_copy(table_hbm.at[idx_v], rows_v, sem).wait()
            pltpu.sync_copy(rows_v, out_hbm.at[pl.ds(off, SC_GATHER_ROWS)])

    return pl.kernel(
        body,
        out_type=jax.ShapeDtypeStruct((n, width), table.dtype),
        mesh=mesh,
        scratch_types=[
            pltpu.VMEM((SC_GATHER_ROWS,), jnp.int32),
            pltpu.VMEM((SC_GATHER_ROWS, width), table.dtype),
            pltpu.SemaphoreType.DMA,
        ],
        name="sc_gather_rows",
    )(table, idx)


def _sc_scatter_rows(src, pos_kmajor, n_rows):
    n_tok, width = src.shape
    n_workers = SC_CORES * SC_SUBCORES
    per_worker = n_tok // n_workers
    n_chunks = per_worker // SC_GATHER_ROWS
    assert n_chunks * SC_GATHER_ROWS * n_workers == n_tok
    mesh = plsc.VectorSubcoreMesh(core_axis_name="c", subcore_axis_name="s",
                                  num_cores=SC_CORES, num_subcores=SC_SUBCORES)

    def body(src_hbm, idx_hbm, out_hbm, idx_v, rows_v):
        wid = lax.axis_index("s") * SC_CORES + lax.axis_index("c")
        base = wid * per_worker

        @pl.loop(0, n_chunks)
        def _(ci):
            off = pl.multiple_of(base + ci * SC_GATHER_ROWS, SC_GATHER_ROWS)
            pltpu.sync_copy(src_hbm.at[pl.ds(off, SC_GATHER_ROWS)], rows_v)
            for k in range(TOP_K):
                koff = pl.multiple_of(k * n_tok + off, SC_GATHER_ROWS)
                pltpu.sync_copy(idx_hbm.at[pl.ds(koff, SC_GATHER_ROWS)], idx_v)
                pltpu.sync_copy(rows_v, out_hbm.at[idx_v])

    return pl.kernel(
        body,
        out_type=jax.ShapeDtypeStruct((n_rows, width), src.dtype),
        mesh=mesh,
        scratch_types=[
            pltpu.VMEM((SC_GATHER_ROWS,), jnp.int32),
            pltpu.VMEM((SC_GATHER_ROWS, width), src.dtype),
        ],
        name="sc_scatter_rows",
    )(src, pos_kmajor)


def _moe_ffn_kernel(te_ref, nt_ref, x_ref, wgu_ref, bgu_ref, wd_ref, bd_ref, o_ref,
                    wgu_bf, wd_bf):
    i = pl.program_id(0)
    n_valid = nt_ref[0]

    new_expert = jnp.logical_or(i == 0, te_ref[i] != te_ref[jnp.maximum(i - 1, 0)])

    @pl.when(jnp.logical_and(i < n_valid, new_expert))
    def _():
        wgu_bf[...] = wgu_ref[0].astype(jnp.bfloat16)
        wd_bf[...] = wd_ref[0].astype(jnp.bfloat16)

    @pl.when(i < n_valid)
    def _():
        xb = _unpack_bf16_pair(x_ref[...]).astype(jnp.bfloat16)
        gu = jnp.dot(xb, wgu_bf[...], preferred_element_type=jnp.float32) + bgu_ref[0]
        gate = jnp.minimum(gu[:, :D_EXPERT], SWIGLU_LIMIT)
        up = jnp.clip(gu[:, D_EXPERT:], -SWIGLU_LIMIT, SWIGLU_LIMIT)
        act = (up + 1.0) * (gate * jax.nn.sigmoid(SWIGLU_ALPHA * gate))
        out = jnp.dot(act.astype(jnp.bfloat16), wd_bf[...],
                      preferred_element_type=jnp.float32) + bd_ref[0]
        half = D_MODEL // 2
        o_ref[...] = _pack_bf16_pair(out[:, :half], out[:, half:])

    @pl.when(i >= n_valid)
    def _():
        o_ref[...] = jnp.zeros_like(o_ref)


def _moe_ffn(tile_expert, n_valid, xg, wgu, bgu, wd, bd, tm):
    n_tiles = xg.shape[0] // tm
    grid_spec = pltpu.PrefetchScalarGridSpec(
        num_scalar_prefetch=2,
        grid=(n_tiles,),
        in_specs=[
            pl.BlockSpec((tm, D_MODEL // 2), lambda i, te, nt: (jnp.minimum(i, nt[0] - 1), 0)),
            pl.BlockSpec((1, D_MODEL, 2 * D_EXPERT), lambda i, te, nt: (te[i], 0, 0)),
            pl.BlockSpec((1, 1, 2 * D_EXPERT), lambda i, te, nt: (te[i], 0, 0)),
            pl.BlockSpec((1, D_EXPERT, D_MODEL), lambda i, te, nt: (te[i], 0, 0)),
            pl.BlockSpec((1, 1, D_MODEL), lambda i, te, nt: (te[i], 0, 0)),
        ],
        out_specs=pl.BlockSpec((tm, D_MODEL // 2), lambda i, te, nt: (i, 0)),
        scratch_shapes=[
            pltpu.VMEM((D_MODEL, 2 * D_EXPERT), jnp.bfloat16),
            pltpu.VMEM((D_EXPERT, D_MODEL), jnp.bfloat16),
        ],
    )
    return pl.pallas_call(
        _moe_ffn_kernel,
        grid_spec=grid_spec,
        out_shape=jax.ShapeDtypeStruct((n_tiles * tm, D_MODEL // 2), jnp.uint32),
        compiler_params=_cparams(("arbitrary",)),
        name="moe_ffn",
    )(tile_expert, n_valid, xg, wgu, bgu.reshape(N_EXPERTS, 1, -1), wd,
      bd.reshape(N_EXPERTS, 1, -1))


def _moe_combine_kernel(rows_ref, wgt_ref, x1_ref, mod_ref, g_ref, o_ref):
    ts = x1_ref.shape[1]
    w = wgt_ref[0]
    y = jnp.zeros(x1_ref.shape[1:], jnp.float32)
    for k in range(TOP_K):
        y = y + w[:, k:k + 1] * _unpack_bf16_pair(rows_ref[k * ts:(k + 1) * ts, :])
    mod = mod_ref[0]
    o_ref[0] = x1_ref[0] + mod[5:6] * (_rms(y) * g_ref[...])


def _moe_combine(rows, wgt, x1, mod3, g_post, ts):
    bsz, seq, _ = x1.shape
    per_b = seq // ts
    return pl.pallas_call(
        _moe_combine_kernel,
        grid=(bsz, per_b),
        in_specs=[
            pl.BlockSpec((TOP_K * ts, D_MODEL // 2), lambda b, i: (b * per_b + i, 0)),
            pl.BlockSpec((1, ts, LANES), lambda b, i: (b, i, 0)),
            pl.BlockSpec((1, ts, D_MODEL), lambda b, i: (b, i, 0)),
            pl.BlockSpec((1, N_MOD, D_MODEL), lambda b, i: (b, 0, 0)),
            pl.BlockSpec((1, D_MODEL), lambda b, i: (0, 0)),
        ],
        out_specs=pl.BlockSpec((1, ts, D_MODEL), lambda b, i: (b, i, 0)),
        out_shape=jax.ShapeDtypeStruct((bsz, seq, D_MODEL), jnp.float32),
        compiler_params=_cparams(("arbitrary", "arbitrary")),
        name="moe_combine",
    )(rows, wgt, x1, mod3, g_post)


def _routing(top_idx, rank, counts, tm, n_tiles):
    padded = ((counts + tm - 1) // tm) * tm
    pend = jnp.cumsum(padded)
    pstart = pend - padded
    onehot = top_idx[:, :, None] == jnp.arange(N_EXPERTS, dtype=jnp.int32)[None, None, :]
    pos = rank + jnp.sum(jnp.where(onehot, pstart[None, None, :], 0), axis=-1)
    n_valid = (pend[-1] // tm).astype(jnp.int32)
    tile_row0 = jnp.arange(n_tiles, dtype=jnp.int32) * tm
    tile_expert = jnp.minimum(
        jnp.sum((tile_row0[:, None] >= pend[None, :]).astype(jnp.int32), axis=1), N_EXPERTS - 1)
    last_oh = jnp.arange(n_tiles, dtype=jnp.int32) == jnp.maximum(n_valid - 1, 0)
    last_expert = jnp.sum(jnp.where(last_oh, tile_expert, 0))
    tile_expert = jnp.where(jnp.arange(n_tiles, dtype=jnp.int32) < n_valid, tile_expert, last_expert)
    return pos.astype(jnp.int32), tile_expert.astype(jnp.int32), n_valid.reshape(1)


def _alibi_slopes(n_heads):
    return 2.0 ** (-8.0 * jnp.arange(1, n_heads + 1, dtype=jnp.float32) / n_heads)


def _layer(x, c, w_mod, b_mod, g_pre_mix, g_post_mix, w_in, lamv, g_subln, w_branch_sb,
           w_branch_diff, w_out, g_pre_ffn, g_post_ffn, w_router, b_router, w_gate_up,
           b_gate_up, w_down, b_down, tiles):
    ts_in, tq, ts_merge, tm, ts_comb = tiles
    bsz, seq, d = x.shape
    n_tok = bsz * seq
    bf = jnp.bfloat16

    mod, lam = _mod_proj(c, w_mod, b_mod, lamv)
    mod3 = mod.reshape(bsz, N_MOD, d)

    o_vsb = 2 * SB_WIDTH
    o_qdf = 3 * SB_WIDTH
    o_vdf = o_qdf + 2 * DIFF_QK_WIDTH
    o_g = o_vdf + DIFF_V_WIDTH
    w_main = jnp.concatenate([w_in[:, :o_vsb], w_in[:, o_qdf:o_vdf], w_in[:, o_g:]], axis=1).astype(bf)
    w_vt = jnp.concatenate([w_in[:, o_vsb:o_qdf], w_in[:, o_vdf:o_g]], axis=1).T.astype(bf)

    main, vt = _in_proj(x, mod3, g_pre_mix.reshape(1, d), w_main, w_vt, ts_in, tq)
    y_sb = _sb_attn(main, vt, tq, tq)
    y_df = _diff_attn(main, vt, _alibi_slopes(DIFF_HEADS), lam,
                      g_subln.reshape(DIFF_V_DIM, 1), tq, tq)

    wr = jnp.zeros((d, LANES), jnp.float32).at[:, :N_EXPERTS].set(w_router)
    wrh = wr.astype(bf)
    wrl = (wr - wrh.astype(jnp.float32)).astype(bf)
    br = jnp.full((1, LANES), NEG_BIG, jnp.float32).at[0, :N_EXPERTS].set(b_router)
    x1, h2p, top_idx, top_w, rank, counts = _merge_router(
        y_sb, y_df, main, x, mod3, w_branch_sb.astype(bf), w_branch_diff.astype(bf),
        w_out.astype(bf), g_post_mix.reshape(1, d), g_pre_ffn.reshape(1, d), wrh, wrl, br, ts_merge)

    n_tiles = (n_tok * TOP_K) // tm + N_EXPERTS
    pos, tile_expert, n_valid = _routing(
        top_idx.reshape(n_tok, LANES)[:, :TOP_K], rank.reshape(n_tok, LANES)[:, :TOP_K],
        counts[0, :N_EXPERTS], tm, n_tiles)
    xg = _sc_scatter_rows(h2p.reshape(n_tok, d // 2), pos.T.reshape(TOP_K * n_tok), n_tiles * tm)
    rows = _moe_ffn(tile_expert, n_valid, xg, w_gate_up, b_gate_up, w_down, b_down, tm)
    pos_steps = pos.reshape(n_tok // ts_comb, ts_comb, TOP_K).swapaxes(1, 2).reshape(n_tok * TOP_K)
    tok_rows = _sc_gather_rows(rows, pos_steps)
    return _moe_combine(tok_rows, top_w, x1, mod3, g_post_ffn.reshape(1, d), ts_comb)


def kernel(x, c, w_mod, b_mod, g_pre_mix, g_post_mix, w_in, lambda_q1, lambda_k1, lambda_q2,
           lambda_k2, g_subln, w_branch_sb, w_branch_diff, w_out, g_pre_ffn, g_post_ffn,
           w_router, b_router, w_gate_up, b_gate_up, w_down, b_down):
    depth = w_mod.shape[0]
    assert depth == 1, "LAM_INIT is the layer-0 value"
    for l in range(depth):
        lamv = jnp.stack([lambda_q1[l], lambda_k1[l], lambda_q2[l], lambda_k2[l]])
        x = _layer(x, c, w_mod[l], b_mod[l], g_pre_mix[l], g_post_mix[l], w_in[l], lamv,
                   g_subln[l], w_branch_sb[l], w_branch_diff[l], w_out[l], g_pre_ffn[l],
                   g_post_ffn[l], w_router[l], b_router[l], w_gate_up[l], b_gate_up[l],
                   w_down[l], b_down[l], _Tiles())
    return x
```

```python
import functools
import math
from typing import NamedTuple

import jax
import jax.numpy as jnp
from jax import lax
from jax.experimental import pallas as pl
from jax.experimental.pallas import tpu as pltpu
from jax.experimental.pallas import tpu_sc as plsc

D_MODEL = 1024
SB_HEADS = 8
SB_HEAD_DIM = 64
SB_WIDTH = SB_HEADS * SB_HEAD_DIM
DIFF_HEADS = 4
DIFF_HEAD_DIM = 64
DIFF_V_DIM = 2 * DIFF_HEAD_DIM
DIFF_QK_WIDTH = DIFF_HEADS * 2 * DIFF_HEAD_DIM
DIFF_V_WIDTH = DIFF_HEADS * DIFF_V_DIM
N_EXPERTS = 32
TOP_K = 4
D_EXPERT = D_MODEL
SWIGLU_LIMIT = 7.0
SWIGLU_ALPHA = 1.702
RMS_EPS = 1e-6
N_MOD = 6
LAM_INIT = 0.8 - 0.6 * math.exp(-0.3 * 0)

LANES = 128
SUBLANES = 8
NEG_BIG = -1e30
EXP_ZERO_MARGIN = 110.0
NORM_SLACK = 1.01
SC_CORES = 2
SC_SUBCORES = 16
SC_GATHER_ROWS = 128
MERGE_SUB = 256
SB_HEADS_PER_STEP = 4
DIFF_HEADS_PER_STEP = 2

MAIN_WIDTH = 2 * SB_WIDTH + 2 * DIFF_QK_WIDTH + 2 * D_MODEL
VT_ROWS = SB_WIDTH + DIFF_V_WIDTH
COLBLK_K_SB = SB_WIDTH // LANES
COLBLK_Q_DF = 2 * SB_WIDTH // LANES
COLBLK_K_DF = COLBLK_Q_DF + DIFF_QK_WIDTH // LANES
GATE_COL0 = 2 * SB_WIDTH + 2 * DIFF_QK_WIDTH

V7X_VMEM_BYTES = 64 * 1024 * 1024
VMEM_LIMIT = V7X_VMEM_BYTES - 8 * 1024 * 1024


class _Tiles(NamedTuple):
    in_proj: int = 512
    attn: int = 256
    merge: int = 512
    expert: int = 512
    combine: int = 256


def _cparams(sem, vmem=VMEM_LIMIT):
    return pltpu.CompilerParams(dimension_semantics=sem, vmem_limit_bytes=vmem)


def _rms(x):
    return x * lax.rsqrt(jnp.mean(x * x, axis=-1, keepdims=True) + RMS_EPS)


def _mod_kernel(c_ref, w_ref, b_ref, lamv_ref, mod_ref, lam_ref):
    c = c_ref[...]
    ca = c * jax.nn.sigmoid(c)
    mod_ref[...] = jnp.dot(ca, w_ref[...], preferred_element_type=jnp.float32,
                           precision=lax.Precision.HIGHEST) + b_ref[...]
    lv = lamv_ref[...]
    s1 = jnp.sum(lv[0:1] * lv[1:2], axis=-1, keepdims=True)
    s2 = jnp.sum(lv[2:3] * lv[3:4], axis=-1, keepdims=True)
    lam = jnp.exp(s1) - jnp.exp(s2) + LAM_INIT
    lam_ref[...] = jnp.broadcast_to(lam, lam_ref.shape)


def _mod_proj(c, w_mod, b_mod, lamv):
    bsz = c.shape[0]
    tn = 1536
    n = w_mod.shape[1]
    return pl.pallas_call(
        _mod_kernel,
        grid=(n // tn,),
        in_specs=[
            pl.BlockSpec((bsz, D_MODEL), lambda j: (0, 0)),
            pl.BlockSpec((D_MODEL, tn), lambda j: (0, j)),
            pl.BlockSpec((1, tn), lambda j: (0, j)),
            pl.BlockSpec((4, DIFF_HEAD_DIM), lambda j: (0, 0)),
        ],
        out_specs=[
            pl.BlockSpec((bsz, tn), lambda j: (0, j)),
            pl.BlockSpec((SUBLANES, LANES), lambda j: (0, 0)),
        ],
        out_shape=[
            jax.ShapeDtypeStruct((bsz, n), jnp.float32),
            jax.ShapeDtypeStruct((SUBLANES, LANES), jnp.float32),
        ],
        compiler_params=_cparams(("arbitrary",)),
        name="mod_proj",
    )(c, w_mod, b_mod.reshape(1, n), lamv)


IN_CHUNK = 1024


def _in_proj_kernel(x_ref, mod_ref, g_ref, wm_ref, wvt_ref, main_ref, vt_ref, h_scr, *, tk):
    x = x_ref[0]
    mod = mod_ref[0]
    h = _rms(x) * g_ref[...]
    h = h * (1.0 + mod[1:2]) + mod[0:1]
    hb = h.astype(jnp.bfloat16)
    groups = tk // SUBLANES
    cols = []
    for ct in range(D_MODEL // LANES):
        h_scr[ct] = h[:, ct * LANES:(ct + 1) * LANES]
        pieces = []
        for blk in range(h.shape[0] // tk):
            for g in range(groups):
                pieces.append(h_scr[ct, pl.ds(blk * tk + g, SUBLANES, stride=groups), :])
        cols.append(jnp.concatenate(pieces, axis=0))
    hpb = jnp.concatenate(cols, axis=1).astype(jnp.bfloat16)

    half = IN_CHUNK // 2
    for ci in range(MAIN_WIDTH // IN_CHUNK):
        c0 = ci * IN_CHUNK
        if c0 == 0:
            q = jnp.dot(hb, wm_ref[:, :half], preferred_element_type=jnp.float32)
            k = jnp.dot(hpb, wm_ref[:, half:IN_CHUNK], preferred_element_type=jnp.float32)
            main_ref[0, :, :half] = (q * 0.0625).astype(jnp.bfloat16)
            main_ref[0, :, half:IN_CHUNK] = k.astype(jnp.bfloat16)
            continue
        p = jnp.dot(hb, wm_ref[:, c0:c0 + IN_CHUNK], preferred_element_type=jnp.float32)
        if c0 < GATE_COL0:
            main_ref[0, :, c0:c0 + half] = (p[:, :half] * 0.125).astype(jnp.bfloat16)
            main_ref[0, :, c0 + half:c0 + IN_CHUNK] = p[:, half:].astype(jnp.bfloat16)
        else:
            main_ref[0, :, c0:c0 + IN_CHUNK] = jax.nn.sigmoid(p).astype(jnp.bfloat16)
    nt = (((1,), (1,)), ((), ()))
    vt_sb = lax.dot_general(wvt_ref[:SB_WIDTH, :], hpb, nt, preferred_element_type=jnp.float32)
    vt_df = lax.dot_general(wvt_ref[SB_WIDTH:, :], hb, nt, preferred_element_type=jnp.float32)
    vt_ref[0, :SB_WIDTH, :] = vt_sb.astype(jnp.bfloat16)
    vt_ref[0, SB_WIDTH:, :] = vt_df.astype(jnp.bfloat16)


def _in_proj(x, mod3, g_pre, w_main, w_vt, ts, tk):
    bsz, seq, _ = x.shape
    assert ts % tk == 0
    return pl.pallas_call(
        functools.partial(_in_proj_kernel, tk=tk),
        grid=(bsz, seq // ts),
        in_specs=[
            pl.BlockSpec((1, ts, D_MODEL), lambda b, i: (b, i, 0)),
            pl.BlockSpec((1, N_MOD, D_MODEL), lambda b, i: (b, 0, 0)),
            pl.BlockSpec((1, D_MODEL), lambda b, i: (0, 0)),
            pl.BlockSpec((D_MODEL, MAIN_WIDTH), lambda b, i: (0, 0)),
            pl.BlockSpec((VT_ROWS, D_MODEL), lambda b, i: (0, 0)),
        ],
        out_specs=[
            pl.BlockSpec((1, ts, MAIN_WIDTH), lambda b, i: (b, i, 0)),
            pl.BlockSpec((1, VT_ROWS, ts), lambda b, i: (b, 0, i)),
        ],
        out_shape=[
            jax.ShapeDtypeStruct((bsz, seq, MAIN_WIDTH), jnp.bfloat16),
            jax.ShapeDtypeStruct((bsz, VT_ROWS, seq), jnp.bfloat16),
        ],
        scratch_shapes=[pltpu.VMEM((D_MODEL // LANES, ts, LANES), jnp.float32)],
        compiler_params=_cparams(("arbitrary", "arbitrary")),
        name="in_proj",
    )(x, mod3, g_pre, w_main, w_vt)


def _suffix_excl_prod8(tot):
    sub = lax.broadcasted_iota(jnp.int32, tot.shape, 0)
    x = jnp.where(sub < SUBLANES - 1, pltpu.roll(tot, SUBLANES - 1, 0), 1.0)
    for sh in (1, 2, 4):
        x = x * jnp.where(sub + sh < SUBLANES, pltpu.roll(x, SUBLANES - sh, 0), 1.0)
    return x


def _sb_scores(k_ref, q_heads, s_ref, slot, j, tk):
    rows = pl.ds(pl.multiple_of(j * tk, tk), tk)
    for h, q_h in enumerate(q_heads):
        kb = k_ref[0, rows, (h // 2) * LANES:(h // 2 + 1) * LANES]
        s_ref[slot, h] = lax.dot_general(kb, q_h, (((1,), (1,)), ((), ())),
                                         preferred_element_type=jnp.float32)


def _sb_weights(zt, c8, ok, groups):
    tq = zt.shape[1]
    r = 0.5 - 0.5 * jnp.tanh(zt)
    if ok is not None:
        r = jnp.where(ok, r, 1.0)
    rg = [r[g * SUBLANES:(g + 1) * SUBLANES, :] for g in range(groups)]
    tot = rg[0]
    for g in range(1, groups):
        tot = tot * rg[g]
    p = c8 * _suffix_excl_prod8(tot)
    pieces = [None] * groups
    for g in range(groups - 1, -1, -1):
        pn = p * rg[g]
        pieces[g] = p - pn
        p = pn
    a = jnp.concatenate(pieces, axis=0).astype(jnp.bfloat16)
    return a, jnp.broadcast_to(p[0:1, :], (SUBLANES, tq))


def _sb_attn_kernel(q_ref, k_ref, v_ref, o_ref, acc_ref, c_ref, s_ref, ok_ref, *, tq, tk):
    i = pl.program_id(2)
    groups = tk // SUBLANES
    q_heads = []
    for pair_idx in range(SB_HEADS_PER_STEP // 2):
        q2 = q_ref[0, :, pair_idx * LANES:(pair_idx + 1) * LANES]
        lane = lax.broadcasted_iota(jnp.int32, q2.shape, 1)
        zero = jnp.zeros_like(q2)
        q_heads += [jnp.where(lane < SB_HEAD_DIM, q2, zero), jnp.where(lane < SB_HEAD_DIM, zero, q2)]

    def step(j, slot, masked):
        _sb_scores(k_ref, q_heads, s_ref, 1 - slot, jnp.maximum(j - 1, 0), tk)
        ok = (ok_ref[...] > 0.5) if masked else None
        off = pl.multiple_of(j * tk, tk)
        ws = []
        for h in range(SB_HEADS_PER_STEP):
            a, c_new = _sb_weights(s_ref[slot, h], c_ref[h], ok, groups)
            c_ref[h] = c_new
            ws.append(a)
        for h in range(SB_HEADS_PER_STEP):
            vt_h = v_ref[0, h * SB_HEAD_DIM:(h + 1) * SB_HEAD_DIM, pl.ds(off, tk)]
            acc_ref[h] += jnp.dot(vt_h, ws[h], preferred_element_type=jnp.float32)

    @pl.when(i == 0)
    def _():
        row = lax.broadcasted_iota(jnp.int32, (tk, tq), 0)
        col = lax.broadcasted_iota(jnp.int32, (tk, tq), 1)
        ok_ref[...] = jnp.where((row % SUBLANES) * groups + row // SUBLANES < col, 1.0, 0.0)

    acc_ref[...] = jnp.zeros_like(acc_ref)
    c_ref[...] = jnp.ones_like(c_ref)
    _sb_scores(k_ref, q_heads, s_ref, 0, i, tk)
    step(i, 0, True)
    c_ref[...] = c_ref[...] * jnp.where(i >= 1, 1.0, 0.0)
    step(jnp.maximum(i - 1, 0), 1, False)

    def stick_left():
        return jnp.max(c_ref[...]) > 0.0

    n_rest = jnp.maximum(i - 1, 0)

    def more(state):
        m, go = state
        return jnp.logical_and(m < n_rest // 2, go)

    def pair(state):
        m, _ = state
        j = i - 2 - 2 * m
        step(j, 0, False)
        go_on = stick_left()

        @pl.when(go_on)
        def _():
            step(j - 1, 1, False)

        return m + 1, jnp.logical_and(go_on, stick_left())

    m_done, go = lax.while_loop(more, pair, (jnp.int32(0), stick_left()))

    @pl.when(jnp.logical_and(jnp.logical_and(n_rest % 2 == 1, m_done == n_rest // 2), go))
    def _():
        step(0, 0, False)

    ot = jnp.concatenate([acc_ref[h] for h in range(SB_HEADS_PER_STEP)], axis=0)
    o_ref[0] = ot.astype(jnp.bfloat16)


def _sb_attn(main, vt, tq, tk):
    bsz, seq, _ = main.shape
    assert tq == tk
    width = SB_HEADS_PER_STEP * SB_HEAD_DIM
    k_blk0 = SB_WIDTH // width
    kern = functools.partial(_sb_attn_kernel, tq=tq, tk=tk)
    return pl.pallas_call(
        kern,
        grid=(bsz, SB_WIDTH // width, seq // tq),
        in_specs=[
            pl.BlockSpec((1, tq, width), lambda b, p, i: (b, i, p)),
            pl.BlockSpec((1, seq, width), lambda b, p, i: (b, 0, k_blk0 + p)),
            pl.BlockSpec((1, width, seq), lambda b, p, i: (b, p, 0)),
        ],
        out_specs=pl.BlockSpec((1, width, tq), lambda b, p, i: (b, p, i)),
        out_shape=jax.ShapeDtypeStruct((bsz, SB_WIDTH, seq), jnp.bfloat16),
        scratch_shapes=[
            pltpu.VMEM((SB_HEADS_PER_STEP, SB_HEAD_DIM, tq), jnp.float32),
            pltpu.VMEM((SB_HEADS_PER_STEP, SUBLANES, tq), jnp.float32),
            pltpu.VMEM((2, SB_HEADS_PER_STEP, tk, tq), jnp.float32),
            pltpu.VMEM((tk, tq), jnp.float32),
        ],
        compiler_params=_cparams(("arbitrary", "arbitrary", "arbitrary")),
        name="sb_attn",
    )(main, main, vt)


def _diff_attn_kernel(slopes_ref, inv_slopes_ref, q_ref, k_ref, v_ref, lam_ref, g_ref, o_ref,
                      acc_ref, m_ref, l_ref, s_ref, smax_ref, kn_ref, bias_ref, *, tq, tk):
    pair = pl.program_id(1)
    i = pl.program_id(2)
    heads = range(DIFF_HEADS_PER_STEP)
    slopes = [slopes_ref[DIFF_HEADS_PER_STEP * pair + e] for e in heads]

    @pl.when(i == 0)
    def _():
        row0 = lax.broadcasted_iota(jnp.int32, (tk, tq), 0)
        col0 = lax.broadcasted_iota(jnp.int32, (tk, tq), 1)
        for e in heads:
            kf = k_ref[0, :, e * LANES:(e + 1) * LANES].astype(jnp.float32)
            kn2 = jnp.max(jnp.sum(kf * kf, axis=-1, keepdims=True), axis=0, keepdims=True)
            kn_ref[e] = jnp.broadcast_to(kn2, kn_ref.shape[1:])
            bias_ref[e] = slopes[e] * (row0 - col0).astype(jnp.float32)

    q_tiles, q_maps = [], []
    for e in heads:
        q2 = q_ref[0, :, e * LANES:(e + 1) * LANES]
        lane = lax.broadcasted_iota(jnp.int32, q2.shape, 1)
        zero = jnp.zeros_like(q2)
        q_tiles.append(q2)
        q_maps.append((jnp.where(lane < DIFF_HEAD_DIM, q2, zero),
                       jnp.where(lane < DIFF_HEAD_DIM, zero, q2)))

    def scores(e, slot, j, masked, dead=None):
        kb = k_ref[0, pl.ds(pl.multiple_of(j * tk, tk), tk), e * LANES:(e + 1) * LANES]
        for m in range(2):
            s = lax.dot_general(kb, q_maps[e][m], (((1,), (1,)), ((), ())),
                                preferred_element_type=jnp.float32) + bias_ref[e]
            if masked:
                row = lax.broadcasted_iota(jnp.int32, (tk, tq), 0)
                col = lax.broadcasted_iota(jnp.int32, (tk, tq), 1)
                s = jnp.where(row <= col, s, NEG_BIG)
            if dead is not None:
                s = jnp.where(dead, NEG_BIG, s)
            s_ref[slot, 2 * e + m] = s
            smax_ref[slot, 2 * e + m] = jnp.max(s, axis=0, keepdims=True)

    def probs(e, j, slot):
        cb = slopes[e] * ((j - i) * tk).astype(jnp.float32)
        ps, alphas = [], []
        for m in range(2):
            idx = 2 * e + m
            m_old = m_ref[idx]
            m_new = jnp.maximum(m_old, smax_ref[slot, idx] + cb)
            alpha = jnp.exp(m_old - m_new)
            p = jnp.exp(s_ref[slot, idx] - (m_new - cb))
            l_ref[idx] = alpha * l_ref[idx] + jnp.sum(p, axis=0, keepdims=True)
            m_ref[idx] = m_new
            ps.append(p.astype(jnp.bfloat16))
            alphas.append(alpha)
        return ps, alphas

    def values(e, j, ps, alphas):
        vtb = v_ref[0, e * DIFF_V_DIM:(e + 1) * DIFF_V_DIM, pl.ds(pl.multiple_of(j * tk, tk), tk)]
        for m in range(2):
            idx = 2 * e + m
            acc_ref[idx] = alphas[m] * acc_ref[idx] + jnp.dot(
                vtb, ps[m], preferred_element_type=jnp.float32)

    def step(es, j, slot, next_dead=None):
        for e in es:
            scores(e, 1 - slot, jnp.maximum(j - 1, 0), False, next_dead)
        pa = [probs(e, j, slot) for e in es]
        for e, (ps, alphas) in zip(es, pa):
            values(e, j, ps, alphas)

    acc_ref[...] = jnp.zeros_like(acc_ref)
    m_ref[...] = jnp.full_like(m_ref, NEG_BIG)
    l_ref[...] = jnp.zeros_like(l_ref)
    for e in heads:
        scores(e, 0, i, True)
    step(heads, i, 0, next_dead=(i == 0))
    step(heads, jnp.maximum(i - 1, 0), 1)

    for e in heads:
        qf = q_tiles[e].astype(jnp.float32)
        qn2 = jnp.max(jnp.sum(qf * qf, axis=-1, keepdims=True), axis=0, keepdims=True)
        zabs = jnp.sqrt(qn2 * kn_ref[e, 0:1, 0:1]) * NORM_SLACK
        m_lo = jnp.min(jnp.minimum(m_ref[2 * e], m_ref[2 * e + 1]), axis=1, keepdims=True)
        reach = (EXP_ZERO_MARGIN + zabs - m_lo) * inv_slopes_ref[DIFF_HEADS_PER_STEP * pair + e]
        n_need = jnp.floor(jnp.clip((reach - 1.0) * (1.0 / tk), -1.0, 1e6)) + 1.0
        n_back = jnp.maximum(jnp.minimum(i, jnp.max(n_need).astype(jnp.int32)) - 1, 0)

        def quad(n, carry, e=e):
            j = i - 2 - 4 * n
            step([e], j, 0)
            step([e], j - 1, 1)
            step([e], j - 2, 0)
            step([e], j - 3, 1)
            return carry

        n_quads = n_back // 4
        lax.fori_loop(0, n_quads, quad, 0)
        rest = n_back - 4 * n_quads

        @pl.when(rest >= 2)
        def _(e=e, n_quads=n_quads):
            j = i - 2 - 4 * n_quads
            step([e], j, 0)
            step([e], j - 1, 1)

        @pl.when(rest % 2 == 1)
        def _(e=e, n_back=n_back):
            step([e], i - 1 - n_back, 0)

    lam = lam_ref[0:1, 0:1]
    outs = []
    for e in heads:
        o = acc_ref[2 * e] / l_ref[2 * e] - lam * (acc_ref[2 * e + 1] / l_ref[2 * e + 1])
        ms = jnp.mean(o * o, axis=0, keepdims=True)
        outs.append(o * lax.rsqrt(ms + RMS_EPS) * g_ref[...] * (1.0 - LAM_INIT))
    o_ref[0] = jnp.concatenate(outs, axis=0).astype(jnp.bfloat16)


def _diff_attn(main, vt, slopes, lam, g_col, tq, tk):
    bsz, seq, _ = main.shape
    assert tq == tk
    kern = functools.partial(_diff_attn_kernel, tq=tq, tk=tk)
    n_maps = 2 * DIFF_HEADS_PER_STEP
    qk_width = DIFF_HEADS_PER_STEP * 2 * DIFF_HEAD_DIM
    v_width = DIFF_HEADS_PER_STEP * DIFF_V_DIM
    q_blk0 = COLBLK_Q_DF * LANES // qk_width
    k_blk0 = COLBLK_K_DF * LANES // qk_width
    v_blk0 = SB_WIDTH // v_width
    return pl.pallas_call(
        kern,
        grid=(bsz, DIFF_HEADS // DIFF_HEADS_PER_STEP, seq // tq),
        in_specs=[
            pl.BlockSpec(memory_space=pltpu.SMEM),
            pl.BlockSpec(memory_space=pltpu.SMEM),
            pl.BlockSpec((1, tq, qk_width), lambda b, h, i: (b, i, q_blk0 + h)),
            pl.BlockSpec((1, seq, qk_width), lambda b, h, i: (b, 0, k_blk0 + h)),
            pl.BlockSpec((1, v_width, seq), lambda b, h, i: (b, v_blk0 + h, 0)),
            pl.BlockSpec((SUBLANES, LANES), lambda b, h, i: (0, 0)),
            pl.BlockSpec((DIFF_V_DIM, 1), lambda b, h, i: (0, 0)),
        ],
        out_specs=pl.BlockSpec((1, v_width, tq), lambda b, h, i: (b, h, i)),
        out_shape=jax.ShapeDtypeStruct((bsz, DIFF_V_WIDTH, seq), jnp.bfloat16),
        scratch_shapes=[
            pltpu.VMEM((n_maps, DIFF_V_DIM, tq), jnp.float32),
            pltpu.VMEM((n_maps, 1, tq), jnp.float32),
            pltpu.VMEM((n_maps, 1, tq), jnp.float32),
            pltpu.VMEM((2, n_maps, tk, tq), jnp.float32),
            pltpu.VMEM((2, n_maps, 1, tq), jnp.float32),
            pltpu.VMEM((DIFF_HEADS_PER_STEP, SUBLANES, LANES), jnp.float32),
            pltpu.VMEM((DIFF_HEADS_PER_STEP, tk, tq), jnp.float32),
        ],
        compiler_params=_cparams(("arbitrary", "arbitrary", "arbitrary")),
        name="diff_attn",
    )(slopes, 1.0 / slopes, main, main, vt, lam, g_col)


def _pack_bf16_pair(a, b):
    ab = pltpu.bitcast(a.astype(jnp.bfloat16).astype(jnp.float32), jnp.uint32)
    bb = pltpu.bitcast(b.astype(jnp.bfloat16).astype(jnp.float32), jnp.uint32)
    return ab | (bb >> 16)


def _unpack_bf16_pair(w):
    hi = pltpu.bitcast(w & jnp.uint32(0xFFFF0000), jnp.float32)
    lo = pltpu.bitcast(w << 16, jnp.float32)
    return jnp.concatenate([hi, lo], axis=1)


def _merge_router_kernel(ysb_ref, ydf_ref, gates_ref, x_ref, mod_ref, wsb_ref, wdf_ref, wout_ref,
                         gpost_ref, gpre_ref, wrh_ref, wrl_ref, br_ref,
                         x1_ref, h2_ref, idx_ref, wgt_ref, rank_ref, cnt_ref, base_ref):
    first = jnp.logical_and(pl.program_id(0) == 0, pl.program_id(1) == 0)

    @pl.when(first)
    def _():
        base_ref[...] = jnp.zeros_like(base_ref)

    mod = mod_ref[0]
    subs = [slice(s * MERGE_SUB, (s + 1) * MERGE_SUB) for s in range(x_ref.shape[1] // MERGE_SUB)]
    half = D_MODEL // 2
    branch = [(jnp.dot(ysb_ref[0, :, rows].T, wsb_ref[...], preferred_element_type=jnp.float32),
               jnp.dot(ydf_ref[0, :, rows].T, wdf_ref[...], preferred_element_type=jnp.float32))
              for rows in subs]
    merged = []
    for rows, (a, b) in zip(subs, branch):
        g = gates_ref[0, rows, :].astype(jnp.float32)
        merged.append((g[:, :D_MODEL] * a + g[:, D_MODEL:] * b).astype(jnp.bfloat16))
    mixes = [jnp.dot(m, wout_ref[...], preferred_element_type=jnp.float32) for m in merged]
    h2s = []
    for rows, mix in zip(subs, mixes):
        x1 = x_ref[0, rows, :] + mod[2:3] * (_rms(mix) * gpost_ref[...])
        x1_ref[0, rows, :] = x1
        h2 = _rms(x1) * gpre_ref[...]
        h2 = h2 * (1.0 + mod[4:5]) + mod[3:4]
        h2_ref[0, rows, :] = _pack_bf16_pair(h2[:, :half], h2[:, half:])
        h2s.append(h2)
    logit_list = []
    for h2 in h2s:
        hh = h2.astype(jnp.bfloat16)
        hl = (h2 - hh.astype(jnp.float32)).astype(jnp.bfloat16)
        logit_list.append(jnp.dot(hh, wrh_ref[...], preferred_element_type=jnp.float32)
                          + jnp.dot(hh, wrl_ref[...], preferred_element_type=jnp.float32)
                          + jnp.dot(hl, wrh_ref[...], preferred_element_type=jnp.float32)
                          + br_ref[...])
    for rows, logits in zip(subs, logit_list):
        _route_rows(rows, logits, idx_ref, wgt_ref, rank_ref, cnt_ref, base_ref)


def _route_rows(rows, logits, idx_ref, wgt_ref, rank_ref, cnt_ref, base_ref):
    lane = lax.broadcasted_iota(jnp.int32, logits.shape, 1)
    lanef = lane.astype(jnp.float32)
    vals, idxs = [], []
    cur = logits
    for _ in range(TOP_K):
        mx = jnp.max(cur, axis=-1, keepdims=True)
        ix = jnp.min(jnp.where(cur == mx, lanef, float(LANES)), axis=-1, keepdims=True)
        cur = jnp.where(lanef == ix, -jnp.inf, cur)
        vals.append(mx)
        idxs.append(ix)
    es = [jnp.exp(v - vals[0]) for v in vals]
    den = es[0] + es[1] + es[2] + es[3]
    oi = jnp.zeros(logits.shape, jnp.float32)
    ow = jnp.zeros(logits.shape, jnp.float32)
    for k in range(TOP_K):
        oi = jnp.where(lane == k, idxs[k], oi)
        ow = jnp.where(lane == k, es[k] / den, ow)
    idx_ref[0, rows, :] = oi.astype(jnp.int32)
    wgt_ref[0, rows, :] = ow

    ts = logits.shape[0]
    member = jnp.zeros(logits.shape, jnp.float32)
    for k in range(TOP_K):
        member = member + (lanef == idxs[k]).astype(jnp.float32)
    rr = lax.broadcasted_iota(jnp.int32, (ts, ts), 0)
    cc = lax.broadcasted_iota(jnp.int32, (ts, ts), 1)
    lower = jnp.where(cc < rr, 1.0, 0.0).astype(jnp.bfloat16)
    before = jnp.dot(lower, member.astype(jnp.bfloat16), preferred_element_type=jnp.float32)
    base = base_ref[0:1, :]
    rank_all = before + base
    orank = jnp.zeros(logits.shape, jnp.float32)
    for k in range(TOP_K):
        rk = jnp.sum(jnp.where(lanef == idxs[k], rank_all, 0.0), axis=-1, keepdims=True)
        orank = jnp.where(lane == k, rk, orank)
    rank_ref[0, rows, :] = orank.astype(jnp.int32)
    new_base = base + jnp.sum(member, axis=0, keepdims=True)
    base_ref[...] = jnp.broadcast_to(new_base, base_ref.shape)
    cnt_ref[...] = jnp.broadcast_to(new_base, cnt_ref.shape).astype(jnp.int32)


def _merge_router(ysb, ydf, main, x, mod3, wsb, wdf, wout, gpost, gpre, wrh, wrl, br, ts):
    bsz, seq, _ = x.shape
    const = lambda b, i: (0, 0)
    return pl.pallas_call(
        _merge_router_kernel,
        grid=(bsz, seq // ts),
        in_specs=[
            pl.BlockSpec((1, SB_WIDTH, ts), lambda b, i: (b, 0, i)),
            pl.BlockSpec((1, DIFF_V_WIDTH, ts), lambda b, i: (b, 0, i)),
            pl.BlockSpec((1, ts, 2 * D_MODEL), lambda b, i: (b, i, GATE_COL0 // (2 * D_MODEL))),
            pl.BlockSpec((1, ts, D_MODEL), lambda b, i: (b, i, 0)),
            pl.BlockSpec((1, N_MOD, D_MODEL), lambda b, i: (b, 0, 0)),
            pl.BlockSpec((SB_WIDTH, D_MODEL), const),
            pl.BlockSpec((DIFF_V_WIDTH, D_MODEL), const),
            pl.BlockSpec((D_MODEL, D_MODEL), const),
            pl.BlockSpec((1, D_MODEL), const),
            pl.BlockSpec((1, D_MODEL), const),
            pl.BlockSpec((D_MODEL, LANES), const),
            pl.BlockSpec((D_MODEL, LANES), const),
            pl.BlockSpec((1, LANES), const),
        ],
        out_specs=[
            pl.BlockSpec((1, ts, D_MODEL), lambda b, i: (b, i, 0)),
            pl.BlockSpec((1, ts, D_MODEL // 2), lambda b, i: (b, i, 0)),
            pl.BlockSpec((1, ts, LANES), lambda b, i: (b, i, 0)),
            pl.BlockSpec((1, ts, LANES), lambda b, i: (b, i, 0)),
            pl.BlockSpec((1, ts, LANES), lambda b, i: (b, i, 0)),
            pl.BlockSpec((SUBLANES, LANES), const),
        ],
        out_shape=[
            jax.ShapeDtypeStruct((bsz, seq, D_MODEL), jnp.float32),
            jax.ShapeDtypeStruct((bsz, seq, D_MODEL // 2), jnp.uint32),
            jax.ShapeDtypeStruct((bsz, seq, LANES), jnp.int32),
            jax.ShapeDtypeStruct((bsz, seq, LANES), jnp.float32),
            jax.ShapeDtypeStruct((bsz, seq, LANES), jnp.int32),
            jax.ShapeDtypeStruct((SUBLANES, LANES), jnp.int32),
        ],
        scratch_shapes=[pltpu.VMEM((SUBLANES, LANES), jnp.float32)],
        compiler_params=_cparams(("arbitrary", "arbitrary")),
        name="merge_router",
    )(ysb, ydf, main, x, mod3, wsb, wdf, wout, gpost, gpre, wrh, wrl, br)


def _sc_gather_rows(table, idx):
    n = idx.shape[0]
    width = table.shape[1]
    n_workers = SC_CORES * SC_SUBCORES
    per_worker = n // n_workers
    n_chunks = per_worker // SC_GATHER_ROWS
    assert n_chunks * SC_GATHER_ROWS * n_workers == n
    mesh = plsc.VectorSubcoreMesh(core_axis_name="c", subcore_axis_name="s",
                                  num_cores=SC_CORES, num_subcores=SC_SUBCORES)

    def body(table_hbm, idx_hbm, out_hbm, idx_v, rows_v, sem):
        wid = lax.axis_index("s") * SC_CORES + lax.axis_index("c")
        base = wid * per_worker

        @pl.loop(0, n_chunks)
        def _(ci):
            off = pl.multiple_of(base + ci * SC_GATHER_ROWS, SC_GATHER_ROWS)
            pltpu.sync_copy(idx_hbm.at[pl.ds(off, SC_GATHER_ROWS)], idx_v)
            pltpu.async_copy(table_hbm.at[idx_v], rows_v, sem).wait()
            pltpu.sync_copy(rows_v, out_hbm.at[pl.ds(off, SC_GATHER_ROWS)])

    return pl.kernel(
        body,
        out_type=jax.ShapeDtypeStruct((n, width), table.dtype),
        mesh=mesh,
        scratch_types=[
            pltpu.VMEM((SC_GATHER_ROWS,), jnp.int32),
            pltpu.VMEM((SC_GATHER_ROWS, width), table.dtype),
            pltpu.SemaphoreType.DMA,
        ],
        name="sc_gather_rows",
    )(table, idx)


def _sc_scatter_rows(src, pos_kmajor, n_rows):
    n_tok, width = src.shape
    n_workers = SC_CORES * SC_SUBCORES
    per_worker = n_tok // n_workers
    n_chunks = per_worker // SC_GATHER_ROWS
    assert n_chunks * SC_GATHER_ROWS * n_workers == n_tok
    mesh = plsc.VectorSubcoreMesh(core_axis_name="c", subcore_axis_name="s",
                                  num_cores=SC_CORES, num_subcores=SC_SUBCORES)

    def body(src_hbm, idx_hbm, out_hbm, idx_v, rows_v):
        wid = lax.axis_index("s") * SC_CORES + lax.axis_index("c")
        base = wid * per_worker

        @pl.loop(0, n_chunks)
        def _(ci):
            off = pl.multiple_of(base + ci * SC_GATHER_ROWS, SC_GATHER_ROWS)
            pltpu.sync_copy(src_hbm.at[pl.ds(off, SC_GATHER_ROWS)], rows_v)
            for k in range(TOP_K):
                koff = pl.multiple_of(k * n_tok + off, SC_GATHER_ROWS)
                pltpu.sync_copy(idx_hbm.at[pl.ds(koff, SC_GATHER_ROWS)], idx_v)
                pltpu.sync_copy(rows_v, out_hbm.at[idx_v])

    return pl.kernel(
        body,
        out_type=jax.ShapeDtypeStruct((n_rows, width), src.dtype),
        mesh=mesh,
        scratch_types=[
            pltpu.VMEM((SC_GATHER_ROWS,), jnp.int32),
            pltpu.VMEM((SC_GATHER_ROWS, width), src.dtype),
        ],
        name="sc_scatter_rows",
    )(src, pos_kmajor)


def _moe_ffn_kernel(te_ref, nt_ref, x_ref, wgu_ref, bgu_ref, wd_ref, bd_ref, o_ref,
                    wgu_bf, wd_bf):
    i = pl.program_id(0)
    n_valid = nt_ref[0]

    new_expert = jnp.logical_or(i == 0, te_ref[i] != te_ref[jnp.maximum(i - 1, 0)])

    @pl.when(jnp.logical_and(i < n_valid, new_expert))
    def _():
        wgu_bf[...] = wgu_ref[0].astype(jnp.bfloat16)
        wd_bf[...] = wd_ref[0].astype(jnp.bfloat16)

    @pl.when(i < n_valid)
    def _():
        xb = _unpack_bf16_pair(x_ref[...]).astype(jnp.bfloat16)
        gu = jnp.dot(xb, wgu_bf[...], preferred_element_type=jnp.float32) + bgu_ref[0]
        gate = jnp.minimum(gu[:, :D_EXPERT], SWIGLU_LIMIT)
        up = jnp.clip(gu[:, D_EXPERT:], -SWIGLU_LIMIT, SWIGLU_LIMIT)
        act = (up + 1.0) * (gate * jax.nn.sigmoid(SWIGLU_ALPHA * gate))
        out = jnp.dot(act.astype(jnp.bfloat16), wd_bf[...],
                      preferred_element_type=jnp.float32) + bd_ref[0]
        half = D_MODEL // 2
        o_ref[...] = _pack_bf16_pair(out[:, :half], out[:, half:])

    @pl.when(i >= n_valid)
    def _():
        o_ref[...] = jnp.zeros_like(o_ref)


def _moe_ffn(tile_expert, n_valid, xg, wgu, bgu, wd, bd, tm):
    n_tiles = xg.shape[0] // tm
    grid_spec = pltpu.PrefetchScalarGridSpec(
        num_scalar_prefetch=2,
        grid=(n_tiles,),
        in_specs=[
            pl.BlockSpec((tm, D_MODEL // 2), lambda i, te, nt: (jnp.minimum(i, nt[0] - 1), 0)),
            pl.BlockSpec((1, D_MODEL, 2 * D_EXPERT), lambda i, te, nt: (te[i], 0, 0)),
            pl.BlockSpec((1, 1, 2 * D_EXPERT), lambda i, te, nt: (te[i], 0, 0)),
            pl.BlockSpec((1, D_EXPERT, D_MODEL), lambda i, te, nt: (te[i], 0, 0)),
            pl.BlockSpec((1, 1, D_MODEL), lambda i, te, nt: (te[i], 0, 0)),
        ],
        out_specs=pl.BlockSpec((tm, D_MODEL // 2), lambda i, te, nt: (i, 0)),
        scratch_shapes=[
            pltpu.VMEM((D_MODEL, 2 * D_EXPERT), jnp.bfloat16),
            pltpu.VMEM((D_EXPERT, D_MODEL), jnp.bfloat16),
        ],
    )
    return pl.pallas_call(
        _moe_ffn_kernel,
        grid_spec=grid_spec,
        out_shape=jax.ShapeDtypeStruct((n_tiles * tm, D_MODEL // 2), jnp.uint32),
        compiler_params=_cparams(("arbitrary",)),
        name="moe_ffn",
    )(tile_expert, n_valid, xg, wgu, bgu.reshape(N_EXPERTS, 1, -1), wd,
      bd.reshape(N_EXPERTS, 1, -1))


def _moe_combine_kernel(rows_ref, wgt_ref, x1_ref, mod_ref, g_ref, o_ref):
    ts = x1_ref.shape[1]
    w = wgt_ref[0]
    y = jnp.zeros(x1_ref.shape[1:], jnp.float32)
    for k in range(TOP_K):
        y = y + w[:, k:k + 1] * _unpack_bf16_pair(rows_ref[k * ts:(k + 1) * ts, :])
    mod = mod_ref[0]
    o_ref[0] = x1_ref[0] + mod[5:6] * (_rms(y) * g_ref[...])


def _moe_combine(rows, wgt, x1, mod3, g_post, ts):
    bsz, seq, _ = x1.shape
    per_b = seq // ts
    return pl.pallas_call(
        _moe_combine_kernel,
        grid=(bsz, per_b),
        in_specs=[
            pl.BlockSpec((TOP_K * ts, D_MODEL // 2), lambda b, i: (b * per_b + i, 0)),
            pl.BlockSpec((1, ts, LANES), lambda b, i: (b, i, 0)),
            pl.BlockSpec((1, ts, D_MODEL), lambda b, i: (b, i, 0)),
            pl.BlockSpec((1, N_MOD, D_MODEL), lambda b, i: (b, 0, 0)),
            pl.BlockSpec((1, D_MODEL), lambda b, i: (0, 0)),
        ],
        out_specs=pl.BlockSpec((1, ts, D_MODEL), lambda b, i: (b, i, 0)),
        out_shape=jax.ShapeDtypeStruct((bsz, seq, D_MODEL), jnp.float32),
        compiler_params=_cparams(("arbitrary", "arbitrary")),
        name="moe_combine",
    )(rows, wgt, x1, mod3, g_post)


def _routing(top_idx, rank, counts, tm, n_tiles):
    padded = ((counts + tm - 1) // tm) * tm
    pend = jnp.cumsum(padded)
    pstart = pend - padded
    onehot = top_idx[:, :, None] == jnp.arange(N_EXPERTS, dtype=jnp.int32)[None, None, :]
    pos = rank + jnp.sum(jnp.where(onehot, pstart[None, None, :], 0), axis=-1)
    n_valid = (pend[-1] // tm).astype(jnp.int32)
    tile_row0 = jnp.arange(n_tiles, dtype=jnp.int32) * tm
    tile_expert = jnp.minimum(
        jnp.sum((tile_row0[:, None] >= pend[None, :]).astype(jnp.int32), axis=1), N_EXPERTS - 1)
    last_oh = jnp.arange(n_tiles, dtype=jnp.int32) == jnp.maximum(n_valid - 1, 0)
    last_expert = jnp.sum(jnp.where(last_oh, tile_expert, 0))
    tile_expert = jnp.where(jnp.arange(n_tiles, dtype=jnp.int32) < n_valid, tile_expert, last_expert)
    return pos.astype(jnp.int32), tile_expert.astype(jnp.int32), n_valid.reshape(1)


def _alibi_slopes(n_heads):
    return 2.0 ** (-8.0 * jnp.arange(1, n_heads + 1, dtype=jnp.float32) / n_heads)


def _layer(x, c, w_mod, b_mod, g_pre_mix, g_post_mix, w_in, lamv, g_subln, w_branch_sb,
           w_branch_diff, w_out, g_pre_ffn, g_post_ffn, w_router, b_router, w_gate_up,
           b_gate_up, w_down, b_down, tiles):
    ts_in, tq, ts_merge, tm, ts_comb = tiles
    bsz, seq, d = x.shape
    n_tok = bsz * seq
    bf = jnp.bfloat16

    mod, lam = _mod_proj(c, w_mod, b_mod, lamv)
    mod3 = mod.reshape(bsz, N_MOD, d)

    o_vsb = 2 * SB_WIDTH
    o_qdf = 3 * SB_WIDTH
    o_vdf = o_qdf + 2 * DIFF_QK_WIDTH
    o_g = o_vdf + DIFF_V_WIDTH
    w_main = jnp.concatenate([w_in[:, :o_vsb], w_in[:, o_qdf:o_vdf], w_in[:, o_g:]], axis=1).astype(bf)
    w_vt = jnp.concatenate([w_in[:, o_vsb:o_qdf], w_in[:, o_vdf:o_g]], axis=1).T.astype(bf)

    main, vt = _in_proj(x, mod3, g_pre_mix.reshape(1, d), w_main, w_vt, ts_in, tq)
    y_sb = _sb_attn(main, vt, tq, tq)
    y_df = _diff_attn(main, vt, _alibi_slopes(DIFF_HEADS), lam,
                      g_subln.reshape(DIFF_V_DIM, 1), tq, tq)

    wr = jnp.zeros((d, LANES), jnp.float32).at[:, :N_EXPERTS].set(w_router)
    wrh = wr.astype(bf)
    wrl = (wr - wrh.astype(jnp.float32)).astype(bf)
    br = jnp.full((1, LANES), NEG_BIG, jnp.float32).at[0, :N_EXPERTS].set(b_router)
    x1, h2p, top_idx, top_w, rank, counts = _merge_router(
        y_sb, y_df, main, x, mod3, w_branch_sb.astype(bf), w_branch_diff.astype(bf),
        w_out.astype(bf), g_post_mix.reshape(1, d), g_pre_ffn.reshape(1, d), wrh, wrl, br, ts_merge)

    n_tiles = (n_tok * TOP_K) // tm + N_EXPERTS
    pos, tile_expert, n_valid = _routing(
        top_idx.reshape(n_tok, LANES)[:, :TOP_K], rank.reshape(n_tok, LANES)[:, :TOP_K],
        counts[0, :N_EXPERTS], tm, n_tiles)
    xg = _sc_scatter_rows(h2p.reshape(n_tok, d // 2), pos.T.reshape(TOP_K * n_tok), n_tiles * tm)
    rows = _moe_ffn(tile_expert, n_valid, xg, w_gate_up, b_gate_up, w_down, b_down, tm)
    pos_steps = pos.reshape(n_tok // ts_comb, ts_comb, TOP_K).swapaxes(1, 2).reshape(n_tok * TOP_K)
    tok_rows = _sc_gather_rows(rows, pos_steps)
    return _moe_combine(tok_rows, top_w, x1, mod3, g_post_ffn.reshape(1, d), ts_comb)


def kernel(x, c, w_mod, b_mod, g_pre_mix, g_post_mix, w_in, lambda_q1, lambda_k1, lambda_q2,
           lambda_k2, g_subln, w_branch_sb, w_branch_diff, w_out, g_pre_ffn, g_post_ffn,
           w_router, b_router, w_gate_up, b_gate_up, w_down, b_down):
    depth = w_mod.shape[0]
    assert depth == 1, "LAM_INIT is the layer-0 value"
    for l in range(depth):
        lamv = jnp.stack([lambda_q1[l], lambda_k1[l], lambda_q2[l], lambda_k2[l]])
        x = _layer(x, c, w_mod[l], b_mod[l], g_pre_mix[l], g_post_mix[l], w_in[l], lamv,
                   g_subln[l], w_branch_sb[l], w_branch_diff[l], w_out[l], g_pre_ffn[l],
                   g_post_ffn[l], w_router[l], b_router[l], w_gate_up[l], b_gate_up[l],
                   w_down[l], b_down[l], _Tiles())
    return x
```

```python
import functools
import math
from typing import NamedTuple

import jax
import jax.numpy as jnp
from jax import lax
from jax.experimental import pallas as pl
from jax.experimental.pallas import tpu as pltpu
from jax.experimental.pallas import tpu_sc as plsc

D_MODEL = 1024
SB_HEADS = 8
SB_HEAD_DIM = 64
SB_WIDTH = SB_HEADS * SB_HEAD_DIM
DIFF_HEADS = 4
DIFF_HEAD_DIM = 64
DIFF_V_DIM = 2 * DIFF_HEAD_DIM
DIFF_QK_WIDTH = DIFF_HEADS * 2 * DIFF_HEAD_DIM
DIFF_V_WIDTH = DIFF_HEADS * DIFF_V_DIM
N_EXPERTS = 32
TOP_K = 4
D_EXPERT = D_MODEL
SWIGLU_LIMIT = 7.0
SWIGLU_ALPHA = 1.702
RMS_EPS = 1e-6
N_MOD = 6
LAM_INIT = 0.8 - 0.6 * math.exp(-0.3 * 0)

LANES = 128
SUBLANES = 8
NEG_BIG = -1e30
EXP_ZERO_MARGIN = 110.0
NORM_SLACK = 1.01
SC_CORES = 2
SC_SUBCORES = 16
SC_GATHER_ROWS = 128
MERGE_SUB = 256
SB_HEADS_PER_STEP = 4
DIFF_HEADS_PER_STEP = 4

MAIN_WIDTH = 2 * SB_WIDTH + 2 * DIFF_QK_WIDTH + 2 * D_MODEL
VT_ROWS = SB_WIDTH + DIFF_V_WIDTH
COLBLK_K_SB = SB_WIDTH // LANES
COLBLK_Q_DF = 2 * SB_WIDTH // LANES
COLBLK_K_DF = COLBLK_Q_DF + DIFF_QK_WIDTH // LANES
GATE_COL0 = 2 * SB_WIDTH + 2 * DIFF_QK_WIDTH

V7X_VMEM_BYTES = 64 * 1024 * 1024
VMEM_LIMIT = V7X_VMEM_BYTES - 8 * 1024 * 1024


class _Tiles(NamedTuple):
    in_proj: int = 512
    attn: int = 256
    merge: int = 1024
    expert: int = 512
    combine: int = 256


def _cparams(sem, vmem=VMEM_LIMIT):
    return pltpu.CompilerParams(dimension_semantics=sem, vmem_limit_bytes=vmem)


def _rms(x):
    return x * lax.rsqrt(jnp.mean(x * x, axis=-1, keepdims=True) + RMS_EPS)


def _mod_kernel(c_ref, w_ref, b_ref, lamv_ref, mod_ref, lam_ref):
    c = c_ref[...]
    ca = c * jax.nn.sigmoid(c)
    mod_ref[...] = jnp.dot(ca, w_ref[...], preferred_element_type=jnp.float32,
                           precision=lax.Precision.HIGHEST) + b_ref[...]
    lv = lamv_ref[...]
    s1 = jnp.sum(lv[0:1] * lv[1:2], axis=-1, keepdims=True)
    s2 = jnp.sum(lv[2:3] * lv[3:4], axis=-1, keepdims=True)
    lam = jnp.exp(s1) - jnp.exp(s2) + LAM_INIT
    lam_ref[...] = jnp.broadcast_to(lam, lam_ref.shape)


def _mod_proj(c, w_mod, b_mod, lamv):
    bsz = c.shape[0]
    tn = 1536
    n = w_mod.shape[1]
    return pl.pallas_call(
        _mod_kernel,
        grid=(n // tn,),
        in_specs=[
            pl.BlockSpec((bsz, D_MODEL), lambda j: (0, 0)),
            pl.BlockSpec((D_MODEL, tn), lambda j: (0, j)),
            pl.BlockSpec((1, tn), lambda j: (0, j)),
            pl.BlockSpec((4, DIFF_HEAD_DIM), lambda j: (0, 0)),
        ],
        out_specs=[
            pl.BlockSpec((bsz, tn), lambda j: (0, j)),
            pl.BlockSpec((SUBLANES, LANES), lambda j: (0, 0)),
        ],
        out_shape=[
            jax.ShapeDtypeStruct((bsz, n), jnp.float32),
            jax.ShapeDtypeStruct((SUBLANES, LANES), jnp.float32),
        ],
        compiler_params=_cparams(("arbitrary",)),
        name="mod_proj",
    )(c, w_mod, b_mod.reshape(1, n), lamv)


IN_CHUNK = 1024


def _in_proj_kernel(x_ref, mod_ref, g_ref, wm_ref, wvt_ref, main_ref, vt_ref, h_scr, *, tk):
    x = x_ref[0]
    mod = mod_ref[0]
    h = _rms(x) * g_ref[...]
    h = h * (1.0 + mod[1:2]) + mod[0:1]
    hb = h.astype(jnp.bfloat16)
    groups = tk // SUBLANES
    cols = []
    for ct in range(D_MODEL // LANES):
        h_scr[ct] = h[:, ct * LANES:(ct + 1) * LANES]
        pieces = []
        for blk in range(h.shape[0] // tk):
            for g in range(groups):
                pieces.append(h_scr[ct, pl.ds(blk * tk + g, SUBLANES, stride=groups), :])
        cols.append(jnp.concatenate(pieces, axis=0))
    hpb = jnp.concatenate(cols, axis=1).astype(jnp.bfloat16)

    half = IN_CHUNK // 2
    for ci in range(MAIN_WIDTH // IN_CHUNK):
        c0 = ci * IN_CHUNK
        if c0 == 0:
            q = jnp.dot(hb, wm_ref[:, :half], preferred_element_type=jnp.float32)
            k = jnp.dot(hpb, wm_ref[:, half:IN_CHUNK], preferred_element_type=jnp.float32)
            main_ref[0, :, :half] = (q * 0.0625).astype(jnp.bfloat16)
            main_ref[0, :, half:IN_CHUNK] = k.astype(jnp.bfloat16)
            continue
        p = jnp.dot(hb, wm_ref[:, c0:c0 + IN_CHUNK], preferred_element_type=jnp.float32)
        if c0 < GATE_COL0:
            main_ref[0, :, c0:c0 + half] = (p[:, :half] * 0.125).astype(jnp.bfloat16)
            main_ref[0, :, c0 + half:c0 + IN_CHUNK] = p[:, half:].astype(jnp.bfloat16)
        else:
            main_ref[0, :, c0:c0 + IN_CHUNK] = jax.nn.sigmoid(p).astype(jnp.bfloat16)
    nt = (((1,), (1,)), ((), ()))
    vt_sb = lax.dot_general(wvt_ref[:SB_WIDTH, :], hpb, nt, preferred_element_type=jnp.float32)
    vt_df = lax.dot_general(wvt_ref[SB_WIDTH:, :], hb, nt, preferred_element_type=jnp.float32)
    vt_ref[0, :SB_WIDTH, :] = vt_sb.astype(jnp.bfloat16)
    vt_ref[0, SB_WIDTH:, :] = vt_df.astype(jnp.bfloat16)


def _in_proj(x, mod3, g_pre, w_main, w_vt, ts, tk):
    bsz, seq, _ = x.shape
    assert ts % tk == 0
    return pl.pallas_call(
        functools.partial(_in_proj_kernel, tk=tk),
        grid=(bsz, seq // ts),
        in_specs=[
            pl.BlockSpec((1, ts, D_MODEL), lambda b, i: (b, i, 0)),
            pl.BlockSpec((1, N_MOD, D_MODEL), lambda b, i: (b, 0, 0)),
            pl.BlockSpec((1, D_MODEL), lambda b, i: (0, 0)),
            pl.BlockSpec((D_MODEL, MAIN_WIDTH), lambda b, i: (0, 0)),
            pl.BlockSpec((VT_ROWS, D_MODEL), lambda b, i: (0, 0)),
        ],
        out_specs=[
            pl.BlockSpec((1, ts, MAIN_WIDTH), lambda b, i: (b, i, 0)),
            pl.BlockSpec((1, VT_ROWS, ts), lambda b, i: (b, 0, i)),
        ],
        out_shape=[
            jax.ShapeDtypeStruct((bsz, seq, MAIN_WIDTH), jnp.bfloat16),
            jax.ShapeDtypeStruct((bsz, VT_ROWS, seq), jnp.bfloat16),
        ],
        scratch_shapes=[pltpu.VMEM((D_MODEL // LANES, ts, LANES), jnp.float32)],
        compiler_params=_cparams(("arbitrary", "arbitrary")),
        name="in_proj",
    )(x, mod3, g_pre, w_main, w_vt)


def _suffix_excl_prod8(tot):
    sub = lax.broadcasted_iota(jnp.int32, tot.shape, 0)
    x = jnp.where(sub < SUBLANES - 1, pltpu.roll(tot, SUBLANES - 1, 0), 1.0)
    for sh in (1, 2, 4):
        x = x * jnp.where(sub + sh < SUBLANES, pltpu.roll(x, SUBLANES - sh, 0), 1.0)
    return x


def _sb_scores(k_ref, q_heads, s_ref, slot, j, tk):
    rows = pl.ds(pl.multiple_of(j * tk, tk), tk)
    for h, q_h in enumerate(q_heads):
        kb = k_ref[0, rows, (h // 2) * LANES:(h // 2 + 1) * LANES]
        s_ref[slot, h] = lax.dot_general(kb, q_h, (((1,), (1,)), ((), ())),
                                         preferred_element_type=jnp.float32)


def _sb_weights(zt, c8, ok, groups):
    tq = zt.shape[1]
    r = 0.5 - 0.5 * jnp.tanh(zt)
    if ok is not None:
        r = jnp.where(ok, r, 1.0)
    rg = [r[g * SUBLANES:(g + 1) * SUBLANES, :] for g in range(groups)]
    tot = rg[0]
    for g in range(1, groups):
        tot = tot * rg[g]
    p = c8 * _suffix_excl_prod8(tot)
    pieces = [None] * groups
    for g in range(groups - 1, -1, -1):
        pn = p * rg[g]
        pieces[g] = p - pn
        p = pn
    a = jnp.concatenate(pieces, axis=0).astype(jnp.bfloat16)
    return a, jnp.broadcast_to(p[0:1, :], (SUBLANES, tq))


def _sb_attn_kernel(q_ref, k_ref, v_ref, o_ref, acc_ref, c_ref, s_ref, ok_ref, *, tq, tk):
    i = pl.program_id(2)
    groups = tk // SUBLANES
    q_heads = []
    for pair_idx in range(SB_HEADS_PER_STEP // 2):
        q2 = q_ref[0, :, pair_idx * LANES:(pair_idx + 1) * LANES]
        lane = lax.broadcasted_iota(jnp.int32, q2.shape, 1)
        zero = jnp.zeros_like(q2)
        q_heads += [jnp.where(lane < SB_HEAD_DIM, q2, zero), jnp.where(lane < SB_HEAD_DIM, zero, q2)]

    def step(j, slot, masked):
        _sb_scores(k_ref, q_heads, s_ref, 1 - slot, jnp.maximum(j - 1, 0), tk)
        ok = (ok_ref[...] > 0.5) if masked else None
        off = pl.multiple_of(j * tk, tk)
        ws = []
        for h in range(SB_HEADS_PER_STEP):
            a, c_new = _sb_weights(s_ref[slot, h], c_ref[h], ok, groups)
            c_ref[h] = c_new
            ws.append(a)
        for h in range(SB_HEADS_PER_STEP):
            vt_h = v_ref[0, h * SB_HEAD_DIM:(h + 1) * SB_HEAD_DIM, pl.ds(off, tk)]
            acc_ref[h] += jnp.dot(vt_h, ws[h], preferred_element_type=jnp.float32)

    @pl.when(i == 0)
    def _():
        row = lax.broadcasted_iota(jnp.int32, (tk, tq), 0)
        col = lax.broadcasted_iota(jnp.int32, (tk, tq), 1)
        ok_ref[...] = jnp.where((row % SUBLANES) * groups + row // SUBLANES < col, 1.0, 0.0)

    acc_ref[...] = jnp.zeros_like(acc_ref)
    c_ref[...] = jnp.ones_like(c_ref)
    _sb_scores(k_ref, q_heads, s_ref, 0, i, tk)
    step(i, 0, True)
    c_ref[...] = c_ref[...] * jnp.where(i >= 1, 1.0, 0.0)
    step(jnp.maximum(i - 1, 0), 1, False)

    def stick_left():
        return jnp.max(c_ref[...]) > 0.0

    n_rest = jnp.maximum(i - 1, 0)

    def more(state):
        m, go = state
        return jnp.logical_and(m < n_rest // 2, go)

    def pair(state):
        m, _ = state
        j = i - 2 - 2 * m
        step(j, 0, False)
        go_on = stick_left()

        @pl.when(go_on)
        def _():
            step(j - 1, 1, False)

        return m + 1, jnp.logical_and(go_on, stick_left())

    m_done, go = lax.while_loop(more, pair, (jnp.int32(0), stick_left()))

    @pl.when(jnp.logical_and(jnp.logical_and(n_rest % 2 == 1, m_done == n_rest // 2), go))
    def _():
        step(0, 0, False)

    ot = jnp.concatenate([acc_ref[h] for h in range(SB_HEADS_PER_STEP)], axis=0)
    o_ref[0] = ot.astype(jnp.bfloat16)


def _sb_attn(main, vt, tq, tk):
    bsz, seq, _ = main.shape
    assert tq == tk
    width = SB_HEADS_PER_STEP * SB_HEAD_DIM
    k_blk0 = SB_WIDTH // width
    kern = functools.partial(_sb_attn_kernel, tq=tq, tk=tk)
    return pl.pallas_call(
        kern,
        grid=(bsz, SB_WIDTH // width, seq // tq),
        in_specs=[
            pl.BlockSpec((1, tq, width), lambda b, p, i: (b, i, p)),
            pl.BlockSpec((1, seq, width), lambda b, p, i: (b, 0, k_blk0 + p)),
            pl.BlockSpec((1, width, seq), lambda b, p, i: (b, p, 0)),
        ],
        out_specs=pl.BlockSpec((1, width, tq), lambda b, p, i: (b, p, i)),
        out_shape=jax.ShapeDtypeStruct((bsz, SB_WIDTH, seq), jnp.bfloat16),
        scratch_shapes=[
            pltpu.VMEM((SB_HEADS_PER_STEP, SB_HEAD_DIM, tq), jnp.float32),
            pltpu.VMEM((SB_HEADS_PER_STEP, SUBLANES, tq), jnp.float32),
            pltpu.VMEM((2, SB_HEADS_PER_STEP, tk, tq), jnp.float32),
            pltpu.VMEM((tk, tq), jnp.float32),
        ],
        compiler_params=_cparams(("arbitrary", "arbitrary", "arbitrary")),
        name="sb_attn",
    )(main, main, vt)


def _diff_attn_kernel(slopes_ref, inv_slopes_ref, q_ref, k_ref, v_ref, lam_ref, g_ref, o_ref,
                      acc_ref, m_ref, l_ref, s_ref, smax_ref, kn_ref, bias_ref, *, tq, tk):
    pair = pl.program_id(1)
    i = pl.program_id(2)
    heads = range(DIFF_HEADS_PER_STEP)
    slopes = [slopes_ref[DIFF_HEADS_PER_STEP * pair + e] for e in heads]

    @pl.when(i == 0)
    def _():
        row0 = lax.broadcasted_iota(jnp.int32, (tk, tq), 0)
        col0 = lax.broadcasted_iota(jnp.int32, (tk, tq), 1)
        for e in heads:
            kf = k_ref[0, :, e * LANES:(e + 1) * LANES].astype(jnp.float32)
            kn2 = jnp.max(jnp.sum(kf * kf, axis=-1, keepdims=True), axis=0, keepdims=True)
            kn_ref[e] = jnp.broadcast_to(kn2, kn_ref.shape[1:])
            bias_ref[e] = slopes[e] * (row0 - col0).astype(jnp.float32)

    q_tiles, q_maps = [], []
    for e in heads:
        q2 = q_ref[0, :, e * LANES:(e + 1) * LANES]
        lane = lax.broadcasted_iota(jnp.int32, q2.shape, 1)
        zero = jnp.zeros_like(q2)
        q_tiles.append(q2)
        q_maps.append((jnp.where(lane < DIFF_HEAD_DIM, q2, zero),
                       jnp.where(lane < DIFF_HEAD_DIM, zero, q2)))

    def scores(e, slot, j, masked, dead=None):
        kb = k_ref[0, pl.ds(pl.multiple_of(j * tk, tk), tk), e * LANES:(e + 1) * LANES]
        for m in range(2):
            s = lax.dot_general(kb, q_maps[e][m], (((1,), (1,)), ((), ())),
                                preferred_element_type=jnp.float32) + bias_ref[e]
            if masked:
                row = lax.broadcasted_iota(jnp.int32, (tk, tq), 0)
                col = lax.broadcasted_iota(jnp.int32, (tk, tq), 1)
                s = jnp.where(row <= col, s, NEG_BIG)
            if dead is not None:
                s = jnp.where(dead, NEG_BIG, s)
            s_ref[slot, 2 * e + m] = s
            smax_ref[slot, 2 * e + m] = jnp.max(s, axis=0, keepdims=True)

    def probs(e, j, slot):
        cb = slopes[e] * ((j - i) * tk).astype(jnp.float32)
        ps, alphas = [], []
        for m in range(2):
            idx = 2 * e + m
            m_old = m_ref[idx]
            m_new = jnp.maximum(m_old, smax_ref[slot, idx] + cb)
            alpha = jnp.exp(m_old - m_new)
            p = jnp.exp(s_ref[slot, idx] - (m_new - cb))
            l_ref[idx] = alpha * l_ref[idx] + jnp.sum(p, axis=0, keepdims=True)
            m_ref[idx] = m_new
            ps.append(p.astype(jnp.bfloat16))
            alphas.append(alpha)
        return ps, alphas

    def values(e, j, ps, alphas):
        vtb = v_ref[0, e * DIFF_V_DIM:(e + 1) * DIFF_V_DIM, pl.ds(pl.multiple_of(j * tk, tk), tk)]
        for m in range(2):
            idx = 2 * e + m
            acc_ref[idx] = alphas[m] * acc_ref[idx] + jnp.dot(
                vtb, ps[m], preferred_element_type=jnp.float32)

    def step(es, j, slot, next_dead=None):
        for e in es:
            scores(e, 1 - slot, jnp.maximum(j - 1, 0), False, next_dead)
        pa = [probs(e, j, slot) for e in es]
        for e, (ps, alphas) in zip(es, pa):
            values(e, j, ps, alphas)

    acc_ref[...] = jnp.zeros_like(acc_ref)
    m_ref[...] = jnp.full_like(m_ref, NEG_BIG)
    l_ref[...] = jnp.zeros_like(l_ref)
    for e in heads:
        scores(e, 0, i, True)
    step(heads, i, 0, next_dead=(i == 0))
    step(heads, jnp.maximum(i - 1, 0), 1)

    for e in heads:
        qf = q_tiles[e].astype(jnp.float32)
        qn2 = jnp.max(jnp.sum(qf * qf, axis=-1, keepdims=True), axis=0, keepdims=True)
        zabs = jnp.sqrt(qn2 * kn_ref[e, 0:1, 0:1]) * NORM_SLACK
        m_lo = jnp.min(jnp.minimum(m_ref[2 * e], m_ref[2 * e + 1]), axis=1, keepdims=True)
        reach = (EXP_ZERO_MARGIN + zabs - m_lo) * inv_slopes_ref[DIFF_HEADS_PER_STEP * pair + e]
        n_need = jnp.floor(jnp.clip((reach - 1.0) * (1.0 / tk), -1.0, 1e6)) + 1.0
        n_back = jnp.maximum(jnp.minimum(i, jnp.max(n_need).astype(jnp.int32)) - 1, 0)

        def quad(n, carry, e=e):
            j = i - 2 - 4 * n
            step([e], j, 0)
            step([e], j - 1, 1)
            step([e], j - 2, 0)
            step([e], j - 3, 1)
            return carry

        n_quads = n_back // 4
        lax.fori_loop(0, n_quads, quad, 0)
        rest = n_back - 4 * n_quads

        @pl.when(rest >= 2)
        def _(e=e, n_quads=n_quads):
            j = i - 2 - 4 * n_quads
            step([e], j, 0)
            step([e], j - 1, 1)

        @pl.when(rest % 2 == 1)
        def _(e=e, n_back=n_back):
            step([e], i - 1 - n_back, 0)

    lam = lam_ref[0:1, 0:1]
    outs = []
    for e in heads:
        o = acc_ref[2 * e] / l_ref[2 * e] - lam * (acc_ref[2 * e + 1] / l_ref[2 * e + 1])
        ms = jnp.mean(o * o, axis=0, keepdims=True)
        outs.append(o * lax.rsqrt(ms + RMS_EPS) * g_ref[...] * (1.0 - LAM_INIT))
    o_ref[0] = jnp.concatenate(outs, axis=0).astype(jnp.bfloat16)


def _diff_attn(main, vt, slopes, lam, g_col, tq, tk):
    bsz, seq, _ = main.shape
    assert tq == tk
    kern = functools.partial(_diff_attn_kernel, tq=tq, tk=tk)
    n_maps = 2 * DIFF_HEADS_PER_STEP
    qk_width = DIFF_HEADS_PER_STEP * 2 * DIFF_HEAD_DIM
    v_width = DIFF_HEADS_PER_STEP * DIFF_V_DIM
    q_blk0 = COLBLK_Q_DF * LANES // qk_width
    k_blk0 = COLBLK_K_DF * LANES // qk_width
    v_blk0 = SB_WIDTH // v_width
    return pl.pallas_call(
        kern,
        grid=(bsz, DIFF_HEADS // DIFF_HEADS_PER_STEP, seq // tq),
        in_specs=[
            pl.BlockSpec(memory_space=pltpu.SMEM),
            pl.BlockSpec(memory_space=pltpu.SMEM),
            pl.BlockSpec((1, tq, qk_width), lambda b, h, i: (b, i, q_blk0 + h)),
            pl.BlockSpec((1, seq, qk_width), lambda b, h, i: (b, 0, k_blk0 + h)),
            pl.BlockSpec((1, v_width, seq), lambda b, h, i: (b, v_blk0 + h, 0)),
            pl.BlockSpec((SUBLANES, LANES), lambda b, h, i: (0, 0)),
            pl.BlockSpec((DIFF_V_DIM, 1), lambda b, h, i: (0, 0)),
        ],
        out_specs=pl.BlockSpec((1, v_width, tq), lambda b, h, i: (b, h, i)),
        out_shape=jax.ShapeDtypeStruct((bsz, DIFF_V_WIDTH, seq), jnp.bfloat16),
        scratch_shapes=[
            pltpu.VMEM((n_maps, DIFF_V_DIM, tq), jnp.float32),
            pltpu.VMEM((n_maps, 1, tq), jnp.float32),
            pltpu.VMEM((n_maps, 1, tq), jnp.float32),
            pltpu.VMEM((2, n_maps, tk, tq), jnp.float32),
            pltpu.VMEM((2, n_maps, 1, tq), jnp.float32),
            pltpu.VMEM((DIFF_HEADS_PER_STEP, SUBLANES, LANES), jnp.float32),
            pltpu.VMEM((DIFF_HEADS_PER_STEP, tk, tq), jnp.float32),
        ],
        compiler_params=_cparams(("arbitrary", "arbitrary", "arbitrary")),
        name="diff_attn",
    )(slopes, 1.0 / slopes, main, main, vt, lam, g_col)


def _pack_bf16_pair(a, b):
    ab = pltpu.bitcast(a.astype(jnp.bfloat16).astype(jnp.float32), jnp.uint32)
    bb = pltpu.bitcast(b.astype(jnp.bfloat16).astype(jnp.float32), jnp.uint32)
    return ab | (bb >> 16)


def _unpack_bf16_pair(w):
    hi = pltpu.bitcast(w & jnp.uint32(0xFFFF0000), jnp.float32)
    lo = pltpu.bitcast(w << 16, jnp.float32)
    return jnp.concatenate([hi, lo], axis=1)


def _merge_router_kernel(ysb_ref, ydf_ref, gates_ref, x_ref, mod_ref, wsb_ref, wdf_ref, wout_ref,
                         gpost_ref, gpre_ref, wrh_ref, wrl_ref, br_ref,
                         x1_ref, h2_ref, idx_ref, wgt_ref, rank_ref, cnt_ref, base_ref):
    first = jnp.logical_and(pl.program_id(0) == 0, pl.program_id(1) == 0)

    @pl.when(first)
    def _():
        base_ref[...] = jnp.zeros_like(base_ref)

    mod = mod_ref[0]
    subs = [slice(s * MERGE_SUB, (s + 1) * MERGE_SUB) for s in range(x_ref.shape[1] // MERGE_SUB)]
    half = D_MODEL // 2
    branch = [(jnp.dot(ysb_ref[0, :, rows].T, wsb_ref[...], preferred_element_type=jnp.float32),
               jnp.dot(ydf_ref[0, :, rows].T, wdf_ref[...], preferred_element_type=jnp.float32))
              for rows in subs]
    merged = []
    for rows, (a, b) in zip(subs, branch):
        g = gates_ref[0, rows, :].astype(jnp.float32)
        merged.append((g[:, :D_MODEL] * a + g[:, D_MODEL:] * b).astype(jnp.bfloat16))
    mixes = [jnp.dot(m, wout_ref[...], preferred_element_type=jnp.float32) for m in merged]
    h2s = []
    for rows, mix in zip(subs, mixes):
        x1 = x_ref[0, rows, :] + mod[2:3] * (_rms(mix) * gpost_ref[...])
        x1_ref[0, rows, :] = x1
        h2 = _rms(x1) * gpre_ref[...]
        h2 = h2 * (1.0 + mod[4:5]) + mod[3:4]
        h2_ref[0, rows, :] = _pack_bf16_pair(h2[:, :half], h2[:, half:])
        h2s.append(h2)
    logit_list = []
    for h2 in h2s:
        hh = h2.astype(jnp.bfloat16)
        hl = (h2 - hh.astype(jnp.float32)).astype(jnp.bfloat16)
        logit_list.append(jnp.dot(hh, wrh_ref[...], preferred_element_type=jnp.float32)
                          + jnp.dot(hh, wrl_ref[...], preferred_element_type=jnp.float32)
                          + jnp.dot(hl, wrh_ref[...], preferred_element_type=jnp.float32)
                          + br_ref[...])
    for rows, logits in zip(subs, logit_list):
        _route_rows(rows, logits, idx_ref, wgt_ref, rank_ref, cnt_ref, base_ref)


def _route_rows(rows, logits, idx_ref, wgt_ref, rank_ref, cnt_ref, base_ref):
    lane = lax.broadcasted_iota(jnp.int32, logits.shape, 1)
    lanef = lane.astype(jnp.float32)
    vals, idxs = [], []
    cur = logits
    for _ in range(TOP_K):
        mx = jnp.max(cur, axis=-1, keepdims=True)
        ix = jnp.min(jnp.where(cur == mx, lanef, float(LANES)), axis=-1, keepdims=True)
        cur = jnp.where(lanef == ix, -jnp.inf, cur)
        vals.append(mx)
        idxs.append(ix)
    es = [jnp.exp(v - vals[0]) for v in vals]
    den = es[0] + es[1] + es[2] + es[3]
    oi = jnp.zeros(logits.shape, jnp.float32)
    ow = jnp.zeros(logits.shape, jnp.float32)
    for k in range(TOP_K):
        oi = jnp.where(lane == k, idxs[k], oi)
        ow = jnp.where(lane == k, es[k] / den, ow)
    idx_ref[0, rows, :] = oi.astype(jnp.int32)
    wgt_ref[0, rows, :] = ow

    ts = logits.shape[0]
    member = jnp.zeros(logits.shape, jnp.float32)
    for k in range(TOP_K):
        member = member + (lanef == idxs[k]).astype(jnp.float32)
    rr = lax.broadcasted_iota(jnp.int32, (ts, ts), 0)
    cc = lax.broadcasted_iota(jnp.int32, (ts, ts), 1)
    lower = jnp.where(cc < rr, 1.0, 0.0).astype(jnp.bfloat16)
    before = jnp.dot(lower, member.astype(jnp.bfloat16), preferred_element_type=jnp.float32)
    base = base_ref[0:1, :]
    rank_all = before + base
    orank = jnp.zeros(logits.shape, jnp.float32)
    for k in range(TOP_K):
        rk = jnp.sum(jnp.where(lanef == idxs[k], rank_all, 0.0), axis=-1, keepdims=True)
        orank = jnp.where(lane == k, rk, orank)
    rank_ref[0, rows, :] = orank.astype(jnp.int32)
    new_base = base + jnp.sum(member, axis=0, keepdims=True)
    base_ref[...] = jnp.broadcast_to(new_base, base_ref.shape)
    cnt_ref[...] = jnp.broadcast_to(new_base, cnt_ref.shape).astype(jnp.int32)


def _merge_router(ysb, ydf, main, x, mod3, wsb, wdf, wout, gpost, gpre, wrh, wrl, br, ts):
    bsz, seq, _ = x.shape
    const = lambda b, i: (0, 0)
    return pl.pallas_call(
        _merge_router_kernel,
        grid=(bsz, seq // ts),
        in_specs=[
            pl.BlockSpec((1, SB_WIDTH, ts), lambda b, i: (b, 0, i)),
            pl.BlockSpec((1, DIFF_V_WIDTH, ts), lambda b, i: (b, 0, i)),
            pl.BlockSpec((1, ts, 2 * D_MODEL), lambda b, i: (b, i, GATE_COL0 // (2 * D_MODEL))),
            pl.BlockSpec((1, ts, D_MODEL), lambda b, i: (b, i, 0)),
            pl.BlockSpec((1, N_MOD, D_MODEL), lambda b, i: (b, 0, 0)),
            pl.BlockSpec((SB_WIDTH, D_MODEL), const),
            pl.BlockSpec((DIFF_V_WIDTH, D_MODEL), const),
            pl.BlockSpec((D_MODEL, D_MODEL), const),
            pl.BlockSpec((1, D_MODEL), const),
            pl.BlockSpec((1, D_MODEL), const),
            pl.BlockSpec((D_MODEL, LANES), const),
            pl.BlockSpec((D_MODEL, LANES), const),
            pl.BlockSpec((1, LANES), const),
        ],
        out_specs=[
            pl.BlockSpec((1, ts, D_MODEL), lambda b, i: (b, i, 0)),
            pl.BlockSpec((1, ts, D_MODEL // 2), lambda b, i: (b, i, 0)),
            pl.BlockSpec((1, ts, LANES), lambda b, i: (b, i, 0)),
            pl.BlockSpec((1, ts, LANES), lambda b, i: (b, i, 0)),
            pl.BlockSpec((1, ts, LANES), lambda b, i: (b, i, 0)),
            pl.BlockSpec((SUBLANES, LANES), const),
        ],
        out_shape=[
            jax.ShapeDtypeStruct((bsz, seq, D_MODEL), jnp.float32),
            jax.ShapeDtypeStruct((bsz, seq, D_MODEL // 2), jnp.uint32),
            jax.ShapeDtypeStruct((bsz, seq, LANES), jnp.int32),
            jax.ShapeDtypeStruct((bsz, seq, LANES), jnp.float32),
            jax.ShapeDtypeStruct((bsz, seq, LANES), jnp.int32),
            jax.ShapeDtypeStruct((SUBLANES, LANES), jnp.int32),
        ],
        scratch_shapes=[pltpu.VMEM((SUBLANES, LANES), jnp.float32)],
        compiler_params=_cparams(("arbitrary", "arbitrary")),
        name="merge_router",
    )(ysb, ydf, main, x, mod3, wsb, wdf, wout, gpost, gpre, wrh, wrl, br)


def _sc_gather_rows(table, idx):
    n = idx.shape[0]
    width = table.shape[1]
    n_workers = SC_CORES * SC_SUBCORES
    per_worker = n // n_workers
    n_chunks = per_worker // SC_GATHER_ROWS
    assert n_chunks * SC_GATHER_ROWS * n_workers == n
    mesh = plsc.VectorSubcoreMesh(core_axis_name="c", subcore_axis_name="s",
                                  num_cores=SC_CORES, num_subcores=SC_SUBCORES)

    def body(table_hbm, idx_hbm, out_hbm, idx_v, rows_v, sem):
        wid = lax.axis_index("s") * SC_CORES + lax.axis_index("c")
        base = wid * per_worker

        @pl.loop(0, n_chunks)
        def _(ci):
            off = pl.multiple_of(base + ci * SC_GATHER_ROWS, SC_GATHER_ROWS)
            pltpu.sync_copy(idx_hbm.at[pl.ds(off, SC_GATHER_ROWS)], idx_v)
            pltpu.async_copy(table_hbm.at[idx_v], rows_v, sem).wait()
            pltpu.sync_copy(rows_v, out_hbm.at[pl.ds(off, SC_GATHER_ROWS)])

    return pl.kernel(
        body,
        out_type=jax.ShapeDtypeStruct((n, width), table.dtype),
        mesh=mesh,
        scratch_types=[
            pltpu.VMEM((SC_GATHER_ROWS,), jnp.int32),
            pltpu.VMEM((SC_GATHER_ROWS, width), table.dtype),
            pltpu.SemaphoreType.DMA,
        ],
        name="sc_gather_rows",
    )(table, idx)


def _sc_scatter_rows(src, pos_kmajor, n_rows):
    n_tok, width = src.shape
    n_workers = SC_CORES * SC_SUBCORES
    per_worker = n_tok // n_workers
    n_chunks = per_worker // SC_GATHER_ROWS
    assert n_chunks * SC_GATHER_ROWS * n_workers == n_tok
    mesh = plsc.VectorSubcoreMesh(core_axis_name="c", subcore_axis_name="s",
                                  num_cores=SC_CORES, num_subcores=SC_SUBCORES)

    def body(src_hbm, idx_hbm, out_hbm, idx_v, rows_v):
        wid = lax.axis_index("s") * SC_CORES + lax.axis_index("c")
        base = wid * per_worker

        @pl.loop(0, n_chunks)
        def _(ci):
            off = pl.multiple_of(base + ci * SC_GATHER_ROWS, SC_GATHER_ROWS)
            pltpu.sync_copy(src_hbm.at[pl.ds(off, SC_GATHER_ROWS)], rows_v)
            for k in range(TOP_K):
                koff = pl.multiple_of(k * n_tok + off, SC_GATHER_ROWS)
                pltpu.sync_copy(idx_hbm.at[pl.ds(koff, SC_GATHER_ROWS)], idx_v)
                pltpu.sync_copy(rows_v, out_hbm.at[idx_v])

    return pl.kernel(
        body,
        out_type=jax.ShapeDtypeStruct((n_rows, width), src.dtype),
        mesh=mesh,
        scratch_types=[
            pltpu.VMEM((SC_GATHER_ROWS,), jnp.int32),
            pltpu.VMEM((SC_GATHER_ROWS, width), src.dtype),
        ],
        name="sc_scatter_rows",
    )(src, pos_kmajor)


def _moe_ffn_kernel(te_ref, nt_ref, x_ref, wgu_ref, bgu_ref, wd_ref, bd_ref, o_ref,
                    wgu_bf, wd_bf):
    i = pl.program_id(0)
    n_valid = nt_ref[0]

    new_expert = jnp.logical_or(i == 0, te_ref[i] != te_ref[jnp.maximum(i - 1, 0)])

    @pl.when(jnp.logical_and(i < n_valid, new_expert))
    def _():
        wgu_bf[...] = wgu_ref[0].astype(jnp.bfloat16)
        wd_bf[...] = wd_ref[0].astype(jnp.bfloat16)

    @pl.when(i < n_valid)
    def _():
        xb = _unpack_bf16_pair(x_ref[...]).astype(jnp.bfloat16)
        gu = jnp.dot(xb, wgu_bf[...], preferred_element_type=jnp.float32) + bgu_ref[0]
        gate = jnp.minimum(gu[:, :D_EXPERT], SWIGLU_LIMIT)
        up = jnp.clip(gu[:, D_EXPERT:], -SWIGLU_LIMIT, SWIGLU_LIMIT)
        act = (up + 1.0) * (gate * jax.nn.sigmoid(SWIGLU_ALPHA * gate))
        out = jnp.dot(act.astype(jnp.bfloat16), wd_bf[...],
                      preferred_element_type=jnp.float32) + bd_ref[0]
        half = D_MODEL // 2
        o_ref[...] = _pack_bf16_pair(out[:, :half], out[:, half:])

    @pl.when(i >= n_valid)
    def _():
        o_ref[...] = jnp.zeros_like(o_ref)


def _moe_ffn(tile_expert, n_valid, xg, wgu, bgu, wd, bd, tm):
    n_tiles = xg.shape[0] // tm
    grid_spec = pltpu.PrefetchScalarGridSpec(
        num_scalar_prefetch=2,
        grid=(n_tiles,),
        in_specs=[
            pl.BlockSpec((tm, D_MODEL // 2), lambda i, te, nt: (jnp.minimum(i, nt[0] - 1), 0)),
            pl.BlockSpec((1, D_MODEL, 2 * D_EXPERT), lambda i, te, nt: (te[i], 0, 0)),
            pl.BlockSpec((1, 1, 2 * D_EXPERT), lambda i, te, nt: (te[i], 0, 0)),
            pl.BlockSpec((1, D_EXPERT, D_MODEL), lambda i, te, nt: (te[i], 0, 0)),
            pl.BlockSpec((1, 1, D_MODEL), lambda i, te, nt: (te[i], 0, 0)),
        ],
        out_specs=pl.BlockSpec((tm, D_MODEL // 2), lambda i, te, nt: (i, 0)),
        scratch_shapes=[
            pltpu.VMEM((D_MODEL, 2 * D_EXPERT), jnp.bfloat16),
            pltpu.VMEM((D_EXPERT, D_MODEL), jnp.bfloat16),
        ],
    )
    return pl.pallas_call(
        _moe_ffn_kernel,
        grid_spec=grid_spec,
        out_shape=jax.ShapeDtypeStruct((n_tiles * tm, D_MODEL // 2), jnp.uint32),
        compiler_params=_cparams(("arbitrary",)),
        name="moe_ffn",
    )(tile_expert, n_valid, xg, wgu, bgu.reshape(N_EXPERTS, 1, -1), wd,
      bd.reshape(N_EXPERTS, 1, -1))


def _moe_combine_kernel(rows_ref, wgt_ref, x1_ref, mod_ref, g_ref, o_ref):
    ts = x1_ref.shape[1]
    w = wgt_ref[0]
    y = jnp.zeros(x1_ref.shape[1:], jnp.float32)
    for k in range(TOP_K):
        y = y + w[:, k:k + 1] * _unpack_bf16_pair(rows_ref[k * ts:(k + 1) * ts, :])
    mod = mod_ref[0]
    o_ref[0] = x1_ref[0] + mod[5:6] * (_rms(y) * g_ref[...])


def _moe_combine(rows, wgt, x1, mod3, g_post, ts):
    bsz, seq, _ = x1.shape
    per_b = seq // ts
    return pl.pallas_call(
        _moe_combine_kernel,
        grid=(bsz, per_b),
        in_specs=[
            pl.BlockSpec((TOP_K * ts, D_MODEL // 2), lambda b, i: (b * per_b + i, 0)),
            pl.BlockSpec((1, ts, LANES), lambda b, i: (b, i, 0)),
            pl.BlockSpec((1, ts, D_MODEL), lambda b, i: (b, i, 0)),
            pl.BlockSpec((1, N_MOD, D_MODEL), lambda b, i: (b, 0, 0)),
            pl.BlockSpec((1, D_MODEL), lambda b, i: (0, 0)),
        ],
        out_specs=pl.BlockSpec((1, ts, D_MODEL), lambda b, i: (b, i, 0)),
        out_shape=jax.ShapeDtypeStruct((bsz, seq, D_MODEL), jnp.float32),
        compiler_params=_cparams(("arbitrary", "arbitrary")),
        name="moe_combine",
    )(rows, wgt, x1, mod3, g_post)


def _routing(top_idx, rank, counts, tm, n_tiles):
    padded = ((counts + tm - 1) // tm) * tm
    pend = jnp.cumsum(padded)
    pstart = pend - padded
    onehot = top_idx[:, :, None] == jnp.arange(N_EXPERTS, dtype=jnp.int32)[None, None, :]
    pos = rank + jnp.sum(jnp.where(onehot, pstart[None, None, :], 0), axis=-1)
    n_valid = (pend[-1] // tm).astype(jnp.int32)
    tile_row0 = jnp.arange(n_tiles, dtype=jnp.int32) * tm
    tile_expert = jnp.minimum(
        jnp.sum((tile_row0[:, None] >= pend[None, :]).astype(jnp.int32), axis=1), N_EXPERTS - 1)
    last_oh = jnp.arange(n_tiles, dtype=jnp.int32) == jnp.maximum(n_valid - 1, 0)
    last_expert = jnp.sum(jnp.where(last_oh, tile_expert, 0))
    tile_expert = jnp.where(jnp.arange(n_tiles, dtype=jnp.int32) < n_valid, tile_expert, last_expert)
    return pos.astype(jnp.int32), tile_expert.astype(jnp.int32), n_valid.reshape(1)


def _alibi_slopes(n_heads):
    return 2.0 ** (-8.0 * jnp.arange(1, n_heads + 1, dtype=jnp.float32) / n_heads)


def _layer(x, c, w_mod, b_mod, g_pre_mix, g_post_mix, w_in, lamv, g_subln, w_branch_sb,
           w_branch_diff, w_out, g_pre_ffn, g_post_ffn, w_router, b_router, w_gate_up,
           b_gate_up, w_down, b_down, tiles):
    ts_in, tq, ts_merge, tm, ts_comb = tiles
    bsz, seq, d = x.shape
    n_tok = bsz * seq
    bf = jnp.bfloat16

    mod, lam = _mod_proj(c, w_mod, b_mod, lamv)
    mod3 = mod.reshape(bsz, N_MOD, d)

    o_vsb = 2 * SB_WIDTH
    o_qdf = 3 * SB_WIDTH
    o_vdf = o_qdf + 2 * DIFF_QK_WIDTH
    o_g = o_vdf + DIFF_V_WIDTH
    w_main = jnp.concatenate([w_in[:, :o_vsb], w_in[:, o_qdf:o_vdf], w_in[:, o_g:]], axis=1).astype(bf)
    w_vt = jnp.concatenate([w_in[:, o_vsb:o_qdf], w_in[:, o_vdf:o_g]], axis=1).T.astype(bf)

    main, vt = _in_proj(x, mod3, g_pre_mix.reshape(1, d), w_main, w_vt, ts_in, tq)
    y_sb = _sb_attn(main, vt, tq, tq)
    y_df = _diff_attn(main, vt, _alibi_slopes(DIFF_HEADS), lam,
                      g_subln.reshape(DIFF_V_DIM, 1), tq, tq)

    wr = jnp.zeros((d, LANES), jnp.float32).at[:, :N_EXPERTS].set(w_router)
    wrh = wr.astype(bf)
    wrl = (wr - wrh.astype(jnp.float32)).astype(bf)
    br = jnp.full((1, LANES), NEG_BIG, jnp.float32).at[0, :N_EXPERTS].set(b_router)
    x1, h2p, top_idx, top_w, rank, counts = _merge_router(
        y_sb, y_df, main, x, mod3, w_branch_sb.astype(bf), w_branch_diff.astype(bf),
        w_out.astype(bf), g_post_mix.reshape(1, d), g_pre_ffn.reshape(1, d), wrh, wrl, br, ts_merge)

    n_tiles = (n_tok * TOP_K) // tm + N_EXPERTS
    pos, tile_expert, n_valid = _routing(
        top_idx.reshape(n_tok, LANES)[:, :TOP_K], rank.reshape(n_tok, LANES)[:, :TOP_K],
        counts[0, :N_EXPERTS], tm, n_tiles)
    xg = _sc_scatter_rows(h2p.reshape(n_tok, d // 2), pos.T.reshape(TOP_K * n_tok), n_tiles * tm)
    rows = _moe_ffn(tile_expert, n_valid, xg, w_gate_up, b_gate_up, w_down, b_down, tm)
    pos_steps = pos.reshape(n_tok // ts_comb, ts_comb, TOP_K).swapaxes(1, 2).reshape(n_tok * TOP_K)
    tok_rows = _sc_gather_rows(rows, pos_steps)
    return _moe_combine(tok_rows, top_w, x1, mod3, g_post_ffn.reshape(1, d), ts_comb)


def kernel(x, c, w_mod, b_mod, g_pre_mix, g_post_mix, w_in, lambda_q1, lambda_k1, lambda_q2,
           lambda_k2, g_subln, w_branch_sb, w_branch_diff, w_out, g_pre_ffn, g_post_ffn,
           w_router, b_router, w_gate_up, b_gate_up, w_down, b_down):
    depth = w_mod.shape[0]
    assert depth == 1, "LAM_INIT is the layer-0 value"
    for l in range(depth):
        lamv = jnp.stack([lambda_q1[l], lambda_k1[l], lambda_q2[l], lambda_k2[l]])
        x = _layer(x, c, w_mod[l], b_mod[l], g_pre_mix[l], g_post_mix[l], w_in[l], lamv,
                   g_subln[l], w_branch_sb[l], w_branch_diff[l], w_out[l], g_pre_ffn[l],
                   g_post_ffn[l], w_router[l], b_router[l], w_gate_up[l], b_gate_up[l],
                   w_down[l], b_down[l], _Tiles())
    return x
```
